```python
import jax, jax.numpy as jnp
from jax import lax
import numpy as np

D_MODEL = 1024
BATCH = 32
SEQ = 2048
DEPTH = 2

N_MIXERS = 2
EXPAND = 2
D_INNER = EXPAND * D_MODEL
HEAD_DIM = 64
N_HEADS = D_INNER // HEAD_DIM
Q_BLOCK = 128
NORM_EPS = 1e-6
N_FOX = (DEPTH + N_MIXERS - 1) // N_MIXERS
N_SB = DEPTH // N_MIXERS
FOX_IN = 4 * D_INNER + N_HEADS
SB_IN = 4 * D_INNER

kernel_name = "hybrid_fox_stickbreaking_adaln"


def rmsnorm(x, g):
    xf = x.astype(jnp.float32)
    y = xf * lax.rsqrt(jnp.mean(xf * xf, axis=-1, keepdims=True) + NORM_EPS)
    return (y * g.astype(jnp.float32)).astype(x.dtype)


def adaln(c, w_ada, b_ada):
    mod = jax.nn.silu(c) @ w_ada + b_ada
    shift, scale, gate = jnp.split(mod, 3, axis=-1)
    return shift[:, None, :], scale[:, None, :], gate[:, None, :]


def split_heads(t):
    b, s, _ = t.shape
    return t.reshape(b, s, N_HEADS, HEAD_DIM).transpose(0, 2, 1, 3)


def merge_heads(t):
    b, h, s, d = t.shape
    return t.transpose(0, 2, 1, 3).reshape(b, s, h * d)


def forgetting_attention(q, k, v, log_f):
    seq = q.shape[2]
    cum = jnp.cumsum(log_f, axis=-1)
    sm_scale = HEAD_DIM ** -0.5
    outs = []
    for blk in range(seq // Q_BLOCK):
        q0, q1 = blk * Q_BLOCK, (blk + 1) * Q_BLOCK
        qb, kp, vp = q[:, :, q0:q1], k[:, :, :q1], v[:, :, :q1]
        logits = jnp.einsum("bhtd,bhsd->bhts", qb, kp,
                            preferred_element_type=jnp.float32) * sm_scale
        logits = logits + cum[:, :, q0:q1, None] - cum[:, :, None, :q1]
        t_idx = jnp.arange(q0, q1)[:, None]
        s_idx = jnp.arange(q1)[None, :]
        logits = jnp.where(s_idx <= t_idx, logits, -jnp.inf)
        p = jax.nn.softmax(logits, axis=-1)
        outs.append(jnp.einsum("bhts,bhsd->bhtd", p.astype(vp.dtype), vp))
    return jnp.concatenate(outs, axis=2)


def stick_breaking_attention(q, k, v):
    seq = q.shape[2]
    sm_scale = HEAD_DIM ** -0.5
    outs = []
    for blk in range(seq // Q_BLOCK):
        q0, q1 = blk * Q_BLOCK, (blk + 1) * Q_BLOCK
        qb, kp, vp = q[:, :, q0:q1], k[:, :, :q1], v[:, :, :q1]
        z = jnp.einsum("bhtd,bhsd->bhts", qb, kp,
                       preferred_element_type=jnp.float32) * sm_scale
        t_idx = jnp.arange(q0, q1)[:, None]
        s_idx = jnp.arange(q1)[None, :]
        strict = s_idx < t_idx
        log_beta = jax.nn.log_sigmoid(z)
        log_keep = jnp.where(strict, jax.nn.log_sigmoid(-z), 0.0)
        after = lax.cumsum(log_keep, axis=3, reverse=True) - log_keep
        a = jnp.where(strict, jnp.exp(log_beta + after), 0.0)
        outs.append(jnp.einsum("bhts,bhsd->bhtd", a.astype(vp.dtype), vp))
    return jnp.concatenate(outs, axis=2)


def fox_layer(x, c, norm_g, w_ada, b_ada, w_in, b_f, w_out):
    shift, scale, gate = adaln(c, w_ada, b_ada)
    h = rmsnorm(x, norm_g) * (1.0 + scale) + shift
    proj = h @ w_in
    q, k, v, zg = (proj[..., i * D_INNER:(i + 1) * D_INNER] for i in range(4))
    f_logit = proj[..., 4 * D_INNER:] + b_f
    log_f = jax.nn.log_sigmoid(f_logit.astype(jnp.float32)).transpose(0, 2, 1)
    o = forgetting_attention(split_heads(q), split_heads(k), split_heads(v), log_f)
    y = (merge_heads(o) * jax.nn.silu(zg)) @ w_out
    return x + gate * y


def sb_layer(x, c, norm_g, w_ada, b_ada, w_in, w_out):
    shift, scale, gate = adaln(c, w_ada, b_ada)
    h = rmsnorm(x, norm_g) * (1.0 + scale) + shift
    proj = h @ w_in
    q, k, v, zg = (proj[..., i * D_INNER:(i + 1) * D_INNER] for i in range(4))
    o = stick_breaking_attention(split_heads(q), split_heads(k), split_heads(v))
    y = (merge_heads(o) * jax.nn.silu(zg)) @ w_out
    return x + gate * y


def _fwd_setup_inputs(seed: int = 0) -> dict:
    key = jax.random.key(seed)
    ks = jax.random.split(key, 16)
    f32 = jnp.float32
    din = D_MODEL ** -0.5
    dinner = D_INNER ** -0.5
    x = jax.random.normal(ks[0], (BATCH, SEQ, D_MODEL), f32)
    c = jax.random.normal(ks[1], (BATCH, D_MODEL), f32)
    fox_norm_g = 1.0 + 0.02 * jax.random.normal(ks[2], (N_FOX, D_MODEL), f32)
    fox_w_ada = 0.5 * din * jax.random.normal(ks[3], (N_FOX, D_MODEL, 3 * D_MODEL), f32)
    fox_b_ada = 0.02 * jax.random.normal(ks[4], (N_FOX, 3 * D_MODEL), f32)
    fox_w_in = din * jax.random.normal(ks[5], (N_FOX, D_MODEL, FOX_IN), f32)
    fox_b_f = 1.0 + 0.5 * jax.random.normal(ks[6], (N_FOX, N_HEADS), f32)
    fox_w_out = dinner * jax.random.normal(ks[7], (N_FOX, D_INNER, D_MODEL), f32)
    sb_norm_g = 1.0 + 0.02 * jax.random.normal(ks[8], (N_SB, D_MODEL), f32)
    sb_w_ada = 0.5 * din * jax.random.normal(ks[9], (N_SB, D_MODEL, 3 * D_MODEL), f32)
    sb_b_ada = 0.02 * jax.random.normal(ks[10], (N_SB, 3 * D_MODEL), f32)
    sb_w_in = din * jax.random.normal(ks[11], (N_SB, D_MODEL, SB_IN), f32)
    sb_w_out = dinner * jax.random.normal(ks[12], (N_SB, D_INNER, D_MODEL), f32)
    final_norm_g = 1.0 + 0.02 * jax.random.normal(ks[13], (D_MODEL,), f32)
    return {"x": x, "c": c,
            "fox_norm_g": fox_norm_g, "fox_w_ada": fox_w_ada, "fox_b_ada": fox_b_ada,
            "fox_w_in": fox_w_in, "fox_b_f": fox_b_f, "fox_w_out": fox_w_out,
            "sb_norm_g": sb_norm_g, "sb_w_ada": sb_w_ada, "sb_b_ada": sb_b_ada,
            "sb_w_in": sb_w_in, "sb_w_out": sb_w_out,
            "final_norm_g": final_norm_g}


def _fwd_reference(x, c, fox_norm_g, fox_w_ada, fox_b_ada, fox_w_in, fox_b_f, fox_w_out,
              sb_norm_g, sb_w_ada, sb_b_ada, sb_w_in, sb_w_out, final_norm_g):
    for i in range(DEPTH):
        j = i // N_MIXERS
        if i % N_MIXERS == 0:
            x = fox_layer(x, c, fox_norm_g[j], fox_w_ada[j], fox_b_ada[j],
                          fox_w_in[j], fox_b_f[j], fox_w_out[j])
        else:
            x = sb_layer(x, c, sb_norm_g[j], sb_w_ada[j], sb_b_ada[j],
                         sb_w_in[j], sb_w_out[j])
    return rmsnorm(x, final_norm_g)


import jax as _jax
import jax.numpy as _jnp

TWIN_FORMAT = 'train_step'
FWD_PARAMS = ['x', 'c', 'fox_norm_g', 'fox_w_ada', 'fox_b_ada', 'fox_w_in', 'fox_b_f', 'fox_w_out', 'sb_norm_g', 'sb_w_ada', 'sb_b_ada', 'sb_w_in', 'sb_w_out', 'final_norm_g']
TWIN_WEIGHTS = ['fox_norm_g', 'fox_w_ada', 'fox_b_ada', 'fox_w_in', 'fox_b_f', 'fox_w_out', 'sb_norm_g', 'sb_w_ada', 'sb_b_ada', 'sb_w_in', 'sb_w_out', 'final_norm_g']
TWIN_DIFF_INPUT = 'x'
TWIN_INPUTS = ['x', 'c', 'fox_norm_g', 'fox_w_ada', 'fox_b_ada', 'fox_w_in', 'fox_b_f', 'fox_w_out', 'sb_norm_g', 'sb_w_ada', 'sb_b_ada', 'sb_w_in', 'sb_w_out', 'final_norm_g', 'loss_target', 'm_fox_norm_g', 'm_fox_w_ada', 'm_fox_b_ada', 'm_fox_w_in', 'm_fox_b_f', 'm_fox_w_out', 'm_sb_norm_g', 'm_sb_w_ada', 'm_sb_b_ada', 'm_sb_w_in', 'm_sb_w_out', 'm_final_norm_g', 'v_fox_norm_g', 'v_fox_w_ada', 'v_fox_b_ada', 'v_fox_w_in', 'v_fox_b_f', 'v_fox_w_out', 'v_sb_norm_g', 'v_sb_w_ada', 'v_sb_b_ada', 'v_sb_w_in', 'v_sb_w_out', 'v_final_norm_g']
TWIN_OUTPUTS = ['loss', 'grad_x', 'grad_fox_norm_g', 'grad_fox_w_ada', 'grad_fox_b_ada', 'grad_fox_w_in', 'grad_fox_b_f', 'grad_fox_w_out', 'grad_sb_norm_g', 'grad_sb_w_ada', 'grad_sb_b_ada', 'grad_sb_w_in', 'grad_sb_w_out', 'grad_final_norm_g', 'delta_fox_norm_g', 'delta_fox_w_ada', 'delta_fox_b_ada', 'delta_fox_w_in', 'delta_fox_b_f', 'delta_fox_w_out', 'delta_sb_norm_g', 'delta_sb_w_ada', 'delta_sb_b_ada', 'delta_sb_w_in', 'delta_sb_w_out', 'delta_final_norm_g', 'new_m_fox_norm_g', 'new_m_fox_w_ada', 'new_m_fox_b_ada', 'new_m_fox_w_in', 'new_m_fox_b_f', 'new_m_fox_w_out', 'new_m_sb_norm_g', 'new_m_sb_w_ada', 'new_m_sb_b_ada', 'new_m_sb_w_in', 'new_m_sb_w_out', 'new_m_final_norm_g', 'new_v_fox_norm_g', 'new_v_fox_w_ada', 'new_v_fox_b_ada', 'new_v_fox_w_in', 'new_v_fox_b_f', 'new_v_fox_w_out', 'new_v_sb_norm_g', 'new_v_sb_w_ada', 'new_v_sb_b_ada', 'new_v_sb_w_in', 'new_v_sb_w_out', 'new_v_final_norm_g']
TWIN_LEAF_KINDS = {'loss': 'loss', 'grad_x': 'grad_x', 'grad_fox_norm_g': 'grad_w', 'grad_fox_w_ada': 'grad_w', 'grad_fox_b_ada': 'grad_w', 'grad_fox_w_in': 'grad_w', 'grad_fox_b_f': 'grad_w', 'grad_fox_w_out': 'grad_w', 'grad_sb_norm_g': 'grad_w', 'grad_sb_w_ada': 'grad_w', 'grad_sb_b_ada': 'grad_w', 'grad_sb_w_in': 'grad_w', 'grad_sb_w_out': 'grad_w', 'grad_final_norm_g': 'grad_w', 'delta_fox_norm_g': 'delta_w', 'delta_fox_w_ada': 'delta_w', 'delta_fox_b_ada': 'delta_w', 'delta_fox_w_in': 'delta_w', 'delta_fox_b_f': 'delta_w', 'delta_fox_w_out': 'delta_w', 'delta_sb_norm_g': 'delta_w', 'delta_sb_w_ada': 'delta_w', 'delta_sb_b_ada': 'delta_w', 'delta_sb_w_in': 'delta_w', 'delta_sb_w_out': 'delta_w', 'delta_final_norm_g': 'delta_w', 'new_m_fox_norm_g': 'new_m', 'new_m_fox_w_ada': 'new_m', 'new_m_fox_b_ada': 'new_m', 'new_m_fox_w_in': 'new_m', 'new_m_fox_b_f': 'new_m', 'new_m_fox_w_out': 'new_m', 'new_m_sb_norm_g': 'new_m', 'new_m_sb_w_ada': 'new_m', 'new_m_sb_b_ada': 'new_m', 'new_m_sb_w_in': 'new_m', 'new_m_sb_w_out': 'new_m', 'new_m_final_norm_g': 'new_m', 'new_v_fox_norm_g': 'new_v', 'new_v_fox_w_ada': 'new_v', 'new_v_fox_b_ada': 'new_v', 'new_v_fox_w_in': 'new_v', 'new_v_fox_b_f': 'new_v', 'new_v_fox_w_out': 'new_v', 'new_v_sb_norm_g': 'new_v', 'new_v_sb_w_ada': 'new_v', 'new_v_sb_b_ada': 'new_v', 'new_v_sb_w_in': 'new_v', 'new_v_sb_w_out': 'new_v', 'new_v_final_norm_g': 'new_v'}


def _forward(args):
    return _fwd_reference(*[args[k] for k in FWD_PARAMS])


def _output_shape():
    out = _jax.eval_shape(lambda: _forward(_fwd_setup_inputs(0)))
    return out.shape, out.dtype

N_MICROBATCH = 1
ADAM_LR = 0.001
ADAM_B1 = 0.9
ADAM_B2 = 0.999
ADAM_EPS = 1e-08
ADAM_WD = 0.01
ADAM_STEP = 10
PER_EXAMPLE_BATCH_AXIS = {'x': 0, 'c': 0, 'loss_target': 0}
SHARED_INPUTS = []
_WEIGHT_DTYPES = {'fox_norm_g': _jnp.float32, 'fox_w_ada': _jnp.float32, 'fox_b_ada': _jnp.float32, 'fox_w_in': _jnp.float32, 'fox_b_f': _jnp.float32, 'fox_w_out': _jnp.float32, 'sb_norm_g': _jnp.float32, 'sb_w_ada': _jnp.float32, 'sb_b_ada': _jnp.float32, 'sb_w_in': _jnp.float32, 'sb_w_out': _jnp.float32, 'final_norm_g': _jnp.float32}
MOMENT_SCALE = {'fox_norm_g': 5.352423e-02, 'fox_w_ada': 5.055871e-02, 'fox_b_ada': 8.203298e-02, 'fox_w_in': 2.089463e-02, 'fox_b_f': 9.590000e-02, 'fox_w_out': 3.458812e-02, 'sb_norm_g': 6.102934e-02, 'sb_w_ada': 5.406168e-02, 'sb_b_ada': 8.676244e-02, 'sb_w_in': 2.144854e-02, 'sb_w_out': 3.819519e-02, 'final_norm_g': 6.393746e+01}


def _to_microbatches(a, axis):
    t = _jnp.moveaxis(a, axis, 0)
    t = t.reshape((N_MICROBATCH, t.shape[0] // N_MICROBATCH) + t.shape[1:])
    return _jnp.moveaxis(t, 1, axis + 1)


def setup_inputs(seed: int = 0) -> dict:
    inp = _fwd_setup_inputs(seed)
    key = _jax.random.fold_in(_jax.random.key(seed), 7919)
    shape, _ = _output_shape()
    out = dict(inp)
    out["loss_target"] = _jax.random.normal(_jax.random.fold_in(key, 0), shape, _jnp.float32)
    for i, name in enumerate(TWIN_WEIGHTS):
        w = inp[name].astype(_jnp.float32)
        if MOMENT_SCALE is None:
            s = _jnp.sqrt(_jnp.mean(_jnp.square(w)) + 1e-30)
        else:
            s = MOMENT_SCALE[name]
        km, kv = _jax.random.split(_jax.random.fold_in(key, i + 1))
        out[name] = w
        out["m_" + name] = s * _jax.random.normal(km, w.shape, _jnp.float32)
        out["v_" + name] = (s * s) * _jax.random.uniform(kv, w.shape, _jnp.float32, 0.5, 1.5)
    if N_MICROBATCH > 1:
        for name, axis in PER_EXAMPLE_BATCH_AXIS.items():
            out[name] = _to_microbatches(out[name], axis)
    return {'x': out['x'], 'c': out['c'], 'fox_norm_g': out['fox_norm_g'], 'fox_w_ada': out['fox_w_ada'], 'fox_b_ada': out['fox_b_ada'], 'fox_w_in': out['fox_w_in'], 'fox_b_f': out['fox_b_f'], 'fox_w_out': out['fox_w_out'], 'sb_norm_g': out['sb_norm_g'], 'sb_w_ada': out['sb_w_ada'], 'sb_b_ada': out['sb_b_ada'], 'sb_w_in': out['sb_w_in'], 'sb_w_out': out['sb_w_out'], 'final_norm_g': out['final_norm_g'], 'loss_target': out['loss_target'], 'm_fox_norm_g': out['m_fox_norm_g'], 'm_fox_w_ada': out['m_fox_w_ada'], 'm_fox_b_ada': out['m_fox_b_ada'], 'm_fox_w_in': out['m_fox_w_in'], 'm_fox_b_f': out['m_fox_b_f'], 'm_fox_w_out': out['m_fox_w_out'], 'm_sb_norm_g': out['m_sb_norm_g'], 'm_sb_w_ada': out['m_sb_w_ada'], 'm_sb_b_ada': out['m_sb_b_ada'], 'm_sb_w_in': out['m_sb_w_in'], 'm_sb_w_out': out['m_sb_w_out'], 'm_final_norm_g': out['m_final_norm_g'], 'v_fox_norm_g': out['v_fox_norm_g'], 'v_fox_w_ada': out['v_fox_w_ada'], 'v_fox_b_ada': out['v_fox_b_ada'], 'v_fox_w_in': out['v_fox_w_in'], 'v_fox_b_f': out['v_fox_b_f'], 'v_fox_w_out': out['v_fox_w_out'], 'v_sb_norm_g': out['v_sb_norm_g'], 'v_sb_w_ada': out['v_sb_w_ada'], 'v_sb_b_ada': out['v_sb_b_ada'], 'v_sb_w_in': out['v_sb_w_in'], 'v_sb_w_out': out['v_sb_w_out'], 'v_final_norm_g': out['v_final_norm_g']}


def _loss(weights, diff, rest, loss_target):
    with _jax.named_scope("forward"):
        args = {**rest, TWIN_DIFF_INPUT: diff, **{k: w.astype(_WEIGHT_DTYPES[k]) for k, w in weights.items()}}
        y = _forward(args)
    with _jax.named_scope("loss_head"):
        err = _jnp.square(y.astype(_jnp.float32) - loss_target)
        return 0.5 * _jnp.sum(_jnp.mean(err, axis=-1)) if err.ndim else 0.5 * err


def _adamw(w, g, m, v):
    m = ADAM_B1 * m + (1.0 - ADAM_B1) * g
    v = ADAM_B2 * v + (1.0 - ADAM_B2) * _jnp.square(g)
    m_hat = m / (1.0 - ADAM_B1 ** ADAM_STEP)
    v_hat = v / (1.0 - ADAM_B2 ** ADAM_STEP)
    delta = -ADAM_LR * (m_hat / (_jnp.sqrt(v_hat) + ADAM_EPS) + ADAM_WD * w)
    return delta, m, v


def reference(x, c, fox_norm_g, fox_w_ada, fox_b_ada, fox_w_in, fox_b_f, fox_w_out, sb_norm_g, sb_w_ada, sb_b_ada, sb_w_in, sb_w_out, final_norm_g, loss_target, m_fox_norm_g, m_fox_w_ada, m_fox_b_ada, m_fox_w_in, m_fox_b_f, m_fox_w_out, m_sb_norm_g, m_sb_w_ada, m_sb_b_ada, m_sb_w_in, m_sb_w_out, m_final_norm_g, v_fox_norm_g, v_fox_w_ada, v_fox_b_ada, v_fox_w_in, v_fox_b_f, v_fox_w_out, v_sb_norm_g, v_sb_w_ada, v_sb_b_ada, v_sb_w_in, v_sb_w_out, v_final_norm_g):
    given = dict(x=x, c=c, fox_norm_g=fox_norm_g, fox_w_ada=fox_w_ada, fox_b_ada=fox_b_ada, fox_w_in=fox_w_in, fox_b_f=fox_b_f, fox_w_out=fox_w_out, sb_norm_g=sb_norm_g, sb_w_ada=sb_w_ada, sb_b_ada=sb_b_ada, sb_w_in=sb_w_in, sb_w_out=sb_w_out, final_norm_g=final_norm_g, loss_target=loss_target, m_fox_norm_g=m_fox_norm_g, m_fox_w_ada=m_fox_w_ada, m_fox_b_ada=m_fox_b_ada, m_fox_w_in=m_fox_w_in, m_fox_b_f=m_fox_b_f, m_fox_w_out=m_fox_w_out, m_sb_norm_g=m_sb_norm_g, m_sb_w_ada=m_sb_w_ada, m_sb_b_ada=m_sb_b_ada, m_sb_w_in=m_sb_w_in, m_sb_w_out=m_sb_w_out, m_final_norm_g=m_final_norm_g, v_fox_norm_g=v_fox_norm_g, v_fox_w_ada=v_fox_w_ada, v_fox_b_ada=v_fox_b_ada, v_fox_w_in=v_fox_w_in, v_fox_b_f=v_fox_b_f, v_fox_w_out=v_fox_w_out, v_sb_norm_g=v_sb_norm_g, v_sb_w_ada=v_sb_w_ada, v_sb_b_ada=v_sb_b_ada, v_sb_w_in=v_sb_w_in, v_sb_w_out=v_sb_w_out, v_final_norm_g=v_final_norm_g)
    weights = {n: given[n] for n in TWIN_WEIGHTS}
    shared = {n: given[n] for n in SHARED_INPUTS}
    per_example = {n: given[n] for n in ['x', 'c']}
    grad_fn = _jax.value_and_grad(_loss, argnums=(0, 1))

    def one_microbatch(ex, loss_target):
        ex = dict(ex)
        diff = ex.pop(TWIN_DIFF_INPUT)
        return grad_fn(weights, diff, {**shared, **ex}, loss_target)

    if N_MICROBATCH == 1:
        loss, (grad_w, grad_x) = one_microbatch(per_example, given["loss_target"])
    else:
        def body(carry, xs):
            loss_sum, grad_sum = carry
            l_k, (gw_k, gx_k) = one_microbatch(xs[0], xs[1])
            with _jax.named_scope("update"):
                return (loss_sum + l_k, _jax.tree.map(_jnp.add, grad_sum, gw_k)), gx_k

        init = (_jnp.zeros((), _jnp.float32), _jax.tree.map(_jnp.zeros_like, weights))
        (loss, grad_w), grad_x = _jax.lax.scan(body, init, (per_example, given["loss_target"]))
    with _jax.named_scope("update"):
        delta_w, new_m, new_v = {}, {}, {}
        for n in TWIN_WEIGHTS:
            delta_w[n], new_m[n], new_v[n] = _adamw(weights[n], grad_w[n], given["m_" + n], given["v_" + n])
    return (loss, grad_x, *[grad_w[n] for n in TWIN_WEIGHTS], *[delta_w[n] for n in TWIN_WEIGHTS],
            *[new_m[n] for n in TWIN_WEIGHTS], *[new_v[n] for n in TWIN_WEIGHTS])
```

```python
import functools

import jax
import jax.numpy as jnp
from jax import lax
from jax.experimental import pallas as pl
from jax.experimental.pallas import tpu as pltpu

F32 = jnp.float32
BF16 = jnp.bfloat16
NDEV = 8
VMEM_LIMIT = 56 * 1024 * 1024
NORM_EPS = 1e-6
ADAM_LR, ADAM_B1, ADAM_B2, ADAM_EPS, ADAM_WD, ADAM_STEP = 0.001, 0.9, 0.999, 1e-08, 0.01, 10
MESH = pl.DeviceIdType.MESH
ANY = pl.BlockSpec(memory_space=pl.ANY)


def _cp(*sem):
    return pltpu.CompilerParams(dimension_semantics=sem, vmem_limit_bytes=VMEM_LIMIT)


def _my_index():
    return 4 * lax.axis_index("x") + 2 * lax.axis_index("y") + lax.axis_index("c")


def _flip(k):
    x, y, c = lax.axis_index("x"), lax.axis_index("y"), lax.axis_index("c")
    kx, ky, kc = (k >> 2) & 1, (k >> 1) & 1, k & 1
    px = 1 - x if kx else x
    py = 1 - y if ky else y
    pc = 1 - c if kc else c
    return (px, py, pc), 4 * px + 2 * py + pc


def all_gather(arrs, name):
    n = len(arrs)

    def body(*refs):
        ins, outs = refs[:n], refs[n:2 * n]
        send_sems, recv_sems, local_sems = refs[2 * n:]
        me = _my_index()
        copies = []
        for a in range(n):
            loc = pltpu.make_async_copy(ins[a], outs[a].at[me], local_sems.at[a])
            loc.start()
            copies.append(loc)
            for k in range(1, NDEV):
                peer, _ = _flip(k)
                cp = pltpu.make_async_remote_copy(
                    src_ref=ins[a], dst_ref=outs[a].at[me],
                    send_sem=send_sems.at[a * (NDEV - 1) + k - 1], recv_sem=recv_sems.at[a * (NDEV - 1) + k - 1],
                    device_id=peer, device_id_type=MESH)
                cp.start()
                copies.append(cp)
        for cp in copies:
            cp.wait()

    return pl.pallas_call(
        body, name=name,
        out_shape=[jax.ShapeDtypeStruct((NDEV,) + a.shape, a.dtype) for a in arrs],
        in_specs=[ANY] * n, out_specs=[ANY] * n,
        scratch_shapes=[pltpu.SemaphoreType.DMA((n * (NDEV - 1),)), pltpu.SemaphoreType.DMA((n * (NDEV - 1),)),
                        pltpu.SemaphoreType.DMA((n,))],
    )(*arrs)


def all_to_all(arrs, name):
    n = len(arrs)

    def body(*refs):
        ins, outs = refs[:n], refs[n:2 * n]
        send_sems, recv_sems, local_sems = refs[2 * n:]
        me = _my_index()
        copies = []
        for a in range(n):
            loc = pltpu.make_async_copy(ins[a].at[me], outs[a].at[me], local_sems.at[a])
            loc.start()
            copies.append(loc)
            for k in range(1, NDEV):
                peer, pidx = _flip(k)
                cp = pltpu.make_async_remote_copy(
                    src_ref=ins[a].at[pidx], dst_ref=outs[a].at[me],
                    send_sem=send_sems.at[a * (NDEV - 1) + k - 1], recv_sem=recv_sems.at[a * (NDEV - 1) + k - 1],
                    device_id=peer, device_id_type=MESH)
                cp.start()
                copies.append(cp)
        for cp in copies:
            cp.wait()

    return pl.pallas_call(
        body, name=name,
        out_shape=[jax.ShapeDtypeStruct(a.shape, a.dtype) for a in arrs],
        in_specs=[ANY] * n, out_specs=[ANY] * n,
        scratch_shapes=[pltpu.SemaphoreType.DMA((n * (NDEV - 1),)), pltpu.SemaphoreType.DMA((n * (NDEV - 1),)),
                        pltpu.SemaphoreType.DMA((n,))],
    )(*arrs)


def _tile(n, pref):
    t = min(n, pref)
    while n % t:
        t //= 2
    return t


def adamw(land, w, m, v, name):
    slots, r, c = land.shape
    tr = _tile(r, 64)
    bc1 = 1.0 - ADAM_B1 ** ADAM_STEP
    bc2 = 1.0 - ADAM_B2 ** ADAM_STEP

    def body(land_ref, w_ref, m_ref, v_ref, g_ref, d_ref, nm_ref, nv_ref):
        g = land_ref[0]
        for s in range(1, slots):
            g = g + land_ref[s]
        nm = ADAM_B1 * m_ref[...] + (1.0 - ADAM_B1) * g
        nv = ADAM_B2 * v_ref[...] + (1.0 - ADAM_B2) * (g * g)
        m_hat = nm / bc1
        v_hat = nv / bc2
        g_ref[...] = g
        nm_ref[...] = nm
        nv_ref[...] = nv
        d_ref[...] = -ADAM_LR * (m_hat / (jnp.sqrt(v_hat) + ADAM_EPS) + ADAM_WD * w_ref[...])

    blk = pl.BlockSpec((tr, c), lambda i: (i, 0))
    return pl.pallas_call(
        body, name=name, grid=(r // tr,),
        in_specs=[pl.BlockSpec((slots, tr, c), lambda i: (0, i, 0)), blk, blk, blk],
        out_specs=[blk] * 4,
        out_shape=[jax.ShapeDtypeStruct((r, c), F32)] * 4,
        compiler_params=_cp("parallel"),
    )(land, w, m, v)


def sum_slots(land, d_model, name):
    slots, _, n = land.shape

    def body(land_ref, o_ref, loss_ref):
        g = land_ref[0]
        for s in range(1, slots):
            g = g + land_ref[s]
        o_ref[...] = g
        sq = jnp.sum(g[:, n - d_model:], axis=-1, keepdims=True)
        loss_ref[...] = jnp.broadcast_to(sq * (0.5 / d_model), (1, 128))

    return pl.pallas_call(
        body, name=name,
        out_shape=[jax.ShapeDtypeStruct((1, n), F32), jax.ShapeDtypeStruct((1, 128), F32)],
    )(land)


def _sigmoid(x):
    return 1.0 / (1.0 + jnp.exp(-x))


def adaln_fwd(c16, w_ada, b_ada, name):
    d3 = w_ada.shape[1]

    def body(c_ref, w_ref, b_ref, o_ref):
        cc = c_ref[...]
        sc = (cc * _sigmoid(cc)).astype(BF16)
        o_ref[...] = jnp.dot(sc, w_ref[...], preferred_element_type=F32) + b_ref[...]

    return pl.pallas_call(body, name=name, out_shape=jax.ShapeDtypeStruct((16, d3), F32),
                          compiler_params=pltpu.CompilerParams(vmem_limit_bytes=VMEM_LIMIT))(c16, w_ada, b_ada)


def adaln_bwd(c16, dshift16, p16, dgate16, scale16, g, name):
    d = c16.shape[1]
    nb = 3 * d // NDEV

    def body(c_ref, ds_ref, p_ref, dg_ref, sc_ref, g_ref, dw_ref, db_ref, dng_ref, dmod_scr):
        j = pl.program_id(0)

        @pl.when(j == 0)
        def _():
            p = p_ref[...]
            dmod = jnp.concatenate([ds_ref[...], p * g_ref[...], dg_ref[...]], axis=-1)
            dmod_scr[...] = dmod
            db_ref[...] = jnp.sum(dmod, axis=0, keepdims=True)
            dng_ref[...] = jnp.sum((1.0 + sc_ref[...]) * p, axis=0, keepdims=True)

        cc = c_ref[...]
        sc = (cc * _sigmoid(cc)).astype(BF16)
        start = pl.multiple_of(j * nb, 128)
        dm = dmod_scr[:, pl.ds(start, nb)].astype(BF16)
        dw_ref[0] = lax.dot_general(sc, dm, (((0,), (0,)), ((), ())), preferred_element_type=F32)

    full = lambda shape: pl.BlockSpec(shape, lambda j: (0,) * len(shape))
    return pl.pallas_call(
        body, name=name, grid=(NDEV,),
        in_specs=[full((16, d))] * 5 + [full((1, d))],
        out_specs=[pl.BlockSpec((1, d, nb), lambda j: (j, 0, 0)), full((1, 3 * d)), full((1, d))],
        out_shape=[jax.ShapeDtypeStruct((NDEV, d, nb), F32), jax.ShapeDtypeStruct((1, 3 * d), F32),
                   jax.ShapeDtypeStruct((1, d), F32)],
        scratch_shapes=[pltpu.VMEM((16, 3 * d), F32)],
        compiler_params=_cp("arbitrary"),
    )(c16, dshift16, p16, dgate16, scale16, g)


def _modulated_norm(x, g, scale, shift):
    rstd = lax.rsqrt(jnp.mean(x * x, axis=-1, keepdims=True) + NORM_EPS)
    return ((x * rstd) * g) * (1.0 + scale) + shift


def normmod_matmul(x, g, scale, shift, w, out_dtype, seq, emit_h, name):
    t, d = x.shape
    n = w.shape[1]
    tm = _tile(seq, 512)
    tn = 512 if n % 512 == 0 else n
    per_seq = seq // tm

    def body(x_ref, g_ref, sc_ref, sh_ref, w_ref, *rest):
        if emit_h:
            o_ref, h_ref, h_scr = rest
        else:
            o_ref, h_scr = rest

        @pl.when(pl.program_id(1) == 0)
        def _():
            h = _modulated_norm(x_ref[...], g_ref[...], sc_ref[0], sh_ref[0]).astype(BF16)
            h_scr[...] = h
            if emit_h:
                h_ref[...] = h

        o_ref[...] = jnp.dot(h_scr[...], w_ref[...], preferred_element_type=F32).astype(out_dtype)

    mod_spec = pl.BlockSpec((1, 1, d), lambda i, j: (i // per_seq, 0, 0))
    out_specs = [pl.BlockSpec((tm, tn), lambda i, j: (i, j))]
    out_shape = [jax.ShapeDtypeStruct((t, n), out_dtype)]
    if emit_h:
        out_specs.append(pl.BlockSpec((tm, d), lambda i, j: (i, 0)))
        out_shape.append(jax.ShapeDtypeStruct((t, d), BF16))
    return pl.pallas_call(
        body, name=name, grid=(t // tm, n // tn),
        in_specs=[pl.BlockSpec((tm, d), lambda i, j: (i, 0)), pl.BlockSpec((1, d), lambda i, j: (0, 0)),
                  mod_spec, mod_spec, pl.BlockSpec((d, tn), lambda i, j: (0, j))],
        out_specs=out_specs, out_shape=out_shape,
        scratch_shapes=[pltpu.VMEM((tm, d), BF16)],
        compiler_params=_cp("parallel", "arbitrary"),
    )(x, g, scale, shift, w)


def out_proj_fwd(o, zf, w_out, x, gate, seq, name):
    t, di = o.shape
    d = x.shape[1]
    tm = _tile(seq, 256)
    per_seq = seq // tm

    def body(o_ref, z_ref, w_ref, x_ref, gt_ref, xn_ref, y_ref, u_ref):
        z = z_ref[...]
        u = (o_ref[...] * (z * _sigmoid(z))).astype(BF16)
        y = jnp.dot(u, w_ref[...], preferred_element_type=F32)
        u_ref[...] = u
        y_ref[...] = y
        xn_ref[...] = x_ref[...] + gt_ref[0] * y

    row = lambda c: pl.BlockSpec((tm, c), lambda i: (i, 0))
    return pl.pallas_call(
        body, name=name, grid=(t // tm,),
        in_specs=[row(di), row(di), pl.BlockSpec((di, d), lambda i: (0, 0)), row(d),
                  pl.BlockSpec((1, 1, d), lambda i: (i // per_seq, 0, 0))],
        out_specs=[row(d), row(d), row(di)],
        out_shape=[jax.ShapeDtypeStruct((t, d), F32), jax.ShapeDtypeStruct((t, d), F32),
                   jax.ShapeDtypeStruct((t, di), BF16)],
        compiler_params=_cp("parallel"),
    )(o, zf, w_out, x, gate)


def out_proj_bwd(dxo, y, gate, w_out, o, zf, seq, name):
    t, d = dxo.shape
    di = o.shape[1]
    tm = _tile(seq, 256)
    per_seq = seq // tm

    def body(dx_ref, y_ref, gt_ref, w_ref, o_ref, z_ref, do_ref, dz_ref, dy_ref, dgt_ref):
        dx = dx_ref[...]
        part = jnp.sum(dx * y_ref[...], axis=0, keepdims=True)

        @pl.when(pl.program_id(0) % per_seq == 0)
        def _():
            dgt_ref[0] = part

        @pl.when(pl.program_id(0) % per_seq != 0)
        def _():
            dgt_ref[0] += part

        dy = (dx * gt_ref[0]).astype(BF16)
        dy_ref[...] = dy
        du = lax.dot_general(dy, w_ref[...], (((1,), (1,)), ((), ())), preferred_element_type=F32)
        z = z_ref[...]
        sg = _sigmoid(z)
        do_ref[...] = (du * (z * sg)).astype(BF16)
        dz_ref[...] = (du * o_ref[...] * (sg * (1.0 + z * (1.0 - sg)))).astype(BF16)

    row = lambda c: pl.BlockSpec((tm, c), lambda i: (i, 0))
    seq_spec = pl.BlockSpec((1, 1, d), lambda i: (i // per_seq, 0, 0))
    return pl.pallas_call(
        body, name=name, grid=(t // tm,),
        in_specs=[row(d), row(d), seq_spec, pl.BlockSpec((di, d), lambda i: (0, 0)), row(di), row(di)],
        out_specs=[row(di), row(di), row(d), seq_spec],
        out_shape=[jax.ShapeDtypeStruct((t, di), BF16), jax.ShapeDtypeStruct((t, di), BF16),
                   jax.ShapeDtypeStruct((t, d), BF16), jax.ShapeDtypeStruct((t // seq, 1, d), F32)],
        compiler_params=_cp("arbitrary"),
    )(dxo, y, gate, w_out, o, zf)


def tn_matmul(a, b, name):
    t, m = a.shape
    n = b.shape[1]
    tn = 512 if n % 512 == 0 else n
    tk = _tile(t, 512)

    def body(a_ref, b_ref, o_ref):
        part = lax.dot_general(a_ref[...], b_ref[...], (((0,), (0,)), ((), ())), preferred_element_type=F32)

        @pl.when(pl.program_id(1) == 0)
        def _():
            o_ref[...] = part

        @pl.when(pl.program_id(1) != 0)
        def _():
            o_ref[...] += part

    return pl.pallas_call(
        body, name=name, grid=(n // tn, t // tk),
        in_specs=[pl.BlockSpec((tk, m), lambda j, k: (k, 0)), pl.BlockSpec((tk, tn), lambda j, k: (k, j))],
        out_specs=pl.BlockSpec((m, tn), lambda j, k: (0, j)),
        out_shape=jax.ShapeDtypeStruct((m, n), F32),
        compiler_params=_cp("parallel", "arbitrary"),
    )(a, b)


def dh_norm_bwd(parts, w_t, extra, x, g, scale, dxo, seq, name):
    t, d = x.shape
    nparts = len(parts)
    kw = parts[0].shape[1]
    tm = _tile(seq, 256)
    per_seq = seq // tm
    has_extra = extra is not None

    def body(*refs):
        p_refs = refs[:nparts]
        w_ref = refs[nparts]
        pos = nparts + 1
        if has_extra:
            e_ref, we_ref = refs[pos], refs[pos + 1]
            pos += 2
        x_ref, g_ref, sc_ref, dxo_ref, dx_ref, pp_ref, ss_ref, acc = refs[pos:]
        i, k = pl.program_id(0), pl.program_id(1)

        for kk in range(nparts):
            @pl.when(k == kk)
            def _(kk=kk):
                part = jnp.dot(p_refs[kk][...], w_ref[...], preferred_element_type=F32)
                if kk == 0:
                    if has_extra:
                        part = part + jnp.dot(e_ref[...], we_ref[...], preferred_element_type=F32)
                    acc[...] = part
                else:
                    acc[...] += part

        @pl.when(k == nparts - 1)
        def _():
            dh = acc[...]
            xx = x_ref[...]
            rstd = lax.rsqrt(jnp.mean(xx * xx, axis=-1, keepdims=True) + NORM_EPS)
            xhat = xx * rstd
            dxhat = dh * (g_ref[...] * (1.0 + sc_ref[0]))
            dx_ref[...] = dxo_ref[...] + rstd * (dxhat - xhat * jnp.mean(dxhat * xhat, axis=-1, keepdims=True))
            pp = jnp.sum(dh * xhat, axis=0, keepdims=True)
            ss = jnp.sum(dh, axis=0, keepdims=True)

            @pl.when(i % per_seq == 0)
            def _():
                pp_ref[0] = pp
                ss_ref[0] = ss

            @pl.when(i % per_seq != 0)
            def _():
                pp_ref[0] += pp
                ss_ref[0] += ss

    row = lambda c: pl.BlockSpec((tm, c), lambda i, k: (i, 0))
    seq_spec = pl.BlockSpec((1, 1, d), lambda i, k: (i // per_seq, 0, 0))
    in_specs = [row(kw)] * nparts + [pl.BlockSpec((kw, d), lambda i, k: (k, 0))]
    args = list(parts) + [w_t]
    if has_extra:
        ke = extra[0].shape[1]
        in_specs += [row(ke), pl.BlockSpec((ke, d), lambda i, k: (0, 0))]
        args += list(extra)
    in_specs += [row(d), pl.BlockSpec((1, d), lambda i, k: (0, 0)), seq_spec, row(d)]
    args += [x, g, scale, dxo]
    return pl.pallas_call(
        body, name=name, grid=(t // tm, nparts),
        in_specs=in_specs,
        out_specs=[row(d), seq_spec, seq_spec],
        out_shape=[jax.ShapeDtypeStruct((t, d), F32), jax.ShapeDtypeStruct((t // seq, 1, d), F32),
                   jax.ShapeDtypeStruct((t // seq, 1, d), F32)],
        scratch_shapes=[pltpu.VMEM((tm, d), F32)],
        compiler_params=_cp("arbitrary", "arbitrary"),
    )(*args)


def final_loss(x, g, target, name):
    t, d = x.shape
    tm = _tile(t, 512)

    def body(x_ref, g_ref, t_ref, dx_ref, dg_ref, sq_ref):
        xx = x_ref[...]
        gg = g_ref[...]
        rstd = lax.rsqrt(jnp.mean(xx * xx, axis=-1, keepdims=True) + NORM_EPS)
        xhat = xx * rstd
        err = xhat * gg - t_ref[...]
        dy = err * (1.0 / d)
        dxhat = dy * gg
        dx_ref[...] = rstd * (dxhat - xhat * jnp.mean(dxhat * xhat, axis=-1, keepdims=True))
        dg = jnp.sum(dy * xhat, axis=0, keepdims=True)
        sq = jnp.sum(err * err, axis=0, keepdims=True)

        @pl.when(pl.program_id(0) == 0)
        def _():
            dg_ref[...] = dg
            sq_ref[...] = sq

        @pl.when(pl.program_id(0) != 0)
        def _():
            dg_ref[...] += dg
            sq_ref[...] += sq

    row = pl.BlockSpec((tm, d), lambda i: (i, 0))
    vec = pl.BlockSpec((1, d), lambda i: (0, 0))
    return pl.pallas_call(
        body, name=name, grid=(t // tm,),
        in_specs=[row, vec, row], out_specs=[row, vec, vec],
        out_shape=[jax.ShapeDtypeStruct((t, d), F32), jax.ShapeDtypeStruct((1, d), F32),
                   jax.ShapeDtypeStruct((1, d), F32)],
        compiler_params=_cp("arbitrary"),
    )(x, g, target)


def _split3(x):
    x1 = x.astype(BF16)
    r1 = x - x1.astype(F32)
    x2 = r1.astype(BF16)
    x3 = (r1 - x2.astype(F32)).astype(BF16)
    return x1, x2, x3


def _dot3(x, u):
    x1, x2, x3 = _split3(x)
    return (jnp.dot(x1, u, preferred_element_type=F32) + jnp.dot(x2, u, preferred_element_type=F32)
            + jnp.dot(x3, u, preferred_element_type=F32))


def _log_sigmoid(x):
    return jnp.minimum(x, 0.0) - jnp.log1p(jnp.exp(-jnp.abs(x)))


def forget_cumsum(fl_t, b_f, name):
    b, h, s = fl_t.shape

    def body(f_ref, b_ref, o_ref):
        lf = _log_sigmoid(f_ref[0] + b_ref[...])
        u = (lax.broadcasted_iota(jnp.int32, (s, s), 0) <= lax.broadcasted_iota(jnp.int32, (s, s), 1)).astype(BF16)
        o_ref[0] = _dot3(lf, u)

    return pl.pallas_call(
        body, name=name, grid=(b,),
        in_specs=[pl.BlockSpec((1, h, s), lambda i: (i, 0, 0)), pl.BlockSpec((h, 1), lambda i: (0, 0))],
        out_specs=pl.BlockSpec((1, h, s), lambda i: (i, 0, 0)),
        out_shape=jax.ShapeDtypeStruct((b, h, s), F32),
        compiler_params=_cp("parallel"),
    )(fl_t, b_f)


def forget_cumsum_bwd(dcum, fl_t, b_f, name):
    b, h, s = fl_t.shape

    def body(d_ref, f_ref, b_ref, o_ref, db_ref):
        u = (lax.broadcasted_iota(jnp.int32, (s, s), 0) >= lax.broadcasted_iota(jnp.int32, (s, s), 1)).astype(BF16)
        dlf = _dot3(d_ref[0], u)
        df = dlf * _sigmoid(-(f_ref[0] + b_ref[...]))
        o_ref[0] = df
        part = jnp.sum(df, axis=-1, keepdims=True)

        @pl.when(pl.program_id(0) == 0)
        def _():
            db_ref[...] = part

        @pl.when(pl.program_id(0) != 0)
        def _():
            db_ref[...] += part

    blk = pl.BlockSpec((1, h, s), lambda i: (i, 0, 0))
    return pl.pallas_call(
        body, name=name, grid=(b,),
        in_specs=[blk, blk, pl.BlockSpec((h, 1), lambda i: (0, 0))],
        out_specs=[blk, pl.BlockSpec((h, 1), lambda i: (0, 0))],
        out_shape=[jax.ShapeDtypeStruct((b, h, s), F32), jax.ShapeDtypeStruct((h, 1), F32)],
        compiler_params=_cp("arbitrary"),
    )(dcum, fl_t, b_f)


def _nt(a, b):
    return lax.dot_general(a, b, (((1,), (1,)), ((), ())), preferred_element_type=F32)


def _tn(a, b):
    return lax.dot_general(a, b, (((0,), (0,)), ((), ())), preferred_element_type=F32)


def _dot2(x, u):
    hi = x.astype(BF16)
    lo = (x - hi.astype(F32)).astype(BF16)
    return jnp.dot(hi, u, preferred_element_type=F32) + jnp.dot(lo, u, preferred_element_type=F32)


def _attn_specs(seq, di, tq):
    nq = seq // tq
    cb = di // 128
    q_spec = pl.BlockSpec((tq, 128), lambda b, hp, i: (b * nq + i, hp))
    k_spec = pl.BlockSpec((seq, 128), lambda b, hp, i: (b, cb + hp))
    v_spec = pl.BlockSpec((seq, 128), lambda b, hp, i: (b, 2 * cb + hp))
    kv_out = pl.BlockSpec((seq, 128), lambda b, hp, i: (b, hp))
    return q_spec, k_spec, v_spec, kv_out


def fox_fwd(qkv, ck, seq, di, dh, name):
    t = qkv.shape[0]
    nb = t // seq
    tq = _tile(seq, 256)
    nq = seq // tq
    scale = dh ** -0.5
    hpb = 128 // dh
    q_spec, k_spec, v_spec, _ = _attn_specs(seq, di, tq)

    def body(q_ref, k_ref, v_ref, ck_ref, o_ref, lse_ref):
        i = pl.program_id(2)
        row = lax.broadcasted_iota(jnp.int32, (tq, tq), 0)
        col = lax.broadcasted_iota(jnp.int32, (tq, tq), 1)
        causal = col <= row
        for hh in range(hpb):
            lanes = slice(hh * dh, (hh + 1) * dh)
            qh = q_ref[:, lanes]

            def block(j, masked):
                start = pl.multiple_of(j * tq, tq)
                kh = k_ref[pl.ds(start, tq), lanes]
                vh = v_ref[pl.ds(start, tq), lanes]
                s = _nt(qh, kh) * scale - ck_ref[0, hh, pl.ds(j, 1), :]
                if masked:
                    s = jnp.where(causal, s, -jnp.inf)
                return s, vh

            s, vh = block(i, True)
            m = jnp.max(s, axis=-1, keepdims=True)
            p = jnp.exp(s - m)
            l = jnp.sum(p, axis=-1, keepdims=True)
            acc = _dot2(p, vh)

            def step(jj, carry):
                m, l, acc = carry
                s, vh = block(i - 1 - jj, False)
                m_new = jnp.maximum(m, jnp.max(s, axis=-1, keepdims=True))
                alpha = jnp.exp(m - m_new)
                p = jnp.exp(s - m_new)
                l = alpha * l + jnp.sum(p, axis=-1, keepdims=True)
                acc = alpha * acc + _dot2(p, vh)
                return m_new, l, acc

            m, l, acc = lax.fori_loop(0, i, step, (m, l, acc))
            o_ref[:, lanes] = acc / l
            lse_ref[0, 0, :, hh:hh + 1] = m + jnp.log(l)

    return pl.pallas_call(
        body, name=name, grid=(nb, di // 128, nq),
        in_specs=[q_spec, k_spec, v_spec,
                  pl.BlockSpec((1, hpb, nq, tq), lambda b, hp, i: (b, hp, 0, 0))],
        out_specs=[q_spec, pl.BlockSpec((1, 1, tq, hpb), lambda b, hp, i: (b, hp, i, 0))],
        out_shape=[jax.ShapeDtypeStruct((t, di), F32), jax.ShapeDtypeStruct((nb, di // 128, seq, hpb), F32)],
        compiler_params=_cp("parallel", "parallel", "arbitrary"),
    )(qkv, qkv, qkv, ck)


def fox_bwd(qkv, ck, o, do, lse, seq, di, dh, name):
    t = qkv.shape[0]
    nb = t // seq
    tq = _tile(seq, 256)
    nq = seq // tq
    scale = dh ** -0.5
    hpb = 128 // dh
    q_spec, k_spec, v_spec, kv_out = _attn_specs(seq, di, tq)
    ck_spec = pl.BlockSpec((1, hpb, nq, tq), lambda b, hp, i: (b, hp, 0, 0))

    def body(q_ref, k_ref, v_ref, ck_ref, o_ref, do_ref, lse_ref, dq_ref, dk_ref, dv_ref, dck_ref, dk_scr, dv_scr):
        i = pl.program_id(2)

        @pl.when(i == 0)
        def _():
            dk_scr[...] = jnp.zeros_like(dk_scr)
            dv_scr[...] = jnp.zeros_like(dv_scr)
            dck_ref[...] = jnp.zeros_like(dck_ref)

        row = lax.broadcasted_iota(jnp.int32, (tq, tq), 0)
        col = lax.broadcasted_iota(jnp.int32, (tq, tq), 1)
        causal = col <= row
        for hh in range(hpb):
            lanes = slice(hh * dh, (hh + 1) * dh)
            qh = q_ref[:, lanes]
            doh = do_ref[:, lanes]
            delta = jnp.sum(doh.astype(F32) * o_ref[:, lanes], axis=-1, keepdims=True)
            lse = lse_ref[0, 0, :, hh:hh + 1]

            def block(j, masked, dq_acc):
                start = pl.multiple_of(j * tq, tq)
                kh = k_ref[pl.ds(start, tq), lanes]
                vh = v_ref[pl.ds(start, tq), lanes]
                s = _nt(qh, kh) * scale - ck_ref[0, hh, pl.ds(j, 1), :]
                p = jnp.exp(s - lse)
                if masked:
                    p = jnp.where(causal, p, 0.0)
                ds = p * (_nt(doh, vh) - delta)
                dsb = ds.astype(BF16)
                dk_scr[pl.ds(start, tq), lanes] += _tn(dsb, qh) * scale
                dv_scr[pl.ds(start, tq), lanes] += _tn(p.astype(BF16), doh)
                dck_ref[0, hh, pl.ds(j, 1), :] -= jnp.sum(ds, axis=0, keepdims=True)
                return dq_acc + jnp.dot(dsb, kh, preferred_element_type=F32)

            dq_acc = block(i, True, jnp.zeros((tq, dh), F32))
            dq_acc = lax.fori_loop(0, i, lambda jj, a: block(i - 1 - jj, False, a), dq_acc)
            dq_ref[:, lanes] = (dq_acc * scale).astype(BF16)

        @pl.when(i == nq - 1)
        def _():
            dk_ref[...] = dk_scr[...].astype(BF16)
            dv_ref[...] = dv_scr[...].astype(BF16)

    return pl.pallas_call(
        body, name=name, grid=(nb, di // 128, nq),
        in_specs=[q_spec, k_spec, v_spec, ck_spec, q_spec, q_spec,
                  pl.BlockSpec((1, 1, tq, hpb), lambda b, hp, i: (b, hp, i, 0))],
        out_specs=[q_spec, kv_out, kv_out, ck_spec],
        out_shape=[jax.ShapeDtypeStruct((t, di), BF16)] * 3 + [jax.ShapeDtypeStruct(ck.shape, F32)],
        scratch_shapes=[pltpu.VMEM((seq, 128), F32), pltpu.VMEM((seq, 128), F32)],
        compiler_params=_cp("parallel", "parallel", "arbitrary"),
    )(qkv, qkv, qkv, ck, o, do, lse)


def _sb_logits(qh, kh, scale, strict):
    z = _nt(qh, kh) * scale
    e = jnp.exp(-jnp.abs(z))
    lb = jnp.minimum(z, 0.0) - jnp.log1p(e)
    lk = lb - z
    if strict is not None:
        lk = jnp.where(strict, lk, 0.0)
    return z, e, lb, lk


def sb_fwd(qkv, seq, di, dh, name):
    t = qkv.shape[0]
    nb = t // seq
    tq = _tile(seq, 256)
    nq = seq // tq
    scale = dh ** -0.5
    hpb = 128 // dh
    q_spec, k_spec, v_spec, _ = _attn_specs(seq, di, tq)

    def body(q_ref, k_ref, v_ref, o_ref, rt_ref):
        i = pl.program_id(2)
        row = lax.broadcasted_iota(jnp.int32, (tq, tq), 0)
        col = lax.broadcasted_iota(jnp.int32, (tq, tq), 1)
        strict = col < row
        u_after = (row > col).astype(BF16)
        for hh in range(hpb):
            lanes = slice(hh * dh, (hh + 1) * dh)
            qh = q_ref[:, lanes]

            def block(j, mask, carry):
                r, acc = carry
                start = pl.multiple_of(j * tq, tq)
                kh = k_ref[pl.ds(start, tq), lanes]
                vh = v_ref[pl.ds(start, tq), lanes]
                _, _, lb, lk = _sb_logits(qh, kh, scale, mask)
                a = jnp.exp(lb + _dot2(lk, u_after) + r)
                if mask is not None:
                    a = jnp.where(mask, a, 0.0)
                acc = acc + jnp.dot(a.astype(BF16), vh, preferred_element_type=F32)
                return r + jnp.sum(lk, axis=-1, keepdims=True), acc

            carry = block(i, strict, (jnp.zeros((tq, 1), F32), jnp.zeros((tq, dh), F32)))
            carry = lax.fori_loop(0, i, lambda jj, cr: block(i - 1 - jj, None, cr), carry)
            o_ref[:, lanes] = carry[1]
            rt_ref[0, 0, :, hh:hh + 1] = carry[0]

    return pl.pallas_call(
        body, name=name, grid=(nb, di // 128, nq),
        in_specs=[q_spec, k_spec, v_spec],
        out_specs=[q_spec, pl.BlockSpec((1, 1, tq, hpb), lambda b, hp, i: (b, hp, i, 0))],
        out_shape=[jax.ShapeDtypeStruct((t, di), F32), jax.ShapeDtypeStruct((nb, di // 128, seq, hpb), F32)],
        compiler_params=_cp("parallel", "parallel", "arbitrary"),
    )(qkv, qkv, qkv)


def sb_bwd(qkv, do, rtot, seq, di, dh, name):
    t = qkv.shape[0]
    nb = t // seq
    tq = _tile(seq, 256)
    nq = seq // tq
    scale = dh ** -0.5
    hpb = 128 // dh
    q_spec, k_spec, v_spec, kv_out = _attn_specs(seq, di, tq)

    def body(q_ref, k_ref, v_ref, do_ref, rt_ref, dq_ref, dk_ref, dv_ref, dk_scr, dv_scr):
        i = pl.program_id(2)

        @pl.when(i == 0)
        def _():
            dk_scr[...] = jnp.zeros_like(dk_scr)
            dv_scr[...] = jnp.zeros_like(dv_scr)

        row = lax.broadcasted_iota(jnp.int32, (tq, tq), 0)
        col = lax.broadcasted_iota(jnp.int32, (tq, tq), 1)
        strict = col < row
        u_upto = (row <= col).astype(BF16)
        u_before = (row < col).astype(BF16)
        for hh in range(hpb):
            lanes = slice(hh * dh, (hh + 1) * dh)
            qh = q_ref[:, lanes]
            doh = do_ref[:, lanes]
            rt = rt_ref[0, 0, :, hh:hh + 1]

            def block(j, mask, carry):
                lc, gc, dq_acc = carry
                start = pl.multiple_of(j * tq, tq)
                kh = k_ref[pl.ds(start, tq), lanes]
                vh = v_ref[pl.ds(start, tq), lanes]
                z, e, lb, lk = _sb_logits(qh, kh, scale, mask)
                a = jnp.exp(lb + ((rt - lc) - _dot2(lk, u_upto)))
                if mask is not None:
                    a = jnp.where(mask, a, 0.0)
                de = a * _nt(doh, vh)
                g = gc + _dot2(de, u_before)
                rcp = 1.0 / (1.0 + e)
                small = e * rcp
                sig = jnp.where(z >= 0, rcp, small)
                dz = de * jnp.where(z >= 0, small, rcp) - g * sig
                if mask is not None:
                    dz = jnp.where(mask, dz, 0.0)
                dzb = dz.astype(BF16)
                dk_scr[pl.ds(start, tq), lanes] += _tn(dzb, qh) * scale
                dv_scr[pl.ds(start, tq), lanes] += _tn(a.astype(BF16), doh)
                return (lc + jnp.sum(lk, axis=-1, keepdims=True), gc + jnp.sum(de, axis=-1, keepdims=True),
                        dq_acc + jnp.dot(dzb, kh, preferred_element_type=F32))

            zero = jnp.zeros((tq, 1), F32)
            carry = lax.fori_loop(0, i, lambda j, cr: block(j, None, cr), (zero, zero, jnp.zeros((tq, dh), F32)))
            carry = block(i, strict, carry)
            dq_ref[:, lanes] = (carry[2] * scale).astype(BF16)

        @pl.when(i == nq - 1)
        def _():
            dk_ref[...] = dk_scr[...].astype(BF16)
            dv_ref[...] = dv_scr[...].astype(BF16)

    return pl.pallas_call(
        body, name=name, grid=(nb, di // 128, nq),
        in_specs=[q_spec, k_spec, v_spec, q_spec, pl.BlockSpec((1, 1, tq, hpb), lambda b, hp, i: (b, hp, i, 0))],
        out_specs=[q_spec, kv_out, kv_out],
        out_shape=[jax.ShapeDtypeStruct((t, di), BF16)] * 3,
        scratch_shapes=[pltpu.VMEM((seq, 128), F32), pltpu.VMEM((seq, 128), F32)],
        compiler_params=_cp("parallel", "parallel", "arbitrary"),
    )(qkv, qkv, qkv, do, rtot)


def _cols(g):
    return jnp.transpose(g, (1, 0, 2)).reshape(g.shape[1], NDEV * g.shape[2])


def _col_blocks(w):
    r, c8 = w.shape
    return jnp.transpose(w.reshape(r, NDEV, c8 // NDEV), (1, 0, 2))


def _pad_rows16(a):
    return jnp.pad(a, ((0, 16 - a.shape[0]), (0, 0)))


def _pad_cols(a, n):
    return jnp.pad(a, ((0, 0), (0, n - a.shape[1])))


def kernel(x, c, fox_norm_g, fox_w_ada, fox_b_ada, fox_w_in, fox_b_f, fox_w_out, sb_norm_g, sb_w_ada, sb_b_ada, sb_w_in, sb_w_out, final_norm_g, loss_target, m_fox_norm_g, m_fox_w_ada, m_fox_b_ada, m_fox_w_in, m_fox_b_f, m_fox_w_out, m_sb_norm_g, m_sb_w_ada, m_sb_b_ada, m_sb_w_in, m_sb_w_out, m_final_norm_g, v_fox_norm_g, v_fox_w_ada, v_fox_b_ada, v_fox_w_in, v_fox_b_f, v_fox_w_out, v_sb_norm_g, v_sb_w_ada, v_sb_b_ada, v_sb_w_in, v_sb_w_out, v_final_norm_g):
    nb, seq, d = x.shape
    t = nb * seq
    h = fox_b_f.shape[-1]
    di = fox_w_out.shape[1] * NDEV
    dh = di // h
    tq = _tile(seq, 256)
    me = _my_index()

    big = [w[0].astype(BF16) for w in (fox_w_ada, fox_w_in, fox_w_out, sb_w_ada, sb_w_in, sb_w_out)]
    gathered = all_gather(big + [sb_norm_g, sb_b_ada], "gather_weights")
    fox_wada, fox_win, sb_wada, sb_win = (_cols(gathered[k]) for k in (0, 1, 3, 4))
    fox_wout = gathered[2].reshape(di, d)
    sb_wout = gathered[5].reshape(di, d)
    sb_g = _cols(gathered[6])
    sb_bada = _cols(gathered[7])

    x0 = x.reshape(t, d)
    target = loss_target.reshape(t, d)
    c16 = _pad_rows16(c)

    def layer_fwd(xin, g, wada, bada, win, wout, b_f, tag):
        mod = adaln_fwd(c16, wada, bada, tag + "_adaln")[:nb]
        shift, scale, gate = (mod[:, k * d:(k + 1) * d].reshape(nb, 1, d) for k in range(3))
        qkv, hmod = normmod_matmul(xin, g, scale, shift, win[:, :3 * di], BF16, seq, True, tag + "_qkv")
        w_z = win[:, 3 * di:]
        if b_f is not None:
            w_z = _pad_cols(w_z, di + 128)
        (zf,) = normmod_matmul(xin, g, scale, shift, w_z, F32, seq, False, tag + "_z")
        saved = dict(x=xin, g=g, scale=scale, gate=gate, qkv=qkv, h=hmod, zf=zf, win=win, wout=wout)
        if b_f is not None:
            fl_t = jnp.transpose(zf[:, di:di + h].reshape(nb, seq, h), (0, 2, 1))
            bf_col = b_f.reshape(h, 1)
            cum = forget_cumsum(fl_t, bf_col, tag + "_cum")
            ck = cum.reshape(nb, h, seq // tq, tq)
            o, lse = fox_fwd(qkv, ck, seq, di, dh, tag + "_attn")
            saved.update(fl_t=fl_t, bf_col=bf_col, ck=ck, lse=lse)
        else:
            o, rtot = sb_fwd(qkv, seq, di, dh, tag + "_attn")
            saved.update(rtot=rtot)
        xout, y, u = out_proj_fwd(o, zf, wout, xin, gate, seq, tag + "_out")
        saved.update(o=o, y=y, u=u)
        return xout, saved

    def layer_bwd(dxo, sv, is_fox, tag):
        do, dz, dy, dgate = out_proj_bwd(dxo, sv["y"], sv["gate"], sv["wout"], sv["o"], sv["zf"], seq, tag + "_dout")
        dwout = tn_matmul(sv["u"], dy, tag + "_dwout")
        extra = None
        if is_fox:
            dq, dk, dv, dck = fox_bwd(sv["qkv"], sv["ck"], sv["o"], do, sv["lse"], seq, di, dh, tag + "_dattn")
            df_t, dbf = forget_cumsum_bwd(dck.reshape(nb, h, seq), sv["fl_t"], sv["bf_col"], tag + "_dcum")
            df = _pad_cols(jnp.transpose(df_t, (0, 2, 1)).reshape(t, h), 128).astype(BF16)
            extra = (df, jnp.transpose(_pad_cols(sv["win"][:, 4 * di:], 128)))
        else:
            dq, dk, dv = sb_bwd(sv["qkv"], do, sv["rtot"], seq, di, dh, tag + "_dattn")
        parts = [dq, dk, dv, dz]
        dwin = [tn_matmul(sv["h"], p, tag + "_dwin%d" % k) for k, p in enumerate(parts)]
        if is_fox:
            dwin.append(tn_matmul(sv["h"], df, tag + "_dwinf")[:, :h])
        dwin = jnp.concatenate(dwin, axis=1)
        w_t = jnp.transpose(sv["win"][:, :4 * di])
        dxin, psum, ssum = dh_norm_bwd(parts, w_t, extra, sv["x"], sv["g"], sv["scale"], dxo, seq, tag + "_dh")
        pad = lambda a: _pad_rows16(a.reshape(nb, d))
        dwada, dbada, dng = adaln_bwd(c16, pad(ssum), pad(psum), pad(dgate), pad(sv["scale"]), sv["g"], tag + "_dadaln")
        grads = dict(wada=dwada, bada=dbada, ng=dng, win=_col_blocks(dwin), wout=dwout.reshape(NDEV, di // NDEV, d))
        if is_fox:
            grads["bf"] = dbf.reshape(1, h)
        return dxin, grads

    x1, sv_fox = layer_fwd(x0, fox_norm_g, fox_wada, fox_b_ada, fox_win, fox_wout, fox_b_f, "fox")
    x2, sv_sb = layer_fwd(x1, sb_g, sb_wada, sb_bada, sb_win, sb_wout, None, "sb")
    dx2, dgf, sq = final_loss(x2, final_norm_g.reshape(1, d), target, "loss_head")
    dx1, g_sb = layer_bwd(dx2, sv_sb, False, "sb")
    dx0, g_fox = layer_bwd(dx1, sv_fox, True, "fox")

    land = all_to_all([g_fox["wada"], g_fox["win"], g_fox["wout"], g_sb["wada"], g_sb["win"], g_sb["wout"]],
                      "scatter_grads")
    small = jnp.concatenate([g_fox["ng"], g_fox["bada"], _pad_cols(g_fox["bf"], 128), g_sb["ng"], g_sb["bada"],
                             dgf, sq], axis=1)
    (small_all,) = all_gather([small], "gather_small")
    small_sum, loss_row = sum_slots(small_all, d, "sum_small")
    loss = loss_row[0, 0]

    offs = {}
    pos = 0
    for nm, width in (("fox_ng", d), ("fox_bada", 3 * d), ("fox_bf", 128), ("sb_ng", d), ("sb_bada", 3 * d), ("fin_g", d)):
        offs[nm] = (pos, width)
        pos += width

    def small_grad(nm, width=None, shard=False):
        p0, wd = offs[nm]
        wd = width or wd
        if shard:
            blk = wd // NDEV
            return lax.dynamic_slice(small_sum, (0, p0 + me * blk), (1, blk))
        return small_sum[:, p0:p0 + wd]

    results = {}
    big_params = [("fox_w_ada", fox_w_ada, m_fox_w_ada, v_fox_w_ada), ("fox_w_in", fox_w_in, m_fox_w_in, v_fox_w_in),
                  ("fox_w_out", fox_w_out, m_fox_w_out, v_fox_w_out), ("sb_w_ada", sb_w_ada, m_sb_w_ada, v_sb_w_ada),
                  ("sb_w_in", sb_w_in, m_sb_w_in, v_sb_w_in), ("sb_w_out", sb_w_out, m_sb_w_out, v_sb_w_out)]
    for k, (nm, w, m, v) in enumerate(big_params):
        outs = adamw(land[k], w[0], m[0], v[0], "adamw_" + nm)
        results[nm] = [o[None] for o in outs]
    small_params = [("fox_norm_g", fox_norm_g, m_fox_norm_g, v_fox_norm_g, small_grad("fox_ng")),
                    ("fox_b_ada", fox_b_ada, m_fox_b_ada, v_fox_b_ada, small_grad("fox_bada")),
                    ("fox_b_f", fox_b_f, m_fox_b_f, v_fox_b_f, small_grad("fox_bf", h)),
                    ("sb_norm_g", sb_norm_g, m_sb_norm_g, v_sb_norm_g, small_grad("sb_ng", shard=True)),
                    ("sb_b_ada", sb_b_ada, m_sb_b_ada, v_sb_b_ada, small_grad("sb_bada", shard=True)),
                    ("final_norm_g", final_norm_g.reshape(1, d), m_final_norm_g.reshape(1, d),
                     v_final_norm_g.reshape(1, d), small_grad("fin_g"))]
    for nm, w, m, v, g in small_params:
        outs = adamw(g[None], w, m, v, "adamw_" + nm)
        if nm == "final_norm_g":
            outs = [o.reshape(d) for o in outs]
        results[nm] = outs

    order = ["fox_norm_g", "fox_w_ada", "fox_b_ada", "fox_w_in", "fox_b_f", "fox_w_out", "sb_norm_g", "sb_w_ada",
             "sb_b_ada", "sb_w_in", "sb_w_out", "final_norm_g"]
    out = [loss, dx0.reshape(nb, seq, d)]
    for k in range(4):
        out += [results[nm][k] for nm in order]
    return tuple(out)
```

```python
import math

import jax
import jax.numpy as jnp
from jax import lax
from jax.experimental import pallas as pl
from jax.experimental.pallas import tpu as pltpu

F32 = jnp.float32
BF16 = jnp.bfloat16
NDEV = 8
VMEM_LIMIT = 56 * 1024 * 1024
NORM_EPS = 1e-6
ADAM_LR, ADAM_B1, ADAM_B2, ADAM_EPS, ADAM_WD, ADAM_STEP = 0.001, 0.9, 0.999, 1e-08, 0.01, 10
MESH = pl.DeviceIdType.MESH
ANY = pl.BlockSpec(memory_space=pl.ANY)


def _cp(*sem):
    return pltpu.CompilerParams(dimension_semantics=sem, vmem_limit_bytes=VMEM_LIMIT)


def _my_index():
    return 4 * lax.axis_index("x") + 2 * lax.axis_index("y") + lax.axis_index("c")


def _flip(k):
    x, y, c = lax.axis_index("x"), lax.axis_index("y"), lax.axis_index("c")
    kx, ky, kc = (k >> 2) & 1, (k >> 1) & 1, k & 1
    px = 1 - x if kx else x
    py = 1 - y if ky else y
    pc = 1 - c if kc else c
    return (px, py, pc), 4 * px + 2 * py + pc


def all_gather(arrs, name):
    n = len(arrs)

    def body(*refs):
        ins, outs = refs[:n], refs[n:2 * n]
        send_sems, recv_sems, local_sems = refs[2 * n:]
        me = _my_index()
        copies = []
        for a in range(n):
            loc = pltpu.make_async_copy(ins[a], outs[a].at[me], local_sems.at[a])
            loc.start()
            copies.append(loc)
            for k in range(1, NDEV):
                peer, _ = _flip(k)
                cp = pltpu.make_async_remote_copy(
                    src_ref=ins[a], dst_ref=outs[a].at[me],
                    send_sem=send_sems.at[a * (NDEV - 1) + k - 1], recv_sem=recv_sems.at[a * (NDEV - 1) + k - 1],
                    device_id=peer, device_id_type=MESH)
                cp.start()
                copies.append(cp)
        for cp in copies:
            cp.wait()

    return pl.pallas_call(
        body, name=name,
        out_shape=[jax.ShapeDtypeStruct((NDEV,) + a.shape, a.dtype) for a in arrs],
        in_specs=[ANY] * n, out_specs=[ANY] * n,
        scratch_shapes=[pltpu.SemaphoreType.DMA((n * (NDEV - 1),)), pltpu.SemaphoreType.DMA((n * (NDEV - 1),)),
                        pltpu.SemaphoreType.DMA((n,))],
    )(*arrs)


def all_to_all(arrs, name):
    n = len(arrs)

    def body(*refs):
        ins, outs = refs[:n], refs[n:2 * n]
        send_sems, recv_sems, local_sems = refs[2 * n:]
        me = _my_index()
        copies = []
        for a in range(n):
            loc = pltpu.make_async_copy(ins[a].at[me], outs[a].at[me], local_sems.at[a])
            loc.start()
            copies.append(loc)
            for k in range(1, NDEV):
                peer, pidx = _flip(k)
                cp = pltpu.make_async_remote_copy(
                    src_ref=ins[a].at[pidx], dst_ref=outs[a].at[me],
                    send_sem=send_sems.at[a * (NDEV - 1) + k - 1], recv_sem=recv_sems.at[a * (NDEV - 1) + k - 1],
                    device_id=peer, device_id_type=MESH)
                cp.start()
                copies.append(cp)
        for cp in copies:
            cp.wait()

    return pl.pallas_call(
        body, name=name,
        out_shape=[jax.ShapeDtypeStruct(a.shape, a.dtype) for a in arrs],
        in_specs=[ANY] * n, out_specs=[ANY] * n,
        scratch_shapes=[pltpu.SemaphoreType.DMA((n * (NDEV - 1),)), pltpu.SemaphoreType.DMA((n * (NDEV - 1),)),
                        pltpu.SemaphoreType.DMA((n,))],
    )(*arrs)


def _tile(n, pref):
    t = min(n, pref)
    while n % t:
        t //= 2
    return t


def adamw(land, w, m, v, name):
    slots, r, c = land.shape
    tr = _tile(r, 64)
    bc1 = 1.0 - ADAM_B1 ** ADAM_STEP
    bc2 = 1.0 - ADAM_B2 ** ADAM_STEP

    def body(land_ref, w_ref, m_ref, v_ref, g_ref, d_ref, nm_ref, nv_ref):
        g = land_ref[0]
        for s in range(1, slots):
            g = g + land_ref[s]
        nm = ADAM_B1 * m_ref[...] + (1.0 - ADAM_B1) * g
        nv = ADAM_B2 * v_ref[...] + (1.0 - ADAM_B2) * (g * g)
        m_hat = nm / bc1
        v_hat = nv / bc2
        g_ref[...] = g
        nm_ref[...] = nm
        nv_ref[...] = nv
        d_ref[...] = -ADAM_LR * (m_hat / (jnp.sqrt(v_hat) + ADAM_EPS) + ADAM_WD * w_ref[...])

    blk = pl.BlockSpec((tr, c), lambda i: (i, 0))
    return pl.pallas_call(
        body, name=name, grid=(r // tr,),
        in_specs=[pl.BlockSpec((slots, tr, c), lambda i: (0, i, 0)), blk, blk, blk],
        out_specs=[blk] * 4,
        out_shape=[jax.ShapeDtypeStruct((r, c), F32)] * 4,
        compiler_params=_cp("parallel"),
    )(land, w, m, v)


def sum_slots(land, d_model, name):
    slots, _, n = land.shape

    def body(land_ref, o_ref, loss_ref):
        g = land_ref[0]
        for s in range(1, slots):
            g = g + land_ref[s]
        o_ref[...] = g
        sq = jnp.sum(g[:, n - d_model:], axis=-1, keepdims=True)
        loss_ref[...] = jnp.broadcast_to(sq * (0.5 / d_model), (1, 128))

    return pl.pallas_call(
        body, name=name,
        out_shape=[jax.ShapeDtypeStruct((1, n), F32), jax.ShapeDtypeStruct((1, 128), F32)],
    )(land)


def _sigmoid(x):
    return 1.0 / (1.0 + jnp.exp(-x))


def adaln_fwd(c16, w_ada, b_ada, name):
    d3 = w_ada.shape[1]

    def body(c_ref, w_ref, b_ref, o_ref):
        cc = c_ref[...]
        sc = (cc * _sigmoid(cc)).astype(BF16)
        o_ref[...] = jnp.dot(sc, w_ref[...], preferred_element_type=F32) + b_ref[...]

    return pl.pallas_call(body, name=name, out_shape=jax.ShapeDtypeStruct((16, d3), F32),
                          compiler_params=pltpu.CompilerParams(vmem_limit_bytes=VMEM_LIMIT))(c16, w_ada, b_ada)


def adaln_bwd(c16, dshift16, p16, dgate16, scale16, g, name):
    d = c16.shape[1]
    nb = 3 * d // NDEV

    def body(c_ref, ds_ref, p_ref, dg_ref, sc_ref, g_ref, dw_ref, db_ref, dng_ref, dmod_scr):
        j = pl.program_id(0)

        @pl.when(j == 0)
        def _():
            p = p_ref[...]
            dmod = jnp.concatenate([ds_ref[...], p * g_ref[...], dg_ref[...]], axis=-1)
            dmod_scr[...] = dmod
            db_ref[...] = jnp.sum(dmod, axis=0, keepdims=True)
            dng_ref[...] = jnp.sum((1.0 + sc_ref[...]) * p, axis=0, keepdims=True)

        cc = c_ref[...]
        sc = (cc * _sigmoid(cc)).astype(BF16)
        start = pl.multiple_of(j * nb, 128)
        dm = dmod_scr[:, pl.ds(start, nb)].astype(BF16)
        dw_ref[0] = lax.dot_general(sc, dm, (((0,), (0,)), ((), ())), preferred_element_type=F32)

    full = lambda shape: pl.BlockSpec(shape, lambda j: (0,) * len(shape))
    return pl.pallas_call(
        body, name=name, grid=(NDEV,),
        in_specs=[full((16, d))] * 5 + [full((1, d))],
        out_specs=[pl.BlockSpec((1, d, nb), lambda j: (j, 0, 0)), full((1, 3 * d)), full((1, d))],
        out_shape=[jax.ShapeDtypeStruct((NDEV, d, nb), F32), jax.ShapeDtypeStruct((1, 3 * d), F32),
                   jax.ShapeDtypeStruct((1, d), F32)],
        scratch_shapes=[pltpu.VMEM((16, 3 * d), F32)],
        compiler_params=_cp("arbitrary"),
    )(c16, dshift16, p16, dgate16, scale16, g)


def _modulated_norm(x, g, scale, shift):
    rstd = lax.rsqrt(jnp.mean(x * x, axis=-1, keepdims=True) + NORM_EPS)
    return ((x * rstd) * g) * (1.0 + scale) + shift


def normmod_matmul(x, g, scale, shift, w, out_dtype, seq, emit_h, name):
    t, d = x.shape
    n = w.shape[1]
    tm = _tile(seq, 512)
    tn = 512 if n % 512 == 0 else n
    per_seq = seq // tm

    def body(x_ref, g_ref, sc_ref, sh_ref, w_ref, *rest):
        if emit_h:
            o_ref, h_ref, h_scr = rest
        else:
            o_ref, h_scr = rest

        @pl.when(pl.program_id(1) == 0)
        def _():
            h = _modulated_norm(x_ref[...], g_ref[...], sc_ref[0], sh_ref[0]).astype(BF16)
            h_scr[...] = h
            if emit_h:
                h_ref[...] = h

        o_ref[...] = jnp.dot(h_scr[...], w_ref[...], preferred_element_type=F32).astype(out_dtype)

    mod_spec = pl.BlockSpec((1, 1, d), lambda i, j: (i // per_seq, 0, 0))
    out_specs = [pl.BlockSpec((tm, tn), lambda i, j: (i, j))]
    out_shape = [jax.ShapeDtypeStruct((t, n), out_dtype)]
    if emit_h:
        out_specs.append(pl.BlockSpec((tm, d), lambda i, j: (i, 0)))
        out_shape.append(jax.ShapeDtypeStruct((t, d), BF16))
    return pl.pallas_call(
        body, name=name, grid=(t // tm, n // tn),
        in_specs=[pl.BlockSpec((tm, d), lambda i, j: (i, 0)), pl.BlockSpec((1, d), lambda i, j: (0, 0)),
                  mod_spec, mod_spec, pl.BlockSpec((d, tn), lambda i, j: (0, j))],
        out_specs=out_specs, out_shape=out_shape,
        scratch_shapes=[pltpu.VMEM((tm, d), BF16)],
        compiler_params=_cp("parallel", "arbitrary"),
    )(x, g, scale, shift, w)


def out_proj_fwd(o, zf, w_out, x, gate, seq, name):
    t, di = o.shape
    d = x.shape[1]
    tm = _tile(seq, 256)
    per_seq = seq // tm

    def body(o_ref, z_ref, w_ref, x_ref, gt_ref, xn_ref, y_ref, u_ref):
        z = z_ref[...]
        u = (o_ref[...] * (z * _sigmoid(z))).astype(BF16)
        y = jnp.dot(u, w_ref[...], preferred_element_type=F32)
        u_ref[...] = u
        y_ref[...] = y
        xn_ref[...] = x_ref[...] + gt_ref[0] * y

    row = lambda c: pl.BlockSpec((tm, c), lambda i: (i, 0))
    return pl.pallas_call(
        body, name=name, grid=(t // tm,),
        in_specs=[row(di), row(di), pl.BlockSpec((di, d), lambda i: (0, 0)), row(d),
                  pl.BlockSpec((1, 1, d), lambda i: (i // per_seq, 0, 0))],
        out_specs=[row(d), row(d), row(di)],
        out_shape=[jax.ShapeDtypeStruct((t, d), F32), jax.ShapeDtypeStruct((t, d), F32),
                   jax.ShapeDtypeStruct((t, di), BF16)],
        compiler_params=_cp("parallel"),
    )(o, zf, w_out, x, gate)


def out_proj_bwd(dxo, y, gate, w_out, o, zf, seq, name):
    t, d = dxo.shape
    di = o.shape[1]
    tm = _tile(seq, 256)
    per_seq = seq // tm

    def body(dx_ref, y_ref, gt_ref, w_ref, o_ref, z_ref, do_ref, dz_ref, dy_ref, dgt_ref):
        dx = dx_ref[...]
        part = jnp.sum(dx * y_ref[...], axis=0, keepdims=True)

        @pl.when(pl.program_id(0) % per_seq == 0)
        def _():
            dgt_ref[0] = part

        @pl.when(pl.program_id(0) % per_seq != 0)
        def _():
            dgt_ref[0] += part

        dy = (dx * gt_ref[0]).astype(BF16)
        dy_ref[...] = dy
        du = lax.dot_general(dy, w_ref[...], (((1,), (1,)), ((), ())), preferred_element_type=F32)
        z = z_ref[...]
        sg = _sigmoid(z)
        do_ref[...] = (du * (z * sg)).astype(BF16)
        dz_ref[...] = (du * o_ref[...] * (sg * (1.0 + z * (1.0 - sg)))).astype(BF16)

    row = lambda c: pl.BlockSpec((tm, c), lambda i: (i, 0))
    seq_spec = pl.BlockSpec((1, 1, d), lambda i: (i // per_seq, 0, 0))
    return pl.pallas_call(
        body, name=name, grid=(t // tm,),
        in_specs=[row(d), row(d), seq_spec, pl.BlockSpec((di, d), lambda i: (0, 0)), row(di), row(di)],
        out_specs=[row(di), row(di), row(d), seq_spec],
        out_shape=[jax.ShapeDtypeStruct((t, di), BF16), jax.ShapeDtypeStruct((t, di), BF16),
                   jax.ShapeDtypeStruct((t, d), BF16), jax.ShapeDtypeStruct((t // seq, 1, d), F32)],
        compiler_params=_cp("arbitrary"),
    )(dxo, y, gate, w_out, o, zf)


def tn_matmul(a, b, name):
    t, m = a.shape
    n = b.shape[1]
    tn = 512 if n % 512 == 0 else n
    tk = _tile(t, 512)

    def body(a_ref, b_ref, o_ref):
        part = lax.dot_general(a_ref[...], b_ref[...], (((0,), (0,)), ((), ())), preferred_element_type=F32)

        @pl.when(pl.program_id(1) == 0)
        def _():
            o_ref[...] = part

        @pl.when(pl.program_id(1) != 0)
        def _():
            o_ref[...] += part

    return pl.pallas_call(
        body, name=name, grid=(n // tn, t // tk),
        in_specs=[pl.BlockSpec((tk, m), lambda j, k: (k, 0)), pl.BlockSpec((tk, tn), lambda j, k: (k, j))],
        out_specs=pl.BlockSpec((m, tn), lambda j, k: (0, j)),
        out_shape=jax.ShapeDtypeStruct((m, n), F32),
        compiler_params=_cp("parallel", "arbitrary"),
    )(a, b)


def dh_norm_bwd(parts, w_t, extra, x, g, scale, dxo, seq, name):
    t, d = x.shape
    nparts = len(parts)
    kw = parts[0].shape[1]
    tm = _tile(seq, 256)
    per_seq = seq // tm
    has_extra = extra is not None

    def body(*refs):
        p_refs = refs[:nparts]
        w_ref = refs[nparts]
        pos = nparts + 1
        if has_extra:
            e_ref, we_ref = refs[pos], refs[pos + 1]
            pos += 2
        x_ref, g_ref, sc_ref, dxo_ref, dx_ref, pp_ref, ss_ref, acc = refs[pos:]
        i, k = pl.program_id(0), pl.program_id(1)

        for kk in range(nparts):
            @pl.when(k == kk)
            def _(kk=kk):
                part = jnp.dot(p_refs[kk][...], w_ref[...], preferred_element_type=F32)
                if kk == 0:
                    if has_extra:
                        part = part + jnp.dot(e_ref[...], we_ref[...], preferred_element_type=F32)
                    acc[...] = part
                else:
                    acc[...] += part

        @pl.when(k == nparts - 1)
        def _():
            dh = acc[...]
            xx = x_ref[...]
            rstd = lax.rsqrt(jnp.mean(xx * xx, axis=-1, keepdims=True) + NORM_EPS)
            xhat = xx * rstd
            dxhat = dh * (g_ref[...] * (1.0 + sc_ref[0]))
            dx_ref[...] = dxo_ref[...] + rstd * (dxhat - xhat * jnp.mean(dxhat * xhat, axis=-1, keepdims=True))
            pp = jnp.sum(dh * xhat, axis=0, keepdims=True)
            ss = jnp.sum(dh, axis=0, keepdims=True)

            @pl.when(i % per_seq == 0)
            def _():
                pp_ref[0] = pp
                ss_ref[0] = ss

            @pl.when(i % per_seq != 0)
            def _():
                pp_ref[0] += pp
                ss_ref[0] += ss

    row = lambda c: pl.BlockSpec((tm, c), lambda i, k: (i, 0))
    seq_spec = pl.BlockSpec((1, 1, d), lambda i, k: (i // per_seq, 0, 0))
    in_specs = [row(kw)] * nparts + [pl.BlockSpec((kw, d), lambda i, k: (k, 0))]
    args = list(parts) + [w_t]
    if has_extra:
        ke = extra[0].shape[1]
        in_specs += [row(ke), pl.BlockSpec((ke, d), lambda i, k: (0, 0))]
        args += list(extra)
    in_specs += [row(d), pl.BlockSpec((1, d), lambda i, k: (0, 0)), seq_spec, row(d)]
    args += [x, g, scale, dxo]
    return pl.pallas_call(
        body, name=name, grid=(t // tm, nparts),
        in_specs=in_specs,
        out_specs=[row(d), seq_spec, seq_spec],
        out_shape=[jax.ShapeDtypeStruct((t, d), F32), jax.ShapeDtypeStruct((t // seq, 1, d), F32),
                   jax.ShapeDtypeStruct((t // seq, 1, d), F32)],
        scratch_shapes=[pltpu.VMEM((tm, d), F32)],
        compiler_params=_cp("arbitrary", "arbitrary"),
    )(*args)


def final_loss(x, g, target, name):
    t, d = x.shape
    tm = _tile(t, 512)

    def body(x_ref, g_ref, t_ref, dx_ref, dg_ref, sq_ref):
        xx = x_ref[...]
        gg = g_ref[...]
        rstd = lax.rsqrt(jnp.mean(xx * xx, axis=-1, keepdims=True) + NORM_EPS)
        xhat = xx * rstd
        err = xhat * gg - t_ref[...]
        dy = err * (1.0 / d)
        dxhat = dy * gg
        dx_ref[...] = rstd * (dxhat - xhat * jnp.mean(dxhat * xhat, axis=-1, keepdims=True))
        dg = jnp.sum(dy * xhat, axis=0, keepdims=True)
        sq = jnp.sum(err * err, axis=0, keepdims=True)

        @pl.when(pl.program_id(0) == 0)
        def _():
            dg_ref[...] = dg
            sq_ref[...] = sq

        @pl.when(pl.program_id(0) != 0)
        def _():
            dg_ref[...] += dg
            sq_ref[...] += sq

    row = pl.BlockSpec((tm, d), lambda i: (i, 0))
    vec = pl.BlockSpec((1, d), lambda i: (0, 0))
    return pl.pallas_call(
        body, name=name, grid=(t // tm,),
        in_specs=[row, vec, row], out_specs=[row, vec, vec],
        out_shape=[jax.ShapeDtypeStruct((t, d), F32), jax.ShapeDtypeStruct((1, d), F32),
                   jax.ShapeDtypeStruct((1, d), F32)],
        compiler_params=_cp("arbitrary"),
    )(x, g, target)


def _split3(x):
    x1 = x.astype(BF16)
    r1 = x - x1.astype(F32)
    x2 = r1.astype(BF16)
    x3 = (r1 - x2.astype(F32)).astype(BF16)
    return x1, x2, x3


def _dot3(x, u):
    x1, x2, x3 = _split3(x)
    return (jnp.dot(x1, u, preferred_element_type=F32) + jnp.dot(x2, u, preferred_element_type=F32)
            + jnp.dot(x3, u, preferred_element_type=F32))


def _log_sigmoid(x):
    return jnp.minimum(x, 0.0) - jnp.log1p(jnp.exp(-jnp.abs(x)))


def forget_cumsum(fl_t, b_f, name):
    b, h, s = fl_t.shape

    def body(f_ref, b_ref, o_ref):
        lf = _log_sigmoid(f_ref[0] + b_ref[...])
        u = (lax.broadcasted_iota(jnp.int32, (s, s), 0) <= lax.broadcasted_iota(jnp.int32, (s, s), 1)).astype(BF16)
        o_ref[0] = _dot3(lf, u)

    return pl.pallas_call(
        body, name=name, grid=(b,),
        in_specs=[pl.BlockSpec((1, h, s), lambda i: (i, 0, 0)), pl.BlockSpec((h, 1), lambda i: (0, 0))],
        out_specs=pl.BlockSpec((1, h, s), lambda i: (i, 0, 0)),
        out_shape=jax.ShapeDtypeStruct((b, h, s), F32),
        compiler_params=_cp("parallel"),
    )(fl_t, b_f)


def forget_cumsum_bwd(dcum, fl_t, b_f, name):
    b, h, s = fl_t.shape

    def body(d_ref, f_ref, b_ref, o_ref, db_ref):
        u = (lax.broadcasted_iota(jnp.int32, (s, s), 0) >= lax.broadcasted_iota(jnp.int32, (s, s), 1)).astype(BF16)
        dlf = _dot3(d_ref[0], u)
        df = dlf * _sigmoid(-(f_ref[0] + b_ref[...]))
        o_ref[0] = df
        part = jnp.sum(df, axis=-1, keepdims=True)

        @pl.when(pl.program_id(0) == 0)
        def _():
            db_ref[...] = part

        @pl.when(pl.program_id(0) != 0)
        def _():
            db_ref[...] += part

    blk = pl.BlockSpec((1, h, s), lambda i: (i, 0, 0))
    return pl.pallas_call(
        body, name=name, grid=(b,),
        in_specs=[blk, blk, pl.BlockSpec((h, 1), lambda i: (0, 0))],
        out_specs=[blk, pl.BlockSpec((h, 1), lambda i: (0, 0))],
        out_shape=[jax.ShapeDtypeStruct((b, h, s), F32), jax.ShapeDtypeStruct((h, 1), F32)],
        compiler_params=_cp("arbitrary"),
    )(dcum, fl_t, b_f)


def _nt(a, b):
    return lax.dot_general(a, b, (((1,), (1,)), ((), ())), preferred_element_type=F32)


def _attn_dims(t, seq, dh):
    tq = _tile(seq, 512)
    tk = _tile(seq, 256)
    return t // seq, tq, tk, seq // tq, tq // tk, 128 // dh


def _scaled_q(q_ref, dh):
    scale = dh ** -0.5
    if math.log2(dh) % 2 == 0:
        return (q_ref[...].astype(F32) * scale).astype(BF16), None
    return q_ref[...], scale


def _diag_masks(tq, tk, r, strict):
    row = lax.broadcasted_iota(jnp.int32, (tq, tk), 0)
    col = lax.broadcasted_iota(jnp.int32, (tq, tk), 1)
    return [(col + d * tk < row) if strict else (col + d * tk <= row) for d in range(r)]


def _transposed(x):
    return jnp.transpose(x.astype(F32)).astype(BF16)


def _store_transposed(dst_ref, src_scr, tk):
    for jb in range(src_scr.shape[0]):
        dst_ref[jb * tk:(jb + 1) * tk, :] = jnp.transpose(src_scr[jb]).astype(BF16)


def _dot2(x, u):
    hi = x.astype(BF16)
    lo = (x - hi.astype(F32)).astype(BF16)
    return jnp.dot(hi, u, preferred_element_type=F32) + jnp.dot(lo, u, preferred_element_type=F32)


def _attn_specs(seq, di, tq):
    nq = seq // tq
    cb = di // 128
    q_spec = pl.BlockSpec((tq, 128), lambda b, hp, i: (b * nq + i, hp))
    k_spec = pl.BlockSpec((seq, 128), lambda b, hp, i: (b, cb + hp))
    v_spec = pl.BlockSpec((seq, 128), lambda b, hp, i: (b, 2 * cb + hp))
    kv_out = pl.BlockSpec((seq, 128), lambda b, hp, i: (b, hp))
    return q_spec, k_spec, v_spec, kv_out


def fox_fwd(qkv, ck, seq, di, dh, name):
    t = qkv.shape[0]
    nb, tq, tk, nq, r, hpb = _attn_dims(t, seq, dh)
    heads = [slice(hh * dh, (hh + 1) * dh) for hh in range(hpb)]
    q_spec, k_spec, v_spec, _ = _attn_specs(seq, di, tq)

    def body(q_ref, k_ref, v_ref, ck_ref, o_ref, lse_ref):
        i = pl.program_id(2)
        masks = _diag_masks(tq, tk, r, False)
        q, scale = _scaled_q(q_ref, dh)
        qs = [q[:, hd] for hd in heads]

        def scores(hh, j):
            start = pl.multiple_of(j * tk, tk)
            kh = k_ref[pl.ds(start, tk), heads[hh]]
            vh = v_ref[pl.ds(start, tk), heads[hh]]
            s = _nt(qs[hh], kh)
            if scale is not None:
                s = s * scale
            return s - ck_ref[0, hh, pl.ds(j, 1), :], vh

        def update(state, s, vh):
            m, l, acc = state
            m_new = jnp.maximum(m, jnp.max(s, axis=-1, keepdims=True))
            alpha = jnp.exp(m - m_new)
            p = jnp.exp(s - m_new)
            return m_new, alpha * l + jnp.sum(p, axis=-1, keepdims=True), alpha * acc + _dot2(p, vh)

        states = []
        for hh in range(hpb):
            s, vh = scores(hh, r * i)
            s = jnp.where(masks[0], s, -jnp.inf)
            m = jnp.max(s, axis=-1, keepdims=True)
            p = jnp.exp(s - m)
            states.append((m, jnp.sum(p, axis=-1, keepdims=True), _dot2(p, vh)))
        for d in range(1, r):
            for hh in range(hpb):
                s, vh = scores(hh, r * i + d)
                states[hh] = update(states[hh], jnp.where(masks[d], s, -jnp.inf), vh)

        def step(jj, states):
            return tuple(update(states[hh], *scores(hh, r * i - 1 - jj)) for hh in range(hpb))

        states = lax.fori_loop(0, r * i, step, tuple(states))
        for hh in range(hpb):
            m, l, acc = states[hh]
            o_ref[:, heads[hh]] = acc / l
            lse_ref[0, 0, :, hh:hh + 1] = m + jnp.log(l)

    return pl.pallas_call(
        body, name=name, grid=(nb, di // 128, nq),
        in_specs=[q_spec, k_spec, v_spec,
                  pl.BlockSpec((1, hpb, seq // tk, tk), lambda b, hp, i: (b, hp, 0, 0))],
        out_specs=[q_spec, pl.BlockSpec((1, 1, tq, hpb), lambda b, hp, i: (b, hp, i, 0))],
        out_shape=[jax.ShapeDtypeStruct((t, di), F32), jax.ShapeDtypeStruct((nb, di // 128, seq, hpb), F32)],
        compiler_params=_cp("parallel", "parallel", "arbitrary"),
    )(qkv, qkv, qkv, ck)


def fox_bwd(qkv, ck, o, do, lse, seq, di, dh, name):
    t = qkv.shape[0]
    nb, tq, tk, nq, r, hpb = _attn_dims(t, seq, dh)
    heads = [slice(hh * dh, (hh + 1) * dh) for hh in range(hpb)]
    q_spec, k_spec, v_spec, kv_out = _attn_specs(seq, di, tq)
    ck_spec = pl.BlockSpec((1, hpb, seq // tk, tk), lambda b, hp, i: (b, hp, 0, 0))

    def body(q_ref, k_ref, v_ref, ck_ref, o_ref, do_ref, lse_ref, dq_ref, dk_ref, dv_ref, dck_ref, dkt_scr, dvt_scr):
        i = pl.program_id(2)

        @pl.when(i == 0)
        def _():
            dkt_scr[...] = jnp.zeros_like(dkt_scr)
            dvt_scr[...] = jnp.zeros_like(dvt_scr)
            dck_ref[...] = jnp.zeros_like(dck_ref)

        masks = _diag_masks(tq, tk, r, False)
        q, scale = _scaled_q(q_ref, dh)
        do = do_ref[...]
        q_t, do_t = _transposed(q), _transposed(do)
        qs = [q[:, hd] for hd in heads]
        dos = [do[:, hd] for hd in heads]
        deltas = [jnp.sum(dos[hh].astype(F32) * o_ref[:, heads[hh]], axis=-1, keepdims=True) for hh in range(hpb)]
        lses = [lse_ref[0, 0, :, hh:hh + 1] for hh in range(hpb)]

        def block(hh, j, mask, dq_acc):
            start = pl.multiple_of(j * tk, tk)
            kh = k_ref[pl.ds(start, tk), heads[hh]]
            vh = v_ref[pl.ds(start, tk), heads[hh]]
            s = _nt(qs[hh], kh)
            if scale is not None:
                s = s * scale
            p = jnp.exp(s - ck_ref[0, hh, pl.ds(j, 1), :] - lses[hh])
            if mask is not None:
                p = jnp.where(mask, p, 0.0)
            ds = p * (_nt(dos[hh], vh) - deltas[hh])
            dsb = ds.astype(BF16)
            dkt = jnp.dot(q_t[heads[hh], :], dsb, preferred_element_type=F32)
            if scale is not None:
                dkt = dkt * scale
            dkt_scr[j, heads[hh], :] += dkt
            dvt_scr[j, heads[hh], :] += jnp.dot(do_t[heads[hh], :], p.astype(BF16), preferred_element_type=F32)
            dck_ref[0, hh, pl.ds(j, 1), :] -= jnp.sum(ds, axis=0, keepdims=True)
            return dq_acc + jnp.dot(dsb, kh, preferred_element_type=F32)

        accs = [jnp.zeros((tq, dh), F32)] * hpb
        for d in range(r):
            accs = [block(hh, r * i + d, masks[d], accs[hh]) for hh in range(hpb)]
        accs = lax.fori_loop(0, r * i, lambda j, a: tuple(block(hh, j, None, a[hh]) for hh in range(hpb)), tuple(accs))
        for hh in range(hpb):
            dq_ref[:, heads[hh]] = (accs[hh] * dh ** -0.5).astype(BF16)

        @pl.when(i == nq - 1)
        def _():
            _store_transposed(dk_ref, dkt_scr, tk)
            _store_transposed(dv_ref, dvt_scr, tk)

    return pl.pallas_call(
        body, name=name, grid=(nb, di // 128, nq),
        in_specs=[q_spec, k_spec, v_spec, ck_spec, q_spec, q_spec,
                  pl.BlockSpec((1, 1, tq, hpb), lambda b, hp, i: (b, hp, i, 0))],
        out_specs=[q_spec, kv_out, kv_out, ck_spec],
        out_shape=[jax.ShapeDtypeStruct((t, di), BF16)] * 3 + [jax.ShapeDtypeStruct(ck.shape, F32)],
        scratch_shapes=[pltpu.VMEM((seq // tk, 128, tk), F32), pltpu.VMEM((seq // tk, 128, tk), F32)],
        compiler_params=_cp("parallel", "parallel", "arbitrary"),
    )(qkv, qkv, qkv, ck, o, do, lse)


def _sb_logits(qh, kh, scale, strict):
    z = _nt(qh, kh)
    if scale is not None:
        z = z * scale
    e = jnp.exp(-jnp.abs(z))
    lb = jnp.minimum(z, 0.0) - jnp.log1p(e)
    lk = lb - z
    if strict is not None:
        lk = jnp.where(strict, lk, 0.0)
    return z, e, lb, lk


def sb_fwd(qkv, seq, di, dh, name):
    t = qkv.shape[0]
    nb, tq, tk, nq, r, hpb = _attn_dims(t, seq, dh)
    heads = [slice(hh * dh, (hh + 1) * dh) for hh in range(hpb)]
    q_spec, k_spec, v_spec, _ = _attn_specs(seq, di, tq)

    def body(q_ref, k_ref, v_ref, o_ref, rt_ref):
        i = pl.program_id(2)
        masks = _diag_masks(tq, tk, r, True)
        row = lax.broadcasted_iota(jnp.int32, (tk, tk), 0)
        col = lax.broadcasted_iota(jnp.int32, (tk, tk), 1)
        u_after = (row > col).astype(BF16)
        q, scale = _scaled_q(q_ref, dh)
        qs = [q[:, hd] for hd in heads]

        def block(hh, j, mask, state):
            rr, acc = state
            start = pl.multiple_of(j * tk, tk)
            kh = k_ref[pl.ds(start, tk), heads[hh]]
            vh = v_ref[pl.ds(start, tk), heads[hh]]
            _, _, lb, lk = _sb_logits(qs[hh], kh, scale, mask)
            a = jnp.exp(lb + _dot2(lk, u_after) + rr)
            if mask is not None:
                a = jnp.where(mask, a, 0.0)
            acc = acc + jnp.dot(a.astype(BF16), vh, preferred_element_type=F32)
            return rr + jnp.sum(lk, axis=-1, keepdims=True), acc

        states = [(jnp.zeros((tq, 1), F32), jnp.zeros((tq, dh), F32))] * hpb
        for d in reversed(range(r)):
            states = [block(hh, r * i + d, masks[d], states[hh]) for hh in range(hpb)]
        states = lax.fori_loop(
            0, r * i, lambda jj, st: tuple(block(hh, r * i - 1 - jj, None, st[hh]) for hh in range(hpb)), tuple(states))
        for hh in range(hpb):
            o_ref[:, heads[hh]] = states[hh][1]
            rt_ref[0, 0, :, hh:hh + 1] = states[hh][0]

    return pl.pallas_call(
        body, name=name, grid=(nb, di // 128, nq),
        in_specs=[q_spec, k_spec, v_spec],
        out_specs=[q_spec, pl.BlockSpec((1, 1, tq, hpb), lambda b, hp, i: (b, hp, i, 0))],
        out_shape=[jax.ShapeDtypeStruct((t, di), F32), jax.ShapeDtypeStruct((nb, di // 128, seq, hpb), F32)],
        compiler_params=_cp("parallel", "parallel", "arbitrary"),
    )(qkv, qkv, qkv)


def sb_bwd(qkv, do, rtot, seq, di, dh, name):
    t = qkv.shape[0]
    nb, tq, tk, nq, r, hpb = _attn_dims(t, seq, dh)
    heads = [slice(hh * dh, (hh + 1) * dh) for hh in range(hpb)]
    q_spec, k_spec, v_spec, kv_out = _attn_specs(seq, di, tq)

    def body(q_ref, k_ref, v_ref, do_ref, rt_ref, dq_ref, dk_ref, dv_ref, dkt_scr, dvt_scr):
        i = pl.program_id(2)

        @pl.when(i == 0)
        def _():
            dkt_scr[...] = jnp.zeros_like(dkt_scr)
            dvt_scr[...] = jnp.zeros_like(dvt_scr)

        masks = _diag_masks(tq, tk, r, True)
        row = lax.broadcasted_iota(jnp.int32, (tk, tk), 0)
        col = lax.broadcasted_iota(jnp.int32, (tk, tk), 1)
        u_upto = (row <= col).astype(BF16)
        u_before = (row < col).astype(BF16)
        q, scale = _scaled_q(q_ref, dh)
        do = do_ref[...]
        q_t, do_t = _transposed(q), _transposed(do)
        qs = [q[:, hd] for hd in heads]
        dos = [do[:, hd] for hd in heads]
        rts = [rt_ref[0, 0, :, hh:hh + 1] for hh in range(hpb)]

        def block(hh, j, mask, state):
            lc, gc, dq_acc = state
            start = pl.multiple_of(j * tk, tk)
            kh = k_ref[pl.ds(start, tk), heads[hh]]
            vh = v_ref[pl.ds(start, tk), heads[hh]]
            z, e, lb, lk = _sb_logits(qs[hh], kh, scale, mask)
            a = jnp.exp(lb + ((rts[hh] - lc) - _dot2(lk, u_upto)))
            if mask is not None:
                a = jnp.where(mask, a, 0.0)
            de = a * _nt(dos[hh], vh)
            g = gc + _dot2(de, u_before)
            rcp = 1.0 / (1.0 + e)
            small = e * rcp
            sig = jnp.where(z >= 0, rcp, small)
            dz = de * jnp.where(z >= 0, small, rcp) - g * sig
            if mask is not None:
                dz = jnp.where(mask, dz, 0.0)
            dzb = dz.astype(BF16)
            dkt = jnp.dot(q_t[heads[hh], :], dzb, preferred_element_type=F32)
            if scale is not None:
                dkt = dkt * scale
            dkt_scr[j, heads[hh], :] += dkt
            dvt_scr[j, heads[hh], :] += jnp.dot(do_t[heads[hh], :], a.astype(BF16), preferred_element_type=F32)
            return (lc + jnp.sum(lk, axis=-1, keepdims=True), gc + jnp.sum(de, axis=-1, keepdims=True),
                    dq_acc + jnp.dot(dzb, kh, preferred_element_type=F32))

        zero = jnp.zeros((tq, 1), F32)
        states = lax.fori_loop(0, r * i, lambda j, st: tuple(block(hh, j, None, st[hh]) for hh in range(hpb)),
                               ((zero, zero, jnp.zeros((tq, dh), F32)),) * hpb)
        for d in range(r):
            states = [block(hh, r * i + d, masks[d], states[hh]) for hh in range(hpb)]
        for hh in range(hpb):
            dq_ref[:, heads[hh]] = (states[hh][2] * dh ** -0.5).astype(BF16)

        @pl.when(i == nq - 1)
        def _():
            _store_transposed(dk_ref, dkt_scr, tk)
            _store_transposed(dv_ref, dvt_scr, tk)

    return pl.pallas_call(
        body, name=name, grid=(nb, di // 128, nq),
        in_specs=[q_spec, k_spec, v_spec, q_spec, pl.BlockSpec((1, 1, tq, hpb), lambda b, hp, i: (b, hp, i, 0))],
        out_specs=[q_spec, kv_out, kv_out],
        out_shape=[jax.ShapeDtypeStruct((t, di), BF16)] * 3,
        scratch_shapes=[pltpu.VMEM((seq // tk, 128, tk), F32), pltpu.VMEM((seq // tk, 128, tk), F32)],
        compiler_params=_cp("parallel", "parallel", "arbitrary"),
    )(qkv, qkv, qkv, do, rtot)


def _cols(g):
    return jnp.transpose(g, (1, 0, 2)).reshape(g.shape[1], NDEV * g.shape[2])


def _col_blocks(w):
    r, c8 = w.shape
    return jnp.transpose(w.reshape(r, NDEV, c8 // NDEV), (1, 0, 2))


def _pad_rows16(a):
    return jnp.pad(a, ((0, 16 - a.shape[0]), (0, 0)))


def _pad_cols(a, n):
    return jnp.pad(a, ((0, 0), (0, n - a.shape[1])))


def kernel(x, c, fox_norm_g, fox_w_ada, fox_b_ada, fox_w_in, fox_b_f, fox_w_out, sb_norm_g, sb_w_ada, sb_b_ada, sb_w_in, sb_w_out, final_norm_g, loss_target, m_fox_norm_g, m_fox_w_ada, m_fox_b_ada, m_fox_w_in, m_fox_b_f, m_fox_w_out, m_sb_norm_g, m_sb_w_ada, m_sb_b_ada, m_sb_w_in, m_sb_w_out, m_final_norm_g, v_fox_norm_g, v_fox_w_ada, v_fox_b_ada, v_fox_w_in, v_fox_b_f, v_fox_w_out, v_sb_norm_g, v_sb_w_ada, v_sb_b_ada, v_sb_w_in, v_sb_w_out, v_final_norm_g):
    nb, seq, d = x.shape
    t = nb * seq
    h = fox_b_f.shape[-1]
    di = fox_w_out.shape[1] * NDEV
    dh = di // h
    tq = _tile(seq, 256)
    me = _my_index()

    big = [w[0].astype(BF16) for w in (fox_w_ada, fox_w_in, fox_w_out, sb_w_ada, sb_w_in, sb_w_out)]
    gathered = all_gather(big + [sb_norm_g, sb_b_ada], "gather_weights")
    fox_wada, fox_win, sb_wada, sb_win = (_cols(gathered[k]) for k in (0, 1, 3, 4))
    fox_wout = gathered[2].reshape(di, d)
    sb_wout = gathered[5].reshape(di, d)
    sb_g = _cols(gathered[6])
    sb_bada = _cols(gathered[7])

    x0 = x.reshape(t, d)
    target = loss_target.reshape(t, d)
    c16 = _pad_rows16(c)

    def layer_fwd(xin, g, wada, bada, win, wout, b_f, tag):
        mod = adaln_fwd(c16, wada, bada, tag + "_adaln")[:nb]
        shift, scale, gate = (mod[:, k * d:(k + 1) * d].reshape(nb, 1, d) for k in range(3))
        qkv, hmod = normmod_matmul(xin, g, scale, shift, win[:, :3 * di], BF16, seq, True, tag + "_qkv")
        w_z = win[:, 3 * di:]
        if b_f is not None:
            w_z = _pad_cols(w_z, di + 128)
        (zf,) = normmod_matmul(xin, g, scale, shift, w_z, F32, seq, False, tag + "_z")
        saved = dict(x=xin, g=g, scale=scale, gate=gate, qkv=qkv, h=hmod, zf=zf, win=win, wout=wout)
        if b_f is not None:
            fl_t = jnp.transpose(zf[:, di:di + h].reshape(nb, seq, h), (0, 2, 1))
            bf_col = b_f.reshape(h, 1)
            cum = forget_cumsum(fl_t, bf_col, tag + "_cum")
            ck = cum.reshape(nb, h, seq // tq, tq)
            o, lse = fox_fwd(qkv, ck, seq, di, dh, tag + "_attn")
            saved.update(fl_t=fl_t, bf_col=bf_col, ck=ck, lse=lse)
        else:
            o, rtot = sb_fwd(qkv, seq, di, dh, tag + "_attn")
            saved.update(rtot=rtot)
        xout, y, u = out_proj_fwd(o, zf, wout, xin, gate, seq, tag + "_out")
        saved.update(o=o, y=y, u=u)
        return xout, saved

    def layer_bwd(dxo, sv, is_fox, tag):
        do, dz, dy, dgate = out_proj_bwd(dxo, sv["y"], sv["gate"], sv["wout"], sv["o"], sv["zf"], seq, tag + "_dout")
        dwout = tn_matmul(sv["u"], dy, tag + "_dwout")
        extra = None
        if is_fox:
            dq, dk, dv, dck = fox_bwd(sv["qkv"], sv["ck"], sv["o"], do, sv["lse"], seq, di, dh, tag + "_dattn")
            df_t, dbf = forget_cumsum_bwd(dck.reshape(nb, h, seq), sv["fl_t"], sv["bf_col"], tag + "_dcum")
            df = _pad_cols(jnp.transpose(df_t, (0, 2, 1)).reshape(t, h), 128).astype(BF16)
            extra = (df, jnp.transpose(_pad_cols(sv["win"][:, 4 * di:], 128)))
        else:
            dq, dk, dv = sb_bwd(sv["qkv"], do, sv["rtot"], seq, di, dh, tag + "_dattn")
        parts = [dq, dk, dv, dz]
        dwin = [tn_matmul(sv["h"], p, tag + "_dwin%d" % k) for k, p in enumerate(parts)]
        if is_fox:
            dwin.append(tn_matmul(sv["h"], df, tag + "_dwinf")[:, :h])
        dwin = jnp.concatenate(dwin, axis=1)
        w_t = jnp.transpose(sv["win"][:, :4 * di])
        dxin, psum, ssum = dh_norm_bwd(parts, w_t, extra, sv["x"], sv["g"], sv["scale"], dxo, seq, tag + "_dh")
        pad = lambda a: _pad_rows16(a.reshape(nb, d))
        dwada, dbada, dng = adaln_bwd(c16, pad(ssum), pad(psum), pad(dgate), pad(sv["scale"]), sv["g"], tag + "_dadaln")
        grads = dict(wada=dwada, bada=dbada, ng=dng, win=_col_blocks(dwin), wout=dwout.reshape(NDEV, di // NDEV, d))
        if is_fox:
            grads["bf"] = dbf.reshape(1, h)
        return dxin, grads

    x1, sv_fox = layer_fwd(x0, fox_norm_g, fox_wada, fox_b_ada, fox_win, fox_wout, fox_b_f, "fox")
    x2, sv_sb = layer_fwd(x1, sb_g, sb_wada, sb_bada, sb_win, sb_wout, None, "sb")
    dx2, dgf, sq = final_loss(x2, final_norm_g.reshape(1, d), target, "loss_head")
    dx1, g_sb = layer_bwd(dx2, sv_sb, False, "sb")
    dx0, g_fox = layer_bwd(dx1, sv_fox, True, "fox")

    land = all_to_all([g_fox["wada"], g_fox["win"], g_fox["wout"], g_sb["wada"], g_sb["win"], g_sb["wout"]],
                      "scatter_grads")
    small = jnp.concatenate([g_fox["ng"], g_fox["bada"], _pad_cols(g_fox["bf"], 128), g_sb["ng"], g_sb["bada"],
                             dgf, sq], axis=1)
    (small_all,) = all_gather([small], "gather_small")
    small_sum, loss_row = sum_slots(small_all, d, "sum_small")
    loss = loss_row[0, 0]

    offs = {}
    pos = 0
    for nm, width in (("fox_ng", d), ("fox_bada", 3 * d), ("fox_bf", 128), ("sb_ng", d), ("sb_bada", 3 * d), ("fin_g", d)):
        offs[nm] = (pos, width)
        pos += width

    def small_grad(nm, width=None, shard=False):
        p0, wd = offs[nm]
        wd = width or wd
        if shard:
            blk = wd // NDEV
            return lax.dynamic_slice(small_sum, (0, p0 + me * blk), (1, blk))
        return small_sum[:, p0:p0 + wd]

    results = {}
    big_params = [("fox_w_ada", fox_w_ada, m_fox_w_ada, v_fox_w_ada), ("fox_w_in", fox_w_in, m_fox_w_in, v_fox_w_in),
                  ("fox_w_out", fox_w_out, m_fox_w_out, v_fox_w_out), ("sb_w_ada", sb_w_ada, m_sb_w_ada, v_sb_w_ada),
                  ("sb_w_in", sb_w_in, m_sb_w_in, v_sb_w_in), ("sb_w_out", sb_w_out, m_sb_w_out, v_sb_w_out)]
    for k, (nm, w, m, v) in enumerate(big_params):
        outs = adamw(land[k], w[0], m[0], v[0], "adamw_" + nm)
        results[nm] = [o[None] for o in outs]
    small_params = [("fox_norm_g", fox_norm_g, m_fox_norm_g, v_fox_norm_g, small_grad("fox_ng")),
                    ("fox_b_ada", fox_b_ada, m_fox_b_ada, v_fox_b_ada, small_grad("fox_bada")),
                    ("fox_b_f", fox_b_f, m_fox_b_f, v_fox_b_f, small_grad("fox_bf", h)),
                    ("sb_norm_g", sb_norm_g, m_sb_norm_g, v_sb_norm_g, small_grad("sb_ng", shard=True)),
                    ("sb_b_ada", sb_b_ada, m_sb_b_ada, v_sb_b_ada, small_grad("sb_bada", shard=True)),
                    ("final_norm_g", final_norm_g.reshape(1, d), m_final_norm_g.reshape(1, d),
                     v_final_norm_g.reshape(1, d), small_grad("fin_g"))]
    for nm, w, m, v, g in small_params:
        outs = adamw(g[None], w, m, v, "adamw_" + nm)
        if nm == "final_norm_g":
            outs = [o.reshape(d) for o in outs]
        results[nm] = outs

    order = ["fox_norm_g", "fox_w_ada", "fox_b_ada", "fox_w_in", "fox_b_f", "fox_w_out", "sb_norm_g", "sb_w_ada",
             "sb_b_ada", "sb_w_in", "sb_w_out", "final_norm_g"]
    out = [loss, dx0.reshape(nb, seq, d)]
    for k in range(4):
        out += [results[nm][k] for nm in order]
    return tuple(out)
```

```python
import math

import jax
import jax.numpy as jnp
from jax import lax
from jax.experimental import pallas as pl
from jax.experimental.pallas import tpu as pltpu

F32 = jnp.float32
BF16 = jnp.bfloat16
NDEV = 8
VMEM_LIMIT = 56 * 1024 * 1024
NORM_EPS = 1e-6
ADAM_LR, ADAM_B1, ADAM_B2, ADAM_EPS, ADAM_WD, ADAM_STEP = 0.001, 0.9, 0.999, 1e-08, 0.01, 10
MESH = pl.DeviceIdType.MESH
ANY = pl.BlockSpec(memory_space=pl.ANY)


def _cp(*sem):
    return pltpu.CompilerParams(dimension_semantics=sem, vmem_limit_bytes=VMEM_LIMIT)


def _my_index():
    return 4 * lax.axis_index("x") + 2 * lax.axis_index("y") + lax.axis_index("c")


def _flip(k):
    x, y, c = lax.axis_index("x"), lax.axis_index("y"), lax.axis_index("c")
    kx, ky, kc = (k >> 2) & 1, (k >> 1) & 1, k & 1
    px = 1 - x if kx else x
    py = 1 - y if ky else y
    pc = 1 - c if kc else c
    return (px, py, pc), 4 * px + 2 * py + pc


def all_gather(arrs, name):
    n = len(arrs)

    def body(*refs):
        ins, outs = refs[:n], refs[n:2 * n]
        send_sems, recv_sems, local_sems = refs[2 * n:]
        me = _my_index()
        copies = []
        for a in range(n):
            loc = pltpu.make_async_copy(ins[a], outs[a].at[me], local_sems.at[a])
            loc.start()
            copies.append(loc)
            for k in range(1, NDEV):
                peer, _ = _flip(k)
                cp = pltpu.make_async_remote_copy(
                    src_ref=ins[a], dst_ref=outs[a].at[me],
                    send_sem=send_sems.at[a * (NDEV - 1) + k - 1], recv_sem=recv_sems.at[a * (NDEV - 1) + k - 1],
                    device_id=peer, device_id_type=MESH)
                cp.start()
                copies.append(cp)
        for cp in copies:
            cp.wait()

    return pl.pallas_call(
        body, name=name,
        out_shape=[jax.ShapeDtypeStruct((NDEV,) + a.shape, a.dtype) for a in arrs],
        in_specs=[ANY] * n, out_specs=[ANY] * n,
        scratch_shapes=[pltpu.SemaphoreType.DMA((n * (NDEV - 1),)), pltpu.SemaphoreType.DMA((n * (NDEV - 1),)),
                        pltpu.SemaphoreType.DMA((n,))],
    )(*arrs)


def all_to_all(arrs, name):
    n = len(arrs)

    def body(*refs):
        ins, outs = refs[:n], refs[n:2 * n]
        send_sems, recv_sems, local_sems = refs[2 * n:]
        me = _my_index()
        copies = []
        for a in range(n):
            loc = pltpu.make_async_copy(ins[a].at[me], outs[a].at[me], local_sems.at[a])
            loc.start()
            copies.append(loc)
            for k in range(1, NDEV):
                peer, pidx = _flip(k)
                cp = pltpu.make_async_remote_copy(
                    src_ref=ins[a].at[pidx], dst_ref=outs[a].at[me],
                    send_sem=send_sems.at[a * (NDEV - 1) + k - 1], recv_sem=recv_sems.at[a * (NDEV - 1) + k - 1],
                    device_id=peer, device_id_type=MESH)
                cp.start()
                copies.append(cp)
        for cp in copies:
            cp.wait()

    return pl.pallas_call(
        body, name=name,
        out_shape=[jax.ShapeDtypeStruct(a.shape, a.dtype) for a in arrs],
        in_specs=[ANY] * n, out_specs=[ANY] * n,
        scratch_shapes=[pltpu.SemaphoreType.DMA((n * (NDEV - 1),)), pltpu.SemaphoreType.DMA((n * (NDEV - 1),)),
                        pltpu.SemaphoreType.DMA((n,))],
    )(*arrs)


def _tile(n, pref):
    t = min(n, pref)
    while n % t:
        t //= 2
    return t


def adamw(land, w, m, v, name):
    slots, r, c = land.shape
    tr = _tile(r, 64)
    bc1 = 1.0 - ADAM_B1 ** ADAM_STEP
    bc2 = 1.0 - ADAM_B2 ** ADAM_STEP

    def body(land_ref, w_ref, m_ref, v_ref, g_ref, d_ref, nm_ref, nv_ref):
        g = land_ref[0]
        for s in range(1, slots):
            g = g + land_ref[s]
        nm = ADAM_B1 * m_ref[...] + (1.0 - ADAM_B1) * g
        nv = ADAM_B2 * v_ref[...] + (1.0 - ADAM_B2) * (g * g)
        m_hat = nm / bc1
        v_hat = nv / bc2
        g_ref[...] = g
        nm_ref[...] = nm
        nv_ref[...] = nv
        d_ref[...] = -ADAM_LR * (m_hat / (jnp.sqrt(v_hat) + ADAM_EPS) + ADAM_WD * w_ref[...])

    blk = pl.BlockSpec((tr, c), lambda i: (i, 0))
    return pl.pallas_call(
        body, name=name, grid=(r // tr,),
        in_specs=[pl.BlockSpec((slots, tr, c), lambda i: (0, i, 0)), blk, blk, blk],
        out_specs=[blk] * 4,
        out_shape=[jax.ShapeDtypeStruct((r, c), F32)] * 4,
        compiler_params=_cp("parallel"),
    )(land, w, m, v)


def sum_slots(land, d_model, name):
    slots, _, n = land.shape

    def body(land_ref, o_ref, loss_ref):
        g = land_ref[0]
        for s in range(1, slots):
            g = g + land_ref[s]
        o_ref[...] = g
        sq = jnp.sum(g[:, n - d_model:], axis=-1, keepdims=True)
        loss_ref[...] = jnp.broadcast_to(sq * (0.5 / d_model), (1, 128))

    return pl.pallas_call(
        body, name=name,
        out_shape=[jax.ShapeDtypeStruct((1, n), F32), jax.ShapeDtypeStruct((1, 128), F32)],
    )(land)


def _sigmoid(x):
    return 1.0 / (1.0 + jnp.exp(-x))


def adaln_fwd(c16, w_ada, b_ada, name):
    d3 = w_ada.shape[1]

    def body(c_ref, w_ref, b_ref, o_ref):
        cc = c_ref[...]
        sc = (cc * _sigmoid(cc)).astype(BF16)
        o_ref[...] = jnp.dot(sc, w_ref[...], preferred_element_type=F32) + b_ref[...]

    return pl.pallas_call(body, name=name, out_shape=jax.ShapeDtypeStruct((16, d3), F32),
                          compiler_params=pltpu.CompilerParams(vmem_limit_bytes=VMEM_LIMIT))(c16, w_ada, b_ada)


def adaln_bwd(c16, dshift16, p16, dgate16, scale16, g, name):
    d = c16.shape[1]
    nb = 3 * d // NDEV

    def body(c_ref, ds_ref, p_ref, dg_ref, sc_ref, g_ref, dw_ref, db_ref, dng_ref, dmod_scr):
        j = pl.program_id(0)

        @pl.when(j == 0)
        def _():
            p = p_ref[...]
            dmod = jnp.concatenate([ds_ref[...], p * g_ref[...], dg_ref[...]], axis=-1)
            dmod_scr[...] = dmod
            db_ref[...] = jnp.sum(dmod, axis=0, keepdims=True)
            dng_ref[...] = jnp.sum((1.0 + sc_ref[...]) * p, axis=0, keepdims=True)

        cc = c_ref[...]
        sc = (cc * _sigmoid(cc)).astype(BF16)
        start = pl.multiple_of(j * nb, 128)
        dm = dmod_scr[:, pl.ds(start, nb)].astype(BF16)
        dw_ref[0] = lax.dot_general(sc, dm, (((0,), (0,)), ((), ())), preferred_element_type=F32)

    full = lambda shape: pl.BlockSpec(shape, lambda j: (0,) * len(shape))
    return pl.pallas_call(
        body, name=name, grid=(NDEV,),
        in_specs=[full((16, d))] * 5 + [full((1, d))],
        out_specs=[pl.BlockSpec((1, d, nb), lambda j: (j, 0, 0)), full((1, 3 * d)), full((1, d))],
        out_shape=[jax.ShapeDtypeStruct((NDEV, d, nb), F32), jax.ShapeDtypeStruct((1, 3 * d), F32),
                   jax.ShapeDtypeStruct((1, d), F32)],
        scratch_shapes=[pltpu.VMEM((16, 3 * d), F32)],
        compiler_params=_cp("arbitrary"),
    )(c16, dshift16, p16, dgate16, scale16, g)


def _modulated_norm(x, g, scale, shift):
    rstd = lax.rsqrt(jnp.mean(x * x, axis=-1, keepdims=True) + NORM_EPS)
    return ((x * rstd) * g) * (1.0 + scale) + shift


def normmod_matmul(x, g, scale, shift, w, out_dtype, seq, emit_h, name):
    t, d = x.shape
    n = w.shape[1]
    tm = _tile(seq, 512)
    tn = 512 if n % 512 == 0 else n
    per_seq = seq // tm

    def body(x_ref, g_ref, sc_ref, sh_ref, w_ref, *rest):
        if emit_h:
            o_ref, h_ref, h_scr = rest
        else:
            o_ref, h_scr = rest

        @pl.when(pl.program_id(1) == 0)
        def _():
            h = _modulated_norm(x_ref[...], g_ref[...], sc_ref[0], sh_ref[0]).astype(BF16)
            h_scr[...] = h
            if emit_h:
                h_ref[...] = h

        o_ref[...] = jnp.dot(h_scr[...], w_ref[...], preferred_element_type=F32).astype(out_dtype)

    mod_spec = pl.BlockSpec((1, 1, d), lambda i, j: (i // per_seq, 0, 0))
    out_specs = [pl.BlockSpec((tm, tn), lambda i, j: (i, j))]
    out_shape = [jax.ShapeDtypeStruct((t, n), out_dtype)]
    if emit_h:
        out_specs.append(pl.BlockSpec((tm, d), lambda i, j: (i, 0)))
        out_shape.append(jax.ShapeDtypeStruct((t, d), BF16))
    return pl.pallas_call(
        body, name=name, grid=(t // tm, n // tn),
        in_specs=[pl.BlockSpec((tm, d), lambda i, j: (i, 0)), pl.BlockSpec((1, d), lambda i, j: (0, 0)),
                  mod_spec, mod_spec, pl.BlockSpec((d, tn), lambda i, j: (0, j))],
        out_specs=out_specs, out_shape=out_shape,
        scratch_shapes=[pltpu.VMEM((tm, d), BF16)],
        compiler_params=_cp("parallel", "arbitrary"),
    )(x, g, scale, shift, w)


def out_proj_fwd(o, zf, w_out, x, gate, seq, name):
    t, di = o.shape
    d = x.shape[1]
    tm = _tile(seq, 256)
    per_seq = seq // tm

    def body(o_ref, z_ref, w_ref, x_ref, gt_ref, xn_ref, y_ref, u_ref):
        z = z_ref[...]
        u = (o_ref[...] * (z * _sigmoid(z))).astype(BF16)
        y = jnp.dot(u, w_ref[...], preferred_element_type=F32)
        u_ref[...] = u
        y_ref[...] = y
        xn_ref[...] = x_ref[...] + gt_ref[0] * y

    row = lambda c: pl.BlockSpec((tm, c), lambda i: (i, 0))
    return pl.pallas_call(
        body, name=name, grid=(t // tm,),
        in_specs=[row(di), row(di), pl.BlockSpec((di, d), lambda i: (0, 0)), row(d),
                  pl.BlockSpec((1, 1, d), lambda i: (i // per_seq, 0, 0))],
        out_specs=[row(d), row(d), row(di)],
        out_shape=[jax.ShapeDtypeStruct((t, d), F32), jax.ShapeDtypeStruct((t, d), F32),
                   jax.ShapeDtypeStruct((t, di), BF16)],
        compiler_params=_cp("parallel"),
    )(o, zf, w_out, x, gate)


def out_proj_bwd(dxo, y, gate, w_out, o, zf, seq, name):
    t, d = dxo.shape
    di = o.shape[1]
    tm = _tile(seq, 256)
    per_seq = seq // tm

    def body(dx_ref, y_ref, gt_ref, w_ref, o_ref, z_ref, do_ref, dz_ref, dy_ref, dgt_ref):
        dx = dx_ref[...]
        part = jnp.sum(dx * y_ref[...], axis=0, keepdims=True)

        @pl.when(pl.program_id(0) % per_seq == 0)
        def _():
            dgt_ref[0] = part

        @pl.when(pl.program_id(0) % per_seq != 0)
        def _():
            dgt_ref[0] += part

        dy = (dx * gt_ref[0]).astype(BF16)
        dy_ref[...] = dy
        du = lax.dot_general(dy, w_ref[...], (((1,), (1,)), ((), ())), preferred_element_type=F32)
        z = z_ref[...]
        sg = _sigmoid(z)
        do_ref[...] = (du * (z * sg)).astype(BF16)
        dz_ref[...] = (du * o_ref[...] * (sg * (1.0 + z * (1.0 - sg)))).astype(BF16)

    row = lambda c: pl.BlockSpec((tm, c), lambda i: (i, 0))
    seq_spec = pl.BlockSpec((1, 1, d), lambda i: (i // per_seq, 0, 0))
    return pl.pallas_call(
        body, name=name, grid=(t // tm,),
        in_specs=[row(d), row(d), seq_spec, pl.BlockSpec((di, d), lambda i: (0, 0)), row(di), row(di)],
        out_specs=[row(di), row(di), row(d), seq_spec],
        out_shape=[jax.ShapeDtypeStruct((t, di), BF16), jax.ShapeDtypeStruct((t, di), BF16),
                   jax.ShapeDtypeStruct((t, d), BF16), jax.ShapeDtypeStruct((t // seq, 1, d), F32)],
        compiler_params=_cp("arbitrary"),
    )(dxo, y, gate, w_out, o, zf)


def tn_matmul(a, b, name):
    t, m = a.shape
    n = b.shape[1]
    tn = 512 if n % 512 == 0 else n
    tk = _tile(t, 512)

    def body(a_ref, b_ref, o_ref):
        part = lax.dot_general(a_ref[...], b_ref[...], (((0,), (0,)), ((), ())), preferred_element_type=F32)

        @pl.when(pl.program_id(1) == 0)
        def _():
            o_ref[...] = part

        @pl.when(pl.program_id(1) != 0)
        def _():
            o_ref[...] += part

    return pl.pallas_call(
        body, name=name, grid=(n // tn, t // tk),
        in_specs=[pl.BlockSpec((tk, m), lambda j, k: (k, 0)), pl.BlockSpec((tk, tn), lambda j, k: (k, j))],
        out_specs=pl.BlockSpec((m, tn), lambda j, k: (0, j)),
        out_shape=jax.ShapeDtypeStruct((m, n), F32),
        compiler_params=_cp("parallel", "arbitrary"),
    )(a, b)


def dh_norm_bwd(parts, w_t, extra, x, g, scale, dxo, seq, name):
    t, d = x.shape
    nparts = len(parts)
    kw = parts[0].shape[1]
    tm = _tile(seq, 256)
    per_seq = seq // tm
    has_extra = extra is not None

    def body(*refs):
        p_refs = refs[:nparts]
        w_ref = refs[nparts]
        pos = nparts + 1
        if has_extra:
            e_ref, we_ref = refs[pos], refs[pos + 1]
            pos += 2
        x_ref, g_ref, sc_ref, dxo_ref, dx_ref, pp_ref, ss_ref, acc = refs[pos:]
        i, k = pl.program_id(0), pl.program_id(1)

        for kk in range(nparts):
            @pl.when(k == kk)
            def _(kk=kk):
                part = jnp.dot(p_refs[kk][...], w_ref[...], preferred_element_type=F32)
                if kk == 0:
                    if has_extra:
                        part = part + jnp.dot(e_ref[...], we_ref[...], preferred_element_type=F32)
                    acc[...] = part
                else:
                    acc[...] += part

        @pl.when(k == nparts - 1)
        def _():
            dh = acc[...]
            xx = x_ref[...]
            rstd = lax.rsqrt(jnp.mean(xx * xx, axis=-1, keepdims=True) + NORM_EPS)
            xhat = xx * rstd
            dxhat = dh * (g_ref[...] * (1.0 + sc_ref[0]))
            dx_ref[...] = dxo_ref[...] + rstd * (dxhat - xhat * jnp.mean(dxhat * xhat, axis=-1, keepdims=True))
            pp = jnp.sum(dh * xhat, axis=0, keepdims=True)
            ss = jnp.sum(dh, axis=0, keepdims=True)

            @pl.when(i % per_seq == 0)
            def _():
                pp_ref[0] = pp
                ss_ref[0] = ss

            @pl.when(i % per_seq != 0)
            def _():
                pp_ref[0] += pp
                ss_ref[0] += ss

    row = lambda c: pl.BlockSpec((tm, c), lambda i, k: (i, 0))
    seq_spec = pl.BlockSpec((1, 1, d), lambda i, k: (i // per_seq, 0, 0))
    in_specs = [row(kw)] * nparts + [pl.BlockSpec((kw, d), lambda i, k: (k, 0))]
    args = list(parts) + [w_t]
    if has_extra:
        ke = extra[0].shape[1]
        in_specs += [row(ke), pl.BlockSpec((ke, d), lambda i, k: (0, 0))]
        args += list(extra)
    in_specs += [row(d), pl.BlockSpec((1, d), lambda i, k: (0, 0)), seq_spec, row(d)]
    args += [x, g, scale, dxo]
    return pl.pallas_call(
        body, name=name, grid=(t // tm, nparts),
        in_specs=in_specs,
        out_specs=[row(d), seq_spec, seq_spec],
        out_shape=[jax.ShapeDtypeStruct((t, d), F32), jax.ShapeDtypeStruct((t // seq, 1, d), F32),
                   jax.ShapeDtypeStruct((t // seq, 1, d), F32)],
        scratch_shapes=[pltpu.VMEM((tm, d), F32)],
        compiler_params=_cp("arbitrary", "arbitrary"),
    )(*args)


def final_loss(x, g, target, name):
    t, d = x.shape
    tm = _tile(t, 512)

    def body(x_ref, g_ref, t_ref, dx_ref, dg_ref, sq_ref):
        xx = x_ref[...]
        gg = g_ref[...]
        rstd = lax.rsqrt(jnp.mean(xx * xx, axis=-1, keepdims=True) + NORM_EPS)
        xhat = xx * rstd
        err = xhat * gg - t_ref[...]
        dy = err * (1.0 / d)
        dxhat = dy * gg
        dx_ref[...] = rstd * (dxhat - xhat * jnp.mean(dxhat * xhat, axis=-1, keepdims=True))
        dg = jnp.sum(dy * xhat, axis=0, keepdims=True)
        sq = jnp.sum(err * err, axis=0, keepdims=True)

        @pl.when(pl.program_id(0) == 0)
        def _():
            dg_ref[...] = dg
            sq_ref[...] = sq

        @pl.when(pl.program_id(0) != 0)
        def _():
            dg_ref[...] += dg
            sq_ref[...] += sq

    row = pl.BlockSpec((tm, d), lambda i: (i, 0))
    vec = pl.BlockSpec((1, d), lambda i: (0, 0))
    return pl.pallas_call(
        body, name=name, grid=(t // tm,),
        in_specs=[row, vec, row], out_specs=[row, vec, vec],
        out_shape=[jax.ShapeDtypeStruct((t, d), F32), jax.ShapeDtypeStruct((1, d), F32),
                   jax.ShapeDtypeStruct((1, d), F32)],
        compiler_params=_cp("arbitrary"),
    )(x, g, target)


def _split3(x):
    x1 = x.astype(BF16)
    r1 = x - x1.astype(F32)
    x2 = r1.astype(BF16)
    x3 = (r1 - x2.astype(F32)).astype(BF16)
    return x1, x2, x3


def _dot3(x, u):
    x1, x2, x3 = _split3(x)
    return (jnp.dot(x1, u, preferred_element_type=F32) + jnp.dot(x2, u, preferred_element_type=F32)
            + jnp.dot(x3, u, preferred_element_type=F32))


def _log_sigmoid(x):
    return jnp.minimum(x, 0.0) - jnp.log1p(jnp.exp(-jnp.abs(x)))


def forget_cumsum(fl_t, b_f, name):
    b, h, s = fl_t.shape

    def body(f_ref, b_ref, o_ref):
        lf = _log_sigmoid(f_ref[0] + b_ref[...])
        u = (lax.broadcasted_iota(jnp.int32, (s, s), 0) <= lax.broadcasted_iota(jnp.int32, (s, s), 1)).astype(BF16)
        o_ref[0] = _dot3(lf, u)

    return pl.pallas_call(
        body, name=name, grid=(b,),
        in_specs=[pl.BlockSpec((1, h, s), lambda i: (i, 0, 0)), pl.BlockSpec((h, 1), lambda i: (0, 0))],
        out_specs=pl.BlockSpec((1, h, s), lambda i: (i, 0, 0)),
        out_shape=jax.ShapeDtypeStruct((b, h, s), F32),
        compiler_params=_cp("parallel"),
    )(fl_t, b_f)


def forget_cumsum_bwd(dcum, fl_t, b_f, name):
    b, h, s = fl_t.shape

    def body(d_ref, f_ref, b_ref, o_ref, db_ref):
        u = (lax.broadcasted_iota(jnp.int32, (s, s), 0) >= lax.broadcasted_iota(jnp.int32, (s, s), 1)).astype(BF16)
        dlf = _dot3(d_ref[0], u)
        df = dlf * _sigmoid(-(f_ref[0] + b_ref[...]))
        o_ref[0] = df
        part = jnp.sum(df, axis=-1, keepdims=True)

        @pl.when(pl.program_id(0) == 0)
        def _():
            db_ref[...] = part

        @pl.when(pl.program_id(0) != 0)
        def _():
            db_ref[...] += part

    blk = pl.BlockSpec((1, h, s), lambda i: (i, 0, 0))
    return pl.pallas_call(
        body, name=name, grid=(b,),
        in_specs=[blk, blk, pl.BlockSpec((h, 1), lambda i: (0, 0))],
        out_specs=[blk, pl.BlockSpec((h, 1), lambda i: (0, 0))],
        out_shape=[jax.ShapeDtypeStruct((b, h, s), F32), jax.ShapeDtypeStruct((h, 1), F32)],
        compiler_params=_cp("arbitrary"),
    )(dcum, fl_t, b_f)


def _nt(a, b):
    return lax.dot_general(a, b, (((1,), (1,)), ((), ())), preferred_element_type=F32)


def _attn_dims(t, seq, dh):
    tq = _tile(seq, 512)
    tk = _tile(seq, 256)
    return t // seq, tq, tk, seq // tq, tq // tk, 128 // dh


def _blocks_per_trip(r):
    return 2 if r % 2 == 0 else 1


def _scaled_q(q_ref, dh):
    scale = dh ** -0.5
    if math.log2(dh) % 2 == 0:
        return (q_ref[...].astype(F32) * scale).astype(BF16), None
    return q_ref[...], scale


def _diag_masks(tq, tk, r, strict):
    row = lax.broadcasted_iota(jnp.int32, (tq, tk), 0)
    col = lax.broadcasted_iota(jnp.int32, (tq, tk), 1)
    return [(col + d * tk < row) if strict else (col + d * tk <= row) for d in range(r)]


def _transposed(x):
    return jnp.transpose(x.astype(F32)).astype(BF16)


def _store_transposed(dst_ref, src_scr, tk):
    for jb in range(src_scr.shape[0]):
        dst_ref[jb * tk:(jb + 1) * tk, :] = jnp.transpose(src_scr[jb]).astype(BF16)


def _dot2(x, u):
    hi = x.astype(BF16)
    lo = (x - hi.astype(F32)).astype(BF16)
    return jnp.dot(jnp.concatenate([hi, lo], axis=1), jnp.concatenate([u, u], axis=0), preferred_element_type=F32)


def _attn_specs(seq, di, tq):
    nq = seq // tq
    cb = di // 128
    q_spec = pl.BlockSpec((tq, 128), lambda b, hp, i: (b * nq + i, hp))
    k_spec = pl.BlockSpec((seq, 128), lambda b, hp, i: (b, cb + hp))
    v_spec = pl.BlockSpec((seq, 128), lambda b, hp, i: (b, 2 * cb + hp))
    kv_out = pl.BlockSpec((seq, 128), lambda b, hp, i: (b, hp))
    return q_spec, k_spec, v_spec, kv_out


def fox_fwd(qkv, ck, seq, di, dh, name):
    t = qkv.shape[0]
    nb, tq, tk, nq, r, hpb = _attn_dims(t, seq, dh)
    per_trip = _blocks_per_trip(r)
    heads = [slice(hh * dh, (hh + 1) * dh) for hh in range(hpb)]
    q_spec, k_spec, v_spec, _ = _attn_specs(seq, di, tq)

    def body(q_ref, k_ref, v_ref, ck_ref, o_ref, lse_ref):
        i = pl.program_id(2)
        masks = _diag_masks(tq, tk, r, False)
        q, scale = _scaled_q(q_ref, dh)
        qs = [q[:, hd] for hd in heads]

        def scores(hh, j):
            start = pl.multiple_of(j * tk, tk)
            kh = k_ref[pl.ds(start, tk), heads[hh]]
            vh = v_ref[pl.ds(start, tk), heads[hh]]
            s = _nt(qs[hh], kh)
            if scale is not None:
                s = s * scale
            return s - ck_ref[0, hh, pl.ds(j, 1), :], vh

        def update(state, s, vh):
            m, l, acc = state
            m_new = jnp.maximum(m, jnp.max(s, axis=-1, keepdims=True))
            alpha = jnp.exp(m - m_new)
            p = jnp.exp(s - m_new)
            return m_new, alpha * l + jnp.sum(p, axis=-1, keepdims=True), alpha * acc + _dot2(p, vh)

        states = []
        for hh in range(hpb):
            s, vh = scores(hh, r * i)
            s = jnp.where(masks[0], s, -jnp.inf)
            m = jnp.max(s, axis=-1, keepdims=True)
            p = jnp.exp(s - m)
            states.append((m, jnp.sum(p, axis=-1, keepdims=True), _dot2(p, vh)))
        for d in range(1, r):
            for hh in range(hpb):
                s, vh = scores(hh, r * i + d)
                states[hh] = update(states[hh], jnp.where(masks[d], s, -jnp.inf), vh)

        def step(jj, states):
            for sub in range(per_trip):
                states = tuple(update(states[hh], *scores(hh, r * i - 1 - (per_trip * jj + sub))) for hh in range(hpb))
            return states

        states = lax.fori_loop(0, r * i // per_trip, step, tuple(states))
        for hh in range(hpb):
            m, l, acc = states[hh]
            o_ref[:, heads[hh]] = acc / l
            lse_ref[0, 0, :, hh:hh + 1] = m + jnp.log(l)

    return pl.pallas_call(
        body, name=name, grid=(nb, di // 128, nq),
        in_specs=[q_spec, k_spec, v_spec,
                  pl.BlockSpec((1, hpb, seq // tk, tk), lambda b, hp, i: (b, hp, 0, 0))],
        out_specs=[q_spec, pl.BlockSpec((1, 1, tq, hpb), lambda b, hp, i: (b, hp, i, 0))],
        out_shape=[jax.ShapeDtypeStruct((t, di), F32), jax.ShapeDtypeStruct((nb, di // 128, seq, hpb), F32)],
        compiler_params=_cp("parallel", "parallel", "arbitrary"),
    )(qkv, qkv, qkv, ck)


def fox_bwd(qkv, ck, o, do, lse, seq, di, dh, name):
    t = qkv.shape[0]
    nb, tq, tk, nq, r, hpb = _attn_dims(t, seq, dh)
    per_trip = _blocks_per_trip(r)
    heads = [slice(hh * dh, (hh + 1) * dh) for hh in range(hpb)]
    q_spec, k_spec, v_spec, kv_out = _attn_specs(seq, di, tq)
    ck_spec = pl.BlockSpec((1, hpb, seq // tk, tk), lambda b, hp, i: (b, hp, 0, 0))

    def body(q_ref, k_ref, v_ref, ck_ref, o_ref, do_ref, lse_ref, dq_ref, dk_ref, dv_ref, dck_ref, dkt_scr, dvt_scr):
        i = pl.program_id(2)

        @pl.when(i == 0)
        def _():
            dkt_scr[...] = jnp.zeros_like(dkt_scr)
            dvt_scr[...] = jnp.zeros_like(dvt_scr)
            dck_ref[...] = jnp.zeros_like(dck_ref)

        masks = _diag_masks(tq, tk, r, False)
        q, scale = _scaled_q(q_ref, dh)
        do = do_ref[...]
        q_t, do_t = _transposed(q), _transposed(do)
        qs = [q[:, hd] for hd in heads]
        dos = [do[:, hd] for hd in heads]
        deltas = [jnp.sum(dos[hh].astype(F32) * o_ref[:, heads[hh]], axis=-1, keepdims=True) for hh in range(hpb)]
        lses = [lse_ref[0, 0, :, hh:hh + 1] for hh in range(hpb)]

        def block(hh, j, mask, dq_acc):
            start = pl.multiple_of(j * tk, tk)
            kh = k_ref[pl.ds(start, tk), heads[hh]]
            vh = v_ref[pl.ds(start, tk), heads[hh]]
            s = _nt(qs[hh], kh)
            if scale is not None:
                s = s * scale
            p = jnp.exp(s - ck_ref[0, hh, pl.ds(j, 1), :] - lses[hh])
            if mask is not None:
                p = jnp.where(mask, p, 0.0)
            ds = p * (_nt(dos[hh], vh) - deltas[hh])
            dsb = ds.astype(BF16)
            dkt = jnp.dot(q_t[heads[hh], :], dsb, preferred_element_type=F32)
            if scale is not None:
                dkt = dkt * scale
            dkt_scr[j, heads[hh], :] += dkt
            dvt_scr[j, heads[hh], :] += jnp.dot(do_t[heads[hh], :], p.astype(BF16), preferred_element_type=F32)
            dck_ref[0, hh, pl.ds(j, 1), :] -= jnp.sum(ds, axis=0, keepdims=True)
            return dq_acc + jnp.dot(dsb, kh, preferred_element_type=F32)

        accs = [jnp.zeros((tq, dh), F32)] * hpb
        for d in range(r):
            accs = [block(hh, r * i + d, masks[d], accs[hh]) for hh in range(hpb)]
        def step(jj, accs):
            for sub in range(per_trip):
                accs = tuple(block(hh, per_trip * jj + sub, None, accs[hh]) for hh in range(hpb))
            return accs

        accs = lax.fori_loop(0, r * i // per_trip, step, tuple(accs))
        for hh in range(hpb):
            dq_ref[:, heads[hh]] = (accs[hh] * dh ** -0.5).astype(BF16)

        @pl.when(i == nq - 1)
        def _():
            _store_transposed(dk_ref, dkt_scr, tk)
            _store_transposed(dv_ref, dvt_scr, tk)

    return pl.pallas_call(
        body, name=name, grid=(nb, di // 128, nq),
        in_specs=[q_spec, k_spec, v_spec, ck_spec, q_spec, q_spec,
                  pl.BlockSpec((1, 1, tq, hpb), lambda b, hp, i: (b, hp, i, 0))],
        out_specs=[q_spec, kv_out, kv_out, ck_spec],
        out_shape=[jax.ShapeDtypeStruct((t, di), BF16)] * 3 + [jax.ShapeDtypeStruct(ck.shape, F32)],
        scratch_shapes=[pltpu.VMEM((seq // tk, 128, tk), F32), pltpu.VMEM((seq // tk, 128, tk), F32)],
        compiler_params=_cp("parallel", "parallel", "arbitrary"),
    )(qkv, qkv, qkv, ck, o, do, lse)


def _sb_logits(qh, kh, scale, strict):
    z = _nt(qh, kh)
    if scale is not None:
        z = z * scale
    e = jnp.exp(-jnp.abs(z))
    lb = jnp.minimum(z, 0.0) - jnp.log(1.0 + e)
    lk = lb - z
    return lb, lk, lk if strict is None else jnp.where(strict, lk, 0.0)


def _cum(x, u):
    return jnp.dot(x.astype(BF16), u, preferred_element_type=F32)


def sb_fwd(qkv, seq, di, dh, name):
    t = qkv.shape[0]
    nb, tq, tk, nq, r, hpb = _attn_dims(t, seq, dh)
    per_trip = _blocks_per_trip(r)
    heads = [slice(hh * dh, (hh + 1) * dh) for hh in range(hpb)]
    q_spec, k_spec, v_spec, _ = _attn_specs(seq, di, tq)

    def body(q_ref, k_ref, v_ref, o_ref, rt_ref):
        i = pl.program_id(2)
        masks = _diag_masks(tq, tk, r, True)
        row = lax.broadcasted_iota(jnp.int32, (tk, tk), 0)
        col = lax.broadcasted_iota(jnp.int32, (tk, tk), 1)
        u_after = (row > col).astype(BF16)
        q, scale = _scaled_q(q_ref, dh)
        qs = [q[:, hd] for hd in heads]

        def block(hh, j, mask, state):
            rr, acc = state
            start = pl.multiple_of(j * tk, tk)
            kh = k_ref[pl.ds(start, tk), heads[hh]]
            vh = v_ref[pl.ds(start, tk), heads[hh]]
            lb, _, lk = _sb_logits(qs[hh], kh, scale, mask)
            a = jnp.exp(lb + _cum(lk, u_after) + rr)
            if mask is not None:
                a = jnp.where(mask, a, 0.0)
            acc = acc + jnp.dot(a.astype(BF16), vh, preferred_element_type=F32)
            return rr + jnp.sum(lk, axis=-1, keepdims=True), acc

        states = [(jnp.zeros((tq, 1), F32), jnp.zeros((tq, dh), F32))] * hpb
        for d in reversed(range(r)):
            states = [block(hh, r * i + d, masks[d], states[hh]) for hh in range(hpb)]
        def step(jj, states):
            for sub in range(per_trip):
                states = tuple(block(hh, r * i - 1 - (per_trip * jj + sub), None, states[hh]) for hh in range(hpb))
            return states

        states = lax.fori_loop(0, r * i // per_trip, step, tuple(states))
        for hh in range(hpb):
            o_ref[:, heads[hh]] = states[hh][1]
            rt_ref[0, 0, :, hh:hh + 1] = states[hh][0]

    return pl.pallas_call(
        body, name=name, grid=(nb, di // 128, nq),
        in_specs=[q_spec, k_spec, v_spec],
        out_specs=[q_spec, pl.BlockSpec((1, 1, tq, hpb), lambda b, hp, i: (b, hp, i, 0))],
        out_shape=[jax.ShapeDtypeStruct((t, di), F32), jax.ShapeDtypeStruct((nb, di // 128, seq, hpb), F32)],
        compiler_params=_cp("parallel", "parallel", "arbitrary"),
    )(qkv, qkv, qkv)


def sb_bwd(qkv, do, rtot, seq, di, dh, name):
    t = qkv.shape[0]
    nb, tq, tk, nq, r, hpb = _attn_dims(t, seq, dh)
    per_trip = _blocks_per_trip(r)
    heads = [slice(hh * dh, (hh + 1) * dh) for hh in range(hpb)]
    q_spec, k_spec, v_spec, kv_out = _attn_specs(seq, di, tq)

    def body(q_ref, k_ref, v_ref, do_ref, rt_ref, dq_ref, dk_ref, dv_ref, dkt_scr, dvt_scr):
        i = pl.program_id(2)

        @pl.when(i == 0)
        def _():
            dkt_scr[...] = jnp.zeros_like(dkt_scr)
            dvt_scr[...] = jnp.zeros_like(dvt_scr)

        masks = _diag_masks(tq, tk, r, True)
        row = lax.broadcasted_iota(jnp.int32, (tk, tk), 0)
        col = lax.broadcasted_iota(jnp.int32, (tk, tk), 1)
        u_after = (row > col).astype(BF16)
        u_before = (row < col).astype(BF16)
        q, scale = _scaled_q(q_ref, dh)
        do = do_ref[...]
        q_t, do_t = _transposed(q), _transposed(do)
        qs = [q[:, hd] for hd in heads]
        dos = [do[:, hd] for hd in heads]
        rts = [rt_ref[0, 0, :, hh:hh + 1] for hh in range(hpb)]

        def block(hh, j, mask, state):
            lc, gc, dq_acc = state
            start = pl.multiple_of(j * tk, tk)
            kh = k_ref[pl.ds(start, tk), heads[hh]]
            vh = v_ref[pl.ds(start, tk), heads[hh]]
            lb, lk_all, lk = _sb_logits(qs[hh], kh, scale, mask)
            lc = lc + jnp.sum(lk, axis=-1, keepdims=True)
            a = jnp.exp(lb + ((rts[hh] - lc) + _cum(lk, u_after)))
            if mask is not None:
                a = jnp.where(mask, a, 0.0)
            de = a * _nt(dos[hh], vh)
            g = gc + _cum(de, u_before)
            dz = de * jnp.exp(lk_all) - g * jnp.exp(lb)
            if mask is not None:
                dz = jnp.where(mask, dz, 0.0)
            dzb = dz.astype(BF16)
            dkt = jnp.dot(q_t[heads[hh], :], dzb, preferred_element_type=F32)
            if scale is not None:
                dkt = dkt * scale
            dkt_scr[j, heads[hh], :] += dkt
            dvt_scr[j, heads[hh], :] += jnp.dot(do_t[heads[hh], :], a.astype(BF16), preferred_element_type=F32)
            return (lc, gc + jnp.sum(de, axis=-1, keepdims=True),
                    dq_acc + jnp.dot(dzb, kh, preferred_element_type=F32))

        zero = jnp.zeros((tq, 1), F32)
        def step(jj, states):
            for sub in range(per_trip):
                states = tuple(block(hh, per_trip * jj + sub, None, states[hh]) for hh in range(hpb))
            return states

        states = lax.fori_loop(0, r * i // per_trip, step, ((zero, zero, jnp.zeros((tq, dh), F32)),) * hpb)
        for d in range(r):
            states = [block(hh, r * i + d, masks[d], states[hh]) for hh in range(hpb)]
        for hh in range(hpb):
            dq_ref[:, heads[hh]] = (states[hh][2] * dh ** -0.5).astype(BF16)

        @pl.when(i == nq - 1)
        def _():
            _store_transposed(dk_ref, dkt_scr, tk)
            _store_transposed(dv_ref, dvt_scr, tk)

    return pl.pallas_call(
        body, name=name, grid=(nb, di // 128, nq),
        in_specs=[q_spec, k_spec, v_spec, q_spec, pl.BlockSpec((1, 1, tq, hpb), lambda b, hp, i: (b, hp, i, 0))],
        out_specs=[q_spec, kv_out, kv_out],
        out_shape=[jax.ShapeDtypeStruct((t, di), BF16)] * 3,
        scratch_shapes=[pltpu.VMEM((seq // tk, 128, tk), F32), pltpu.VMEM((seq // tk, 128, tk), F32)],
        compiler_params=_cp("parallel", "parallel", "arbitrary"),
    )(qkv, qkv, qkv, do, rtot)


def _cols(g):
    return jnp.transpose(g, (1, 0, 2)).reshape(g.shape[1], NDEV * g.shape[2])


def _col_blocks(w):
    r, c8 = w.shape
    return jnp.transpose(w.reshape(r, NDEV, c8 // NDEV), (1, 0, 2))


def _pad_rows16(a):
    return jnp.pad(a, ((0, 16 - a.shape[0]), (0, 0)))


def _pad_cols(a, n):
    return jnp.pad(a, ((0, 0), (0, n - a.shape[1])))


def kernel(x, c, fox_norm_g, fox_w_ada, fox_b_ada, fox_w_in, fox_b_f, fox_w_out, sb_norm_g, sb_w_ada, sb_b_ada, sb_w_in, sb_w_out, final_norm_g, loss_target, m_fox_norm_g, m_fox_w_ada, m_fox_b_ada, m_fox_w_in, m_fox_b_f, m_fox_w_out, m_sb_norm_g, m_sb_w_ada, m_sb_b_ada, m_sb_w_in, m_sb_w_out, m_final_norm_g, v_fox_norm_g, v_fox_w_ada, v_fox_b_ada, v_fox_w_in, v_fox_b_f, v_fox_w_out, v_sb_norm_g, v_sb_w_ada, v_sb_b_ada, v_sb_w_in, v_sb_w_out, v_final_norm_g):
    nb, seq, d = x.shape
    t = nb * seq
    h = fox_b_f.shape[-1]
    di = fox_w_out.shape[1] * NDEV
    dh = di // h
    tq = _tile(seq, 256)
    me = _my_index()

    big = [w[0].astype(BF16) for w in (fox_w_ada, fox_w_in, fox_w_out, sb_w_ada, sb_w_in, sb_w_out)]
    gathered = all_gather(big + [sb_norm_g, sb_b_ada], "gather_weights")
    fox_wada, fox_win, sb_wada, sb_win = (_cols(gathered[k]) for k in (0, 1, 3, 4))
    fox_wout = gathered[2].reshape(di, d)
    sb_wout = gathered[5].reshape(di, d)
    sb_g = _cols(gathered[6])
    sb_bada = _cols(gathered[7])

    x0 = x.reshape(t, d)
    target = loss_target.reshape(t, d)
    c16 = _pad_rows16(c)

    def layer_fwd(xin, g, wada, bada, win, wout, b_f, tag):
        mod = adaln_fwd(c16, wada, bada, tag + "_adaln")[:nb]
        shift, scale, gate = (mod[:, k * d:(k + 1) * d].reshape(nb, 1, d) for k in range(3))
        qkv, hmod = normmod_matmul(xin, g, scale, shift, win[:, :3 * di], BF16, seq, True, tag + "_qkv")
        w_z = win[:, 3 * di:]
        if b_f is not None:
            w_z = _pad_cols(w_z, di + 128)
        (zf,) = normmod_matmul(xin, g, scale, shift, w_z, F32, seq, False, tag + "_z")
        saved = dict(x=xin, g=g, scale=scale, gate=gate, qkv=qkv, h=hmod, zf=zf, win=win, wout=wout)
        if b_f is not None:
            fl_t = jnp.transpose(zf[:, di:di + h].reshape(nb, seq, h), (0, 2, 1))
            bf_col = b_f.reshape(h, 1)
            cum = forget_cumsum(fl_t, bf_col, tag + "_cum")
            ck = cum.reshape(nb, h, seq // tq, tq)
            o, lse = fox_fwd(qkv, ck, seq, di, dh, tag + "_attn")
            saved.update(fl_t=fl_t, bf_col=bf_col, ck=ck, lse=lse)
        else:
            o, rtot = sb_fwd(qkv, seq, di, dh, tag + "_attn")
            saved.update(rtot=rtot)
        xout, y, u = out_proj_fwd(o, zf, wout, xin, gate, seq, tag + "_out")
        saved.update(o=o, y=y, u=u)
        return xout, saved

    def layer_bwd(dxo, sv, is_fox, tag):
        do, dz, dy, dgate = out_proj_bwd(dxo, sv["y"], sv["gate"], sv["wout"], sv["o"], sv["zf"], seq, tag + "_dout")
        dwout = tn_matmul(sv["u"], dy, tag + "_dwout")
        extra = None
        if is_fox:
            dq, dk, dv, dck = fox_bwd(sv["qkv"], sv["ck"], sv["o"], do, sv["lse"], seq, di, dh, tag + "_dattn")
            df_t, dbf = forget_cumsum_bwd(dck.reshape(nb, h, seq), sv["fl_t"], sv["bf_col"], tag + "_dcum")
            df = _pad_cols(jnp.transpose(df_t, (0, 2, 1)).reshape(t, h), 128).astype(BF16)
            extra = (df, jnp.transpose(_pad_cols(sv["win"][:, 4 * di:], 128)))
        else:
            dq, dk, dv = sb_bwd(sv["qkv"], do, sv["rtot"], seq, di, dh, tag + "_dattn")
        parts = [dq, dk, dv, dz]
        dwin = [tn_matmul(sv["h"], p, tag + "_dwin%d" % k) for k, p in enumerate(parts)]
        if is_fox:
            dwin.append(tn_matmul(sv["h"], df, tag + "_dwinf")[:, :h])
        dwin = jnp.concatenate(dwin, axis=1)
        w_t = jnp.transpose(sv["win"][:, :4 * di])
        dxin, psum, ssum = dh_norm_bwd(parts, w_t, extra, sv["x"], sv["g"], sv["scale"], dxo, seq, tag + "_dh")
        pad = lambda a: _pad_rows16(a.reshape(nb, d))
        dwada, dbada, dng = adaln_bwd(c16, pad(ssum), pad(psum), pad(dgate), pad(sv["scale"]), sv["g"], tag + "_dadaln")
        grads = dict(wada=dwada, bada=dbada, ng=dng, win=_col_blocks(dwin), wout=dwout.reshape(NDEV, di // NDEV, d))
        if is_fox:
            grads["bf"] = dbf.reshape(1, h)
        return dxin, grads

    x1, sv_fox = layer_fwd(x0, fox_norm_g, fox_wada, fox_b_ada, fox_win, fox_wout, fox_b_f, "fox")
    x2, sv_sb = layer_fwd(x1, sb_g, sb_wada, sb_bada, sb_win, sb_wout, None, "sb")
    dx2, dgf, sq = final_loss(x2, final_norm_g.reshape(1, d), target, "loss_head")
    dx1, g_sb = layer_bwd(dx2, sv_sb, False, "sb")
    dx0, g_fox = layer_bwd(dx1, sv_fox, True, "fox")

    land = all_to_all([g_fox["wada"], g_fox["win"], g_fox["wout"], g_sb["wada"], g_sb["win"], g_sb["wout"]],
                      "scatter_grads")
    small = jnp.concatenate([g_fox["ng"], g_fox["bada"], _pad_cols(g_fox["bf"], 128), g_sb["ng"], g_sb["bada"],
                             dgf, sq], axis=1)
    (small_all,) = all_gather([small], "gather_small")
    small_sum, loss_row = sum_slots(small_all, d, "sum_small")
    loss = loss_row[0, 0]

    offs = {}
    pos = 0
    for nm, width in (("fox_ng", d), ("fox_bada", 3 * d), ("fox_bf", 128), ("sb_ng", d), ("sb_bada", 3 * d), ("fin_g", d)):
        offs[nm] = (pos, width)
        pos += width

    def small_grad(nm, width=None, shard=False):
        p0, wd = offs[nm]
        wd = width or wd
        if shard:
            blk = wd // NDEV
            return lax.dynamic_slice(small_sum, (0, p0 + me * blk), (1, blk))
        return small_sum[:, p0:p0 + wd]

    results = {}
    big_params = [("fox_w_ada", fox_w_ada, m_fox_w_ada, v_fox_w_ada), ("fox_w_in", fox_w_in, m_fox_w_in, v_fox_w_in),
                  ("fox_w_out", fox_w_out, m_fox_w_out, v_fox_w_out), ("sb_w_ada", sb_w_ada, m_sb_w_ada, v_sb_w_ada),
                  ("sb_w_in", sb_w_in, m_sb_w_in, v_sb_w_in), ("sb_w_out", sb_w_out, m_sb_w_out, v_sb_w_out)]
    for k, (nm, w, m, v) in enumerate(big_params):
        outs = adamw(land[k], w[0], m[0], v[0], "adamw_" + nm)
        results[nm] = [o[None] for o in outs]
    small_params = [("fox_norm_g", fox_norm_g, m_fox_norm_g, v_fox_norm_g, small_grad("fox_ng")),
                    ("fox_b_ada", fox_b_ada, m_fox_b_ada, v_fox_b_ada, small_grad("fox_bada")),
                    ("fox_b_f", fox_b_f, m_fox_b_f, v_fox_b_f, small_grad("fox_bf", h)),
                    ("sb_norm_g", sb_norm_g, m_sb_norm_g, v_sb_norm_g, small_grad("sb_ng", shard=True)),
                    ("sb_b_ada", sb_b_ada, m_sb_b_ada, v_sb_b_ada, small_grad("sb_bada", shard=True)),
                    ("final_norm_g", final_norm_g.reshape(1, d), m_final_norm_g.reshape(1, d),
                     v_final_norm_g.reshape(1, d), small_grad("fin_g"))]
    for nm, w, m, v, g in small_params:
        outs = adamw(g[None], w, m, v, "adamw_" + nm)
        if nm == "final_norm_g":
            outs = [o.reshape(d) for o in outs]
        results[nm] = outs

    order = ["fox_norm_g", "fox_w_ada", "fox_b_ada", "fox_w_in", "fox_b_f", "fox_w_out", "sb_norm_g", "sb_w_ada",
             "sb_b_ada", "sb_w_in", "sb_w_out", "final_norm_g"]
    out = [loss, dx0.reshape(nb, seq, d)]
    for k in range(4):
        out += [results[nm][k] for nm in order]
    return tuple(out)
```

```python
import functools
import math

import jax
import jax.numpy as jnp
from jax import lax
from jax.experimental import pallas as pl
from jax.experimental.pallas import tpu as pltpu

F32 = jnp.float32
BF16 = jnp.bfloat16
NDEV = 8
VMEM_LIMIT = 56 * 1024 * 1024
NORM_EPS = 1e-6
ADAM_LR, ADAM_B1, ADAM_B2, ADAM_EPS, ADAM_WD, ADAM_STEP = 0.001, 0.9, 0.999, 1e-08, 0.01, 10
MESH = pl.DeviceIdType.MESH
ANY = pl.BlockSpec(memory_space=pl.ANY)


def _cp(*sem):
    return pltpu.CompilerParams(dimension_semantics=sem, vmem_limit_bytes=VMEM_LIMIT)


def _my_index():
    return 4 * lax.axis_index("x") + 2 * lax.axis_index("y") + lax.axis_index("c")


def _flip(k):
    x, y, c = lax.axis_index("x"), lax.axis_index("y"), lax.axis_index("c")
    kx, ky, kc = (k >> 2) & 1, (k >> 1) & 1, k & 1
    px = 1 - x if kx else x
    py = 1 - y if ky else y
    pc = 1 - c if kc else c
    return (px, py, pc), 4 * px + 2 * py + pc


class Exchange:
    def __init__(self, mode, arrs):
        self.gather = mode == "gather"
        self.arrs = list(arrs)
        self.n = len(self.arrs)
        self.in_specs = [ANY] * self.n
        self.out_specs = [ANY] * self.n
        self.out_shape = [jax.ShapeDtypeStruct(((NDEV,) + a.shape) if self.gather else a.shape, a.dtype)
                          for a in self.arrs]
        self.scratch = [pltpu.SemaphoreType.DMA((self.n * (NDEV - 1),)), pltpu.SemaphoreType.DMA((self.n * (NDEV - 1),)),
                        pltpu.SemaphoreType.DMA((self.n,))]

    def copies(self, ins, outs, sems):
        send_sems, recv_sems, local_sems = sems
        me = _my_index()
        out = []
        for a in range(self.n):
            out.append(pltpu.make_async_copy(ins[a] if self.gather else ins[a].at[me], outs[a].at[me], local_sems.at[a]))
            for k in range(1, NDEV):
                peer, pidx = _flip(k)
                out.append(pltpu.make_async_remote_copy(
                    src_ref=ins[a] if self.gather else ins[a].at[pidx], dst_ref=outs[a].at[me],
                    send_sem=send_sems.at[a * (NDEV - 1) + k - 1], recv_sem=recv_sems.at[a * (NDEV - 1) + k - 1],
                    device_id=peer, device_id_type=MESH))
        return out

    def run(self, name):
        n = self.n

        def body(*refs):
            cps = self.copies(refs[:n], refs[n:2 * n], refs[2 * n:])
            for cp in cps:
                cp.start()
            for cp in cps:
                cp.wait()

        return pl.pallas_call(body, name=name, out_shape=self.out_shape, in_specs=self.in_specs,
                              out_specs=self.out_specs, scratch_shapes=self.scratch)(*self.arrs)


def _ride(exchange, refs, first, last):
    n = exchange.n
    cps = exchange.copies(refs[:n], refs[n:2 * n], refs[2 * n:])

    @pl.when(first)
    def _():
        for cp in cps:
            cp.start()

    @pl.when(last)
    def _():
        for cp in cps:
            cp.wait()


def all_gather(arrs, name):
    return Exchange("gather", arrs).run(name)


def all_to_all(arrs, name):
    return Exchange("scatter", arrs).run(name)


def _tile(n, pref):
    t = min(n, pref)
    while n % t:
        t //= 2
    return t


def _col_tile(n, pref):
    t = _tile(n, pref)
    return t if t % 128 == 0 and t >= 512 else n


def adamw(land, w, m, v, name):
    slots, r, c = land.shape
    tr = _tile(r, 64)
    bc1 = 1.0 - ADAM_B1 ** ADAM_STEP
    bc2 = 1.0 - ADAM_B2 ** ADAM_STEP

    def body(land_ref, w_ref, m_ref, v_ref, g_ref, d_ref, nm_ref, nv_ref):
        g = land_ref[0].astype(F32)
        for s in range(1, slots):
            g = g + land_ref[s].astype(F32)
        nm = ADAM_B1 * m_ref[...] + (1.0 - ADAM_B1) * g
        nv = ADAM_B2 * v_ref[...] + (1.0 - ADAM_B2) * (g * g)
        m_hat = nm / bc1
        v_hat = nv / bc2
        g_ref[...] = g
        nm_ref[...] = nm
        nv_ref[...] = nv
        d_ref[...] = -ADAM_LR * (m_hat / (jnp.sqrt(v_hat) + ADAM_EPS) + ADAM_WD * w_ref[...])

    blk = pl.BlockSpec((tr, c), lambda i: (i, 0))
    return pl.pallas_call(
        body, name=name, grid=(r // tr,),
        in_specs=[pl.BlockSpec((slots, tr, c), lambda i: (0, i, 0)), blk, blk, blk],
        out_specs=[blk] * 4,
        out_shape=[jax.ShapeDtypeStruct((r, c), F32)] * 4,
        compiler_params=_cp("parallel"),
    )(land, w, m, v)


def sum_slots(land, d_model, name):
    slots, _, n = land.shape

    def body(land_ref, o_ref, loss_ref):
        g = land_ref[0]
        for s in range(1, slots):
            g = g + land_ref[s]
        o_ref[...] = g
        sq = jnp.sum(g[:, n - d_model:], axis=-1, keepdims=True)
        loss_ref[...] = jnp.broadcast_to(sq * (0.5 / d_model), (1, 128))

    return pl.pallas_call(
        body, name=name,
        out_shape=[jax.ShapeDtypeStruct((1, n), F32), jax.ShapeDtypeStruct((1, 128), F32)],
    )(land)


def _sigmoid(x):
    return 1.0 / (1.0 + jnp.exp(-x))


def adaln_fwd(c16, w_ada, b_ada, name):
    d3 = w_ada.shape[1]

    def body(c_ref, w_ref, b_ref, o_ref):
        cc = c_ref[...]
        sc = (cc * _sigmoid(cc)).astype(BF16)
        o_ref[...] = jnp.dot(sc, w_ref[...], preferred_element_type=F32) + b_ref[...]

    return pl.pallas_call(body, name=name, out_shape=jax.ShapeDtypeStruct((16, d3), F32),
                          compiler_params=pltpu.CompilerParams(vmem_limit_bytes=VMEM_LIMIT))(c16, w_ada, b_ada)


def adaln_bwd(c16, dshift16, p16, dgate16, scale16, g, name):
    d = c16.shape[1]
    nb = 3 * d // NDEV

    def body(c_ref, ds_ref, p_ref, dg_ref, sc_ref, g_ref, dw_ref, db_ref, dng_ref, dmod_scr):
        j = pl.program_id(0)

        @pl.when(j == 0)
        def _():
            p = p_ref[...]
            dmod = jnp.concatenate([ds_ref[...], p * g_ref[...], dg_ref[...]], axis=-1)
            dmod_scr[...] = dmod
            db_ref[...] = jnp.sum(dmod, axis=0, keepdims=True)
            dng_ref[...] = jnp.sum((1.0 + sc_ref[...]) * p, axis=0, keepdims=True)

        cc = c_ref[...]
        sc = (cc * _sigmoid(cc)).astype(BF16)
        start = pl.multiple_of(j * nb, 128)
        dm = dmod_scr[:, pl.ds(start, nb)].astype(BF16)
        dw_ref[0] = lax.dot_general(sc, dm, (((0,), (0,)), ((), ())), preferred_element_type=F32).astype(BF16)

    full = lambda shape: pl.BlockSpec(shape, lambda j: (0,) * len(shape))
    return pl.pallas_call(
        body, name=name, grid=(NDEV,),
        in_specs=[full((16, d))] * 5 + [full((1, d))],
        out_specs=[pl.BlockSpec((1, d, nb), lambda j: (j, 0, 0)), full((1, 3 * d)), full((1, d))],
        out_shape=[jax.ShapeDtypeStruct((NDEV, d, nb), BF16), jax.ShapeDtypeStruct((1, 3 * d), F32),
                   jax.ShapeDtypeStruct((1, d), F32)],
        scratch_shapes=[pltpu.VMEM((16, 3 * d), F32)],
        compiler_params=_cp("arbitrary"),
    )(c16, dshift16, p16, dgate16, scale16, g)


def _modulated_norm(x, g, scale, shift):
    rstd = lax.rsqrt(jnp.mean(x * x, axis=-1, keepdims=True) + NORM_EPS)
    return ((x * rstd) * g) * (1.0 + scale) + shift


def normmod_matmul(x, g, scale, shift, w, out_dtype, seq, emit_h, name):
    t, d = x.shape
    n = w.shape[1]
    tm = _tile(seq, 512)
    tn = _col_tile(n, 2048)
    per_seq = seq // tm

    def body(x_ref, g_ref, sc_ref, sh_ref, w_ref, *rest):
        if emit_h:
            o_ref, h_ref, h_scr = rest
        else:
            o_ref, h_scr = rest

        @pl.when(pl.program_id(1) == 0)
        def _():
            h = _modulated_norm(x_ref[...], g_ref[...], sc_ref[0], sh_ref[0]).astype(BF16)
            h_scr[...] = h
            if emit_h:
                h_ref[...] = h

        o_ref[...] = jnp.dot(h_scr[...], w_ref[...], preferred_element_type=F32).astype(out_dtype)

    mod_spec = pl.BlockSpec((1, 1, d), lambda i, j: (i // per_seq, 0, 0))
    out_specs = [pl.BlockSpec((tm, tn), lambda i, j: (i, j))]
    out_shape = [jax.ShapeDtypeStruct((t, n), out_dtype)]
    if emit_h:
        out_specs.append(pl.BlockSpec((tm, d), lambda i, j: (i, 0)))
        out_shape.append(jax.ShapeDtypeStruct((t, d), BF16))
    return pl.pallas_call(
        body, name=name, grid=(t // tm, n // tn),
        in_specs=[pl.BlockSpec((tm, d), lambda i, j: (i, 0)), pl.BlockSpec((1, d), lambda i, j: (0, 0)),
                  mod_spec, mod_spec, pl.BlockSpec((d, tn), lambda i, j: (0, j))],
        out_specs=out_specs, out_shape=out_shape,
        scratch_shapes=[pltpu.VMEM((tm, d), BF16)],
        compiler_params=_cp("parallel", "arbitrary"),
    )(x, g, scale, shift, w)


def out_proj_fwd(o, zf, w_out, x, gate, seq, name):
    t, di = o.shape
    d = x.shape[1]
    tm = _tile(seq, 256)
    per_seq = seq // tm

    def body(o_ref, z_ref, w_ref, x_ref, gt_ref, xn_ref, y_ref, u_ref):
        z = z_ref[...]
        u = (o_ref[...] * (z * _sigmoid(z))).astype(BF16)
        y = jnp.dot(u, w_ref[...], preferred_element_type=F32)
        u_ref[...] = u
        y_ref[...] = y
        xn_ref[...] = x_ref[...] + gt_ref[0] * y

    row = lambda c: pl.BlockSpec((tm, c), lambda i: (i, 0))
    return pl.pallas_call(
        body, name=name, grid=(t // tm,),
        in_specs=[row(di), row(di), pl.BlockSpec((di, d), lambda i: (0, 0)), row(d),
                  pl.BlockSpec((1, 1, d), lambda i: (i // per_seq, 0, 0))],
        out_specs=[row(d), row(d), row(di)],
        out_shape=[jax.ShapeDtypeStruct((t, d), F32), jax.ShapeDtypeStruct((t, d), F32),
                   jax.ShapeDtypeStruct((t, di), BF16)],
        compiler_params=_cp("parallel"),
    )(o, zf, w_out, x, gate)


def out_proj_bwd(dxo, y, gate, w_out, o, zf, seq, name):
    t, d = dxo.shape
    di = o.shape[1]
    tm = _tile(seq, 256)
    per_seq = seq // tm

    def body(dx_ref, y_ref, gt_ref, w_ref, o_ref, z_ref, do_ref, dz_ref, dy_ref, dgt_ref):
        dx = dx_ref[...]
        part = jnp.sum(dx * y_ref[...], axis=0, keepdims=True)

        @pl.when(pl.program_id(0) % per_seq == 0)
        def _():
            dgt_ref[0] = part

        @pl.when(pl.program_id(0) % per_seq != 0)
        def _():
            dgt_ref[0] += part

        dy = (dx * gt_ref[0]).astype(BF16)
        dy_ref[...] = dy
        du = lax.dot_general(dy, w_ref[...], (((1,), (1,)), ((), ())), preferred_element_type=F32)
        z = z_ref[...]
        sg = _sigmoid(z)
        do_ref[...] = (du * (z * sg)).astype(BF16)
        dz_ref[...] = (du * o_ref[...] * (sg * (1.0 + z * (1.0 - sg)))).astype(BF16)

    row = lambda c: pl.BlockSpec((tm, c), lambda i: (i, 0))
    seq_spec = pl.BlockSpec((1, 1, d), lambda i: (i // per_seq, 0, 0))
    return pl.pallas_call(
        body, name=name, grid=(t // tm,),
        in_specs=[row(d), row(d), seq_spec, pl.BlockSpec((di, d), lambda i: (0, 0)), row(di), row(di)],
        out_specs=[row(di), row(di), row(d), seq_spec],
        out_shape=[jax.ShapeDtypeStruct((t, di), BF16), jax.ShapeDtypeStruct((t, di), BF16),
                   jax.ShapeDtypeStruct((t, d), BF16), jax.ShapeDtypeStruct((t // seq, 1, d), F32)],
        compiler_params=_cp("arbitrary"),
    )(dxo, y, gate, w_out, o, zf)


def tn_matmul(a, b, name):
    t, m = a.shape
    n = b.shape[1]
    tn = _col_tile(n, 1024)
    tk = _tile(t, 1024)
    nk = t // tk

    def body(a_ref, b_ref, o_ref, acc):
        part = lax.dot_general(a_ref[...], b_ref[...], (((0,), (0,)), ((), ())), preferred_element_type=F32)
        k = pl.program_id(1)

        @pl.when(k == 0)
        def _():
            acc[...] = part

        @pl.when(k != 0)
        def _():
            acc[...] += part

        @pl.when(k == nk - 1)
        def _():
            o_ref[...] = acc[...].astype(BF16)

    return pl.pallas_call(
        body, name=name, grid=(n // tn, nk),
        in_specs=[pl.BlockSpec((tk, m), lambda j, k: (k, 0)), pl.BlockSpec((tk, tn), lambda j, k: (k, j))],
        out_specs=pl.BlockSpec((m, tn), lambda j, k: (0, j)),
        out_shape=jax.ShapeDtypeStruct((m, n), BF16),
        scratch_shapes=[pltpu.VMEM((m, tn), F32)],
        compiler_params=_cp("parallel", "arbitrary"),
    )(a, b)


def dh_norm_bwd(parts, w_t, extra, x, g, scale, dxo, seq, name):
    t, d = x.shape
    nparts = len(parts)
    kw = parts[0].shape[1]
    tm = _tile(seq, 512)
    per_seq = seq // tm
    has_extra = extra is not None

    def body(*refs):
        p_refs = refs[:nparts]
        w_ref = refs[nparts]
        pos = nparts + 1
        if has_extra:
            e_ref, we_ref = refs[pos], refs[pos + 1]
            pos += 2
        x_ref, g_ref, sc_ref, dxo_ref, dx_ref, pp_ref, ss_ref, acc = refs[pos:]
        i, k = pl.program_id(0), pl.program_id(1)

        for kk in range(nparts):
            @pl.when(k == kk)
            def _(kk=kk):
                part = jnp.dot(p_refs[kk][...], w_ref[...], preferred_element_type=F32)
                if kk == 0:
                    if has_extra:
                        part = part + jnp.dot(e_ref[...], we_ref[...], preferred_element_type=F32)
                    acc[...] = part
                else:
                    acc[...] += part

        @pl.when(k == nparts - 1)
        def _():
            dh = acc[...]
            xx = x_ref[...]
            rstd = lax.rsqrt(jnp.mean(xx * xx, axis=-1, keepdims=True) + NORM_EPS)
            xhat = xx * rstd
            dxhat = dh * (g_ref[...] * (1.0 + sc_ref[0]))
            dx_ref[...] = dxo_ref[...] + rstd * (dxhat - xhat * jnp.mean(dxhat * xhat, axis=-1, keepdims=True))
            pp = jnp.sum(dh * xhat, axis=0, keepdims=True)
            ss = jnp.sum(dh, axis=0, keepdims=True)

            @pl.when(i % per_seq == 0)
            def _():
                pp_ref[0] = pp
                ss_ref[0] = ss

            @pl.when(i % per_seq != 0)
            def _():
                pp_ref[0] += pp
                ss_ref[0] += ss

    row = lambda c: pl.BlockSpec((tm, c), lambda i, k: (i, 0))
    seq_spec = pl.BlockSpec((1, 1, d), lambda i, k: (i // per_seq, 0, 0))
    in_specs = [row(kw)] * nparts + [pl.BlockSpec((kw, d), lambda i, k: (k, 0))]
    args = list(parts) + [w_t]
    if has_extra:
        ke = extra[0].shape[1]
        in_specs += [row(ke), pl.BlockSpec((ke, d), lambda i, k: (0, 0))]
        args += list(extra)
    in_specs += [row(d), pl.BlockSpec((1, d), lambda i, k: (0, 0)), seq_spec, row(d)]
    args += [x, g, scale, dxo]
    return pl.pallas_call(
        body, name=name, grid=(t // tm, nparts),
        in_specs=in_specs,
        out_specs=[row(d), seq_spec, seq_spec],
        out_shape=[jax.ShapeDtypeStruct((t, d), F32), jax.ShapeDtypeStruct((t // seq, 1, d), F32),
                   jax.ShapeDtypeStruct((t // seq, 1, d), F32)],
        scratch_shapes=[pltpu.VMEM((tm, d), F32)],
        compiler_params=_cp("arbitrary", "arbitrary"),
    )(*args)


def final_loss(x, g, target, name):
    t, d = x.shape
    tm = _tile(t, 512)

    def body(x_ref, g_ref, t_ref, dx_ref, dg_ref, sq_ref):
        xx = x_ref[...]
        gg = g_ref[...]
        rstd = lax.rsqrt(jnp.mean(xx * xx, axis=-1, keepdims=True) + NORM_EPS)
        xhat = xx * rstd
        err = xhat * gg - t_ref[...]
        dy = err * (1.0 / d)
        dxhat = dy * gg
        dx_ref[...] = rstd * (dxhat - xhat * jnp.mean(dxhat * xhat, axis=-1, keepdims=True))
        dg = jnp.sum(dy * xhat, axis=0, keepdims=True)
        sq = jnp.sum(err * err, axis=0, keepdims=True)

        @pl.when(pl.program_id(0) == 0)
        def _():
            dg_ref[...] = dg
            sq_ref[...] = sq

        @pl.when(pl.program_id(0) != 0)
        def _():
            dg_ref[...] += dg
            sq_ref[...] += sq

    row = pl.BlockSpec((tm, d), lambda i: (i, 0))
    vec = pl.BlockSpec((1, d), lambda i: (0, 0))
    return pl.pallas_call(
        body, name=name, grid=(t // tm,),
        in_specs=[row, vec, row], out_specs=[row, vec, vec],
        out_shape=[jax.ShapeDtypeStruct((t, d), F32), jax.ShapeDtypeStruct((1, d), F32),
                   jax.ShapeDtypeStruct((1, d), F32)],
        compiler_params=_cp("arbitrary"),
    )(x, g, target)


def _split3(x):
    x1 = x.astype(BF16)
    r1 = x - x1.astype(F32)
    x2 = r1.astype(BF16)
    x3 = (r1 - x2.astype(F32)).astype(BF16)
    return x1, x2, x3


def _dot3(x, u):
    x1, x2, x3 = _split3(x)
    return (jnp.dot(x1, u, preferred_element_type=F32) + jnp.dot(x2, u, preferred_element_type=F32)
            + jnp.dot(x3, u, preferred_element_type=F32))


def _log_sigmoid(x):
    return jnp.minimum(x, 0.0) - jnp.log1p(jnp.exp(-jnp.abs(x)))


def forget_cumsum(fl_t, b_f, name):
    b, h, s = fl_t.shape

    def body(f_ref, b_ref, o_ref):
        lf = _log_sigmoid(f_ref[0] + b_ref[...])
        u = (lax.broadcasted_iota(jnp.int32, (s, s), 0) <= lax.broadcasted_iota(jnp.int32, (s, s), 1)).astype(BF16)
        o_ref[0] = _dot3(lf, u)

    return pl.pallas_call(
        body, name=name, grid=(b,),
        in_specs=[pl.BlockSpec((1, h, s), lambda i: (i, 0, 0)), pl.BlockSpec((h, 1), lambda i: (0, 0))],
        out_specs=pl.BlockSpec((1, h, s), lambda i: (i, 0, 0)),
        out_shape=jax.ShapeDtypeStruct((b, h, s), F32),
        compiler_params=_cp("parallel"),
    )(fl_t, b_f)


def forget_cumsum_bwd(dcum, fl_t, b_f, name):
    b, h, s = fl_t.shape

    def body(d_ref, f_ref, b_ref, o_ref, db_ref):
        u = (lax.broadcasted_iota(jnp.int32, (s, s), 0) >= lax.broadcasted_iota(jnp.int32, (s, s), 1)).astype(BF16)
        dlf = _dot3(d_ref[0], u)
        df = dlf * _sigmoid(-(f_ref[0] + b_ref[...]))
        o_ref[0] = df
        part = jnp.sum(df, axis=-1, keepdims=True)

        @pl.when(pl.program_id(0) == 0)
        def _():
            db_ref[...] = part

        @pl.when(pl.program_id(0) != 0)
        def _():
            db_ref[...] += part

    blk = pl.BlockSpec((1, h, s), lambda i: (i, 0, 0))
    return pl.pallas_call(
        body, name=name, grid=(b,),
        in_specs=[blk, blk, pl.BlockSpec((h, 1), lambda i: (0, 0))],
        out_specs=[blk, pl.BlockSpec((h, 1), lambda i: (0, 0))],
        out_shape=[jax.ShapeDtypeStruct((b, h, s), F32), jax.ShapeDtypeStruct((h, 1), F32)],
        compiler_params=_cp("arbitrary"),
    )(dcum, fl_t, b_f)


def _nt(a, b):
    return lax.dot_general(a, b, (((1,), (1,)), ((), ())), preferred_element_type=F32)


def _attn_dims(t, seq, dh):
    tq = _tile(seq, 512)
    tk = _tile(seq, 256)
    return t // seq, tq, tk, seq // tq, tq // tk, 128 // dh


def _blocks_per_trip(r):
    return 2 if r % 2 == 0 else 1


def _scaled_q(q_ref, dh):
    scale = dh ** -0.5
    if math.log2(dh) % 2 == 0:
        return (q_ref[...].astype(F32) * scale).astype(BF16), None
    return q_ref[...], scale


def _diag_masks(tq, tk, r, strict):
    row = lax.broadcasted_iota(jnp.int32, (tq, tk), 0)
    col = lax.broadcasted_iota(jnp.int32, (tq, tk), 1)
    return [(col + d * tk < row) if strict else (col + d * tk <= row) for d in range(r)]


def _transposed(x):
    return jnp.transpose(x.astype(F32)).astype(BF16)


def _store_transposed(dst_ref, src_scr, tk):
    for jb in range(src_scr.shape[0]):
        dst_ref[jb * tk:(jb + 1) * tk, :] = jnp.transpose(src_scr[jb]).astype(BF16)


def _dot2(x, u):
    hi = x.astype(BF16)
    lo = (x - hi.astype(F32)).astype(BF16)
    return jnp.dot(jnp.concatenate([hi, lo], axis=1), jnp.concatenate([u, u], axis=0), preferred_element_type=F32)


def _attn_specs(seq, di, tq):
    nq = seq // tq
    cb = di // 128
    q_spec = pl.BlockSpec((tq, 128), lambda b, hp, i: (b * nq + i, hp))
    k_spec = pl.BlockSpec((seq, 128), lambda b, hp, i: (b, cb + hp))
    v_spec = pl.BlockSpec((seq, 128), lambda b, hp, i: (b, 2 * cb + hp))
    kv_out = pl.BlockSpec((seq, 128), lambda b, hp, i: (b, hp))
    return q_spec, k_spec, v_spec, kv_out


def _grid_ends(grid):
    ids = [pl.program_id(a) for a in range(len(grid))]
    first = functools.reduce(jnp.logical_and, [p == 0 for p in ids])
    last = functools.reduce(jnp.logical_and, [p == g - 1 for p, g in zip(ids, grid)])
    return first, last


def fox_fwd(qkv, ck, seq, di, dh, name, rider=None):
    t = qkv.shape[0]
    nb, tq, tk, nq, r, hpb = _attn_dims(t, seq, dh)
    per_trip = _blocks_per_trip(r)
    heads = [slice(hh * dh, (hh + 1) * dh) for hh in range(hpb)]
    q_spec, k_spec, v_spec, _ = _attn_specs(seq, di, tq)
    grid = (nb, di // 128, nq)
    nr = rider.n if rider else 0

    def body(q_ref, k_ref, v_ref, ck_ref, *rest):
        o_ref, lse_ref = rest[nr:nr + 2]
        if rider:
            _ride(rider, rest[:nr] + rest[nr + 2:], *_grid_ends(grid))
        i = pl.program_id(2)
        masks = _diag_masks(tq, tk, r, False)
        q, scale = _scaled_q(q_ref, dh)
        qs = [q[:, hd] for hd in heads]

        def scores(hh, j):
            start = pl.multiple_of(j * tk, tk)
            kh = k_ref[pl.ds(start, tk), heads[hh]]
            vh = v_ref[pl.ds(start, tk), heads[hh]]
            s = _nt(qs[hh], kh)
            if scale is not None:
                s = s * scale
            return s - ck_ref[0, hh, pl.ds(j, 1), :], vh

        def update(state, s, vh):
            m, l, acc = state
            m_new = jnp.maximum(m, jnp.max(s, axis=-1, keepdims=True))
            alpha = jnp.exp(m - m_new)
            p = jnp.exp(s - m_new)
            return m_new, alpha * l + jnp.sum(p, axis=-1, keepdims=True), alpha * acc + _dot2(p, vh)

        states = []
        for hh in range(hpb):
            s, vh = scores(hh, r * i)
            s = jnp.where(masks[0], s, -jnp.inf)
            m = jnp.max(s, axis=-1, keepdims=True)
            p = jnp.exp(s - m)
            states.append((m, jnp.sum(p, axis=-1, keepdims=True), _dot2(p, vh)))
        for d in range(1, r):
            for hh in range(hpb):
                s, vh = scores(hh, r * i + d)
                states[hh] = update(states[hh], jnp.where(masks[d], s, -jnp.inf), vh)

        def step(jj, states):
            for sub in range(per_trip):
                states = tuple(update(states[hh], *scores(hh, r * i - 1 - (per_trip * jj + sub))) for hh in range(hpb))
            return states

        states = lax.fori_loop(0, r * i // per_trip, step, tuple(states))
        for hh in range(hpb):
            m, l, acc = states[hh]
            o_ref[:, heads[hh]] = acc / l
            lse_ref[0, 0, :, hh:hh + 1] = m + jnp.log(l)

    return pl.pallas_call(
        body, name=name, grid=grid,
        in_specs=[q_spec, k_spec, v_spec,
                  pl.BlockSpec((1, hpb, seq // tk, tk), lambda b, hp, i: (b, hp, 0, 0))] + (rider.in_specs if rider else []),
        out_specs=[q_spec, pl.BlockSpec((1, 1, tq, hpb), lambda b, hp, i: (b, hp, i, 0))]
        + (rider.out_specs if rider else []),
        out_shape=[jax.ShapeDtypeStruct((t, di), F32), jax.ShapeDtypeStruct((nb, di // 128, seq, hpb), F32)]
        + (rider.out_shape if rider else []),
        scratch_shapes=rider.scratch if rider else [],
        compiler_params=_cp("arbitrary", "arbitrary", "arbitrary"),
    )(qkv, qkv, qkv, ck, *(rider.arrs if rider else []))


def fox_bwd(qkv, ck, o, do, lse, seq, di, dh, name, rider=None):
    t = qkv.shape[0]
    nb, tq, tk, nq, r, hpb = _attn_dims(t, seq, dh)
    per_trip = _blocks_per_trip(r)
    heads = [slice(hh * dh, (hh + 1) * dh) for hh in range(hpb)]
    q_spec, k_spec, v_spec, kv_out = _attn_specs(seq, di, tq)
    ck_spec = pl.BlockSpec((1, hpb, seq // tk, tk), lambda b, hp, i: (b, hp, 0, 0))
    grid = (nb, di // 128, nq)
    nr = rider.n if rider else 0

    def body(q_ref, k_ref, v_ref, ck_ref, o_ref, do_ref, lse_ref, *rest):
        dq_ref, dk_ref, dv_ref, dck_ref = rest[nr:nr + 4]
        dkt_scr, dvt_scr = rest[2 * nr + 4:2 * nr + 6]
        if rider:
            _ride(rider, rest[:nr] + rest[nr + 4:2 * nr + 4] + rest[2 * nr + 6:], *_grid_ends(grid))
        i = pl.program_id(2)

        @pl.when(i == 0)
        def _():
            dkt_scr[...] = jnp.zeros_like(dkt_scr)
            dvt_scr[...] = jnp.zeros_like(dvt_scr)
            dck_ref[...] = jnp.zeros_like(dck_ref)

        masks = _diag_masks(tq, tk, r, False)
        q, scale = _scaled_q(q_ref, dh)
        do = do_ref[...]
        q_t, do_t = _transposed(q), _transposed(do)
        qs = [q[:, hd] for hd in heads]
        dos = [do[:, hd] for hd in heads]
        deltas = [jnp.sum(dos[hh].astype(F32) * o_ref[:, heads[hh]], axis=-1, keepdims=True) for hh in range(hpb)]
        lses = [lse_ref[0, 0, :, hh:hh + 1] for hh in range(hpb)]

        def block(hh, j, mask, dq_acc):
            start = pl.multiple_of(j * tk, tk)
            kh = k_ref[pl.ds(start, tk), heads[hh]]
            vh = v_ref[pl.ds(start, tk), heads[hh]]
            s = _nt(qs[hh], kh)
            if scale is not None:
                s = s * scale
            p = jnp.exp(s - ck_ref[0, hh, pl.ds(j, 1), :] - lses[hh])
            if mask is not None:
                p = jnp.where(mask, p, 0.0)
            ds = p * (_nt(dos[hh], vh) - deltas[hh])
            dsb = ds.astype(BF16)
            dkt = jnp.dot(q_t[heads[hh], :], dsb, preferred_element_type=F32)
            if scale is not None:
                dkt = dkt * scale
            dkt_scr[j, heads[hh], :] += dkt
            dvt_scr[j, heads[hh], :] += jnp.dot(do_t[heads[hh], :], p.astype(BF16), preferred_element_type=F32)
            dck_ref[0, hh, pl.ds(j, 1), :] -= jnp.sum(ds, axis=0, keepdims=True)
            return dq_acc + jnp.dot(dsb, kh, preferred_element_type=F32)

        accs = [jnp.zeros((tq, dh), F32)] * hpb
        for d in range(r):
            accs = [block(hh, r * i + d, masks[d], accs[hh]) for hh in range(hpb)]
        def step(jj, accs):
            for sub in range(per_trip):
                accs = tuple(block(hh, per_trip * jj + sub, None, accs[hh]) for hh in range(hpb))
            return accs

        accs = lax.fori_loop(0, r * i // per_trip, step, tuple(accs))
        for hh in range(hpb):
            dq_ref[:, heads[hh]] = (accs[hh] * dh ** -0.5).astype(BF16)

        @pl.when(i == nq - 1)
        def _():
            _store_transposed(dk_ref, dkt_scr, tk)
            _store_transposed(dv_ref, dvt_scr, tk)

    return pl.pallas_call(
        body, name=name, grid=grid,
        in_specs=[q_spec, k_spec, v_spec, ck_spec, q_spec, q_spec,
                  pl.BlockSpec((1, 1, tq, hpb), lambda b, hp, i: (b, hp, i, 0))] + (rider.in_specs if rider else []),
        out_specs=[q_spec, kv_out, kv_out, ck_spec] + (rider.out_specs if rider else []),
        out_shape=[jax.ShapeDtypeStruct((t, di), BF16)] * 3 + [jax.ShapeDtypeStruct(ck.shape, F32)]
        + (rider.out_shape if rider else []),
        scratch_shapes=[pltpu.VMEM((seq // tk, 128, tk), F32), pltpu.VMEM((seq // tk, 128, tk), F32)]
        + (rider.scratch if rider else []),
        compiler_params=_cp("arbitrary", "arbitrary", "arbitrary"),
    )(qkv, qkv, qkv, ck, o, do, lse, *(rider.arrs if rider else []))


def _sb_logits(qh, kh, scale, strict):
    z = _nt(qh, kh)
    if scale is not None:
        z = z * scale
    e = jnp.exp(-jnp.abs(z))
    lb = jnp.minimum(z, 0.0) - jnp.log(1.0 + e)
    lk = lb - z
    return lb, lk, lk if strict is None else jnp.where(strict, lk, 0.0)


def _cum(x, u):
    return jnp.dot(x.astype(BF16), u, preferred_element_type=F32)


def sb_fwd(qkv, seq, di, dh, name):
    t = qkv.shape[0]
    nb, tq, tk, nq, r, hpb = _attn_dims(t, seq, dh)
    per_trip = _blocks_per_trip(r)
    heads = [slice(hh * dh, (hh + 1) * dh) for hh in range(hpb)]
    q_spec, k_spec, v_spec, _ = _attn_specs(seq, di, tq)

    def body(q_ref, k_ref, v_ref, o_ref, rt_ref):
        i = pl.program_id(2)
        masks = _diag_masks(tq, tk, r, True)
        row = lax.broadcasted_iota(jnp.int32, (tk, tk), 0)
        col = lax.broadcasted_iota(jnp.int32, (tk, tk), 1)
        u_after = (row > col).astype(BF16)
        q, scale = _scaled_q(q_ref, dh)
        qs = [q[:, hd] for hd in heads]

        def block(hh, j, mask, state):
            rr, acc = state
            start = pl.multiple_of(j * tk, tk)
            kh = k_ref[pl.ds(start, tk), heads[hh]]
            vh = v_ref[pl.ds(start, tk), heads[hh]]
            lb, _, lk = _sb_logits(qs[hh], kh, scale, mask)
            a = jnp.exp(lb + _cum(lk, u_after) + rr)
            if mask is not None:
                a = jnp.where(mask, a, 0.0)
            acc = acc + jnp.dot(a.astype(BF16), vh, preferred_element_type=F32)
            return rr + jnp.sum(lk, axis=-1, keepdims=True), acc

        states = [(jnp.zeros((tq, 1), F32), jnp.zeros((tq, dh), F32))] * hpb
        for d in reversed(range(r)):
            states = [block(hh, r * i + d, masks[d], states[hh]) for hh in range(hpb)]
        def step(jj, states):
            for sub in range(per_trip):
                states = tuple(block(hh, r * i - 1 - (per_trip * jj + sub), None, states[hh]) for hh in range(hpb))
            return states

        states = lax.fori_loop(0, r * i // per_trip, step, tuple(states))
        for hh in range(hpb):
            o_ref[:, heads[hh]] = states[hh][1]
            rt_ref[0, 0, :, hh:hh + 1] = states[hh][0]

    return pl.pallas_call(
        body, name=name, grid=(nb, di // 128, nq),
        in_specs=[q_spec, k_spec, v_spec],
        out_specs=[q_spec, pl.BlockSpec((1, 1, tq, hpb), lambda b, hp, i: (b, hp, i, 0))],
        out_shape=[jax.ShapeDtypeStruct((t, di), F32), jax.ShapeDtypeStruct((nb, di // 128, seq, hpb), F32)],
        compiler_params=_cp("parallel", "parallel", "arbitrary"),
    )(qkv, qkv, qkv)


def sb_bwd(qkv, do, rtot, seq, di, dh, name):
    t = qkv.shape[0]
    nb, tq, tk, nq, r, hpb = _attn_dims(t, seq, dh)
    per_trip = _blocks_per_trip(r)
    heads = [slice(hh * dh, (hh + 1) * dh) for hh in range(hpb)]
    q_spec, k_spec, v_spec, kv_out = _attn_specs(seq, di, tq)

    def body(q_ref, k_ref, v_ref, do_ref, rt_ref, dq_ref, dk_ref, dv_ref, dkt_scr, dvt_scr):
        i = pl.program_id(2)

        @pl.when(i == 0)
        def _():
            dkt_scr[...] = jnp.zeros_like(dkt_scr)
            dvt_scr[...] = jnp.zeros_like(dvt_scr)

        masks = _diag_masks(tq, tk, r, True)
        row = lax.broadcasted_iota(jnp.int32, (tk, tk), 0)
        col = lax.broadcasted_iota(jnp.int32, (tk, tk), 1)
        u_after = (row > col).astype(BF16)
        u_before = (row < col).astype(BF16)
        q, scale = _scaled_q(q_ref, dh)
        do = do_ref[...]
        q_t, do_t = _transposed(q), _transposed(do)
        qs = [q[:, hd] for hd in heads]
        dos = [do[:, hd] for hd in heads]
        rts = [rt_ref[0, 0, :, hh:hh + 1] for hh in range(hpb)]

        def block(hh, j, mask, state):
            lc, gc, dq_acc = state
            start = pl.multiple_of(j * tk, tk)
            kh = k_ref[pl.ds(start, tk), heads[hh]]
            vh = v_ref[pl.ds(start, tk), heads[hh]]
            lb, lk_all, lk = _sb_logits(qs[hh], kh, scale, mask)
            lc = lc + jnp.sum(lk, axis=-1, keepdims=True)
            a = jnp.exp(lb + ((rts[hh] - lc) + _cum(lk, u_after)))
            if mask is not None:
                a = jnp.where(mask, a, 0.0)
            de = a * _nt(dos[hh], vh)
            g = gc + _cum(de, u_before)
            dz = de * jnp.exp(lk_all) - g * jnp.exp(lb)
            if mask is not None:
                dz = jnp.where(mask, dz, 0.0)
            dzb = dz.astype(BF16)
            dkt = jnp.dot(q_t[heads[hh], :], dzb, preferred_element_type=F32)
            if scale is not None:
                dkt = dkt * scale
            dkt_scr[j, heads[hh], :] += dkt
            dvt_scr[j, heads[hh], :] += jnp.dot(do_t[heads[hh], :], a.astype(BF16), preferred_element_type=F32)
            return (lc, gc + jnp.sum(de, axis=-1, keepdims=True),
                    dq_acc + jnp.dot(dzb, kh, preferred_element_type=F32))

        zero = jnp.zeros((tq, 1), F32)
        def step(jj, states):
            for sub in range(per_trip):
                states = tuple(block(hh, per_trip * jj + sub, None, states[hh]) for hh in range(hpb))
            return states

        states = lax.fori_loop(0, r * i // per_trip, step, ((zero, zero, jnp.zeros((tq, dh), F32)),) * hpb)
        for d in range(r):
            states = [block(hh, r * i + d, masks[d], states[hh]) for hh in range(hpb)]
        for hh in range(hpb):
            dq_ref[:, heads[hh]] = (states[hh][2] * dh ** -0.5).astype(BF16)

        @pl.when(i == nq - 1)
        def _():
            _store_transposed(dk_ref, dkt_scr, tk)
            _store_transposed(dv_ref, dvt_scr, tk)

    return pl.pallas_call(
        body, name=name, grid=(nb, di // 128, nq),
        in_specs=[q_spec, k_spec, v_spec, q_spec, pl.BlockSpec((1, 1, tq, hpb), lambda b, hp, i: (b, hp, i, 0))],
        out_specs=[q_spec, kv_out, kv_out],
        out_shape=[jax.ShapeDtypeStruct((t, di), BF16)] * 3,
        scratch_shapes=[pltpu.VMEM((seq // tk, 128, tk), F32), pltpu.VMEM((seq // tk, 128, tk), F32)],
        compiler_params=_cp("parallel", "parallel", "arbitrary"),
    )(qkv, qkv, qkv, do, rtot)


def _cols(g):
    return jnp.transpose(g, (1, 0, 2)).reshape(g.shape[1], NDEV * g.shape[2])


def _col_blocks(w):
    r, c8 = w.shape
    return jnp.transpose(w.reshape(r, NDEV, c8 // NDEV), (1, 0, 2))


def _pad_rows16(a):
    return jnp.pad(a, ((0, 16 - a.shape[0]), (0, 0)))


def _pad_cols(a, n):
    return jnp.pad(a, ((0, 0), (0, n - a.shape[1])))


def kernel(x, c, fox_norm_g, fox_w_ada, fox_b_ada, fox_w_in, fox_b_f, fox_w_out, sb_norm_g, sb_w_ada, sb_b_ada, sb_w_in, sb_w_out, final_norm_g, loss_target, m_fox_norm_g, m_fox_w_ada, m_fox_b_ada, m_fox_w_in, m_fox_b_f, m_fox_w_out, m_sb_norm_g, m_sb_w_ada, m_sb_b_ada, m_sb_w_in, m_sb_w_out, m_final_norm_g, v_fox_norm_g, v_fox_w_ada, v_fox_b_ada, v_fox_w_in, v_fox_b_f, v_fox_w_out, v_sb_norm_g, v_sb_w_ada, v_sb_b_ada, v_sb_w_in, v_sb_w_out, v_final_norm_g):
    nb, seq, d = x.shape
    t = nb * seq
    h = fox_b_f.shape[-1]
    di = fox_w_out.shape[1] * NDEV
    dh = di // h
    tq = _tile(seq, 256)
    me = _my_index()

    gathered = all_gather([w[0].astype(BF16) for w in (fox_w_ada, fox_w_in, fox_w_out)], "gather_fox_weights")
    fox_wada, fox_win = _cols(gathered[0]), _cols(gathered[1])
    fox_wout = gathered[2].reshape(di, d)
    sb_gather = Exchange("gather", [w[0].astype(BF16) for w in (sb_w_ada, sb_w_in, sb_w_out)] + [sb_norm_g, sb_b_ada])

    x0 = x.reshape(t, d)
    target = loss_target.reshape(t, d)
    c16 = _pad_rows16(c)

    def layer_fwd(xin, g, wada, bada, win, wout, b_f, tag, rider=None):
        mod = adaln_fwd(c16, wada, bada, tag + "_adaln")[:nb]
        shift, scale, gate = (mod[:, k * d:(k + 1) * d].reshape(nb, 1, d) for k in range(3))
        qkv, hmod = normmod_matmul(xin, g, scale, shift, win[:, :3 * di], BF16, seq, True, tag + "_qkv")
        w_z = win[:, 3 * di:]
        if b_f is not None:
            w_z = _pad_cols(w_z, di + 128)
        (zf,) = normmod_matmul(xin, g, scale, shift, w_z, F32, seq, False, tag + "_z")
        saved = dict(x=xin, g=g, scale=scale, gate=gate, qkv=qkv, h=hmod, zf=zf, win=win, wout=wout)
        if b_f is not None:
            fl_t = jnp.transpose(zf[:, di:di + h].reshape(nb, seq, h), (0, 2, 1))
            bf_col = b_f.reshape(h, 1)
            cum = forget_cumsum(fl_t, bf_col, tag + "_cum")
            ck = cum.reshape(nb, h, seq // tq, tq)
            o, lse, *rode = fox_fwd(qkv, ck, seq, di, dh, tag + "_attn", rider)
            saved.update(fl_t=fl_t, bf_col=bf_col, ck=ck, lse=lse, rode=rode)
        else:
            o, rtot = sb_fwd(qkv, seq, di, dh, tag + "_attn")
            saved.update(rtot=rtot)
        xout, y, u = out_proj_fwd(o, zf, wout, xin, gate, seq, tag + "_out")
        saved.update(o=o, y=y, u=u)
        return xout, saved

    def layer_bwd(dxo, sv, is_fox, tag, rider=None):
        do, dz, dy, dgate = out_proj_bwd(dxo, sv["y"], sv["gate"], sv["wout"], sv["o"], sv["zf"], seq, tag + "_dout")
        dwout = tn_matmul(sv["u"], dy, tag + "_dwout")
        extra = None
        if is_fox:
            dq, dk, dv, dck, *rode = fox_bwd(sv["qkv"], sv["ck"], sv["o"], do, sv["lse"], seq, di, dh, tag + "_dattn",
                                             rider)
            df_t, dbf = forget_cumsum_bwd(dck.reshape(nb, h, seq), sv["fl_t"], sv["bf_col"], tag + "_dcum")
            df = _pad_cols(jnp.transpose(df_t, (0, 2, 1)).reshape(t, h), 128).astype(BF16)
            extra = (df, jnp.transpose(_pad_cols(sv["win"][:, 4 * di:], 128)))
        else:
            dq, dk, dv = sb_bwd(sv["qkv"], do, sv["rtot"], seq, di, dh, tag + "_dattn")
        parts = [dq, dk, dv, dz]
        dwin = [tn_matmul(sv["h"], p, tag + "_dwin%d" % k) for k, p in enumerate(parts)]
        if is_fox:
            dwin.append(tn_matmul(sv["h"], df, tag + "_dwinf")[:, :h])
        dwin = jnp.concatenate(dwin, axis=1)
        w_t = jnp.transpose(sv["win"][:, :4 * di])
        dxin, psum, ssum = dh_norm_bwd(parts, w_t, extra, sv["x"], sv["g"], sv["scale"], dxo, seq, tag + "_dh")
        pad = lambda a: _pad_rows16(a.reshape(nb, d))
        dwada, dbada, dng = adaln_bwd(c16, pad(ssum), pad(psum), pad(dgate), pad(sv["scale"]), sv["g"], tag + "_dadaln")
        grads = dict(wada=dwada, bada=dbada, ng=dng, win=_col_blocks(dwin), wout=dwout.reshape(NDEV, di // NDEV, d))
        if is_fox:
            grads["bf"] = dbf.reshape(1, h)
            grads["rode"] = rode
        return dxin, grads

    x1, sv_fox = layer_fwd(x0, fox_norm_g, fox_wada, fox_b_ada, fox_win, fox_wout, fox_b_f, "fox", sb_gather)
    sb_wada, sb_win = _cols(sv_fox["rode"][0]), _cols(sv_fox["rode"][1])
    sb_wout = sv_fox["rode"][2].reshape(di, d)
    sb_g, sb_bada = _cols(sv_fox["rode"][3]), _cols(sv_fox["rode"][4])
    x2, sv_sb = layer_fwd(x1, sb_g, sb_wada, sb_bada, sb_win, sb_wout, None, "sb")
    dx2, dgf, sq = final_loss(x2, final_norm_g.reshape(1, d), target, "loss_head")
    dx1, g_sb = layer_bwd(dx2, sv_sb, False, "sb")
    dx0, g_fox = layer_bwd(dx1, sv_fox, True, "fox", Exchange("scatter", [g_sb["wada"], g_sb["win"], g_sb["wout"]]))

    land = list(all_to_all([g_fox["wada"], g_fox["win"], g_fox["wout"]], "scatter_fox_grads")) + list(g_fox["rode"])
    small = jnp.concatenate([g_fox["ng"], g_fox["bada"], _pad_cols(g_fox["bf"], 128), g_sb["ng"], g_sb["bada"],
                             dgf, sq], axis=1)
    (small_all,) = all_gather([small], "gather_small")
    small_sum, loss_row = sum_slots(small_all, d, "sum_small")
    loss = loss_row[0, 0]

    offs = {}
    pos = 0
    for nm, width in (("fox_ng", d), ("fox_bada", 3 * d), ("fox_bf", 128), ("sb_ng", d), ("sb_bada", 3 * d), ("fin_g", d)):
        offs[nm] = (pos, width)
        pos += width

    def small_grad(nm, width=None, shard=False):
        p0, wd = offs[nm]
        wd = width or wd
        if shard:
            blk = wd // NDEV
            return lax.dynamic_slice(small_sum, (0, p0 + me * blk), (1, blk))
        return small_sum[:, p0:p0 + wd]

    results = {}
    big_params = [("fox_w_ada", fox_w_ada, m_fox_w_ada, v_fox_w_ada), ("fox_w_in", fox_w_in, m_fox_w_in, v_fox_w_in),
                  ("fox_w_out", fox_w_out, m_fox_w_out, v_fox_w_out), ("sb_w_ada", sb_w_ada, m_sb_w_ada, v_sb_w_ada),
                  ("sb_w_in", sb_w_in, m_sb_w_in, v_sb_w_in), ("sb_w_out", sb_w_out, m_sb_w_out, v_sb_w_out)]
    for k, (nm, w, m, v) in enumerate(big_params):
        outs = adamw(land[k], w[0], m[0], v[0], "adamw_" + nm)
        results[nm] = [o[None] for o in outs]
    small_params = [("fox_norm_g", fox_norm_g, m_fox_norm_g, v_fox_norm_g, small_grad("fox_ng")),
                    ("fox_b_ada", fox_b_ada, m_fox_b_ada, v_fox_b_ada, small_grad("fox_bada")),
                    ("fox_b_f", fox_b_f, m_fox_b_f, v_fox_b_f, small_grad("fox_bf", h)),
                    ("sb_norm_g", sb_norm_g, m_sb_norm_g, v_sb_norm_g, small_grad("sb_ng", shard=True)),
                    ("sb_b_ada", sb_b_ada, m_sb_b_ada, v_sb_b_ada, small_grad("sb_bada", shard=True)),
                    ("final_norm_g", final_norm_g.reshape(1, d), m_final_norm_g.reshape(1, d),
                     v_final_norm_g.reshape(1, d), small_grad("fin_g"))]
    for nm, w, m, v, g in small_params:
        outs = adamw(g[None], w, m, v, "adamw_" + nm)
        if nm == "final_norm_g":
            outs = [o.reshape(d) for o in outs]
        results[nm] = outs

    order = ["fox_norm_g", "fox_w_ada", "fox_b_ada", "fox_w_in", "fox_b_f", "fox_w_out", "sb_norm_g", "sb_w_ada",
             "sb_b_ada", "sb_w_in", "sb_w_out", "final_norm_g"]
    out = [loss, dx0.reshape(nb, seq, d)]
    for k in range(4):
        out += [results[nm][k] for nm in order]
    return tuple(out)
```

```python
import functools
import math

import jax
import jax.numpy as jnp
from jax import lax
from jax.experimental import pallas as pl
from jax.experimental.pallas import tpu as pltpu

F32 = jnp.float32
BF16 = jnp.bfloat16
NDEV = 8
VMEM_LIMIT = 56 * 1024 * 1024
NORM_EPS = 1e-6
ADAM_LR, ADAM_B1, ADAM_B2, ADAM_EPS, ADAM_WD, ADAM_STEP = 0.001, 0.9, 0.999, 1e-08, 0.01, 10
MESH = pl.DeviceIdType.MESH
ANY = pl.BlockSpec(memory_space=pl.ANY)


def _cp(*sem):
    return pltpu.CompilerParams(dimension_semantics=sem, vmem_limit_bytes=VMEM_LIMIT)


def _my_index():
    return 4 * lax.axis_index("x") + 2 * lax.axis_index("y") + lax.axis_index("c")


def _flip(k):
    x, y, c = lax.axis_index("x"), lax.axis_index("y"), lax.axis_index("c")
    kx, ky, kc = (k >> 2) & 1, (k >> 1) & 1, k & 1
    px = 1 - x if kx else x
    py = 1 - y if ky else y
    pc = 1 - c if kc else c
    return (px, py, pc), 4 * px + 2 * py + pc


class Exchange:
    def __init__(self, mode, arrs):
        self.gather = mode == "gather"
        self.arrs = list(arrs)
        self.n = len(self.arrs)
        self.in_specs = [ANY] * self.n
        self.out_specs = [ANY] * self.n
        self.out_shape = [jax.ShapeDtypeStruct(((NDEV,) + a.shape) if self.gather else a.shape, a.dtype)
                          for a in self.arrs]
        self.scratch = [pltpu.SemaphoreType.DMA((self.n * (NDEV - 1),)), pltpu.SemaphoreType.DMA((self.n * (NDEV - 1),)),
                        pltpu.SemaphoreType.DMA((self.n,))]

    def copies(self, ins, outs, sems):
        send_sems, recv_sems, local_sems = sems
        me = _my_index()
        out = []
        for a in range(self.n):
            out.append(pltpu.make_async_copy(ins[a] if self.gather else ins[a].at[me], outs[a].at[me], local_sems.at[a]))
            for k in range(1, NDEV):
                peer, pidx = _flip(k)
                out.append(pltpu.make_async_remote_copy(
                    src_ref=ins[a] if self.gather else ins[a].at[pidx], dst_ref=outs[a].at[me],
                    send_sem=send_sems.at[a * (NDEV - 1) + k - 1], recv_sem=recv_sems.at[a * (NDEV - 1) + k - 1],
                    device_id=peer, device_id_type=MESH))
        return out

    def run(self, name):
        n = self.n

        def body(*refs):
            cps = self.copies(refs[:n], refs[n:2 * n], refs[2 * n:])
            for cp in cps:
                cp.start()
            for cp in cps:
                cp.wait()

        return pl.pallas_call(body, name=name, out_shape=self.out_shape, in_specs=self.in_specs,
                              out_specs=self.out_specs, scratch_shapes=self.scratch)(*self.arrs)


def _ride(exchange, refs, first, last):
    n = exchange.n
    cps = exchange.copies(refs[:n], refs[n:2 * n], refs[2 * n:])

    @pl.when(first)
    def _():
        for cp in cps:
            cp.start()

    @pl.when(last)
    def _():
        for cp in cps:
            cp.wait()


def all_gather(arrs, name):
    return Exchange("gather", arrs).run(name)


def all_to_all(arrs, name):
    return Exchange("scatter", arrs).run(name)


def _tile(n, pref):
    t = min(n, pref)
    while n % t:
        t //= 2
    return t


def _col_tile(n, pref):
    t = _tile(n, pref)
    return t if t % 128 == 0 and t >= 512 else n


def adamw(land, w, m, v, name):
    slots, r, c = land.shape
    tr = _tile(r, 64)
    bc1 = 1.0 - ADAM_B1 ** ADAM_STEP
    bc2 = 1.0 - ADAM_B2 ** ADAM_STEP

    def body(land_ref, w_ref, m_ref, v_ref, g_ref, d_ref, nm_ref, nv_ref):
        g = land_ref[0].astype(F32)
        for s in range(1, slots):
            g = g + land_ref[s].astype(F32)
        nm = ADAM_B1 * m_ref[...] + (1.0 - ADAM_B1) * g
        nv = ADAM_B2 * v_ref[...] + (1.0 - ADAM_B2) * (g * g)
        m_hat = nm / bc1
        v_hat = nv / bc2
        g_ref[...] = g
        nm_ref[...] = nm
        nv_ref[...] = nv
        d_ref[...] = -ADAM_LR * (m_hat / (jnp.sqrt(v_hat) + ADAM_EPS) + ADAM_WD * w_ref[...])

    blk = pl.BlockSpec((tr, c), lambda i: (i, 0))
    return pl.pallas_call(
        body, name=name, grid=(r // tr,),
        in_specs=[pl.BlockSpec((slots, tr, c), lambda i: (0, i, 0)), blk, blk, blk],
        out_specs=[blk] * 4,
        out_shape=[jax.ShapeDtypeStruct((r, c), F32)] * 4,
        compiler_params=_cp("parallel"),
    )(land, w, m, v)


def sum_slots(land, d_model, name):
    slots, _, n = land.shape

    def body(land_ref, o_ref, loss_ref):
        g = land_ref[0]
        for s in range(1, slots):
            g = g + land_ref[s]
        o_ref[...] = g
        sq = jnp.sum(g[:, n - d_model:], axis=-1, keepdims=True)
        loss_ref[...] = jnp.broadcast_to(sq * (0.5 / d_model), (1, 128))

    return pl.pallas_call(
        body, name=name,
        out_shape=[jax.ShapeDtypeStruct((1, n), F32), jax.ShapeDtypeStruct((1, 128), F32)],
    )(land)


def _sigmoid(x):
    return 1.0 / (1.0 + jnp.exp(-x))


def adaln_fwd(c16, w_ada, b_ada, name):
    d3 = w_ada.shape[1]

    def body(c_ref, w_ref, b_ref, o_ref):
        cc = c_ref[...]
        sc = (cc * _sigmoid(cc)).astype(BF16)
        o_ref[...] = jnp.dot(sc, w_ref[...], preferred_element_type=F32) + b_ref[...]

    return pl.pallas_call(body, name=name, out_shape=jax.ShapeDtypeStruct((16, d3), F32),
                          compiler_params=pltpu.CompilerParams(vmem_limit_bytes=VMEM_LIMIT))(c16, w_ada, b_ada)


def adaln_bwd(c16, dshift16, p16, dgate16, scale16, g, name):
    d = c16.shape[1]
    nb = 3 * d // NDEV

    def body(c_ref, ds_ref, p_ref, dg_ref, sc_ref, g_ref, dw_ref, db_ref, dng_ref, dmod_scr):
        j = pl.program_id(0)

        @pl.when(j == 0)
        def _():
            p = p_ref[...]
            dmod = jnp.concatenate([ds_ref[...], p * g_ref[...], dg_ref[...]], axis=-1)
            dmod_scr[...] = dmod
            db_ref[...] = jnp.sum(dmod, axis=0, keepdims=True)
            dng_ref[...] = jnp.sum((1.0 + sc_ref[...]) * p, axis=0, keepdims=True)

        cc = c_ref[...]
        sc = (cc * _sigmoid(cc)).astype(BF16)
        start = pl.multiple_of(j * nb, 128)
        dm = dmod_scr[:, pl.ds(start, nb)].astype(BF16)
        dw_ref[0] = lax.dot_general(sc, dm, (((0,), (0,)), ((), ())), preferred_element_type=F32).astype(BF16)

    full = lambda shape: pl.BlockSpec(shape, lambda j: (0,) * len(shape))
    return pl.pallas_call(
        body, name=name, grid=(NDEV,),
        in_specs=[full((16, d))] * 5 + [full((1, d))],
        out_specs=[pl.BlockSpec((1, d, nb), lambda j: (j, 0, 0)), full((1, 3 * d)), full((1, d))],
        out_shape=[jax.ShapeDtypeStruct((NDEV, d, nb), BF16), jax.ShapeDtypeStruct((1, 3 * d), F32),
                   jax.ShapeDtypeStruct((1, d), F32)],
        scratch_shapes=[pltpu.VMEM((16, 3 * d), F32)],
        compiler_params=_cp("arbitrary"),
    )(c16, dshift16, p16, dgate16, scale16, g)


def _modulated_norm(x, g, scale, shift):
    rstd = lax.rsqrt(jnp.mean(x * x, axis=-1, keepdims=True) + NORM_EPS)
    return ((x * rstd) * g) * (1.0 + scale) + shift


def normmod_matmul(x, g, scale, shift, w, out_dtype, seq, emit_h, name):
    t, d = x.shape
    n = w.shape[1]
    tm = _tile(seq, 512)
    tn = _col_tile(n, 2048)
    per_seq = seq // tm

    def body(x_ref, g_ref, sc_ref, sh_ref, w_ref, *rest):
        if emit_h:
            o_ref, h_ref, h_scr = rest
        else:
            o_ref, h_scr = rest

        @pl.when(pl.program_id(1) == 0)
        def _():
            h = _modulated_norm(x_ref[...], g_ref[...], sc_ref[0], sh_ref[0]).astype(BF16)
            h_scr[...] = h
            if emit_h:
                h_ref[...] = h

        o_ref[...] = jnp.dot(h_scr[...], w_ref[...], preferred_element_type=F32).astype(out_dtype)

    mod_spec = pl.BlockSpec((1, 1, d), lambda i, j: (i // per_seq, 0, 0))
    out_specs = [pl.BlockSpec((tm, tn), lambda i, j: (i, j))]
    out_shape = [jax.ShapeDtypeStruct((t, n), out_dtype)]
    if emit_h:
        out_specs.append(pl.BlockSpec((tm, d), lambda i, j: (i, 0)))
        out_shape.append(jax.ShapeDtypeStruct((t, d), BF16))
    return pl.pallas_call(
        body, name=name, grid=(t // tm, n // tn),
        in_specs=[pl.BlockSpec((tm, d), lambda i, j: (i, 0)), pl.BlockSpec((1, d), lambda i, j: (0, 0)),
                  mod_spec, mod_spec, pl.BlockSpec((d, tn), lambda i, j: (0, j))],
        out_specs=out_specs, out_shape=out_shape,
        scratch_shapes=[pltpu.VMEM((tm, d), BF16)],
        compiler_params=_cp("parallel", "arbitrary"),
    )(x, g, scale, shift, w)


def out_proj_fwd(o, zf, w_out, x, gate, seq, name):
    t, di = o.shape
    d = x.shape[1]
    tm = _tile(seq, 256)
    per_seq = seq // tm

    def body(o_ref, z_ref, w_ref, x_ref, gt_ref, xn_ref, y_ref, u_ref):
        z = z_ref[...]
        u = (o_ref[...] * (z * _sigmoid(z))).astype(BF16)
        y = jnp.dot(u, w_ref[...], preferred_element_type=F32)
        u_ref[...] = u
        y_ref[...] = y
        xn_ref[...] = x_ref[...] + gt_ref[0] * y

    row = lambda c: pl.BlockSpec((tm, c), lambda i: (i, 0))
    return pl.pallas_call(
        body, name=name, grid=(t // tm,),
        in_specs=[row(di), row(di), pl.BlockSpec((di, d), lambda i: (0, 0)), row(d),
                  pl.BlockSpec((1, 1, d), lambda i: (i // per_seq, 0, 0))],
        out_specs=[row(d), row(d), row(di)],
        out_shape=[jax.ShapeDtypeStruct((t, d), F32), jax.ShapeDtypeStruct((t, d), F32),
                   jax.ShapeDtypeStruct((t, di), BF16)],
        compiler_params=_cp("parallel"),
    )(o, zf, w_out, x, gate)


def out_proj_bwd(dxo, y, gate, w_out, o, zf, seq, name):
    t, d = dxo.shape
    di = o.shape[1]
    tm = _tile(seq, 256)
    per_seq = seq // tm

    def body(dx_ref, y_ref, gt_ref, w_ref, o_ref, z_ref, do_ref, dz_ref, dy_ref, dgt_ref):
        dx = dx_ref[...]
        part = jnp.sum(dx * y_ref[...], axis=0, keepdims=True)

        @pl.when(pl.program_id(0) % per_seq == 0)
        def _():
            dgt_ref[0] = part

        @pl.when(pl.program_id(0) % per_seq != 0)
        def _():
            dgt_ref[0] += part

        dy = (dx * gt_ref[0]).astype(BF16)
        dy_ref[...] = dy
        du = lax.dot_general(dy, w_ref[...], (((1,), (1,)), ((), ())), preferred_element_type=F32)
        z = z_ref[...]
        sg = _sigmoid(z)
        do_ref[...] = (du * (z * sg)).astype(BF16)
        dz_ref[...] = (du * o_ref[...] * (sg * (1.0 + z * (1.0 - sg)))).astype(BF16)

    row = lambda c: pl.BlockSpec((tm, c), lambda i: (i, 0))
    seq_spec = pl.BlockSpec((1, 1, d), lambda i: (i // per_seq, 0, 0))
    return pl.pallas_call(
        body, name=name, grid=(t // tm,),
        in_specs=[row(d), row(d), seq_spec, pl.BlockSpec((di, d), lambda i: (0, 0)), row(di), row(di)],
        out_specs=[row(di), row(di), row(d), seq_spec],
        out_shape=[jax.ShapeDtypeStruct((t, di), BF16), jax.ShapeDtypeStruct((t, di), BF16),
                   jax.ShapeDtypeStruct((t, d), BF16), jax.ShapeDtypeStruct((t // seq, 1, d), F32)],
        compiler_params=_cp("arbitrary"),
    )(dxo, y, gate, w_out, o, zf)


def tn_matmul(a, b, name):
    t, m = a.shape
    n = b.shape[1]
    tn = _col_tile(n, 1024)
    tk = _tile(t, 1024)
    nk = t // tk

    def body(a_ref, b_ref, o_ref, acc):
        part = lax.dot_general(a_ref[...], b_ref[...], (((0,), (0,)), ((), ())), preferred_element_type=F32)
        k = pl.program_id(1)

        @pl.when(k == 0)
        def _():
            acc[...] = part

        @pl.when(k != 0)
        def _():
            acc[...] += part

        @pl.when(k == nk - 1)
        def _():
            o_ref[...] = acc[...].astype(BF16)

    return pl.pallas_call(
        body, name=name, grid=(n // tn, nk),
        in_specs=[pl.BlockSpec((tk, m), lambda j, k: (k, 0)), pl.BlockSpec((tk, tn), lambda j, k: (k, j))],
        out_specs=pl.BlockSpec((m, tn), lambda j, k: (0, j)),
        out_shape=jax.ShapeDtypeStruct((m, n), BF16),
        scratch_shapes=[pltpu.VMEM((m, tn), F32)],
        compiler_params=_cp("parallel", "arbitrary"),
    )(a, b)


def dh_norm_bwd(parts, w_t, extra, x, g, scale, dxo, seq, name, rider=None):
    t, d = x.shape
    nparts = len(parts)
    kw = parts[0].shape[1]
    tm = _tile(seq, 512)
    per_seq = seq // tm
    has_extra = extra is not None
    grid = (t // tm,)
    nr = rider.n if rider else 0
    n_in = nparts + 1 + (2 if has_extra else 0) + 4

    def body(*refs):
        p_refs = refs[:nparts]
        w_ref = refs[nparts]
        pos = nparts + 1
        if has_extra:
            e_ref, we_ref = refs[pos], refs[pos + 1]
            pos += 2
        x_ref, g_ref, sc_ref, dxo_ref = refs[pos:pos + 4]
        dx_ref, pp_ref, ss_ref = refs[n_in + nr:n_in + nr + 3]
        if rider:
            _ride(rider, refs[n_in:n_in + nr] + refs[n_in + nr + 3:], *_grid_ends(grid))
        i = pl.program_id(0)

        dh = jnp.dot(p_refs[0][...], w_ref[0:kw, :], preferred_element_type=F32)
        for kk in range(1, nparts):
            dh = dh + jnp.dot(p_refs[kk][...], w_ref[kk * kw:(kk + 1) * kw, :], preferred_element_type=F32)
        if has_extra:
            dh = dh + jnp.dot(e_ref[...], we_ref[...], preferred_element_type=F32)
        xx = x_ref[...]
        rstd = lax.rsqrt(jnp.mean(xx * xx, axis=-1, keepdims=True) + NORM_EPS)
        xhat = xx * rstd
        dxhat = dh * (g_ref[...] * (1.0 + sc_ref[0]))
        dx_ref[...] = dxo_ref[...] + rstd * (dxhat - xhat * jnp.mean(dxhat * xhat, axis=-1, keepdims=True))
        pp = jnp.sum(dh * xhat, axis=0, keepdims=True)
        ss = jnp.sum(dh, axis=0, keepdims=True)

        @pl.when(i % per_seq == 0)
        def _():
            pp_ref[0] = pp
            ss_ref[0] = ss

        @pl.when(i % per_seq != 0)
        def _():
            pp_ref[0] += pp
            ss_ref[0] += ss

    row = lambda c: pl.BlockSpec((tm, c), lambda i: (i, 0))
    whole = lambda a: pl.BlockSpec(a.shape, lambda i: (0, 0), pipeline_mode=pl.Buffered(1))
    seq_spec = pl.BlockSpec((1, 1, d), lambda i: (i // per_seq, 0, 0))
    in_specs = [row(kw)] * nparts + [whole(w_t)]
    args = list(parts) + [w_t]
    if has_extra:
        in_specs += [row(extra[0].shape[1]), whole(extra[1])]
        args += list(extra)
    in_specs += [row(d), pl.BlockSpec((1, d), lambda i: (0, 0)), seq_spec, row(d)]
    args += [x, g, scale, dxo]
    return pl.pallas_call(
        body, name=name, grid=grid,
        in_specs=in_specs + (rider.in_specs if rider else []),
        out_specs=[row(d), seq_spec, seq_spec] + (rider.out_specs if rider else []),
        out_shape=[jax.ShapeDtypeStruct((t, d), F32), jax.ShapeDtypeStruct((t // seq, 1, d), F32),
                   jax.ShapeDtypeStruct((t // seq, 1, d), F32)] + (rider.out_shape if rider else []),
        scratch_shapes=rider.scratch if rider else [],
        compiler_params=_cp("arbitrary"),
    )(*args, *(rider.arrs if rider else []))


def final_loss(x, g, target, name):
    t, d = x.shape
    tm = _tile(t, 512)

    def body(x_ref, g_ref, t_ref, dx_ref, dg_ref, sq_ref):
        xx = x_ref[...]
        gg = g_ref[...]
        rstd = lax.rsqrt(jnp.mean(xx * xx, axis=-1, keepdims=True) + NORM_EPS)
        xhat = xx * rstd
        err = xhat * gg - t_ref[...]
        dy = err * (1.0 / d)
        dxhat = dy * gg
        dx_ref[...] = rstd * (dxhat - xhat * jnp.mean(dxhat * xhat, axis=-1, keepdims=True))
        dg = jnp.sum(dy * xhat, axis=0, keepdims=True)
        sq = jnp.sum(err * err, axis=0, keepdims=True)

        @pl.when(pl.program_id(0) == 0)
        def _():
            dg_ref[...] = dg
            sq_ref[...] = sq

        @pl.when(pl.program_id(0) != 0)
        def _():
            dg_ref[...] += dg
            sq_ref[...] += sq

    row = pl.BlockSpec((tm, d), lambda i: (i, 0))
    vec = pl.BlockSpec((1, d), lambda i: (0, 0))
    return pl.pallas_call(
        body, name=name, grid=(t // tm,),
        in_specs=[row, vec, row], out_specs=[row, vec, vec],
        out_shape=[jax.ShapeDtypeStruct((t, d), F32), jax.ShapeDtypeStruct((1, d), F32),
                   jax.ShapeDtypeStruct((1, d), F32)],
        compiler_params=_cp("arbitrary"),
    )(x, g, target)


def _split3(x):
    x1 = x.astype(BF16)
    r1 = x - x1.astype(F32)
    x2 = r1.astype(BF16)
    x3 = (r1 - x2.astype(F32)).astype(BF16)
    return x1, x2, x3


def _dot3(x, u):
    x1, x2, x3 = _split3(x)
    return (jnp.dot(x1, u, preferred_element_type=F32) + jnp.dot(x2, u, preferred_element_type=F32)
            + jnp.dot(x3, u, preferred_element_type=F32))


def _log_sigmoid(x):
    return jnp.minimum(x, 0.0) - jnp.log1p(jnp.exp(-jnp.abs(x)))


def forget_cumsum(fl_t, b_f, name):
    b, h, s = fl_t.shape

    def body(f_ref, b_ref, o_ref):
        lf = _log_sigmoid(f_ref[0] + b_ref[...])
        u = (lax.broadcasted_iota(jnp.int32, (s, s), 0) <= lax.broadcasted_iota(jnp.int32, (s, s), 1)).astype(BF16)
        o_ref[0] = _dot3(lf, u)

    return pl.pallas_call(
        body, name=name, grid=(b,),
        in_specs=[pl.BlockSpec((1, h, s), lambda i: (i, 0, 0)), pl.BlockSpec((h, 1), lambda i: (0, 0))],
        out_specs=pl.BlockSpec((1, h, s), lambda i: (i, 0, 0)),
        out_shape=jax.ShapeDtypeStruct((b, h, s), F32),
        compiler_params=_cp("parallel"),
    )(fl_t, b_f)


def forget_cumsum_bwd(dcum, fl_t, b_f, name):
    b, h, s = fl_t.shape

    def body(d_ref, f_ref, b_ref, o_ref, db_ref):
        u = (lax.broadcasted_iota(jnp.int32, (s, s), 0) >= lax.broadcasted_iota(jnp.int32, (s, s), 1)).astype(BF16)
        dlf = _dot3(d_ref[0], u)
        df = dlf * _sigmoid(-(f_ref[0] + b_ref[...]))
        o_ref[0] = df
        part = jnp.sum(df, axis=-1, keepdims=True)

        @pl.when(pl.program_id(0) == 0)
        def _():
            db_ref[...] = part

        @pl.when(pl.program_id(0) != 0)
        def _():
            db_ref[...] += part

    blk = pl.BlockSpec((1, h, s), lambda i: (i, 0, 0))
    return pl.pallas_call(
        body, name=name, grid=(b,),
        in_specs=[blk, blk, pl.BlockSpec((h, 1), lambda i: (0, 0))],
        out_specs=[blk, pl.BlockSpec((h, 1), lambda i: (0, 0))],
        out_shape=[jax.ShapeDtypeStruct((b, h, s), F32), jax.ShapeDtypeStruct((h, 1), F32)],
        compiler_params=_cp("arbitrary"),
    )(dcum, fl_t, b_f)


def _nt(a, b):
    return lax.dot_general(a, b, (((1,), (1,)), ((), ())), preferred_element_type=F32)


def _attn_dims(t, seq, dh):
    tq = _tile(seq, 512)
    tk = _tile(seq, 256)
    return t // seq, tq, tk, seq // tq, tq // tk, 128 // dh


def _blocks_per_trip(r):
    return 2 if r % 2 == 0 else 1


def _scaled_q(q_ref, dh):
    scale = dh ** -0.5
    if math.log2(dh) % 2 == 0:
        return (q_ref[...].astype(F32) * scale).astype(BF16), None
    return q_ref[...], scale


def _diag_masks(tq, tk, r, strict):
    row = lax.broadcasted_iota(jnp.int32, (tq, tk), 0)
    col = lax.broadcasted_iota(jnp.int32, (tq, tk), 1)
    return [(col + d * tk < row) if strict else (col + d * tk <= row) for d in range(r)]


def _transposed(x):
    return jnp.transpose(x.astype(F32)).astype(BF16)


def _store_transposed(dst_ref, src_scr, tk):
    for jb in range(src_scr.shape[0]):
        dst_ref[jb * tk:(jb + 1) * tk, :] = jnp.transpose(src_scr[jb]).astype(BF16)


def _dot2(x, u):
    hi = x.astype(BF16)
    lo = (x - hi.astype(F32)).astype(BF16)
    return jnp.dot(jnp.concatenate([hi, lo], axis=1), jnp.concatenate([u, u], axis=0), preferred_element_type=F32)


def _attn_specs(seq, di, tq):
    nq = seq // tq
    cb = di // 128
    q_spec = pl.BlockSpec((tq, 128), lambda b, hp, i: (b * nq + i, hp))
    k_spec = pl.BlockSpec((seq, 128), lambda b, hp, i: (b, cb + hp))
    v_spec = pl.BlockSpec((seq, 128), lambda b, hp, i: (b, 2 * cb + hp))
    kv_out = pl.BlockSpec((seq, 128), lambda b, hp, i: (b, hp))
    return q_spec, k_spec, v_spec, kv_out


def _grid_ends(grid):
    ids = [pl.program_id(a) for a in range(len(grid))]
    first = functools.reduce(jnp.logical_and, [p == 0 for p in ids])
    last = functools.reduce(jnp.logical_and, [p == g - 1 for p, g in zip(ids, grid)])
    return first, last


def fox_fwd(qkv, ck, seq, di, dh, name, rider=None):
    t = qkv.shape[0]
    nb, tq, tk, nq, r, hpb = _attn_dims(t, seq, dh)
    per_trip = _blocks_per_trip(r)
    heads = [slice(hh * dh, (hh + 1) * dh) for hh in range(hpb)]
    q_spec, k_spec, v_spec, _ = _attn_specs(seq, di, tq)
    grid = (nb, di // 128, nq)
    nr = rider.n if rider else 0

    def body(q_ref, k_ref, v_ref, ck_ref, *rest):
        o_ref, lse_ref = rest[nr:nr + 2]
        if rider:
            _ride(rider, rest[:nr] + rest[nr + 2:], *_grid_ends(grid))
        i = pl.program_id(2)
        masks = _diag_masks(tq, tk, r, False)
        q, scale = _scaled_q(q_ref, dh)
        qs = [q[:, hd] for hd in heads]

        def scores(hh, j):
            start = pl.multiple_of(j * tk, tk)
            kh = k_ref[pl.ds(start, tk), heads[hh]]
            vh = v_ref[pl.ds(start, tk), heads[hh]]
            s = _nt(qs[hh], kh)
            if scale is not None:
                s = s * scale
            return s - ck_ref[0, hh, pl.ds(j, 1), :], vh

        def update(state, s, vh):
            m, l, acc = state
            m_new = jnp.maximum(m, jnp.max(s, axis=-1, keepdims=True))
            alpha = jnp.exp(m - m_new)
            p = jnp.exp(s - m_new)
            return m_new, alpha * l + jnp.sum(p, axis=-1, keepdims=True), alpha * acc + _dot2(p, vh)

        states = []
        for hh in range(hpb):
            s, vh = scores(hh, r * i)
            s = jnp.where(masks[0], s, -jnp.inf)
            m = jnp.max(s, axis=-1, keepdims=True)
            p = jnp.exp(s - m)
            states.append((m, jnp.sum(p, axis=-1, keepdims=True), _dot2(p, vh)))
        for d in range(1, r):
            for hh in range(hpb):
                s, vh = scores(hh, r * i + d)
                states[hh] = update(states[hh], jnp.where(masks[d], s, -jnp.inf), vh)

        def step(jj, states):
            for sub in range(per_trip):
                states = tuple(update(states[hh], *scores(hh, r * i - 1 - (per_trip * jj + sub))) for hh in range(hpb))
            return states

        states = lax.fori_loop(0, r * i // per_trip, step, tuple(states))
        for hh in range(hpb):
            m, l, acc = states[hh]
            o_ref[:, heads[hh]] = acc / l
            lse_ref[0, 0, :, hh:hh + 1] = m + jnp.log(l)

    return pl.pallas_call(
        body, name=name, grid=grid,
        in_specs=[q_spec, k_spec, v_spec,
                  pl.BlockSpec((1, hpb, seq // tk, tk), lambda b, hp, i: (b, hp, 0, 0))] + (rider.in_specs if rider else []),
        out_specs=[q_spec, pl.BlockSpec((1, 1, tq, hpb), lambda b, hp, i: (b, hp, i, 0))]
        + (rider.out_specs if rider else []),
        out_shape=[jax.ShapeDtypeStruct((t, di), F32), jax.ShapeDtypeStruct((nb, di // 128, seq, hpb), F32)]
        + (rider.out_shape if rider else []),
        scratch_shapes=rider.scratch if rider else [],
        compiler_params=_cp("arbitrary", "arbitrary", "arbitrary"),
    )(qkv, qkv, qkv, ck, *(rider.arrs if rider else []))


def fox_bwd(qkv, ck, o, do, lse, seq, di, dh, name, rider=None):
    t = qkv.shape[0]
    nb, tq, tk, nq, r, hpb = _attn_dims(t, seq, dh)
    per_trip = _blocks_per_trip(r)
    heads = [slice(hh * dh, (hh + 1) * dh) for hh in range(hpb)]
    q_spec, k_spec, v_spec, kv_out = _attn_specs(seq, di, tq)
    ck_spec = pl.BlockSpec((1, hpb, seq // tk, tk), lambda b, hp, i: (b, hp, 0, 0))
    grid = (nb, di // 128, nq)
    nr = rider.n if rider else 0

    def body(q_ref, k_ref, v_ref, ck_ref, o_ref, do_ref, lse_ref, *rest):
        dq_ref, dk_ref, dv_ref, dck_ref = rest[nr:nr + 4]
        dkt_scr, dvt_scr = rest[2 * nr + 4:2 * nr + 6]
        if rider:
            _ride(rider, rest[:nr] + rest[nr + 4:2 * nr + 4] + rest[2 * nr + 6:], *_grid_ends(grid))
        i = pl.program_id(2)

        @pl.when(i == 0)
        def _():
            dkt_scr[...] = jnp.zeros_like(dkt_scr)
            dvt_scr[...] = jnp.zeros_like(dvt_scr)
            dck_ref[...] = jnp.zeros_like(dck_ref)

        masks = _diag_masks(tq, tk, r, False)
        q, scale = _scaled_q(q_ref, dh)
        do = do_ref[...]
        q_t, do_t = _transposed(q), _transposed(do)
        qs = [q[:, hd] for hd in heads]
        dos = [do[:, hd] for hd in heads]
        deltas = [jnp.sum(dos[hh].astype(F32) * o_ref[:, heads[hh]], axis=-1, keepdims=True) for hh in range(hpb)]
        lses = [lse_ref[0, 0, :, hh:hh + 1] for hh in range(hpb)]

        def block(hh, j, mask, dq_acc):
            start = pl.multiple_of(j * tk, tk)
            kh = k_ref[pl.ds(start, tk), heads[hh]]
            vh = v_ref[pl.ds(start, tk), heads[hh]]
            s = _nt(qs[hh], kh)
            if scale is not None:
                s = s * scale
            p = jnp.exp(s - ck_ref[0, hh, pl.ds(j, 1), :] - lses[hh])
            if mask is not None:
                p = jnp.where(mask, p, 0.0)
            ds = p * (_nt(dos[hh], vh) - deltas[hh])
            dsb = ds.astype(BF16)
            dkt = jnp.dot(q_t[heads[hh], :], dsb, preferred_element_type=F32)
            if scale is not None:
                dkt = dkt * scale
            dkt_scr[j, heads[hh], :] += dkt
            dvt_scr[j, heads[hh], :] += jnp.dot(do_t[heads[hh], :], p.astype(BF16), preferred_element_type=F32)
            dck_ref[0, hh, pl.ds(j, 1), :] -= jnp.sum(ds, axis=0, keepdims=True)
            return dq_acc + jnp.dot(dsb, kh, preferred_element_type=F32)

        accs = [jnp.zeros((tq, dh), F32)] * hpb
        for d in range(r):
            accs = [block(hh, r * i + d, masks[d], accs[hh]) for hh in range(hpb)]
        def step(jj, accs):
            for sub in range(per_trip):
                accs = tuple(block(hh, per_trip * jj + sub, None, accs[hh]) for hh in range(hpb))
            return accs

        accs = lax.fori_loop(0, r * i // per_trip, step, tuple(accs))
        for hh in range(hpb):
            dq_ref[:, heads[hh]] = (accs[hh] * dh ** -0.5).astype(BF16)

        @pl.when(i == nq - 1)
        def _():
            _store_transposed(dk_ref, dkt_scr, tk)
            _store_transposed(dv_ref, dvt_scr, tk)

    return pl.pallas_call(
        body, name=name, grid=grid,
        in_specs=[q_spec, k_spec, v_spec, ck_spec, q_spec, q_spec,
                  pl.BlockSpec((1, 1, tq, hpb), lambda b, hp, i: (b, hp, i, 0))] + (rider.in_specs if rider else []),
        out_specs=[q_spec, kv_out, kv_out, ck_spec] + (rider.out_specs if rider else []),
        out_shape=[jax.ShapeDtypeStruct((t, di), BF16)] * 3 + [jax.ShapeDtypeStruct(ck.shape, F32)]
        + (rider.out_shape if rider else []),
        scratch_shapes=[pltpu.VMEM((seq // tk, 128, tk), F32), pltpu.VMEM((seq // tk, 128, tk), F32)]
        + (rider.scratch if rider else []),
        compiler_params=_cp("arbitrary", "arbitrary", "arbitrary"),
    )(qkv, qkv, qkv, ck, o, do, lse, *(rider.arrs if rider else []))


LOG2_E = 1.4426950408889634


def _sb_logits(qh, kh, scale, strict):
    z = _nt(qh, kh) * (LOG2_E if scale is None else scale * LOG2_E)
    e = jnp.exp2(-jnp.abs(z))
    lb = jnp.minimum(z, 0.0) - jnp.log2(1.0 + e)
    lk = lb - z
    return lb, lk, lk if strict is None else jnp.where(strict, lk, 0.0)


def _cum(x, u):
    return jnp.dot(x.astype(BF16), u, preferred_element_type=F32)


def sb_fwd(qkv, seq, di, dh, name):
    t = qkv.shape[0]
    nb, tq, tk, nq, r, hpb = _attn_dims(t, seq, dh)
    per_trip = _blocks_per_trip(r)
    heads = [slice(hh * dh, (hh + 1) * dh) for hh in range(hpb)]
    q_spec, k_spec, v_spec, _ = _attn_specs(seq, di, tq)

    def body(q_ref, k_ref, v_ref, o_ref, rt_ref):
        i = pl.program_id(2)
        masks = _diag_masks(tq, tk, r, True)
        row = lax.broadcasted_iota(jnp.int32, (tk, tk), 0)
        col = lax.broadcasted_iota(jnp.int32, (tk, tk), 1)
        u_after = (row > col).astype(BF16)
        q, scale = _scaled_q(q_ref, dh)
        qs = [q[:, hd] for hd in heads]

        def block(hh, j, mask, state):
            rr, acc = state
            start = pl.multiple_of(j * tk, tk)
            kh = k_ref[pl.ds(start, tk), heads[hh]]
            vh = v_ref[pl.ds(start, tk), heads[hh]]
            lb, _, lk = _sb_logits(qs[hh], kh, scale, mask)
            a = jnp.exp2(lb + _cum(lk, u_after) + rr)
            if mask is not None:
                a = jnp.where(mask, a, 0.0)
            acc = acc + jnp.dot(a.astype(BF16), vh, preferred_element_type=F32)
            return rr + jnp.sum(lk, axis=-1, keepdims=True), acc

        states = [(jnp.zeros((tq, 1), F32), jnp.zeros((tq, dh), F32))] * hpb
        for d in reversed(range(r)):
            states = [block(hh, r * i + d, masks[d], states[hh]) for hh in range(hpb)]
        def step(jj, states):
            for sub in range(per_trip):
                states = tuple(block(hh, r * i - 1 - (per_trip * jj + sub), None, states[hh]) for hh in range(hpb))
            return states

        states = lax.fori_loop(0, r * i // per_trip, step, tuple(states))
        for hh in range(hpb):
            o_ref[:, heads[hh]] = states[hh][1]
            rt_ref[0, 0, :, hh:hh + 1] = states[hh][0]

    return pl.pallas_call(
        body, name=name, grid=(nb, di // 128, nq),
        in_specs=[q_spec, k_spec, v_spec],
        out_specs=[q_spec, pl.BlockSpec((1, 1, tq, hpb), lambda b, hp, i: (b, hp, i, 0))],
        out_shape=[jax.ShapeDtypeStruct((t, di), F32), jax.ShapeDtypeStruct((nb, di // 128, seq, hpb), F32)],
        compiler_params=_cp("parallel", "parallel", "arbitrary"),
    )(qkv, qkv, qkv)


def sb_bwd(qkv, do, rtot, seq, di, dh, name):
    t = qkv.shape[0]
    nb, tq, tk, nq, r, hpb = _attn_dims(t, seq, dh)
    per_trip = _blocks_per_trip(r)
    heads = [slice(hh * dh, (hh + 1) * dh) for hh in range(hpb)]
    q_spec, k_spec, v_spec, kv_out = _attn_specs(seq, di, tq)

    def body(q_ref, k_ref, v_ref, do_ref, rt_ref, dq_ref, dk_ref, dv_ref, dkt_scr, dvt_scr):
        i = pl.program_id(2)

        @pl.when(i == 0)
        def _():
            dkt_scr[...] = jnp.zeros_like(dkt_scr)
            dvt_scr[...] = jnp.zeros_like(dvt_scr)

        masks = _diag_masks(tq, tk, r, True)
        row = lax.broadcasted_iota(jnp.int32, (tk, tk), 0)
        col = lax.broadcasted_iota(jnp.int32, (tk, tk), 1)
        u_after = (row > col).astype(BF16)
        u_before = (row < col).astype(BF16)
        q, scale = _scaled_q(q_ref, dh)
        do = do_ref[...]
        q_t, do_t = _transposed(q), _transposed(do)
        qs = [q[:, hd] for hd in heads]
        dos = [do[:, hd] for hd in heads]
        rts = [rt_ref[0, 0, :, hh:hh + 1] for hh in range(hpb)]

        def block(hh, j, mask, state):
            lc, gc, dq_acc = state
            start = pl.multiple_of(j * tk, tk)
            kh = k_ref[pl.ds(start, tk), heads[hh]]
            vh = v_ref[pl.ds(start, tk), heads[hh]]
            lb, lk_all, lk = _sb_logits(qs[hh], kh, scale, mask)
            lc = lc + jnp.sum(lk, axis=-1, keepdims=True)
            a = jnp.exp2(lb + ((rts[hh] - lc) + _cum(lk, u_after)))
            if mask is not None:
                a = jnp.where(mask, a, 0.0)
            de = a * _nt(dos[hh], vh)
            g = gc + _cum(de, u_before)
            dz = de * jnp.exp2(lk_all) - g * jnp.exp2(lb)
            if mask is not None:
                dz = jnp.where(mask, dz, 0.0)
            dzb = dz.astype(BF16)
            dkt = jnp.dot(q_t[heads[hh], :], dzb, preferred_element_type=F32)
            if scale is not None:
                dkt = dkt * scale
            dkt_scr[j, heads[hh], :] += dkt
            dvt_scr[j, heads[hh], :] += jnp.dot(do_t[heads[hh], :], a.astype(BF16), preferred_element_type=F32)
            return (lc, gc + jnp.sum(de, axis=-1, keepdims=True),
                    dq_acc + jnp.dot(dzb, kh, preferred_element_type=F32))

        zero = jnp.zeros((tq, 1), F32)
        def step(jj, states):
            for sub in range(per_trip):
                states = tuple(block(hh, per_trip * jj + sub, None, states[hh]) for hh in range(hpb))
            return states

        states = lax.fori_loop(0, r * i // per_trip, step, ((zero, zero, jnp.zeros((tq, dh), F32)),) * hpb)
        for d in range(r):
            states = [block(hh, r * i + d, masks[d], states[hh]) for hh in range(hpb)]
        for hh in range(hpb):
            dq_ref[:, heads[hh]] = (states[hh][2] * dh ** -0.5).astype(BF16)

        @pl.when(i == nq - 1)
        def _():
            _store_transposed(dk_ref, dkt_scr, tk)
            _store_transposed(dv_ref, dvt_scr, tk)

    return pl.pallas_call(
        body, name=name, grid=(nb, di // 128, nq),
        in_specs=[q_spec, k_spec, v_spec, q_spec, pl.BlockSpec((1, 1, tq, hpb), lambda b, hp, i: (b, hp, i, 0))],
        out_specs=[q_spec, kv_out, kv_out],
        out_shape=[jax.ShapeDtypeStruct((t, di), BF16)] * 3,
        scratch_shapes=[pltpu.VMEM((seq // tk, 128, tk), F32), pltpu.VMEM((seq // tk, 128, tk), F32)],
        compiler_params=_cp("parallel", "parallel", "arbitrary"),
    )(qkv, qkv, qkv, do, rtot)


def _cols(g):
    return jnp.transpose(g, (1, 0, 2)).reshape(g.shape[1], NDEV * g.shape[2])


def _col_blocks(w):
    r, c8 = w.shape
    return jnp.transpose(w.reshape(r, NDEV, c8 // NDEV), (1, 0, 2))


def _pad_rows16(a):
    return jnp.pad(a, ((0, 16 - a.shape[0]), (0, 0)))


def _pad_cols(a, n):
    return jnp.pad(a, ((0, 0), (0, n - a.shape[1])))


def kernel(x, c, fox_norm_g, fox_w_ada, fox_b_ada, fox_w_in, fox_b_f, fox_w_out, sb_norm_g, sb_w_ada, sb_b_ada, sb_w_in, sb_w_out, final_norm_g, loss_target, m_fox_norm_g, m_fox_w_ada, m_fox_b_ada, m_fox_w_in, m_fox_b_f, m_fox_w_out, m_sb_norm_g, m_sb_w_ada, m_sb_b_ada, m_sb_w_in, m_sb_w_out, m_final_norm_g, v_fox_norm_g, v_fox_w_ada, v_fox_b_ada, v_fox_w_in, v_fox_b_f, v_fox_w_out, v_sb_norm_g, v_sb_w_ada, v_sb_b_ada, v_sb_w_in, v_sb_w_out, v_final_norm_g):
    nb, seq, d = x.shape
    t = nb * seq
    h = fox_b_f.shape[-1]
    di = fox_w_out.shape[1] * NDEV
    dh = di // h
    tq = _tile(seq, 256)
    me = _my_index()

    gathered = all_gather([w[0].astype(BF16) for w in (fox_w_ada, fox_w_in, fox_w_out)], "gather_fox_weights")
    fox_wada, fox_win = _cols(gathered[0]), _cols(gathered[1])
    fox_wout = gathered[2].reshape(di, d)
    sb_gather = Exchange("gather", [w[0].astype(BF16) for w in (sb_w_ada, sb_w_in, sb_w_out)] + [sb_norm_g, sb_b_ada])

    x0 = x.reshape(t, d)
    target = loss_target.reshape(t, d)
    c16 = _pad_rows16(c)

    def layer_fwd(xin, g, wada, bada, win, wout, b_f, tag, rider=None):
        mod = adaln_fwd(c16, wada, bada, tag + "_adaln")[:nb]
        shift, scale, gate = (mod[:, k * d:(k + 1) * d].reshape(nb, 1, d) for k in range(3))
        qkv, hmod = normmod_matmul(xin, g, scale, shift, win[:, :3 * di], BF16, seq, True, tag + "_qkv")
        w_z = win[:, 3 * di:]
        if b_f is not None:
            w_z = _pad_cols(w_z, di + 128)
        (zf,) = normmod_matmul(xin, g, scale, shift, w_z, F32, seq, False, tag + "_z")
        saved = dict(x=xin, g=g, scale=scale, gate=gate, qkv=qkv, h=hmod, zf=zf, win=win, wout=wout)
        if b_f is not None:
            fl_t = jnp.transpose(zf[:, di:di + h].reshape(nb, seq, h), (0, 2, 1))
            bf_col = b_f.reshape(h, 1)
            cum = forget_cumsum(fl_t, bf_col, tag + "_cum")
            ck = cum.reshape(nb, h, seq // tq, tq)
            o, lse, *rode = fox_fwd(qkv, ck, seq, di, dh, tag + "_attn", rider)
            saved.update(fl_t=fl_t, bf_col=bf_col, ck=ck, lse=lse, rode=rode)
        else:
            o, rtot = sb_fwd(qkv, seq, di, dh, tag + "_attn")
            saved.update(rtot=rtot)
        xout, y, u = out_proj_fwd(o, zf, wout, xin, gate, seq, tag + "_out")
        saved.update(o=o, y=y, u=u)
        return xout, saved

    def layer_bwd(dxo, sv, is_fox, tag, rider=None):
        do, dz, dy, dgate = out_proj_bwd(dxo, sv["y"], sv["gate"], sv["wout"], sv["o"], sv["zf"], seq, tag + "_dout")
        dwout = tn_matmul(sv["u"], dy, tag + "_dwout")
        extra = None
        if is_fox:
            dq, dk, dv, dck, *rode = fox_bwd(sv["qkv"], sv["ck"], sv["o"], do, sv["lse"], seq, di, dh, tag + "_dattn",
                                             rider)
            df_t, dbf = forget_cumsum_bwd(dck.reshape(nb, h, seq), sv["fl_t"], sv["bf_col"], tag + "_dcum")
            df = _pad_cols(jnp.transpose(df_t, (0, 2, 1)).reshape(t, h), 128).astype(BF16)
            extra = (df, jnp.transpose(_pad_cols(sv["win"][:, 4 * di:], 128)))
        else:
            dq, dk, dv = sb_bwd(sv["qkv"], do, sv["rtot"], seq, di, dh, tag + "_dattn")
        parts = [dq, dk, dv, dz]
        dwin = [tn_matmul(sv["h"], p, tag + "_dwin%d" % k) for k, p in enumerate(parts)]
        if is_fox:
            dwin.append(tn_matmul(sv["h"], df, tag + "_dwinf")[:, :h])
        dwin = jnp.concatenate(dwin, axis=1)
        w_t = jnp.transpose(sv["win"][:, :4 * di])
        win_blocks, wout_blocks = _col_blocks(dwin), dwout.reshape(NDEV, di // NDEV, d)
        dh_rider = Exchange("scatter", [win_blocks, wout_blocks]) if is_fox else None
        dxin, psum, ssum, *landed = dh_norm_bwd(parts, w_t, extra, sv["x"], sv["g"], sv["scale"], dxo, seq,
                                                tag + "_dh", dh_rider)
        pad = lambda a: _pad_rows16(a.reshape(nb, d))
        dwada, dbada, dng = adaln_bwd(c16, pad(ssum), pad(psum), pad(dgate), pad(sv["scale"]), sv["g"], tag + "_dadaln")
        grads = dict(wada=dwada, bada=dbada, ng=dng, win=win_blocks, wout=wout_blocks)
        if is_fox:
            grads["bf"] = dbf.reshape(1, h)
            grads["rode"] = rode
            grads["landed"] = landed
        return dxin, grads

    x1, sv_fox = layer_fwd(x0, fox_norm_g, fox_wada, fox_b_ada, fox_win, fox_wout, fox_b_f, "fox", sb_gather)
    sb_wada, sb_win = _cols(sv_fox["rode"][0]), _cols(sv_fox["rode"][1])
    sb_wout = sv_fox["rode"][2].reshape(di, d)
    sb_g, sb_bada = _cols(sv_fox["rode"][3]), _cols(sv_fox["rode"][4])
    x2, sv_sb = layer_fwd(x1, sb_g, sb_wada, sb_bada, sb_win, sb_wout, None, "sb")
    dx2, dgf, sq = final_loss(x2, final_norm_g.reshape(1, d), target, "loss_head")
    dx1, g_sb = layer_bwd(dx2, sv_sb, False, "sb")
    dx0, g_fox = layer_bwd(dx1, sv_fox, True, "fox", Exchange("scatter", [g_sb["wada"], g_sb["win"], g_sb["wout"]]))

    land = list(all_to_all([g_fox["wada"]], "scatter_fox_wada_grad")) + list(g_fox["landed"]) + list(g_fox["rode"])
    small = jnp.concatenate([g_fox["ng"], g_fox["bada"], _pad_cols(g_fox["bf"], 128), g_sb["ng"], g_sb["bada"],
                             dgf, sq], axis=1)
    (small_all,) = all_gather([small], "gather_small")
    small_sum, loss_row = sum_slots(small_all, d, "sum_small")
    loss = loss_row[0, 0]

    offs = {}
    pos = 0
    for nm, width in (("fox_ng", d), ("fox_bada", 3 * d), ("fox_bf", 128), ("sb_ng", d), ("sb_bada", 3 * d), ("fin_g", d)):
        offs[nm] = (pos, width)
        pos += width

    def small_grad(nm, width=None, shard=False):
        p0, wd = offs[nm]
        wd = width or wd
        if shard:
            blk = wd // NDEV
            return lax.dynamic_slice(small_sum, (0, p0 + me * blk), (1, blk))
        return small_sum[:, p0:p0 + wd]

    results = {}
    big_params = [("fox_w_ada", fox_w_ada, m_fox_w_ada, v_fox_w_ada), ("fox_w_in", fox_w_in, m_fox_w_in, v_fox_w_in),
                  ("fox_w_out", fox_w_out, m_fox_w_out, v_fox_w_out), ("sb_w_ada", sb_w_ada, m_sb_w_ada, v_sb_w_ada),
                  ("sb_w_in", sb_w_in, m_sb_w_in, v_sb_w_in), ("sb_w_out", sb_w_out, m_sb_w_out, v_sb_w_out)]
    for k, (nm, w, m, v) in enumerate(big_params):
        outs = adamw(land[k], w[0], m[0], v[0], "adamw_" + nm)
        results[nm] = [o[None] for o in outs]
    small_params = [("fox_norm_g", fox_norm_g, m_fox_norm_g, v_fox_norm_g, small_grad("fox_ng")),
                    ("fox_b_ada", fox_b_ada, m_fox_b_ada, v_fox_b_ada, small_grad("fox_bada")),
                    ("fox_b_f", fox_b_f, m_fox_b_f, v_fox_b_f, small_grad("fox_bf", h)),
                    ("sb_norm_g", sb_norm_g, m_sb_norm_g, v_sb_norm_g, small_grad("sb_ng", shard=True)),
                    ("sb_b_ada", sb_b_ada, m_sb_b_ada, v_sb_b_ada, small_grad("sb_bada", shard=True)),
                    ("final_norm_g", final_norm_g.reshape(1, d), m_final_norm_g.reshape(1, d),
                     v_final_norm_g.reshape(1, d), small_grad("fin_g"))]
    for nm, w, m, v, g in small_params:
        outs = adamw(g[None], w, m, v, "adamw_" + nm)
        if nm == "final_norm_g":
            outs = [o.reshape(d) for o in outs]
        results[nm] = outs

    order = ["fox_norm_g", "fox_w_ada", "fox_b_ada", "fox_w_in", "fox_b_f", "fox_w_out", "sb_norm_g", "sb_w_ada",
             "sb_b_ada", "sb_w_in", "sb_w_out", "final_norm_g"]
    out = [loss, dx0.reshape(nb, seq, d)]
    for k in range(4):
        out += [results[nm][k] for nm in order]
    return tuple(out)
```

```python
import functools
import math

import jax
import jax.numpy as jnp
from jax import lax
from jax.experimental import pallas as pl
from jax.experimental.pallas import tpu as pltpu

F32 = jnp.float32
BF16 = jnp.bfloat16
NDEV = 8
VMEM_LIMIT = 56 * 1024 * 1024
NORM_EPS = 1e-6
ADAM_LR, ADAM_B1, ADAM_B2, ADAM_EPS, ADAM_WD, ADAM_STEP = 0.001, 0.9, 0.999, 1e-08, 0.01, 10
ATTN_LANES = 128
ATTN_BLOCKS_PER_TRIP = 2
MESH = pl.DeviceIdType.MESH
ANY = pl.BlockSpec(memory_space=pl.ANY)


def _cp(*sem):
    return pltpu.CompilerParams(dimension_semantics=sem, vmem_limit_bytes=VMEM_LIMIT)


def _my_index():
    return 4 * lax.axis_index("x") + 2 * lax.axis_index("y") + lax.axis_index("c")


def _flip(k):
    x, y, c = lax.axis_index("x"), lax.axis_index("y"), lax.axis_index("c")
    kx, ky, kc = (k >> 2) & 1, (k >> 1) & 1, k & 1
    px = 1 - x if kx else x
    py = 1 - y if ky else y
    pc = 1 - c if kc else c
    return (px, py, pc), 4 * px + 2 * py + pc


class Exchange:
    def __init__(self, mode, arrs):
        self.gather = mode == "gather"
        self.arrs = list(arrs)
        self.n = len(self.arrs)
        self.in_specs = [ANY] * self.n
        self.out_specs = [ANY] * self.n
        self.out_shape = [jax.ShapeDtypeStruct(((NDEV,) + a.shape) if self.gather else a.shape, a.dtype)
                          for a in self.arrs]
        self.scratch = [pltpu.SemaphoreType.DMA((self.n * (NDEV - 1),)), pltpu.SemaphoreType.DMA((self.n * (NDEV - 1),)),
                        pltpu.SemaphoreType.DMA((self.n,))]

    def copies(self, ins, outs, sems):
        send_sems, recv_sems, local_sems = sems
        me = _my_index()
        out = []
        for a in range(self.n):
            out.append(pltpu.make_async_copy(ins[a] if self.gather else ins[a].at[me], outs[a].at[me], local_sems.at[a]))
            for k in range(1, NDEV):
                peer, pidx = _flip(k)
                out.append(pltpu.make_async_remote_copy(
                    src_ref=ins[a] if self.gather else ins[a].at[pidx], dst_ref=outs[a].at[me],
                    send_sem=send_sems.at[a * (NDEV - 1) + k - 1], recv_sem=recv_sems.at[a * (NDEV - 1) + k - 1],
                    device_id=peer, device_id_type=MESH))
        return out

    def run(self, name):
        n = self.n

        def body(*refs):
            cps = self.copies(refs[:n], refs[n:2 * n], refs[2 * n:])
            for cp in cps:
                cp.start()
            for cp in cps:
                cp.wait()

        return pl.pallas_call(body, name=name, out_shape=self.out_shape, in_specs=self.in_specs,
                              out_specs=self.out_specs, scratch_shapes=self.scratch)(*self.arrs)


def _ride(exchange, refs, first, last):
    n = exchange.n
    cps = exchange.copies(refs[:n], refs[n:2 * n], refs[2 * n:])

    @pl.when(first)
    def _():
        for cp in cps:
            cp.start()

    @pl.when(last)
    def _():
        for cp in cps:
            cp.wait()


def all_gather(arrs, name):
    return Exchange("gather", arrs).run(name)


def all_to_all(arrs, name):
    return Exchange("scatter", arrs).run(name)


def _tile(n, pref):
    t = min(n, pref)
    while n % t:
        t //= 2
    return t


def _col_tile(n, pref):
    t = _tile(n, pref)
    return t if t % 128 == 0 and t >= 512 else n


def adamw(land, w, m, v, name):
    slots, r, c = land.shape
    tr = _tile(r, 64)
    bc1 = 1.0 - ADAM_B1 ** ADAM_STEP
    bc2 = 1.0 - ADAM_B2 ** ADAM_STEP

    def body(land_ref, w_ref, m_ref, v_ref, g_ref, d_ref, nm_ref, nv_ref):
        g = land_ref[0].astype(F32)
        for s in range(1, slots):
            g = g + land_ref[s].astype(F32)
        nm = ADAM_B1 * m_ref[...] + (1.0 - ADAM_B1) * g
        nv = ADAM_B2 * v_ref[...] + (1.0 - ADAM_B2) * (g * g)
        m_hat = nm / bc1
        v_hat = nv / bc2
        g_ref[...] = g
        nm_ref[...] = nm
        nv_ref[...] = nv
        d_ref[...] = -ADAM_LR * (m_hat / (jnp.sqrt(v_hat) + ADAM_EPS) + ADAM_WD * w_ref[...])

    blk = pl.BlockSpec((tr, c), lambda i: (i, 0))
    return pl.pallas_call(
        body, name=name, grid=(r // tr,),
        in_specs=[pl.BlockSpec((slots, tr, c), lambda i: (0, i, 0)), blk, blk, blk],
        out_specs=[blk] * 4,
        out_shape=[jax.ShapeDtypeStruct((r, c), F32)] * 4,
        compiler_params=_cp("parallel"),
    )(land, w, m, v)


def sum_slots(land, d_model, name):
    slots, _, n = land.shape

    def body(land_ref, o_ref, loss_ref):
        g = land_ref[0]
        for s in range(1, slots):
            g = g + land_ref[s]
        o_ref[...] = g
        sq = jnp.sum(g[:, n - d_model:], axis=-1, keepdims=True)
        loss_ref[...] = jnp.broadcast_to(sq * (0.5 / d_model), (1, 128))

    return pl.pallas_call(
        body, name=name,
        out_shape=[jax.ShapeDtypeStruct((1, n), F32), jax.ShapeDtypeStruct((1, 128), F32)],
    )(land)


def _sigmoid(x):
    return 1.0 / (1.0 + jnp.exp(-x))


def adaln_fwd(c16, w_ada, b_ada, name):
    d3 = w_ada.shape[1]

    def body(c_ref, w_ref, b_ref, o_ref):
        cc = c_ref[...]
        sc = (cc * _sigmoid(cc)).astype(BF16)
        o_ref[...] = jnp.dot(sc, w_ref[...], preferred_element_type=F32) + b_ref[...]

    return pl.pallas_call(body, name=name, out_shape=jax.ShapeDtypeStruct((16, d3), F32),
                          compiler_params=pltpu.CompilerParams(vmem_limit_bytes=VMEM_LIMIT))(c16, w_ada, b_ada)


def adaln_bwd(c16, dshift16, p16, dgate16, scale16, g, name):
    d = c16.shape[1]
    nb = 3 * d // NDEV

    def body(c_ref, ds_ref, p_ref, dg_ref, sc_ref, g_ref, dw_ref, db_ref, dng_ref, dmod_scr):
        j = pl.program_id(0)

        @pl.when(j == 0)
        def _():
            p = p_ref[...]
            dmod = jnp.concatenate([ds_ref[...], p * g_ref[...], dg_ref[...]], axis=-1)
            dmod_scr[...] = dmod
            db_ref[...] = jnp.sum(dmod, axis=0, keepdims=True)
            dng_ref[...] = jnp.sum((1.0 + sc_ref[...]) * p, axis=0, keepdims=True)

        cc = c_ref[...]
        sc = (cc * _sigmoid(cc)).astype(BF16)
        start = pl.multiple_of(j * nb, 128)
        dm = dmod_scr[:, pl.ds(start, nb)].astype(BF16)
        dw_ref[0] = lax.dot_general(sc, dm, (((0,), (0,)), ((), ())), preferred_element_type=F32).astype(BF16)

    full = lambda shape: pl.BlockSpec(shape, lambda j: (0,) * len(shape))
    return pl.pallas_call(
        body, name=name, grid=(NDEV,),
        in_specs=[full((16, d))] * 5 + [full((1, d))],
        out_specs=[pl.BlockSpec((1, d, nb), lambda j: (j, 0, 0)), full((1, 3 * d)), full((1, d))],
        out_shape=[jax.ShapeDtypeStruct((NDEV, d, nb), BF16), jax.ShapeDtypeStruct((1, 3 * d), F32),
                   jax.ShapeDtypeStruct((1, d), F32)],
        scratch_shapes=[pltpu.VMEM((16, 3 * d), F32)],
        compiler_params=_cp("arbitrary"),
    )(c16, dshift16, p16, dgate16, scale16, g)


def _modulated_norm(x, g, scale, shift):
    rstd = lax.rsqrt(jnp.mean(x * x, axis=-1, keepdims=True) + NORM_EPS)
    return ((x * rstd) * g) * (1.0 + scale) + shift


def normmod_matmul(x, g, scale, shift, w, out_dtype, seq, emit_h, name):
    t, d = x.shape
    n = w.shape[1]
    tm = _tile(seq, 512)
    tn = _col_tile(n, 2048)
    per_seq = seq // tm

    def body(x_ref, g_ref, sc_ref, sh_ref, w_ref, *rest):
        if emit_h:
            o_ref, h_ref, h_scr = rest
        else:
            o_ref, h_scr = rest

        @pl.when(pl.program_id(1) == 0)
        def _():
            h = _modulated_norm(x_ref[...], g_ref[...], sc_ref[0], sh_ref[0]).astype(BF16)
            h_scr[...] = h
            if emit_h:
                h_ref[...] = h

        o_ref[...] = jnp.dot(h_scr[...], w_ref[...], preferred_element_type=F32).astype(out_dtype)

    mod_spec = pl.BlockSpec((1, 1, d), lambda i, j: (i // per_seq, 0, 0))
    out_specs = [pl.BlockSpec((tm, tn), lambda i, j: (i, j))]
    out_shape = [jax.ShapeDtypeStruct((t, n), out_dtype)]
    if emit_h:
        out_specs.append(pl.BlockSpec((tm, d), lambda i, j: (i, 0)))
        out_shape.append(jax.ShapeDtypeStruct((t, d), BF16))
    return pl.pallas_call(
        body, name=name, grid=(t // tm, n // tn),
        in_specs=[pl.BlockSpec((tm, d), lambda i, j: (i, 0)), pl.BlockSpec((1, d), lambda i, j: (0, 0)),
                  mod_spec, mod_spec, pl.BlockSpec((d, tn), lambda i, j: (0, j))],
        out_specs=out_specs, out_shape=out_shape,
        scratch_shapes=[pltpu.VMEM((tm, d), BF16)],
        compiler_params=_cp("parallel", "arbitrary"),
    )(x, g, scale, shift, w)


def out_proj_fwd(o, zf, w_out, x, gate, seq, name):
    t, di = o.shape
    d = x.shape[1]
    tm = _tile(seq, 256)
    per_seq = seq // tm

    def body(o_ref, z_ref, w_ref, x_ref, gt_ref, xn_ref, y_ref, u_ref):
        z = z_ref[...]
        u = (o_ref[...] * (z * _sigmoid(z))).astype(BF16)
        y = jnp.dot(u, w_ref[...], preferred_element_type=F32)
        u_ref[...] = u
        y_ref[...] = y
        xn_ref[...] = x_ref[...] + gt_ref[0] * y

    row = lambda c: pl.BlockSpec((tm, c), lambda i: (i, 0))
    return pl.pallas_call(
        body, name=name, grid=(t // tm,),
        in_specs=[row(di), row(di), pl.BlockSpec((di, d), lambda i: (0, 0)), row(d),
                  pl.BlockSpec((1, 1, d), lambda i: (i // per_seq, 0, 0))],
        out_specs=[row(d), row(d), row(di)],
        out_shape=[jax.ShapeDtypeStruct((t, d), F32), jax.ShapeDtypeStruct((t, d), F32),
                   jax.ShapeDtypeStruct((t, di), BF16)],
        compiler_params=_cp("parallel"),
    )(o, zf, w_out, x, gate)


def out_proj_bwd(dxo, y, gate, w_out, o, zf, seq, name):
    t, d = dxo.shape
    di = o.shape[1]
    tm = _tile(seq, 256)
    per_seq = seq // tm

    def body(dx_ref, y_ref, gt_ref, w_ref, o_ref, z_ref, do_ref, dz_ref, dy_ref, dgt_ref):
        dx = dx_ref[...]
        part = jnp.sum(dx * y_ref[...], axis=0, keepdims=True)

        @pl.when(pl.program_id(0) % per_seq == 0)
        def _():
            dgt_ref[0] = part

        @pl.when(pl.program_id(0) % per_seq != 0)
        def _():
            dgt_ref[0] += part

        dy = (dx * gt_ref[0]).astype(BF16)
        dy_ref[...] = dy
        du = lax.dot_general(dy, w_ref[...], (((1,), (1,)), ((), ())), preferred_element_type=F32)
        z = z_ref[...]
        sg = _sigmoid(z)
        do_ref[...] = (du * (z * sg)).astype(BF16)
        dz_ref[...] = (du * o_ref[...] * (sg * (1.0 + z * (1.0 - sg)))).astype(BF16)

    row = lambda c: pl.BlockSpec((tm, c), lambda i: (i, 0))
    seq_spec = pl.BlockSpec((1, 1, d), lambda i: (i // per_seq, 0, 0))
    return pl.pallas_call(
        body, name=name, grid=(t // tm,),
        in_specs=[row(d), row(d), seq_spec, pl.BlockSpec((di, d), lambda i: (0, 0)), row(di), row(di)],
        out_specs=[row(di), row(di), row(d), seq_spec],
        out_shape=[jax.ShapeDtypeStruct((t, di), BF16), jax.ShapeDtypeStruct((t, di), BF16),
                   jax.ShapeDtypeStruct((t, d), BF16), jax.ShapeDtypeStruct((t // seq, 1, d), F32)],
        compiler_params=_cp("arbitrary"),
    )(dxo, y, gate, w_out, o, zf)


def tn_matmul(a, b, name):
    t, m = a.shape
    n = b.shape[1]
    tn = _col_tile(n, 1024)
    tk = _tile(t, 1024)
    nk = t // tk

    def body(a_ref, b_ref, o_ref, acc):
        part = lax.dot_general(a_ref[...], b_ref[...], (((0,), (0,)), ((), ())), preferred_element_type=F32)
        k = pl.program_id(1)

        @pl.when(k == 0)
        def _():
            acc[...] = part

        @pl.when(k != 0)
        def _():
            acc[...] += part

        @pl.when(k == nk - 1)
        def _():
            o_ref[...] = acc[...].astype(BF16)

    return pl.pallas_call(
        body, name=name, grid=(n // tn, nk),
        in_specs=[pl.BlockSpec((tk, m), lambda j, k: (k, 0)), pl.BlockSpec((tk, tn), lambda j, k: (k, j))],
        out_specs=pl.BlockSpec((m, tn), lambda j, k: (0, j)),
        out_shape=jax.ShapeDtypeStruct((m, n), BF16),
        scratch_shapes=[pltpu.VMEM((m, tn), F32)],
        compiler_params=_cp("parallel", "arbitrary"),
    )(a, b)


def dh_norm_bwd(parts, w_t, extra, x, g, scale, dxo, seq, name, rider=None):
    t, d = x.shape
    nparts = len(parts)
    kw = parts[0].shape[1]
    tm = _tile(seq, 512)
    per_seq = seq // tm
    has_extra = extra is not None
    grid = (t // tm,)
    nr = rider.n if rider else 0
    n_in = nparts + 1 + (2 if has_extra else 0) + 4

    def body(*refs):
        p_refs = refs[:nparts]
        w_ref = refs[nparts]
        pos = nparts + 1
        if has_extra:
            e_ref, we_ref = refs[pos], refs[pos + 1]
            pos += 2
        x_ref, g_ref, sc_ref, dxo_ref = refs[pos:pos + 4]
        dx_ref, pp_ref, ss_ref = refs[n_in + nr:n_in + nr + 3]
        if rider:
            _ride(rider, refs[n_in:n_in + nr] + refs[n_in + nr + 3:], *_grid_ends(grid))
        i = pl.program_id(0)

        dh = jnp.dot(p_refs[0][...], w_ref[0:kw, :], preferred_element_type=F32)
        for kk in range(1, nparts):
            dh = dh + jnp.dot(p_refs[kk][...], w_ref[kk * kw:(kk + 1) * kw, :], preferred_element_type=F32)
        if has_extra:
            dh = dh + jnp.dot(e_ref[...], we_ref[...], preferred_element_type=F32)
        xx = x_ref[...]
        rstd = lax.rsqrt(jnp.mean(xx * xx, axis=-1, keepdims=True) + NORM_EPS)
        xhat = xx * rstd
        dxhat = dh * (g_ref[...] * (1.0 + sc_ref[0]))
        dx_ref[...] = dxo_ref[...] + rstd * (dxhat - xhat * jnp.mean(dxhat * xhat, axis=-1, keepdims=True))
        pp = jnp.sum(dh * xhat, axis=0, keepdims=True)
        ss = jnp.sum(dh, axis=0, keepdims=True)

        @pl.when(i % per_seq == 0)
        def _():
            pp_ref[0] = pp
            ss_ref[0] = ss

        @pl.when(i % per_seq != 0)
        def _():
            pp_ref[0] += pp
            ss_ref[0] += ss

    row = lambda c: pl.BlockSpec((tm, c), lambda i: (i, 0))
    whole = lambda a: pl.BlockSpec(a.shape, lambda i: (0, 0), pipeline_mode=pl.Buffered(1))
    seq_spec = pl.BlockSpec((1, 1, d), lambda i: (i // per_seq, 0, 0))
    in_specs = [row(kw)] * nparts + [whole(w_t)]
    args = list(parts) + [w_t]
    if has_extra:
        in_specs += [row(extra[0].shape[1]), whole(extra[1])]
        args += list(extra)
    in_specs += [row(d), pl.BlockSpec((1, d), lambda i: (0, 0)), seq_spec, row(d)]
    args += [x, g, scale, dxo]
    return pl.pallas_call(
        body, name=name, grid=grid,
        in_specs=in_specs + (rider.in_specs if rider else []),
        out_specs=[row(d), seq_spec, seq_spec] + (rider.out_specs if rider else []),
        out_shape=[jax.ShapeDtypeStruct((t, d), F32), jax.ShapeDtypeStruct((t // seq, 1, d), F32),
                   jax.ShapeDtypeStruct((t // seq, 1, d), F32)] + (rider.out_shape if rider else []),
        scratch_shapes=rider.scratch if rider else [],
        compiler_params=_cp("arbitrary"),
    )(*args, *(rider.arrs if rider else []))


def final_loss(x, g, target, name):
    t, d = x.shape
    tm = _tile(t, 512)

    def body(x_ref, g_ref, t_ref, dx_ref, dg_ref, sq_ref):
        xx = x_ref[...]
        gg = g_ref[...]
        rstd = lax.rsqrt(jnp.mean(xx * xx, axis=-1, keepdims=True) + NORM_EPS)
        xhat = xx * rstd
        err = xhat * gg - t_ref[...]
        dy = err * (1.0 / d)
        dxhat = dy * gg
        dx_ref[...] = rstd * (dxhat - xhat * jnp.mean(dxhat * xhat, axis=-1, keepdims=True))
        dg = jnp.sum(dy * xhat, axis=0, keepdims=True)
        sq = jnp.sum(err * err, axis=0, keepdims=True)

        @pl.when(pl.program_id(0) == 0)
        def _():
            dg_ref[...] = dg
            sq_ref[...] = sq

        @pl.when(pl.program_id(0) != 0)
        def _():
            dg_ref[...] += dg
            sq_ref[...] += sq

    row = pl.BlockSpec((tm, d), lambda i: (i, 0))
    vec = pl.BlockSpec((1, d), lambda i: (0, 0))
    return pl.pallas_call(
        body, name=name, grid=(t // tm,),
        in_specs=[row, vec, row], out_specs=[row, vec, vec],
        out_shape=[jax.ShapeDtypeStruct((t, d), F32), jax.ShapeDtypeStruct((1, d), F32),
                   jax.ShapeDtypeStruct((1, d), F32)],
        compiler_params=_cp("arbitrary"),
    )(x, g, target)


def _split3(x):
    x1 = x.astype(BF16)
    r1 = x - x1.astype(F32)
    x2 = r1.astype(BF16)
    x3 = (r1 - x2.astype(F32)).astype(BF16)
    return x1, x2, x3


def _dot3(x, u):
    x1, x2, x3 = _split3(x)
    return (jnp.dot(x1, u, preferred_element_type=F32) + jnp.dot(x2, u, preferred_element_type=F32)
            + jnp.dot(x3, u, preferred_element_type=F32))


def _log_sigmoid(x):
    return jnp.minimum(x, 0.0) - jnp.log1p(jnp.exp(-jnp.abs(x)))


def forget_cumsum(fl_t, b_f, name):
    b, h, s = fl_t.shape

    def body(f_ref, b_ref, o_ref):
        lf = _log_sigmoid(f_ref[0] + b_ref[...])
        u = (lax.broadcasted_iota(jnp.int32, (s, s), 0) <= lax.broadcasted_iota(jnp.int32, (s, s), 1)).astype(BF16)
        o_ref[0] = _dot3(lf, u)

    return pl.pallas_call(
        body, name=name, grid=(b,),
        in_specs=[pl.BlockSpec((1, h, s), lambda i: (i, 0, 0)), pl.BlockSpec((h, 1), lambda i: (0, 0))],
        out_specs=pl.BlockSpec((1, h, s), lambda i: (i, 0, 0)),
        out_shape=jax.ShapeDtypeStruct((b, h, s), F32),
        compiler_params=_cp("parallel"),
    )(fl_t, b_f)


def forget_cumsum_bwd(dcum, fl_t, b_f, name):
    b, h, s = fl_t.shape

    def body(d_ref, f_ref, b_ref, o_ref, db_ref):
        u = (lax.broadcasted_iota(jnp.int32, (s, s), 0) >= lax.broadcasted_iota(jnp.int32, (s, s), 1)).astype(BF16)
        dlf = _dot3(d_ref[0], u)
        df = dlf * _sigmoid(-(f_ref[0] + b_ref[...]))
        o_ref[0] = df
        part = jnp.sum(df, axis=-1, keepdims=True)

        @pl.when(pl.program_id(0) == 0)
        def _():
            db_ref[...] = part

        @pl.when(pl.program_id(0) != 0)
        def _():
            db_ref[...] += part

    blk = pl.BlockSpec((1, h, s), lambda i: (i, 0, 0))
    return pl.pallas_call(
        body, name=name, grid=(b,),
        in_specs=[blk, blk, pl.BlockSpec((h, 1), lambda i: (0, 0))],
        out_specs=[blk, pl.BlockSpec((h, 1), lambda i: (0, 0))],
        out_shape=[jax.ShapeDtypeStruct((b, h, s), F32), jax.ShapeDtypeStruct((h, 1), F32)],
        compiler_params=_cp("arbitrary"),
    )(dcum, fl_t, b_f)


def _nt(a, b):
    return lax.dot_general(a, b, (((1,), (1,)), ((), ())), preferred_element_type=F32)


def _attn_dims(t, seq, dh):
    tq = _tile(seq, 512)
    tk = _tile(seq, 256)
    return t // seq, tq, tk, seq // tq, tq // tk, ATTN_LANES // dh


def _blocks_per_trip(r):
    return ATTN_BLOCKS_PER_TRIP if r % ATTN_BLOCKS_PER_TRIP == 0 else 1


def _scaled_q(q_ref, dh):
    scale = dh ** -0.5
    if math.log2(dh) % 2 == 0:
        return (q_ref[...].astype(F32) * scale).astype(BF16), None
    return q_ref[...], scale


def _diag_masks(tq, tk, r, strict):
    row = lax.broadcasted_iota(jnp.int32, (tq, tk), 0)
    col = lax.broadcasted_iota(jnp.int32, (tq, tk), 1)
    return [(col + d * tk < row) if strict else (col + d * tk <= row) for d in range(r)]


def _rows(state, lo):
    return tuple(a[lo:] for a in state)


def _put_rows(state, part, lo):
    if lo == 0:
        return tuple(part)
    return tuple(jnp.concatenate([a[:lo], p], axis=0) for a, p in zip(state, part))


def _transposed(x):
    return jnp.transpose(x.astype(F32)).astype(BF16)


def _store_transposed(dst_ref, src_scr, tk):
    for jb in range(src_scr.shape[0]):
        dst_ref[jb * tk:(jb + 1) * tk, :] = jnp.transpose(src_scr[jb]).astype(BF16)


def _dot2(x, u):
    hi = x.astype(BF16)
    lo = (x - hi.astype(F32)).astype(BF16)
    return jnp.dot(jnp.concatenate([hi, lo], axis=1), jnp.concatenate([u, u], axis=0), preferred_element_type=F32)


def _attn_specs(seq, di, tq):
    nq = seq // tq
    cb = di // ATTN_LANES
    q_spec = pl.BlockSpec((tq, ATTN_LANES), lambda b, hp, i: (b * nq + i, hp))
    k_spec = pl.BlockSpec((seq, ATTN_LANES), lambda b, hp, i: (b, cb + hp))
    v_spec = pl.BlockSpec((seq, ATTN_LANES), lambda b, hp, i: (b, 2 * cb + hp))
    kv_out = pl.BlockSpec((seq, ATTN_LANES), lambda b, hp, i: (b, hp))
    return q_spec, k_spec, v_spec, kv_out


def _grid_ends(grid):
    ids = [pl.program_id(a) for a in range(len(grid))]
    first = functools.reduce(jnp.logical_and, [p == 0 for p in ids])
    last = functools.reduce(jnp.logical_and, [p == g - 1 for p, g in zip(ids, grid)])
    return first, last


def fox_fwd(qkv, ck, seq, di, dh, name, rider=None):
    t = qkv.shape[0]
    nb, tq, tk, nq, r, hpb = _attn_dims(t, seq, dh)
    per_trip = _blocks_per_trip(r)
    heads = [slice(hh * dh, (hh + 1) * dh) for hh in range(hpb)]
    q_spec, k_spec, v_spec, _ = _attn_specs(seq, di, tq)
    grid = (nb, di // ATTN_LANES, nq)
    nr = rider.n if rider else 0

    def body(q_ref, k_ref, v_ref, ck_ref, *rest):
        o_ref, lse_ref = rest[nr:nr + 2]
        if rider:
            _ride(rider, rest[:nr] + rest[nr + 2:], *_grid_ends(grid))
        i = pl.program_id(2)
        masks = _diag_masks(tq, tk, r, False)
        q, scale = _scaled_q(q_ref, dh)
        qs = [q[:, hd] for hd in heads]

        def scores(hh, j):
            start = pl.multiple_of(j * tk, tk)
            kh = k_ref[pl.ds(start, tk), heads[hh]]
            vh = v_ref[pl.ds(start, tk), heads[hh]]
            s = _nt(qs[hh], kh)
            if scale is not None:
                s = s * scale
            return s - ck_ref[0, hh, pl.ds(j, 1), :], vh

        def update(state, s, vh):
            m, l, acc = state
            m_new = jnp.maximum(m, jnp.max(s, axis=-1, keepdims=True))
            alpha = jnp.exp(m - m_new)
            p = jnp.exp(s - m_new)
            return m_new, alpha * l + jnp.sum(p, axis=-1, keepdims=True), alpha * acc + _dot2(p, vh)

        states = []
        for hh in range(hpb):
            s, vh = scores(hh, r * i)
            s = jnp.where(masks[0], s, -jnp.inf)
            m = jnp.max(s, axis=-1, keepdims=True)
            p = jnp.exp(s - m)
            states.append((m, jnp.sum(p, axis=-1, keepdims=True), _dot2(p, vh)))
        for d in range(1, r):
            for hh in range(hpb):
                s, vh = scores(hh, r * i + d)
                states[hh] = update(states[hh], jnp.where(masks[d], s, -jnp.inf), vh)

        def step(jj, states):
            for sub in range(per_trip):
                states = tuple(update(states[hh], *scores(hh, r * i - 1 - (per_trip * jj + sub))) for hh in range(hpb))
            return states

        states = lax.fori_loop(0, r * i // per_trip, step, tuple(states))
        for hh in range(hpb):
            m, l, acc = states[hh]
            o_ref[:, heads[hh]] = acc / l
            lse_ref[0, 0, :, hh:hh + 1] = m + jnp.log(l)

    return pl.pallas_call(
        body, name=name, grid=grid,
        in_specs=[q_spec, k_spec, v_spec,
                  pl.BlockSpec((1, hpb, seq // tk, tk), lambda b, hp, i: (b, hp, 0, 0))] + (rider.in_specs if rider else []),
        out_specs=[q_spec, pl.BlockSpec((1, 1, tq, hpb), lambda b, hp, i: (b, hp, i, 0))]
        + (rider.out_specs if rider else []),
        out_shape=[jax.ShapeDtypeStruct((t, di), F32), jax.ShapeDtypeStruct((nb, di // ATTN_LANES, seq, hpb), F32)]
        + (rider.out_shape if rider else []),
        scratch_shapes=rider.scratch if rider else [],
        compiler_params=_cp("arbitrary", "arbitrary", "arbitrary"),
    )(qkv, qkv, qkv, ck, *(rider.arrs if rider else []))


def fox_bwd(qkv, ck, o, do, lse, seq, di, dh, name, rider=None):
    t = qkv.shape[0]
    nb, tq, tk, nq, r, hpb = _attn_dims(t, seq, dh)
    per_trip = _blocks_per_trip(r)
    heads = [slice(hh * dh, (hh + 1) * dh) for hh in range(hpb)]
    q_spec, k_spec, v_spec, kv_out = _attn_specs(seq, di, tq)
    ck_spec = pl.BlockSpec((1, hpb, seq // tk, tk), lambda b, hp, i: (b, hp, 0, 0))
    grid = (nb, di // ATTN_LANES, nq)
    nr = rider.n if rider else 0

    def body(q_ref, k_ref, v_ref, ck_ref, o_ref, do_ref, lse_ref, *rest):
        dq_ref, dk_ref, dv_ref, dck_ref = rest[nr:nr + 4]
        dkt_scr, dvt_scr = rest[2 * nr + 4:2 * nr + 6]
        if rider:
            _ride(rider, rest[:nr] + rest[nr + 4:2 * nr + 4] + rest[2 * nr + 6:], *_grid_ends(grid))
        i = pl.program_id(2)

        @pl.when(i == 0)
        def _():
            dkt_scr[...] = jnp.zeros_like(dkt_scr)
            dvt_scr[...] = jnp.zeros_like(dvt_scr)
            dck_ref[...] = jnp.zeros_like(dck_ref)

        masks = _diag_masks(tq, tk, r, False)
        q, scale = _scaled_q(q_ref, dh)
        do = do_ref[...]
        q_t, do_t = _transposed(q), _transposed(do)
        qs = [q[:, hd] for hd in heads]
        dos = [do[:, hd] for hd in heads]
        deltas = [jnp.sum(dos[hh].astype(F32) * o_ref[:, heads[hh]], axis=-1, keepdims=True) for hh in range(hpb)]
        lses = [lse_ref[0, 0, :, hh:hh + 1] for hh in range(hpb)]

        def block(hh, j, mask, dq_acc, lo=0):
            start = pl.multiple_of(j * tk, tk)
            kh = k_ref[pl.ds(start, tk), heads[hh]]
            vh = v_ref[pl.ds(start, tk), heads[hh]]
            s = _nt(qs[hh][lo:], kh)
            if scale is not None:
                s = s * scale
            p = jnp.exp(s - ck_ref[0, hh, pl.ds(j, 1), :] - lses[hh][lo:])
            if mask is not None:
                p = jnp.where(mask[lo:], p, 0.0)
            ds = p * (_nt(dos[hh][lo:], vh) - deltas[hh][lo:])
            dsb = ds.astype(BF16)
            dkt = jnp.dot(q_t[heads[hh], lo:], dsb, preferred_element_type=F32)
            if scale is not None:
                dkt = dkt * scale
            dkt_scr[j, heads[hh], :] += dkt
            dvt_scr[j, heads[hh], :] += jnp.dot(do_t[heads[hh], lo:], p.astype(BF16), preferred_element_type=F32)
            dck_ref[0, hh, pl.ds(j, 1), :] -= jnp.sum(ds, axis=0, keepdims=True)
            return dq_acc + jnp.dot(dsb, kh, preferred_element_type=F32)

        accs = [jnp.zeros((tq, dh), F32)] * hpb
        for d in range(r):
            lo = d * tk
            accs = [_put_rows((accs[hh],), (block(hh, r * i + d, masks[d], accs[hh][lo:], lo),), lo)[0]
                    for hh in range(hpb)]

        def step(jj, accs):
            for sub in range(per_trip):
                accs = tuple(block(hh, per_trip * jj + sub, None, accs[hh]) for hh in range(hpb))
            return accs

        accs = lax.fori_loop(0, r * i // per_trip, step, tuple(accs))
        for hh in range(hpb):
            dq_ref[:, heads[hh]] = (accs[hh] * dh ** -0.5).astype(BF16)

        @pl.when(i == nq - 1)
        def _():
            _store_transposed(dk_ref, dkt_scr, tk)
            _store_transposed(dv_ref, dvt_scr, tk)

    return pl.pallas_call(
        body, name=name, grid=grid,
        in_specs=[q_spec, k_spec, v_spec, ck_spec, q_spec, q_spec,
                  pl.BlockSpec((1, 1, tq, hpb), lambda b, hp, i: (b, hp, i, 0))] + (rider.in_specs if rider else []),
        out_specs=[q_spec, kv_out, kv_out, ck_spec] + (rider.out_specs if rider else []),
        out_shape=[jax.ShapeDtypeStruct((t, di), BF16)] * 3 + [jax.ShapeDtypeStruct(ck.shape, F32)]
        + (rider.out_shape if rider else []),
        scratch_shapes=[pltpu.VMEM((seq // tk, ATTN_LANES, tk), F32), pltpu.VMEM((seq // tk, ATTN_LANES, tk), F32)]
        + (rider.scratch if rider else []),
        compiler_params=_cp("arbitrary", "arbitrary", "arbitrary"),
    )(qkv, qkv, qkv, ck, o, do, lse, *(rider.arrs if rider else []))


def _sb_logits(qh, kh, scale, strict):
    z = _nt(qh, kh)
    if scale is not None:
        z = z * scale
    e = jnp.exp(-jnp.abs(z))
    lb = jnp.minimum(z, 0.0) - jnp.log(1.0 + e)
    lk = lb - z
    return lb, lk, lk if strict is None else jnp.where(strict, lk, 0.0)


def _cum(x, u):
    return jnp.dot(x.astype(BF16), u, preferred_element_type=F32)


def sb_fwd(qkv, seq, di, dh, name):
    t = qkv.shape[0]
    nb, tq, tk, nq, r, hpb = _attn_dims(t, seq, dh)
    per_trip = _blocks_per_trip(r)
    heads = [slice(hh * dh, (hh + 1) * dh) for hh in range(hpb)]
    q_spec, k_spec, v_spec, _ = _attn_specs(seq, di, tq)

    def body(q_ref, k_ref, v_ref, o_ref, rt_ref):
        i = pl.program_id(2)
        masks = _diag_masks(tq, tk, r, True)
        row = lax.broadcasted_iota(jnp.int32, (tk, tk), 0)
        col = lax.broadcasted_iota(jnp.int32, (tk, tk), 1)
        u_after = (row > col).astype(BF16)
        q, scale = _scaled_q(q_ref, dh)
        qs = [q[:, hd] for hd in heads]

        def block(hh, j, mask, state, lo=0):
            rr, acc = state
            start = pl.multiple_of(j * tk, tk)
            kh = k_ref[pl.ds(start, tk), heads[hh]]
            vh = v_ref[pl.ds(start, tk), heads[hh]]
            if mask is not None:
                mask = mask[lo:]
            lb, _, lk = _sb_logits(qs[hh][lo:], kh, scale, mask)
            a = jnp.exp(lb + _cum(lk, u_after) + rr)
            if mask is not None:
                a = jnp.where(mask, a, 0.0)
            acc = acc + jnp.dot(a.astype(BF16), vh, preferred_element_type=F32)
            return rr + jnp.sum(lk, axis=-1, keepdims=True), acc

        states = [(jnp.zeros((tq, 1), F32), jnp.zeros((tq, dh), F32))] * hpb
        for d in reversed(range(r)):
            lo = d * tk
            states = [_put_rows(states[hh], block(hh, r * i + d, masks[d], _rows(states[hh], lo), lo), lo)
                      for hh in range(hpb)]

        def step(jj, states):
            for sub in range(per_trip):
                states = tuple(block(hh, r * i - 1 - (per_trip * jj + sub), None, states[hh]) for hh in range(hpb))
            return states

        states = lax.fori_loop(0, r * i // per_trip, step, tuple(states))
        for hh in range(hpb):
            o_ref[:, heads[hh]] = states[hh][1]
            rt_ref[0, 0, :, hh:hh + 1] = states[hh][0]

    return pl.pallas_call(
        body, name=name, grid=(nb, di // ATTN_LANES, nq),
        in_specs=[q_spec, k_spec, v_spec],
        out_specs=[q_spec, pl.BlockSpec((1, 1, tq, hpb), lambda b, hp, i: (b, hp, i, 0))],
        out_shape=[jax.ShapeDtypeStruct((t, di), F32), jax.ShapeDtypeStruct((nb, di // ATTN_LANES, seq, hpb), F32)],
        compiler_params=_cp("parallel", "parallel", "arbitrary"),
    )(qkv, qkv, qkv)


def sb_bwd(qkv, do, rtot, seq, di, dh, name):
    t = qkv.shape[0]
    nb, tq, tk, nq, r, hpb = _attn_dims(t, seq, dh)
    per_trip = _blocks_per_trip(r)
    heads = [slice(hh * dh, (hh + 1) * dh) for hh in range(hpb)]
    q_spec, k_spec, v_spec, kv_out = _attn_specs(seq, di, tq)

    def body(q_ref, k_ref, v_ref, do_ref, rt_ref, dq_ref, dk_ref, dv_ref, dkt_scr, dvt_scr):
        i = pl.program_id(2)

        @pl.when(i == 0)
        def _():
            dkt_scr[...] = jnp.zeros_like(dkt_scr)
            dvt_scr[...] = jnp.zeros_like(dvt_scr)

        masks = _diag_masks(tq, tk, r, True)
        row = lax.broadcasted_iota(jnp.int32, (tk, tk), 0)
        col = lax.broadcasted_iota(jnp.int32, (tk, tk), 1)
        u_after = (row > col).astype(BF16)
        u_before = (row < col).astype(BF16)
        q, scale = _scaled_q(q_ref, dh)
        do = do_ref[...]
        q_t, do_t = _transposed(q), _transposed(do)
        qs = [q[:, hd] for hd in heads]
        dos = [do[:, hd] for hd in heads]
        rts = [rt_ref[0, 0, :, hh:hh + 1] for hh in range(hpb)]

        def block(hh, j, mask, state, lo=0):
            lc, gc, dq_acc = state
            start = pl.multiple_of(j * tk, tk)
            kh = k_ref[pl.ds(start, tk), heads[hh]]
            vh = v_ref[pl.ds(start, tk), heads[hh]]
            if mask is not None:
                mask = mask[lo:]
            lb, lk_all, lk = _sb_logits(qs[hh][lo:], kh, scale, mask)
            lc = lc + jnp.sum(lk, axis=-1, keepdims=True)
            a = jnp.exp(lb + ((rts[hh][lo:] - lc) + _cum(lk, u_after)))
            if mask is not None:
                a = jnp.where(mask, a, 0.0)
            de = a * _nt(dos[hh][lo:], vh)
            g = gc + _cum(de, u_before)
            dz = de * jnp.exp(lk_all) - g * jnp.exp(lb)
            if mask is not None:
                dz = jnp.where(mask, dz, 0.0)
            dzb = dz.astype(BF16)
            dkt = jnp.dot(q_t[heads[hh], lo:], dzb, preferred_element_type=F32)
            if scale is not None:
                dkt = dkt * scale
            dkt_scr[j, heads[hh], :] += dkt
            dvt_scr[j, heads[hh], :] += jnp.dot(do_t[heads[hh], lo:], a.astype(BF16), preferred_element_type=F32)
            return (lc, gc + jnp.sum(de, axis=-1, keepdims=True),
                    dq_acc + jnp.dot(dzb, kh, preferred_element_type=F32))

        zero = jnp.zeros((tq, 1), F32)

        def step(jj, states):
            for sub in range(per_trip):
                states = tuple(block(hh, per_trip * jj + sub, None, states[hh]) for hh in range(hpb))
            return states

        states = lax.fori_loop(0, r * i // per_trip, step, ((zero, zero, jnp.zeros((tq, dh), F32)),) * hpb)
        for d in range(r):
            lo = d * tk
            states = [_put_rows(states[hh], block(hh, r * i + d, masks[d], _rows(states[hh], lo), lo), lo)
                      for hh in range(hpb)]
        for hh in range(hpb):
            dq_ref[:, heads[hh]] = (states[hh][2] * dh ** -0.5).astype(BF16)

        @pl.when(i == nq - 1)
        def _():
            _store_transposed(dk_ref, dkt_scr, tk)
            _store_transposed(dv_ref, dvt_scr, tk)

    return pl.pallas_call(
        body, name=name, grid=(nb, di // ATTN_LANES, nq),
        in_specs=[q_spec, k_spec, v_spec, q_spec, pl.BlockSpec((1, 1, tq, hpb), lambda b, hp, i: (b, hp, i, 0))],
        out_specs=[q_spec, kv_out, kv_out],
        out_shape=[jax.ShapeDtypeStruct((t, di), BF16)] * 3,
        scratch_shapes=[pltpu.VMEM((seq // tk, ATTN_LANES, tk), F32), pltpu.VMEM((seq // tk, ATTN_LANES, tk), F32)],
        compiler_params=_cp("parallel", "parallel", "arbitrary"),
    )(qkv, qkv, qkv, do, rtot)


def _cols(g):
    return jnp.transpose(g, (1, 0, 2)).reshape(g.shape[1], NDEV * g.shape[2])


def _col_blocks(w):
    r, c8 = w.shape
    return jnp.transpose(w.reshape(r, NDEV, c8 // NDEV), (1, 0, 2))


def _pad_rows16(a):
    return jnp.pad(a, ((0, 16 - a.shape[0]), (0, 0)))


def _pad_cols(a, n):
    return jnp.pad(a, ((0, 0), (0, n - a.shape[1])))


def kernel(x, c, fox_norm_g, fox_w_ada, fox_b_ada, fox_w_in, fox_b_f, fox_w_out, sb_norm_g, sb_w_ada, sb_b_ada, sb_w_in, sb_w_out, final_norm_g, loss_target, m_fox_norm_g, m_fox_w_ada, m_fox_b_ada, m_fox_w_in, m_fox_b_f, m_fox_w_out, m_sb_norm_g, m_sb_w_ada, m_sb_b_ada, m_sb_w_in, m_sb_w_out, m_final_norm_g, v_fox_norm_g, v_fox_w_ada, v_fox_b_ada, v_fox_w_in, v_fox_b_f, v_fox_w_out, v_sb_norm_g, v_sb_w_ada, v_sb_b_ada, v_sb_w_in, v_sb_w_out, v_final_norm_g):
    nb, seq, d = x.shape
    t = nb * seq
    h = fox_b_f.shape[-1]
    di = fox_w_out.shape[1] * NDEV
    dh = di // h
    tq = _tile(seq, 256)
    me = _my_index()

    gathered = all_gather([w[0].astype(BF16) for w in (fox_w_ada, fox_w_in, fox_w_out)], "gather_fox_weights")
    fox_wada, fox_win = _cols(gathered[0]), _cols(gathered[1])
    fox_wout = gathered[2].reshape(di, d)
    sb_gather = Exchange("gather", [w[0].astype(BF16) for w in (sb_w_ada, sb_w_in, sb_w_out)] + [sb_norm_g, sb_b_ada])

    x0 = x.reshape(t, d)
    target = loss_target.reshape(t, d)
    c16 = _pad_rows16(c)

    def layer_fwd(xin, g, wada, bada, win, wout, b_f, tag, rider=None):
        mod = adaln_fwd(c16, wada, bada, tag + "_adaln")[:nb]
        shift, scale, gate = (mod[:, k * d:(k + 1) * d].reshape(nb, 1, d) for k in range(3))
        qkv, hmod = normmod_matmul(xin, g, scale, shift, win[:, :3 * di], BF16, seq, True, tag + "_qkv")
        w_z = win[:, 3 * di:]
        if b_f is not None:
            w_z = _pad_cols(w_z, di + 128)
        (zf,) = normmod_matmul(xin, g, scale, shift, w_z, F32, seq, False, tag + "_z")
        saved = dict(x=xin, g=g, scale=scale, gate=gate, qkv=qkv, h=hmod, zf=zf, win=win, wout=wout)
        if b_f is not None:
            fl_t = jnp.transpose(zf[:, di:di + h].reshape(nb, seq, h), (0, 2, 1))
            bf_col = b_f.reshape(h, 1)
            cum = forget_cumsum(fl_t, bf_col, tag + "_cum")
            ck = cum.reshape(nb, h, seq // tq, tq)
            o, lse, *rode = fox_fwd(qkv, ck, seq, di, dh, tag + "_attn", rider)
            saved.update(fl_t=fl_t, bf_col=bf_col, ck=ck, lse=lse, rode=rode)
        else:
            o, rtot = sb_fwd(qkv, seq, di, dh, tag + "_attn")
            saved.update(rtot=rtot)
        xout, y, u = out_proj_fwd(o, zf, wout, xin, gate, seq, tag + "_out")
        saved.update(o=o, y=y, u=u)
        return xout, saved

    def layer_bwd(dxo, sv, is_fox, tag, rider=None):
        do, dz, dy, dgate = out_proj_bwd(dxo, sv["y"], sv["gate"], sv["wout"], sv["o"], sv["zf"], seq, tag + "_dout")
        dwout = tn_matmul(sv["u"], dy, tag + "_dwout")
        extra = None
        if is_fox:
            dq, dk, dv, dck, *rode = fox_bwd(sv["qkv"], sv["ck"], sv["o"], do, sv["lse"], seq, di, dh, tag + "_dattn",
                                             rider)
            df_t, dbf = forget_cumsum_bwd(dck.reshape(nb, h, seq), sv["fl_t"], sv["bf_col"], tag + "_dcum")
            df = _pad_cols(jnp.transpose(df_t, (0, 2, 1)).reshape(t, h), 128).astype(BF16)
            extra = (df, jnp.transpose(_pad_cols(sv["win"][:, 4 * di:], 128)))
        else:
            dq, dk, dv = sb_bwd(sv["qkv"], do, sv["rtot"], seq, di, dh, tag + "_dattn")
        parts = [dq, dk, dv, dz]
        dwin = [tn_matmul(sv["h"], p, tag + "_dwin%d" % k) for k, p in enumerate(parts)]
        if is_fox:
            dwin.append(tn_matmul(sv["h"], df, tag + "_dwinf")[:, :h])
        dwin = jnp.concatenate(dwin, axis=1)
        w_t = jnp.transpose(sv["win"][:, :4 * di])
        win_blocks, wout_blocks = _col_blocks(dwin), dwout.reshape(NDEV, di // NDEV, d)
        dh_rider = Exchange("scatter", [win_blocks, wout_blocks]) if is_fox else None
        dxin, psum, ssum, *landed = dh_norm_bwd(parts, w_t, extra, sv["x"], sv["g"], sv["scale"], dxo, seq,
                                                tag + "_dh", dh_rider)
        pad = lambda a: _pad_rows16(a.reshape(nb, d))
        dwada, dbada, dng = adaln_bwd(c16, pad(ssum), pad(psum), pad(dgate), pad(sv["scale"]), sv["g"], tag + "_dadaln")
        grads = dict(wada=dwada, bada=dbada, ng=dng, win=win_blocks, wout=wout_blocks)
        if is_fox:
            grads["bf"] = dbf.reshape(1, h)
            grads["rode"] = rode
            grads["landed"] = landed
        return dxin, grads

    x1, sv_fox = layer_fwd(x0, fox_norm_g, fox_wada, fox_b_ada, fox_win, fox_wout, fox_b_f, "fox", sb_gather)
    sb_wada, sb_win = _cols(sv_fox["rode"][0]), _cols(sv_fox["rode"][1])
    sb_wout = sv_fox["rode"][2].reshape(di, d)
    sb_g, sb_bada = _cols(sv_fox["rode"][3]), _cols(sv_fox["rode"][4])
    x2, sv_sb = layer_fwd(x1, sb_g, sb_wada, sb_bada, sb_win, sb_wout, None, "sb")
    dx2, dgf, sq = final_loss(x2, final_norm_g.reshape(1, d), target, "loss_head")
    dx1, g_sb = layer_bwd(dx2, sv_sb, False, "sb")
    dx0, g_fox = layer_bwd(dx1, sv_fox, True, "fox", Exchange("scatter", [g_sb["wada"], g_sb["win"], g_sb["wout"]]))

    land = list(all_to_all([g_fox["wada"]], "scatter_fox_wada_grad")) + list(g_fox["landed"]) + list(g_fox["rode"])
    small = jnp.concatenate([g_fox["ng"], g_fox["bada"], _pad_cols(g_fox["bf"], 128), g_sb["ng"], g_sb["bada"],
                             dgf, sq], axis=1)
    (small_all,) = all_gather([small], "gather_small")
    small_sum, loss_row = sum_slots(small_all, d, "sum_small")
    loss = loss_row[0, 0]

    offs = {}
    pos = 0
    for nm, width in (("fox_ng", d), ("fox_bada", 3 * d), ("fox_bf", 128), ("sb_ng", d), ("sb_bada", 3 * d), ("fin_g", d)):
        offs[nm] = (pos, width)
        pos += width

    def small_grad(nm, width=None, shard=False):
        p0, wd = offs[nm]
        wd = width or wd
        if shard:
            blk = wd // NDEV
            return lax.dynamic_slice(small_sum, (0, p0 + me * blk), (1, blk))
        return small_sum[:, p0:p0 + wd]

    results = {}
    big_params = [("fox_w_ada", fox_w_ada, m_fox_w_ada, v_fox_w_ada), ("fox_w_in", fox_w_in, m_fox_w_in, v_fox_w_in),
                  ("fox_w_out", fox_w_out, m_fox_w_out, v_fox_w_out), ("sb_w_ada", sb_w_ada, m_sb_w_ada, v_sb_w_ada),
                  ("sb_w_in", sb_w_in, m_sb_w_in, v_sb_w_in), ("sb_w_out", sb_w_out, m_sb_w_out, v_sb_w_out)]
    for k, (nm, w, m, v) in enumerate(big_params):
        outs = adamw(land[k], w[0], m[0], v[0], "adamw_" + nm)
        results[nm] = [o[None] for o in outs]
    small_params = [("fox_norm_g", fox_norm_g, m_fox_norm_g, v_fox_norm_g, small_grad("fox_ng")),
                    ("fox_b_ada", fox_b_ada, m_fox_b_ada, v_fox_b_ada, small_grad("fox_bada")),
                    ("fox_b_f", fox_b_f, m_fox_b_f, v_fox_b_f, small_grad("fox_bf", h)),
                    ("sb_norm_g", sb_norm_g, m_sb_norm_g, v_sb_norm_g, small_grad("sb_ng", shard=True)),
                    ("sb_b_ada", sb_b_ada, m_sb_b_ada, v_sb_b_ada, small_grad("sb_bada", shard=True)),
                    ("final_norm_g", final_norm_g.reshape(1, d), m_final_norm_g.reshape(1, d),
                     v_final_norm_g.reshape(1, d), small_grad("fin_g"))]
    for nm, w, m, v, g in small_params:
        outs = adamw(g[None], w, m, v, "adamw_" + nm)
        if nm == "final_norm_g":
            outs = [o.reshape(d) for o in outs]
        results[nm] = outs

    order = ["fox_norm_g", "fox_w_ada", "fox_b_ada", "fox_w_in", "fox_b_f", "fox_w_out", "sb_norm_g", "sb_w_ada",
             "sb_b_ada", "sb_w_in", "sb_w_out", "final_norm_g"]
    out = [loss, dx0.reshape(nb, seq, d)]
    for k in range(4):
        out += [results[nm][k] for nm in order]
    return tuple(out)
```

```python
import functools
import math

import jax
import jax.numpy as jnp
from jax import lax
from jax.experimental import pallas as pl
from jax.experimental.pallas import tpu as pltpu

F32 = jnp.float32
BF16 = jnp.bfloat16
NDEV = 8
VMEM_LIMIT = 56 * 1024 * 1024
NORM_EPS = 1e-6
ADAM_LR, ADAM_B1, ADAM_B2, ADAM_EPS, ADAM_WD, ADAM_STEP = 0.001, 0.9, 0.999, 1e-08, 0.01, 10
ATTN_LANES = 128
ATTN_BLOCKS_PER_TRIP = 2
MESH = pl.DeviceIdType.MESH
ANY = pl.BlockSpec(memory_space=pl.ANY)


def _cp(*sem):
    return pltpu.CompilerParams(dimension_semantics=sem, vmem_limit_bytes=VMEM_LIMIT)


def _my_index():
    return 4 * lax.axis_index("x") + 2 * lax.axis_index("y") + lax.axis_index("c")


def _flip(k):
    x, y, c = lax.axis_index("x"), lax.axis_index("y"), lax.axis_index("c")
    kx, ky, kc = (k >> 2) & 1, (k >> 1) & 1, k & 1
    px = 1 - x if kx else x
    py = 1 - y if ky else y
    pc = 1 - c if kc else c
    return (px, py, pc), 4 * px + 2 * py + pc


class Exchange:
    def __init__(self, mode, arrs):
        self.gather = mode == "gather"
        self.arrs = list(arrs)
        self.n = len(self.arrs)
        self.in_specs = [ANY] * self.n
        self.out_specs = [ANY] * self.n
        self.out_shape = [jax.ShapeDtypeStruct(((NDEV,) + a.shape) if self.gather else a.shape, a.dtype)
                          for a in self.arrs]
        self.scratch = [pltpu.SemaphoreType.DMA((self.n * (NDEV - 1),)), pltpu.SemaphoreType.DMA((self.n * (NDEV - 1),)),
                        pltpu.SemaphoreType.DMA((self.n,))]

    def copies(self, ins, outs, sems):
        send_sems, recv_sems, local_sems = sems
        me = _my_index()
        out = []
        for a in range(self.n):
            out.append(pltpu.make_async_copy(ins[a] if self.gather else ins[a].at[me], outs[a].at[me], local_sems.at[a]))
            for k in range(1, NDEV):
                peer, pidx = _flip(k)
                out.append(pltpu.make_async_remote_copy(
                    src_ref=ins[a] if self.gather else ins[a].at[pidx], dst_ref=outs[a].at[me],
                    send_sem=send_sems.at[a * (NDEV - 1) + k - 1], recv_sem=recv_sems.at[a * (NDEV - 1) + k - 1],
                    device_id=peer, device_id_type=MESH))
        return out

    def run(self, name):
        n = self.n

        def body(*refs):
            cps = self.copies(refs[:n], refs[n:2 * n], refs[2 * n:])
            for cp in cps:
                cp.start()
            for cp in cps:
                cp.wait()

        return pl.pallas_call(body, name=name, out_shape=self.out_shape, in_specs=self.in_specs,
                              out_specs=self.out_specs, scratch_shapes=self.scratch)(*self.arrs)


def _ride(exchange, refs, first, last):
    n = exchange.n
    cps = exchange.copies(refs[:n], refs[n:2 * n], refs[2 * n:])

    @pl.when(first)
    def _():
        for cp in cps:
            cp.start()

    @pl.when(last)
    def _():
        for cp in cps:
            cp.wait()


def all_gather(arrs, name):
    return Exchange("gather", arrs).run(name)


def all_gather_by_chip(arrs, name):
    n = len(arrs)
    chips = (2, 4, 6)

    def body(*refs):
        ins, outs = refs[:n], refs[n:2 * n]
        send_sems, recv_sems, local_sems = refs[2 * n:]
        me = _my_index()
        sibling, _ = _flip(1)

        def copy(a, k, block, to, src=None):
            return pltpu.make_async_remote_copy(
                src_ref=outs[a].at[block] if src is None else src, dst_ref=outs[a].at[block],
                send_sem=send_sems.at[7 * a + k], recv_sem=recv_sems.at[7 * a + k], device_id=to, device_id_type=MESH)

        own, sent = [], []
        for a in range(n):
            own.append(pltpu.make_async_copy(ins[a], outs[a].at[me], local_sems.at[a]))
            own[-1].start()
            first = [copy(a, 0, me, sibling, ins[a])] + [copy(a, 1 + j, me, _flip(k)[0], ins[a]) for j, k in enumerate(chips)]
            for cp in first:
                cp.start()
            sent += first
        for a in range(n):
            for j, k in enumerate(chips):
                peer, pidx = _flip(k)
                copy(a, 1 + j, pidx, peer).wait_recv()
                sent.append(copy(a, 4 + j, pidx, sibling))
                sent[-1].start()
        for a in range(n):
            copy(a, 0, _flip(1)[1], sibling).wait_recv()
            for j, k in enumerate(chips):
                copy(a, 4 + j, _flip(k + 1)[1], sibling).wait_recv()
        for cp in sent:
            cp.wait_send()
        for cp in own:
            cp.wait()

    return pl.pallas_call(
        body, name=name,
        out_shape=[jax.ShapeDtypeStruct((NDEV,) + a.shape, a.dtype) for a in arrs],
        in_specs=[ANY] * n, out_specs=[ANY] * n,
        scratch_shapes=[pltpu.SemaphoreType.DMA((7 * n,)), pltpu.SemaphoreType.DMA((7 * n,)),
                        pltpu.SemaphoreType.DMA((n,))],
    )(*arrs)


def all_to_all(arrs, name):
    return Exchange("scatter", arrs).run(name)


def _tile(n, pref):
    t = min(n, pref)
    while n % t:
        t //= 2
    return t


def _col_tile(n, pref):
    t = _tile(n, pref)
    return t if t % 128 == 0 and t >= 512 else n


def adamw(land, w, m, v, name):
    slots, r, c = land.shape
    tr = _tile(r, 64)
    bc1 = 1.0 - ADAM_B1 ** ADAM_STEP
    bc2 = 1.0 - ADAM_B2 ** ADAM_STEP

    def body(land_ref, w_ref, m_ref, v_ref, g_ref, d_ref, nm_ref, nv_ref):
        g = land_ref[0].astype(F32)
        for s in range(1, slots):
            g = g + land_ref[s].astype(F32)
        nm = ADAM_B1 * m_ref[...] + (1.0 - ADAM_B1) * g
        nv = ADAM_B2 * v_ref[...] + (1.0 - ADAM_B2) * (g * g)
        m_hat = nm / bc1
        v_hat = nv / bc2
        g_ref[...] = g
        nm_ref[...] = nm
        nv_ref[...] = nv
        d_ref[...] = -ADAM_LR * (m_hat / (jnp.sqrt(v_hat) + ADAM_EPS) + ADAM_WD * w_ref[...])

    blk = pl.BlockSpec((tr, c), lambda i: (i, 0))
    return pl.pallas_call(
        body, name=name, grid=(r // tr,),
        in_specs=[pl.BlockSpec((slots, tr, c), lambda i: (0, i, 0)), blk, blk, blk],
        out_specs=[blk] * 4,
        out_shape=[jax.ShapeDtypeStruct((r, c), F32)] * 4,
        compiler_params=_cp("parallel"),
    )(land, w, m, v)


def sum_slots(land, d_model, name):
    slots, _, n = land.shape

    def body(land_ref, o_ref, loss_ref):
        g = land_ref[0]
        for s in range(1, slots):
            g = g + land_ref[s]
        o_ref[...] = g
        sq = jnp.sum(g[:, n - d_model:], axis=-1, keepdims=True)
        loss_ref[...] = jnp.broadcast_to(sq * (0.5 / d_model), (1, 128))

    return pl.pallas_call(
        body, name=name,
        out_shape=[jax.ShapeDtypeStruct((1, n), F32), jax.ShapeDtypeStruct((1, 128), F32)],
    )(land)


def _sigmoid(x):
    return 1.0 / (1.0 + jnp.exp(-x))


def adaln_fwd(c16, w_ada, b_ada, name):
    d3 = w_ada.shape[1]

    def body(c_ref, w_ref, b_ref, o_ref):
        cc = c_ref[...]
        sc = (cc * _sigmoid(cc)).astype(BF16)
        o_ref[...] = jnp.dot(sc, w_ref[...], preferred_element_type=F32) + b_ref[...]

    return pl.pallas_call(body, name=name, out_shape=jax.ShapeDtypeStruct((16, d3), F32),
                          compiler_params=pltpu.CompilerParams(vmem_limit_bytes=VMEM_LIMIT))(c16, w_ada, b_ada)


def adaln_bwd(c16, dshift16, p16, dgate16, scale16, g, name):
    d = c16.shape[1]
    nb = 3 * d // NDEV

    def body(c_ref, ds_ref, p_ref, dg_ref, sc_ref, g_ref, dw_ref, db_ref, dng_ref, dmod_scr):
        j = pl.program_id(0)

        @pl.when(j == 0)
        def _():
            p = p_ref[...]
            dmod = jnp.concatenate([ds_ref[...], p * g_ref[...], dg_ref[...]], axis=-1)
            dmod_scr[...] = dmod
            db_ref[...] = jnp.sum(dmod, axis=0, keepdims=True)
            dng_ref[...] = jnp.sum((1.0 + sc_ref[...]) * p, axis=0, keepdims=True)

        cc = c_ref[...]
        sc = (cc * _sigmoid(cc)).astype(BF16)
        start = pl.multiple_of(j * nb, 128)
        dm = dmod_scr[:, pl.ds(start, nb)].astype(BF16)
        dw_ref[0] = lax.dot_general(sc, dm, (((0,), (0,)), ((), ())), preferred_element_type=F32).astype(BF16)

    full = lambda shape: pl.BlockSpec(shape, lambda j: (0,) * len(shape))
    return pl.pallas_call(
        body, name=name, grid=(NDEV,),
        in_specs=[full((16, d))] * 5 + [full((1, d))],
        out_specs=[pl.BlockSpec((1, d, nb), lambda j: (j, 0, 0)), full((1, 3 * d)), full((1, d))],
        out_shape=[jax.ShapeDtypeStruct((NDEV, d, nb), BF16), jax.ShapeDtypeStruct((1, 3 * d), F32),
                   jax.ShapeDtypeStruct((1, d), F32)],
        scratch_shapes=[pltpu.VMEM((16, 3 * d), F32)],
        compiler_params=_cp("arbitrary"),
    )(c16, dshift16, p16, dgate16, scale16, g)


def _modulated_norm(x, g, scale, shift):
    rstd = lax.rsqrt(jnp.mean(x * x, axis=-1, keepdims=True) + NORM_EPS)
    return ((x * rstd) * g) * (1.0 + scale) + shift


def normmod_matmul(x, g, scale, shift, w, out_dtype, seq, emit_h, name):
    t, d = x.shape
    n = w.shape[1]
    tm = _tile(seq, 512)
    tn = _col_tile(n, 2048)
    per_seq = seq // tm

    def body(x_ref, g_ref, sc_ref, sh_ref, w_ref, *rest):
        if emit_h:
            o_ref, h_ref, h_scr = rest
        else:
            o_ref, h_scr = rest

        @pl.when(pl.program_id(1) == 0)
        def _():
            h = _modulated_norm(x_ref[...], g_ref[...], sc_ref[0], sh_ref[0]).astype(BF16)
            h_scr[...] = h
            if emit_h:
                h_ref[...] = h

        o_ref[...] = jnp.dot(h_scr[...], w_ref[...], preferred_element_type=F32).astype(out_dtype)

    mod_spec = pl.BlockSpec((1, 1, d), lambda i, j: (i // per_seq, 0, 0))
    out_specs = [pl.BlockSpec((tm, tn), lambda i, j: (i, j))]
    out_shape = [jax.ShapeDtypeStruct((t, n), out_dtype)]
    if emit_h:
        out_specs.append(pl.BlockSpec((tm, d), lambda i, j: (i, 0)))
        out_shape.append(jax.ShapeDtypeStruct((t, d), BF16))
    return pl.pallas_call(
        body, name=name, grid=(t // tm, n // tn),
        in_specs=[pl.BlockSpec((tm, d), lambda i, j: (i, 0)), pl.BlockSpec((1, d), lambda i, j: (0, 0)),
                  mod_spec, mod_spec, pl.BlockSpec((d, tn), lambda i, j: (0, j))],
        out_specs=out_specs, out_shape=out_shape,
        scratch_shapes=[pltpu.VMEM((tm, d), BF16)],
        compiler_params=_cp("parallel", "arbitrary"),
    )(x, g, scale, shift, w)


def out_proj_fwd(o, zf, w_out, x, gate, seq, name):
    t, di = o.shape
    d = x.shape[1]
    tm = _tile(seq, 256)
    per_seq = seq // tm

    def body(o_ref, z_ref, w_ref, x_ref, gt_ref, xn_ref, y_ref, u_ref):
        z = z_ref[...]
        u = (o_ref[...] * (z * _sigmoid(z))).astype(BF16)
        y = jnp.dot(u, w_ref[...], preferred_element_type=F32)
        u_ref[...] = u
        y_ref[...] = y
        xn_ref[...] = x_ref[...] + gt_ref[0] * y

    row = lambda c: pl.BlockSpec((tm, c), lambda i: (i, 0))
    return pl.pallas_call(
        body, name=name, grid=(t // tm,),
        in_specs=[row(di), row(di), pl.BlockSpec((di, d), lambda i: (0, 0)), row(d),
                  pl.BlockSpec((1, 1, d), lambda i: (i // per_seq, 0, 0))],
        out_specs=[row(d), row(d), row(di)],
        out_shape=[jax.ShapeDtypeStruct((t, d), F32), jax.ShapeDtypeStruct((t, d), F32),
                   jax.ShapeDtypeStruct((t, di), BF16)],
        compiler_params=_cp("parallel"),
    )(o, zf, w_out, x, gate)


def out_proj_bwd(dxo, y, gate, w_out, o, zf, seq, name):
    t, d = dxo.shape
    di = o.shape[1]
    tm = _tile(seq, 256)
    per_seq = seq // tm

    def body(dx_ref, y_ref, gt_ref, w_ref, o_ref, z_ref, do_ref, dz_ref, dy_ref, dgt_ref):
        dx = dx_ref[...]
        part = jnp.sum(dx * y_ref[...], axis=0, keepdims=True)

        @pl.when(pl.program_id(0) % per_seq == 0)
        def _():
            dgt_ref[0] = part

        @pl.when(pl.program_id(0) % per_seq != 0)
        def _():
            dgt_ref[0] += part

        dy = (dx * gt_ref[0]).astype(BF16)
        dy_ref[...] = dy
        du = lax.dot_general(dy, w_ref[...], (((1,), (1,)), ((), ())), preferred_element_type=F32)
        z = z_ref[...]
        sg = _sigmoid(z)
        do_ref[...] = (du * (z * sg)).astype(BF16)
        dz_ref[...] = (du * o_ref[...] * (sg * (1.0 + z * (1.0 - sg)))).astype(BF16)

    row = lambda c: pl.BlockSpec((tm, c), lambda i: (i, 0))
    seq_spec = pl.BlockSpec((1, 1, d), lambda i: (i // per_seq, 0, 0))
    return pl.pallas_call(
        body, name=name, grid=(t // tm,),
        in_specs=[row(d), row(d), seq_spec, pl.BlockSpec((di, d), lambda i: (0, 0)), row(di), row(di)],
        out_specs=[row(di), row(di), row(d), seq_spec],
        out_shape=[jax.ShapeDtypeStruct((t, di), BF16), jax.ShapeDtypeStruct((t, di), BF16),
                   jax.ShapeDtypeStruct((t, d), BF16), jax.ShapeDtypeStruct((t // seq, 1, d), F32)],
        compiler_params=_cp("arbitrary"),
    )(dxo, y, gate, w_out, o, zf)


def tn_matmul(a, b, name):
    t, m = a.shape
    n = b.shape[1]
    tn = _col_tile(n, 1024)
    tk = _tile(t, 1024)
    nk = t // tk

    def body(a_ref, b_ref, o_ref, acc):
        part = lax.dot_general(a_ref[...], b_ref[...], (((0,), (0,)), ((), ())), preferred_element_type=F32)
        k = pl.program_id(1)

        @pl.when(k == 0)
        def _():
            acc[...] = part

        @pl.when(k != 0)
        def _():
            acc[...] += part

        @pl.when(k == nk - 1)
        def _():
            o_ref[...] = acc[...].astype(BF16)

    return pl.pallas_call(
        body, name=name, grid=(n // tn, nk),
        in_specs=[pl.BlockSpec((tk, m), lambda j, k: (k, 0)), pl.BlockSpec((tk, tn), lambda j, k: (k, j))],
        out_specs=pl.BlockSpec((m, tn), lambda j, k: (0, j)),
        out_shape=jax.ShapeDtypeStruct((m, n), BF16),
        scratch_shapes=[pltpu.VMEM((m, tn), F32)],
        compiler_params=_cp("parallel", "arbitrary"),
    )(a, b)


def dh_norm_bwd(parts, w_t, extra, x, g, scale, dxo, seq, name, rider=None):
    t, d = x.shape
    nparts = len(parts)
    kw = parts[0].shape[1]
    tm = _tile(seq, 512)
    per_seq = seq // tm
    has_extra = extra is not None
    grid = (t // tm,)
    nr = rider.n if rider else 0
    n_in = nparts + 1 + (2 if has_extra else 0) + 4

    def body(*refs):
        p_refs = refs[:nparts]
        w_ref = refs[nparts]
        pos = nparts + 1
        if has_extra:
            e_ref, we_ref = refs[pos], refs[pos + 1]
            pos += 2
        x_ref, g_ref, sc_ref, dxo_ref = refs[pos:pos + 4]
        dx_ref, pp_ref, ss_ref = refs[n_in + nr:n_in + nr + 3]
        if rider:
            _ride(rider, refs[n_in:n_in + nr] + refs[n_in + nr + 3:], *_grid_ends(grid))
        i = pl.program_id(0)

        dh = jnp.dot(p_refs[0][...], w_ref[0:kw, :], preferred_element_type=F32)
        for kk in range(1, nparts):
            dh = dh + jnp.dot(p_refs[kk][...], w_ref[kk * kw:(kk + 1) * kw, :], preferred_element_type=F32)
        if has_extra:
            dh = dh + jnp.dot(e_ref[...], we_ref[...], preferred_element_type=F32)
        xx = x_ref[...]
        rstd = lax.rsqrt(jnp.mean(xx * xx, axis=-1, keepdims=True) + NORM_EPS)
        xhat = xx * rstd
        dxhat = dh * (g_ref[...] * (1.0 + sc_ref[0]))
        dx_ref[...] = dxo_ref[...] + rstd * (dxhat - xhat * jnp.mean(dxhat * xhat, axis=-1, keepdims=True))
        pp = jnp.sum(dh * xhat, axis=0, keepdims=True)
        ss = jnp.sum(dh, axis=0, keepdims=True)

        @pl.when(i % per_seq == 0)
        def _():
            pp_ref[0] = pp
            ss_ref[0] = ss

        @pl.when(i % per_seq != 0)
        def _():
            pp_ref[0] += pp
            ss_ref[0] += ss

    row = lambda c: pl.BlockSpec((tm, c), lambda i: (i, 0))
    whole = lambda a: pl.BlockSpec(a.shape, lambda i: (0, 0), pipeline_mode=pl.Buffered(1))
    seq_spec = pl.BlockSpec((1, 1, d), lambda i: (i // per_seq, 0, 0))
    in_specs = [row(kw)] * nparts + [whole(w_t)]
    args = list(parts) + [w_t]
    if has_extra:
        in_specs += [row(extra[0].shape[1]), whole(extra[1])]
        args += list(extra)
    in_specs += [row(d), pl.BlockSpec((1, d), lambda i: (0, 0)), seq_spec, row(d)]
    args += [x, g, scale, dxo]
    return pl.pallas_call(
        body, name=name, grid=grid,
        in_specs=in_specs + (rider.in_specs if rider else []),
        out_specs=[row(d), seq_spec, seq_spec] + (rider.out_specs if rider else []),
        out_shape=[jax.ShapeDtypeStruct((t, d), F32), jax.ShapeDtypeStruct((t // seq, 1, d), F32),
                   jax.ShapeDtypeStruct((t // seq, 1, d), F32)] + (rider.out_shape if rider else []),
        scratch_shapes=rider.scratch if rider else [],
        compiler_params=_cp("arbitrary"),
    )(*args, *(rider.arrs if rider else []))


def final_loss(x, g, target, name):
    t, d = x.shape
    tm = _tile(t, 512)

    def body(x_ref, g_ref, t_ref, dx_ref, dg_ref, sq_ref):
        xx = x_ref[...]
        gg = g_ref[...]
        rstd = lax.rsqrt(jnp.mean(xx * xx, axis=-1, keepdims=True) + NORM_EPS)
        xhat = xx * rstd
        err = xhat * gg - t_ref[...]
        dy = err * (1.0 / d)
        dxhat = dy * gg
        dx_ref[...] = rstd * (dxhat - xhat * jnp.mean(dxhat * xhat, axis=-1, keepdims=True))
        dg = jnp.sum(dy * xhat, axis=0, keepdims=True)
        sq = jnp.sum(err * err, axis=0, keepdims=True)

        @pl.when(pl.program_id(0) == 0)
        def _():
            dg_ref[...] = dg
            sq_ref[...] = sq

        @pl.when(pl.program_id(0) != 0)
        def _():
            dg_ref[...] += dg
            sq_ref[...] += sq

    row = pl.BlockSpec((tm, d), lambda i: (i, 0))
    vec = pl.BlockSpec((1, d), lambda i: (0, 0))
    return pl.pallas_call(
        body, name=name, grid=(t // tm,),
        in_specs=[row, vec, row], out_specs=[row, vec, vec],
        out_shape=[jax.ShapeDtypeStruct((t, d), F32), jax.ShapeDtypeStruct((1, d), F32),
                   jax.ShapeDtypeStruct((1, d), F32)],
        compiler_params=_cp("arbitrary"),
    )(x, g, target)


def _split3(x):
    x1 = x.astype(BF16)
    r1 = x - x1.astype(F32)
    x2 = r1.astype(BF16)
    x3 = (r1 - x2.astype(F32)).astype(BF16)
    return x1, x2, x3


def _dot3(x, u):
    x1, x2, x3 = _split3(x)
    return (jnp.dot(x1, u, preferred_element_type=F32) + jnp.dot(x2, u, preferred_element_type=F32)
            + jnp.dot(x3, u, preferred_element_type=F32))


def _log_sigmoid(x):
    return jnp.minimum(x, 0.0) - jnp.log1p(jnp.exp(-jnp.abs(x)))


def forget_cumsum(fl_t, b_f, name):
    b, h, s = fl_t.shape

    def body(f_ref, b_ref, o_ref):
        lf = _log_sigmoid(f_ref[0] + b_ref[...])
        u = (lax.broadcasted_iota(jnp.int32, (s, s), 0) <= lax.broadcasted_iota(jnp.int32, (s, s), 1)).astype(BF16)
        o_ref[0] = _dot3(lf, u)

    return pl.pallas_call(
        body, name=name, grid=(b,),
        in_specs=[pl.BlockSpec((1, h, s), lambda i: (i, 0, 0)), pl.BlockSpec((h, 1), lambda i: (0, 0))],
        out_specs=pl.BlockSpec((1, h, s), lambda i: (i, 0, 0)),
        out_shape=jax.ShapeDtypeStruct((b, h, s), F32),
        compiler_params=_cp("parallel"),
    )(fl_t, b_f)


def forget_cumsum_bwd(dcum, fl_t, b_f, name):
    b, h, s = fl_t.shape

    def body(d_ref, f_ref, b_ref, o_ref, db_ref):
        u = (lax.broadcasted_iota(jnp.int32, (s, s), 0) >= lax.broadcasted_iota(jnp.int32, (s, s), 1)).astype(BF16)
        dlf = _dot3(d_ref[0], u)
        df = dlf * _sigmoid(-(f_ref[0] + b_ref[...]))
        o_ref[0] = df
        part = jnp.sum(df, axis=-1, keepdims=True)

        @pl.when(pl.program_id(0) == 0)
        def _():
            db_ref[...] = part

        @pl.when(pl.program_id(0) != 0)
        def _():
            db_ref[...] += part

    blk = pl.BlockSpec((1, h, s), lambda i: (i, 0, 0))
    return pl.pallas_call(
        body, name=name, grid=(b,),
        in_specs=[blk, blk, pl.BlockSpec((h, 1), lambda i: (0, 0))],
        out_specs=[blk, pl.BlockSpec((h, 1), lambda i: (0, 0))],
        out_shape=[jax.ShapeDtypeStruct((b, h, s), F32), jax.ShapeDtypeStruct((h, 1), F32)],
        compiler_params=_cp("arbitrary"),
    )(dcum, fl_t, b_f)


def _nt(a, b):
    return lax.dot_general(a, b, (((1,), (1,)), ((), ())), preferred_element_type=F32)


def _attn_dims(t, seq, dh):
    tq = _tile(seq, 512)
    tk = _tile(seq, 256)
    return t // seq, tq, tk, seq // tq, tq // tk, ATTN_LANES // dh


def _blocks_per_trip(r):
    return ATTN_BLOCKS_PER_TRIP if r % ATTN_BLOCKS_PER_TRIP == 0 else 1


def _scaled_q(q_ref, dh):
    scale = dh ** -0.5
    if math.log2(dh) % 2 == 0:
        return (q_ref[...].astype(F32) * scale).astype(BF16), None
    return q_ref[...], scale


def _diag_masks(tq, tk, r, strict):
    row = lax.broadcasted_iota(jnp.int32, (tq, tk), 0)
    col = lax.broadcasted_iota(jnp.int32, (tq, tk), 1)
    return [(col + d * tk < row) if strict else (col + d * tk <= row) for d in range(r)]


def _rows(state, lo):
    return tuple(a[lo:] for a in state)


def _put_rows(state, part, lo):
    if lo == 0:
        return tuple(part)
    return tuple(jnp.concatenate([a[:lo], p], axis=0) for a, p in zip(state, part))


def _transposed(x):
    return jnp.transpose(x.astype(F32)).astype(BF16)


def _store_transposed(dst_ref, src_scr, tk):
    for jb in range(src_scr.shape[0]):
        dst_ref[jb * tk:(jb + 1) * tk, :] = jnp.transpose(src_scr[jb]).astype(BF16)


def _dot2(x, u):
    hi = x.astype(BF16)
    lo = (x - hi.astype(F32)).astype(BF16)
    return jnp.dot(jnp.concatenate([hi, lo], axis=1), jnp.concatenate([u, u], axis=0), preferred_element_type=F32)


def _attn_specs(seq, di, tq):
    nq = seq // tq
    cb = di // ATTN_LANES
    q_spec = pl.BlockSpec((tq, ATTN_LANES), lambda b, hp, i: (b * nq + i, hp))
    k_spec = pl.BlockSpec((seq, ATTN_LANES), lambda b, hp, i: (b, cb + hp))
    v_spec = pl.BlockSpec((seq, ATTN_LANES), lambda b, hp, i: (b, 2 * cb + hp))
    kv_out = pl.BlockSpec((seq, ATTN_LANES), lambda b, hp, i: (b, hp))
    return q_spec, k_spec, v_spec, kv_out


def _grid_ends(grid):
    ids = [pl.program_id(a) for a in range(len(grid))]
    first = functools.reduce(jnp.logical_and, [p == 0 for p in ids])
    last = functools.reduce(jnp.logical_and, [p == g - 1 for p, g in zip(ids, grid)])
    return first, last


def fox_fwd(qkv, ck, seq, di, dh, name, rider=None):
    t = qkv.shape[0]
    nb, tq, tk, nq, r, hpb = _attn_dims(t, seq, dh)
    per_trip = _blocks_per_trip(r)
    heads = [slice(hh * dh, (hh + 1) * dh) for hh in range(hpb)]
    assert hpb >= 2, "the row sums of p ride in another head's lanes of the p @ v matmul"
    q_spec, k_spec, v_spec, _ = _attn_specs(seq, di, tq)
    grid = (nb, di // ATTN_LANES, nq)
    nr = rider.n if rider else 0

    def body(q_ref, k_ref, v_ref, ck_ref, *rest):
        o_ref, lse_ref = rest[nr:nr + 2]
        if rider:
            _ride(rider, rest[:nr] + rest[nr + 2:], *_grid_ends(grid))
        i = pl.program_id(2)
        masks = _diag_masks(tq, tk, r, False)
        q, scale = _scaled_q(q_ref, dh)
        qs = [q[:, hd] for hd in heads]
        lane_head = lax.broadcasted_iota(jnp.int32, (tk, ATTN_LANES), 1) // dh

        def scores(hh, j):
            start = pl.multiple_of(j * tk, tk)
            kh = k_ref[pl.ds(start, tk), heads[hh]]
            vt = v_ref[pl.ds(start, tk), :]
            vh = jnp.where(lane_head == hh, vt, jnp.ones_like(vt))
            s = _nt(qs[hh], kh)
            if scale is not None:
                s = s * scale
            return s - ck_ref[0, hh, pl.ds(j, 1), :], vh

        def update(state, s, vh):
            m, acc = state
            m_new = jnp.maximum(m, jnp.max(s, axis=-1, keepdims=True))
            p = jnp.exp(s - m_new)
            return m_new, jnp.exp(m - m_new) * acc + _dot2(p, vh)

        states = []
        for hh in range(hpb):
            s, vh = scores(hh, r * i)
            s = jnp.where(masks[0], s, -jnp.inf)
            m = jnp.max(s, axis=-1, keepdims=True)
            states.append((m, _dot2(jnp.exp(s - m), vh)))
        for d in range(1, r):
            for hh in range(hpb):
                s, vh = scores(hh, r * i + d)
                states[hh] = update(states[hh], jnp.where(masks[d], s, -jnp.inf), vh)

        def step(jj, states):
            for sub in range(per_trip):
                states = tuple(update(states[hh], *scores(hh, r * i - 1 - (per_trip * jj + sub))) for hh in range(hpb))
            return states

        states = lax.fori_loop(0, r * i // per_trip, step, tuple(states))
        for hh in range(hpb):
            m, acc = states[hh]
            other = heads[(hh + 1) % hpb].start
            l = acc[:, other:other + 1]
            o_ref[:, heads[hh]] = acc[:, heads[hh]] / l
            lse_ref[0, 0, :, hh:hh + 1] = m + jnp.log(l)

    return pl.pallas_call(
        body, name=name, grid=grid,
        in_specs=[q_spec, k_spec, v_spec,
                  pl.BlockSpec((1, hpb, seq // tk, tk), lambda b, hp, i: (b, hp, 0, 0))] + (rider.in_specs if rider else []),
        out_specs=[q_spec, pl.BlockSpec((1, 1, tq, hpb), lambda b, hp, i: (b, hp, i, 0))]
        + (rider.out_specs if rider else []),
        out_shape=[jax.ShapeDtypeStruct((t, di), F32), jax.ShapeDtypeStruct((nb, di // ATTN_LANES, seq, hpb), F32)]
        + (rider.out_shape if rider else []),
        scratch_shapes=rider.scratch if rider else [],
        compiler_params=_cp("arbitrary", "arbitrary", "arbitrary"),
    )(qkv, qkv, qkv, ck, *(rider.arrs if rider else []))


def fox_bwd(qkv, ck, o, do, lse, seq, di, dh, name, rider=None):
    t = qkv.shape[0]
    nb, tq, tk, nq, r, hpb = _attn_dims(t, seq, dh)
    per_trip = _blocks_per_trip(r)
    heads = [slice(hh * dh, (hh + 1) * dh) for hh in range(hpb)]
    q_spec, k_spec, v_spec, kv_out = _attn_specs(seq, di, tq)
    ck_spec = pl.BlockSpec((1, hpb, seq // tk, tk), lambda b, hp, i: (b, hp, 0, 0))
    grid = (nb, di // ATTN_LANES, nq)
    nr = rider.n if rider else 0

    def body(q_ref, k_ref, v_ref, ck_ref, o_ref, do_ref, lse_ref, *rest):
        dq_ref, dk_ref, dv_ref, dck_ref = rest[nr:nr + 4]
        dkt_scr, dvt_scr = rest[2 * nr + 4:2 * nr + 6]
        if rider:
            _ride(rider, rest[:nr] + rest[nr + 4:2 * nr + 4] + rest[2 * nr + 6:], *_grid_ends(grid))
        i = pl.program_id(2)

        @pl.when(i == 0)
        def _():
            dkt_scr[...] = jnp.zeros_like(dkt_scr)
            dvt_scr[...] = jnp.zeros_like(dvt_scr)
            dck_ref[...] = jnp.zeros_like(dck_ref)

        masks = _diag_masks(tq, tk, r, False)
        q, scale = _scaled_q(q_ref, dh)
        do = do_ref[...]
        q_t, do_t = _transposed(q), _transposed(do)
        qs = [q[:, hd] for hd in heads]
        dos = [do[:, hd] for hd in heads]
        deltas = [jnp.sum(dos[hh].astype(F32) * o_ref[:, heads[hh]], axis=-1, keepdims=True) for hh in range(hpb)]
        lses = [lse_ref[0, 0, :, hh:hh + 1] for hh in range(hpb)]

        def block(hh, j, mask, dq_acc, lo=0):
            start = pl.multiple_of(j * tk, tk)
            kh = k_ref[pl.ds(start, tk), heads[hh]]
            vh = v_ref[pl.ds(start, tk), heads[hh]]
            s = _nt(qs[hh][lo:], kh)
            if scale is not None:
                s = s * scale
            p = jnp.exp(s - ck_ref[0, hh, pl.ds(j, 1), :] - lses[hh][lo:])
            if mask is not None:
                p = jnp.where(mask[lo:], p, 0.0)
            ds = p * (_nt(dos[hh][lo:], vh) - deltas[hh][lo:])
            dsb = ds.astype(BF16)
            dkt = jnp.dot(q_t[heads[hh], lo:], dsb, preferred_element_type=F32)
            if scale is not None:
                dkt = dkt * scale
            dkt_scr[j, heads[hh], :] += dkt
            dvt_scr[j, heads[hh], :] += jnp.dot(do_t[heads[hh], lo:], p.astype(BF16), preferred_element_type=F32)
            dck_ref[0, hh, pl.ds(j, 1), :] -= jnp.sum(ds, axis=0, keepdims=True)
            return dq_acc + jnp.dot(dsb, kh, preferred_element_type=F32)

        accs = [jnp.zeros((tq, dh), F32)] * hpb
        for d in range(r):
            lo = d * tk
            accs = [_put_rows((accs[hh],), (block(hh, r * i + d, masks[d], accs[hh][lo:], lo),), lo)[0]
                    for hh in range(hpb)]

        def step(jj, accs):
            for sub in range(per_trip):
                accs = tuple(block(hh, per_trip * jj + sub, None, accs[hh]) for hh in range(hpb))
            return accs

        accs = lax.fori_loop(0, r * i // per_trip, step, tuple(accs))
        for hh in range(hpb):
            dq_ref[:, heads[hh]] = (accs[hh] * dh ** -0.5).astype(BF16)

        @pl.when(i == nq - 1)
        def _():
            _store_transposed(dk_ref, dkt_scr, tk)
            _store_transposed(dv_ref, dvt_scr, tk)

    return pl.pallas_call(
        body, name=name, grid=grid,
        in_specs=[q_spec, k_spec, v_spec, ck_spec, q_spec, q_spec,
                  pl.BlockSpec((1, 1, tq, hpb), lambda b, hp, i: (b, hp, i, 0))] + (rider.in_specs if rider else []),
        out_specs=[q_spec, kv_out, kv_out, ck_spec] + (rider.out_specs if rider else []),
        out_shape=[jax.ShapeDtypeStruct((t, di), BF16)] * 3 + [jax.ShapeDtypeStruct(ck.shape, F32)]
        + (rider.out_shape if rider else []),
        scratch_shapes=[pltpu.VMEM((seq // tk, ATTN_LANES, tk), F32), pltpu.VMEM((seq // tk, ATTN_LANES, tk), F32)]
        + (rider.scratch if rider else []),
        compiler_params=_cp("arbitrary", "arbitrary", "arbitrary"),
    )(qkv, qkv, qkv, ck, o, do, lse, *(rider.arrs if rider else []))


def _sb_logits(qh, kh, scale, strict):
    z = _nt(qh, kh)
    if scale is not None:
        z = z * scale
    e = jnp.exp(-jnp.abs(z))
    lb = jnp.minimum(z, 0.0) - jnp.log(1.0 + e)
    lk = lb - z
    return lb, lk if strict is None else jnp.where(strict, lk, 0.0)


def _cum(x, u):
    return jnp.dot(x.astype(BF16), u, preferred_element_type=F32)


def sb_fwd(qkv, seq, di, dh, name):
    t = qkv.shape[0]
    nb, tq, tk, nq, r, hpb = _attn_dims(t, seq, dh)
    per_trip = _blocks_per_trip(r)
    heads = [slice(hh * dh, (hh + 1) * dh) for hh in range(hpb)]
    q_spec, k_spec, v_spec, _ = _attn_specs(seq, di, tq)

    def body(q_ref, k_ref, v_ref, o_ref, rt_ref):
        i = pl.program_id(2)
        masks = _diag_masks(tq, tk, r, True)
        row = lax.broadcasted_iota(jnp.int32, (tk, tk), 0)
        col = lax.broadcasted_iota(jnp.int32, (tk, tk), 1)
        u_after = (row > col).astype(BF16)
        q, scale = _scaled_q(q_ref, dh)
        qs = [q[:, hd] for hd in heads]

        def block(hh, j, mask, state, lo=0):
            rr, acc = state
            start = pl.multiple_of(j * tk, tk)
            kh = k_ref[pl.ds(start, tk), heads[hh]]
            vh = v_ref[pl.ds(start, tk), heads[hh]]
            if mask is not None:
                mask = mask[lo:]
            lb, lk = _sb_logits(qs[hh][lo:], kh, scale, mask)
            a = jnp.exp(lb + _cum(lk, u_after) + rr)
            if mask is not None:
                a = jnp.where(mask, a, 0.0)
            acc = acc + jnp.dot(a.astype(BF16), vh, preferred_element_type=F32)
            return rr + jnp.sum(lk, axis=-1, keepdims=True), acc

        states = [(jnp.zeros((tq, 1), F32), jnp.zeros((tq, dh), F32))] * hpb
        for d in reversed(range(r)):
            lo = d * tk
            states = [_put_rows(states[hh], block(hh, r * i + d, masks[d], _rows(states[hh], lo), lo), lo)
                      for hh in range(hpb)]

        def step(jj, states):
            for sub in range(per_trip):
                states = tuple(block(hh, r * i - 1 - (per_trip * jj + sub), None, states[hh]) for hh in range(hpb))
            return states

        states = lax.fori_loop(0, r * i // per_trip, step, tuple(states))
        for hh in range(hpb):
            o_ref[:, heads[hh]] = states[hh][1]
            rt_ref[0, 0, :, hh:hh + 1] = states[hh][0]

    return pl.pallas_call(
        body, name=name, grid=(nb, di // ATTN_LANES, nq),
        in_specs=[q_spec, k_spec, v_spec],
        out_specs=[q_spec, pl.BlockSpec((1, 1, tq, hpb), lambda b, hp, i: (b, hp, i, 0))],
        out_shape=[jax.ShapeDtypeStruct((t, di), F32), jax.ShapeDtypeStruct((nb, di // ATTN_LANES, seq, hpb), F32)],
        compiler_params=_cp("parallel", "parallel", "arbitrary"),
    )(qkv, qkv, qkv)


def sb_bwd(qkv, do, rtot, seq, di, dh, name):
    t = qkv.shape[0]
    nb, tq, tk, nq, r, hpb = _attn_dims(t, seq, dh)
    per_trip = _blocks_per_trip(r)
    heads = [slice(hh * dh, (hh + 1) * dh) for hh in range(hpb)]
    q_spec, k_spec, v_spec, kv_out = _attn_specs(seq, di, tq)

    def body(q_ref, k_ref, v_ref, do_ref, rt_ref, dq_ref, dk_ref, dv_ref, dkt_scr, dvt_scr):
        i = pl.program_id(2)

        @pl.when(i == 0)
        def _():
            dkt_scr[...] = jnp.zeros_like(dkt_scr)
            dvt_scr[...] = jnp.zeros_like(dvt_scr)

        masks = _diag_masks(tq, tk, r, True)
        row = lax.broadcasted_iota(jnp.int32, (tk, tk), 0)
        col = lax.broadcasted_iota(jnp.int32, (tk, tk), 1)
        u_after = (row > col).astype(BF16)
        u_before = (row < col).astype(BF16)
        q, scale = _scaled_q(q_ref, dh)
        do = do_ref[...]
        q_t, do_t = _transposed(q), _transposed(do)
        qs = [q[:, hd] for hd in heads]
        dos = [do[:, hd] for hd in heads]
        rts = [rt_ref[0, 0, :, hh:hh + 1] for hh in range(hpb)]

        def block(hh, j, mask, state, lo=0):
            lc, gc, dq_acc = state
            start = pl.multiple_of(j * tk, tk)
            kh = k_ref[pl.ds(start, tk), heads[hh]]
            vh = v_ref[pl.ds(start, tk), heads[hh]]
            if mask is not None:
                mask = mask[lo:]
            lb, lk = _sb_logits(qs[hh][lo:], kh, scale, mask)
            lc = lc + jnp.sum(lk, axis=-1, keepdims=True)
            a = jnp.exp(lb + ((rts[hh][lo:] - lc) + _cum(lk, u_after)))
            if mask is not None:
                a = jnp.where(mask, a, 0.0)
            de = a * _nt(dos[hh][lo:], vh)
            g = gc + _cum(de, u_before)
            dz = de - jnp.exp(lb) * (de + g)
            if mask is not None:
                dz = jnp.where(mask, dz, 0.0)
            dzb = dz.astype(BF16)
            dkt = jnp.dot(q_t[heads[hh], lo:], dzb, preferred_element_type=F32)
            if scale is not None:
                dkt = dkt * scale
            dkt_scr[j, heads[hh], :] += dkt
            dvt_scr[j, heads[hh], :] += jnp.dot(do_t[heads[hh], lo:], a.astype(BF16), preferred_element_type=F32)
            return (lc, gc + jnp.sum(de, axis=-1, keepdims=True),
                    dq_acc + jnp.dot(dzb, kh, preferred_element_type=F32))

        zero = jnp.zeros((tq, 1), F32)

        def step(jj, states):
            for sub in range(per_trip):
                states = tuple(block(hh, per_trip * jj + sub, None, states[hh]) for hh in range(hpb))
            return states

        states = lax.fori_loop(0, r * i // per_trip, step, ((zero, zero, jnp.zeros((tq, dh), F32)),) * hpb)
        for d in range(r):
            lo = d * tk
            states = [_put_rows(states[hh], block(hh, r * i + d, masks[d], _rows(states[hh], lo), lo), lo)
                      for hh in range(hpb)]
        for hh in range(hpb):
            dq_ref[:, heads[hh]] = (states[hh][2] * dh ** -0.5).astype(BF16)

        @pl.when(i == nq - 1)
        def _():
            _store_transposed(dk_ref, dkt_scr, tk)
            _store_transposed(dv_ref, dvt_scr, tk)

    return pl.pallas_call(
        body, name=name, grid=(nb, di // ATTN_LANES, nq),
        in_specs=[q_spec, k_spec, v_spec, q_spec, pl.BlockSpec((1, 1, tq, hpb), lambda b, hp, i: (b, hp, i, 0))],
        out_specs=[q_spec, kv_out, kv_out],
        out_shape=[jax.ShapeDtypeStruct((t, di), BF16)] * 3,
        scratch_shapes=[pltpu.VMEM((seq // tk, ATTN_LANES, tk), F32), pltpu.VMEM((seq // tk, ATTN_LANES, tk), F32)],
        compiler_params=_cp("parallel", "parallel", "arbitrary"),
    )(qkv, qkv, qkv, do, rtot)


def _cols(g):
    return jnp.transpose(g, (1, 0, 2)).reshape(g.shape[1], NDEV * g.shape[2])


def _col_blocks(w):
    r, c8 = w.shape
    return jnp.transpose(w.reshape(r, NDEV, c8 // NDEV), (1, 0, 2))


def _pad_rows16(a):
    return jnp.pad(a, ((0, 16 - a.shape[0]), (0, 0)))


def _pad_cols(a, n):
    return jnp.pad(a, ((0, 0), (0, n - a.shape[1])))


def kernel(x, c, fox_norm_g, fox_w_ada, fox_b_ada, fox_w_in, fox_b_f, fox_w_out, sb_norm_g, sb_w_ada, sb_b_ada, sb_w_in, sb_w_out, final_norm_g, loss_target, m_fox_norm_g, m_fox_w_ada, m_fox_b_ada, m_fox_w_in, m_fox_b_f, m_fox_w_out, m_sb_norm_g, m_sb_w_ada, m_sb_b_ada, m_sb_w_in, m_sb_w_out, m_final_norm_g, v_fox_norm_g, v_fox_w_ada, v_fox_b_ada, v_fox_w_in, v_fox_b_f, v_fox_w_out, v_sb_norm_g, v_sb_w_ada, v_sb_b_ada, v_sb_w_in, v_sb_w_out, v_final_norm_g):
    nb, seq, d = x.shape
    t = nb * seq
    h = fox_b_f.shape[-1]
    di = fox_w_out.shape[1] * NDEV
    dh = di // h
    tq = _tile(seq, 256)
    me = _my_index()

    gathered = all_gather_by_chip([w[0].astype(BF16) for w in (fox_w_ada, fox_w_in, fox_w_out)], "gather_fox_weights")
    fox_wada, fox_win = _cols(gathered[0]), _cols(gathered[1])
    fox_wout = gathered[2].reshape(di, d)
    sb_gather = Exchange("gather", [w[0].astype(BF16) for w in (sb_w_ada, sb_w_in, sb_w_out)] + [sb_norm_g, sb_b_ada])

    x0 = x.reshape(t, d)
    target = loss_target.reshape(t, d)
    c16 = _pad_rows16(c)

    def layer_fwd(xin, g, wada, bada, win, wout, b_f, tag, rider=None):
        mod = adaln_fwd(c16, wada, bada, tag + "_adaln")[:nb]
        shift, scale, gate = (mod[:, k * d:(k + 1) * d].reshape(nb, 1, d) for k in range(3))
        qkv, hmod = normmod_matmul(xin, g, scale, shift, win[:, :3 * di], BF16, seq, True, tag + "_qkv")
        w_z = win[:, 3 * di:]
        if b_f is not None:
            w_z = _pad_cols(w_z, di + 128)
        (zf,) = normmod_matmul(xin, g, scale, shift, w_z, F32, seq, False, tag + "_z")
        saved = dict(x=xin, g=g, scale=scale, gate=gate, qkv=qkv, h=hmod, zf=zf, win=win, wout=wout)
        if b_f is not None:
            fl_t = jnp.transpose(zf[:, di:di + h].reshape(nb, seq, h), (0, 2, 1))
            bf_col = b_f.reshape(h, 1)
            cum = forget_cumsum(fl_t, bf_col, tag + "_cum")
            ck = cum.reshape(nb, h, seq // tq, tq)
            o, lse, *rode = fox_fwd(qkv, ck, seq, di, dh, tag + "_attn", rider)
            saved.update(fl_t=fl_t, bf_col=bf_col, ck=ck, lse=lse, rode=rode)
        else:
            o, rtot = sb_fwd(qkv, seq, di, dh, tag + "_attn")
            saved.update(rtot=rtot)
        xout, y, u = out_proj_fwd(o, zf, wout, xin, gate, seq, tag + "_out")
        saved.update(o=o, y=y, u=u)
        return xout, saved

    def layer_bwd(dxo, sv, is_fox, tag, rider=None):
        do, dz, dy, dgate = out_proj_bwd(dxo, sv["y"], sv["gate"], sv["wout"], sv["o"], sv["zf"], seq, tag + "_dout")
        dwout = tn_matmul(sv["u"], dy, tag + "_dwout")
        extra = None
        if is_fox:
            dq, dk, dv, dck, *rode = fox_bwd(sv["qkv"], sv["ck"], sv["o"], do, sv["lse"], seq, di, dh, tag + "_dattn",
                                             rider)
            df_t, dbf = forget_cumsum_bwd(dck.reshape(nb, h, seq), sv["fl_t"], sv["bf_col"], tag + "_dcum")
            df = _pad_cols(jnp.transpose(df_t, (0, 2, 1)).reshape(t, h), 128).astype(BF16)
            extra = (df, jnp.transpose(_pad_cols(sv["win"][:, 4 * di:], 128)))
        else:
            dq, dk, dv = sb_bwd(sv["qkv"], do, sv["rtot"], seq, di, dh, tag + "_dattn")
        parts = [dq, dk, dv, dz]
        dwin = [tn_matmul(sv["h"], p, tag + "_dwin%d" % k) for k, p in enumerate(parts)]
        if is_fox:
            dwin.append(tn_matmul(sv["h"], df, tag + "_dwinf")[:, :h])
        dwin = jnp.concatenate(dwin, axis=1)
        w_t = jnp.transpose(sv["win"][:, :4 * di])
        win_blocks, wout_blocks = _col_blocks(dwin), dwout.reshape(NDEV, di // NDEV, d)
        dh_rider = Exchange("scatter", [win_blocks, wout_blocks]) if is_fox else None
        dxin, psum, ssum, *landed = dh_norm_bwd(parts, w_t, extra, sv["x"], sv["g"], sv["scale"], dxo, seq,
                                                tag + "_dh", dh_rider)
        pad = lambda a: _pad_rows16(a.reshape(nb, d))
        dwada, dbada, dng = adaln_bwd(c16, pad(ssum), pad(psum), pad(dgate), pad(sv["scale"]), sv["g"], tag + "_dadaln")
        grads = dict(wada=dwada, bada=dbada, ng=dng, win=win_blocks, wout=wout_blocks)
        if is_fox:
            grads["bf"] = dbf.reshape(1, h)
            grads["rode"] = rode
            grads["landed"] = landed
        return dxin, grads

    x1, sv_fox = layer_fwd(x0, fox_norm_g, fox_wada, fox_b_ada, fox_win, fox_wout, fox_b_f, "fox", sb_gather)
    sb_wada, sb_win = _cols(sv_fox["rode"][0]), _cols(sv_fox["rode"][1])
    sb_wout = sv_fox["rode"][2].reshape(di, d)
    sb_g, sb_bada = _cols(sv_fox["rode"][3]), _cols(sv_fox["rode"][4])
    x2, sv_sb = layer_fwd(x1, sb_g, sb_wada, sb_bada, sb_win, sb_wout, None, "sb")
    dx2, dgf, sq = final_loss(x2, final_norm_g.reshape(1, d), target, "loss_head")
    dx1, g_sb = layer_bwd(dx2, sv_sb, False, "sb")
    dx0, g_fox = layer_bwd(dx1, sv_fox, True, "fox", Exchange("scatter", [g_sb["wada"], g_sb["win"], g_sb["wout"]]))

    land = list(all_to_all([g_fox["wada"]], "scatter_fox_wada_grad")) + list(g_fox["landed"]) + list(g_fox["rode"])
    small = jnp.concatenate([g_fox["ng"], g_fox["bada"], _pad_cols(g_fox["bf"], 128), g_sb["ng"], g_sb["bada"],
                             dgf, sq], axis=1)
    (small_all,) = all_gather([small], "gather_small")
    small_sum, loss_row = sum_slots(small_all, d, "sum_small")
    loss = loss_row[0, 0]

    offs = {}
    pos = 0
    for nm, width in (("fox_ng", d), ("fox_bada", 3 * d), ("fox_bf", 128), ("sb_ng", d), ("sb_bada", 3 * d), ("fin_g", d)):
        offs[nm] = (pos, width)
        pos += width

    def small_grad(nm, width=None, shard=False):
        p0, wd = offs[nm]
        wd = width or wd
        if shard:
            blk = wd // NDEV
            return lax.dynamic_slice(small_sum, (0, p0 + me * blk), (1, blk))
        return small_sum[:, p0:p0 + wd]

    results = {}
    big_params = [("fox_w_ada", fox_w_ada, m_fox_w_ada, v_fox_w_ada), ("fox_w_in", fox_w_in, m_fox_w_in, v_fox_w_in),
                  ("fox_w_out", fox_w_out, m_fox_w_out, v_fox_w_out), ("sb_w_ada", sb_w_ada, m_sb_w_ada, v_sb_w_ada),
                  ("sb_w_in", sb_w_in, m_sb_w_in, v_sb_w_in), ("sb_w_out", sb_w_out, m_sb_w_out, v_sb_w_out)]
    for k, (nm, w, m, v) in enumerate(big_params):
        outs = adamw(land[k], w[0], m[0], v[0], "adamw_" + nm)
        results[nm] = [o[None] for o in outs]
    small_params = [("fox_norm_g", fox_norm_g, m_fox_norm_g, v_fox_norm_g, small_grad("fox_ng")),
                    ("fox_b_ada", fox_b_ada, m_fox_b_ada, v_fox_b_ada, small_grad("fox_bada")),
                    ("fox_b_f", fox_b_f, m_fox_b_f, v_fox_b_f, small_grad("fox_bf", h)),
                    ("sb_norm_g", sb_norm_g, m_sb_norm_g, v_sb_norm_g, small_grad("sb_ng", shard=True)),
                    ("sb_b_ada", sb_b_ada, m_sb_b_ada, v_sb_b_ada, small_grad("sb_bada", shard=True)),
                    ("final_norm_g", final_norm_g.reshape(1, d), m_final_norm_g.reshape(1, d),
                     v_final_norm_g.reshape(1, d), small_grad("fin_g"))]
    for nm, w, m, v, g in small_params:
        outs = adamw(g[None], w, m, v, "adamw_" + nm)
        if nm == "final_norm_g":
            outs = [o.reshape(d) for o in outs]
        results[nm] = outs

    order = ["fox_norm_g", "fox_w_ada", "fox_b_ada", "fox_w_in", "fox_b_f", "fox_w_out", "sb_norm_g", "sb_w_ada",
             "sb_b_ada", "sb_w_in", "sb_w_out", "final_norm_g"]
    out = [loss, dx0.reshape(nb, seq, d)]
    for k in range(4):
        out += [results[nm][k] for nm in order]
    return tuple(out)
```

```python
import functools
import math

import jax
import jax.numpy as jnp
from jax import lax
from jax.experimental import pallas as pl
from jax.experimental.pallas import tpu as pltpu

F32 = jnp.float32
BF16 = jnp.bfloat16
NDEV = 8
VMEM_LIMIT = 56 * 1024 * 1024
NORM_EPS = 1e-6
ADAM_LR, ADAM_B1, ADAM_B2, ADAM_EPS, ADAM_WD, ADAM_STEP = 0.001, 0.9, 0.999, 1e-08, 0.01, 10
ATTN_LANES = 128
ATTN_BLOCKS_PER_TRIP = 2
MESH = pl.DeviceIdType.MESH
ANY = pl.BlockSpec(memory_space=pl.ANY)


def _cp(*sem):
    return pltpu.CompilerParams(dimension_semantics=sem, vmem_limit_bytes=VMEM_LIMIT)


def _my_index():
    return 4 * lax.axis_index("x") + 2 * lax.axis_index("y") + lax.axis_index("c")


def _flip(k):
    x, y, c = lax.axis_index("x"), lax.axis_index("y"), lax.axis_index("c")
    kx, ky, kc = (k >> 2) & 1, (k >> 1) & 1, k & 1
    px = 1 - x if kx else x
    py = 1 - y if ky else y
    pc = 1 - c if kc else c
    return (px, py, pc), 4 * px + 2 * py + pc


class Exchange:
    def __init__(self, mode, arrs):
        self.gather = mode == "gather"
        self.arrs = list(arrs)
        self.n = len(self.arrs)
        self.in_specs = [ANY] * self.n
        self.out_specs = [ANY] * self.n
        self.out_shape = [jax.ShapeDtypeStruct(((NDEV,) + a.shape) if self.gather else a.shape, a.dtype)
                          for a in self.arrs]
        self.scratch = [pltpu.SemaphoreType.DMA((self.n * (NDEV - 1),)), pltpu.SemaphoreType.DMA((self.n * (NDEV - 1),)),
                        pltpu.SemaphoreType.DMA((self.n,))]

    def copies(self, ins, outs, sems):
        send_sems, recv_sems, local_sems = sems
        me = _my_index()
        out = []
        for a in range(self.n):
            out.append(pltpu.make_async_copy(ins[a] if self.gather else ins[a].at[me], outs[a].at[me], local_sems.at[a]))
            for k in range(1, NDEV):
                peer, pidx = _flip(k)
                out.append(pltpu.make_async_remote_copy(
                    src_ref=ins[a] if self.gather else ins[a].at[pidx], dst_ref=outs[a].at[me],
                    send_sem=send_sems.at[a * (NDEV - 1) + k - 1], recv_sem=recv_sems.at[a * (NDEV - 1) + k - 1],
                    device_id=peer, device_id_type=MESH))
        return out

    def run(self, name):
        n = self.n

        def body(*refs):
            cps = self.copies(refs[:n], refs[n:2 * n], refs[2 * n:])
            for cp in cps:
                cp.start()
            for cp in cps:
                cp.wait()

        return pl.pallas_call(body, name=name, out_shape=self.out_shape, in_specs=self.in_specs,
                              out_specs=self.out_specs, scratch_shapes=self.scratch)(*self.arrs)


def _ride(exchange, refs, first, last):
    n = exchange.n
    cps = exchange.copies(refs[:n], refs[n:2 * n], refs[2 * n:])

    @pl.when(first)
    def _():
        for cp in cps:
            cp.start()

    @pl.when(last)
    def _():
        for cp in cps:
            cp.wait()


def all_gather(arrs, name):
    return Exchange("gather", arrs).run(name)


def all_gather_by_chip(arrs, name):
    n = len(arrs)
    chips = (2, 4, 6)

    def body(*refs):
        ins, outs = refs[:n], refs[n:2 * n]
        send_sems, recv_sems, local_sems = refs[2 * n:]
        me = _my_index()
        sibling, _ = _flip(1)

        def copy(a, k, block, to, src=None):
            return pltpu.make_async_remote_copy(
                src_ref=outs[a].at[block] if src is None else src, dst_ref=outs[a].at[block],
                send_sem=send_sems.at[7 * a + k], recv_sem=recv_sems.at[7 * a + k], device_id=to, device_id_type=MESH)

        own, sent = [], []
        for a in range(n):
            own.append(pltpu.make_async_copy(ins[a], outs[a].at[me], local_sems.at[a]))
            own[-1].start()
            first = [copy(a, 0, me, sibling, ins[a])] + [copy(a, 1 + j, me, _flip(k)[0], ins[a]) for j, k in enumerate(chips)]
            for cp in first:
                cp.start()
            sent += first
        for a in range(n):
            for j, k in enumerate(chips):
                peer, pidx = _flip(k)
                copy(a, 1 + j, pidx, peer).wait_recv()
                sent.append(copy(a, 4 + j, pidx, sibling))
                sent[-1].start()
        for a in range(n):
            copy(a, 0, _flip(1)[1], sibling).wait_recv()
            for j, k in enumerate(chips):
                copy(a, 4 + j, _flip(k + 1)[1], sibling).wait_recv()
        for cp in sent:
            cp.wait_send()
        for cp in own:
            cp.wait()

    return pl.pallas_call(
        body, name=name,
        out_shape=[jax.ShapeDtypeStruct((NDEV,) + a.shape, a.dtype) for a in arrs],
        in_specs=[ANY] * n, out_specs=[ANY] * n,
        scratch_shapes=[pltpu.SemaphoreType.DMA((7 * n,)), pltpu.SemaphoreType.DMA((7 * n,)),
                        pltpu.SemaphoreType.DMA((n,))],
    )(*arrs)


def all_to_all(arrs, name):
    return Exchange("scatter", arrs).run(name)


def _tile(n, pref):
    t = min(n, pref)
    while n % t:
        t //= 2
    return t


def _col_tile(n, pref):
    t = _tile(n, pref)
    return t if t % 128 == 0 and t >= 512 else n


def adamw(land, w, m, v, name):
    slots, r, c = land.shape
    tr = _tile(r, 64)
    bc1 = 1.0 - ADAM_B1 ** ADAM_STEP
    bc2 = 1.0 - ADAM_B2 ** ADAM_STEP
    lead = w.ndim - 2

    def body(land_ref, w_ref, m_ref, v_ref, g_ref, d_ref, nm_ref, nv_ref):
        at = (0,) * lead + (Ellipsis,)
        g = land_ref[0].astype(F32)
        for s in range(1, slots):
            g = g + land_ref[s].astype(F32)
        nm = ADAM_B1 * m_ref[at] + (1.0 - ADAM_B1) * g
        nv = ADAM_B2 * v_ref[at] + (1.0 - ADAM_B2) * (g * g)
        m_hat = nm / bc1
        v_hat = nv / bc2
        g_ref[at] = g
        nm_ref[at] = nm
        nv_ref[at] = nv
        d_ref[at] = -ADAM_LR * (m_hat / (jnp.sqrt(v_hat) + ADAM_EPS) + ADAM_WD * w_ref[at])

    blk = pl.BlockSpec((1,) * lead + (tr, c), lambda i: (0,) * lead + (i, 0))
    return pl.pallas_call(
        body, name=name, grid=(r // tr,),
        in_specs=[pl.BlockSpec((slots, tr, c), lambda i: (0, i, 0)), blk, blk, blk],
        out_specs=[blk] * 4,
        out_shape=[jax.ShapeDtypeStruct(w.shape, F32)] * 4,
        compiler_params=_cp("parallel"),
    )(land, w, m, v)


def sum_slots(land, d_model, name):
    slots, _, n = land.shape

    def body(land_ref, o_ref, loss_ref):
        g = land_ref[0]
        for s in range(1, slots):
            g = g + land_ref[s]
        o_ref[...] = g
        sq = jnp.sum(g[:, n - d_model:], axis=-1, keepdims=True)
        loss_ref[...] = jnp.broadcast_to(sq * (0.5 / d_model), (1, 128))

    return pl.pallas_call(
        body, name=name,
        out_shape=[jax.ShapeDtypeStruct((1, n), F32), jax.ShapeDtypeStruct((1, 128), F32)],
    )(land)


def _sigmoid(x):
    return 1.0 / (1.0 + jnp.exp(-x))


def adaln_fwd(c16, w_ada, b_ada, name):
    d3 = w_ada.shape[1]

    def body(c_ref, w_ref, b_ref, o_ref):
        cc = c_ref[...]
        sc = (cc * _sigmoid(cc)).astype(BF16)
        o_ref[...] = jnp.dot(sc, w_ref[...], preferred_element_type=F32) + b_ref[...]

    return pl.pallas_call(body, name=name, out_shape=jax.ShapeDtypeStruct((16, d3), F32),
                          compiler_params=pltpu.CompilerParams(vmem_limit_bytes=VMEM_LIMIT))(c16, w_ada, b_ada)


def adaln_bwd(c16, dshift16, p16, dgate16, scale16, g, name):
    d = c16.shape[1]
    nb = 3 * d // NDEV

    def body(c_ref, ds_ref, p_ref, dg_ref, sc_ref, g_ref, dw_ref, db_ref, dng_ref, dmod_scr):
        j = pl.program_id(0)

        @pl.when(j == 0)
        def _():
            p = p_ref[...]
            dmod = jnp.concatenate([ds_ref[...], p * g_ref[...], dg_ref[...]], axis=-1)
            dmod_scr[...] = dmod
            db_ref[...] = jnp.sum(dmod, axis=0, keepdims=True)
            dng_ref[...] = jnp.sum((1.0 + sc_ref[...]) * p, axis=0, keepdims=True)

        cc = c_ref[...]
        sc = (cc * _sigmoid(cc)).astype(BF16)
        start = pl.multiple_of(j * nb, 128)
        dm = dmod_scr[:, pl.ds(start, nb)].astype(BF16)
        dw_ref[0] = lax.dot_general(sc, dm, (((0,), (0,)), ((), ())), preferred_element_type=F32).astype(BF16)

    full = lambda shape: pl.BlockSpec(shape, lambda j: (0,) * len(shape))
    return pl.pallas_call(
        body, name=name, grid=(NDEV,),
        in_specs=[full((16, d))] * 5 + [full((1, d))],
        out_specs=[pl.BlockSpec((1, d, nb), lambda j: (j, 0, 0)), full((1, 3 * d)), full((1, d))],
        out_shape=[jax.ShapeDtypeStruct((NDEV, d, nb), BF16), jax.ShapeDtypeStruct((1, 3 * d), F32),
                   jax.ShapeDtypeStruct((1, d), F32)],
        scratch_shapes=[pltpu.VMEM((16, 3 * d), F32)],
        compiler_params=_cp("arbitrary"),
    )(c16, dshift16, p16, dgate16, scale16, g)


def _modulated_norm(x, g, scale, shift):
    rstd = lax.rsqrt(jnp.mean(x * x, axis=-1, keepdims=True) + NORM_EPS)
    return ((x * rstd) * g) * (1.0 + scale) + shift


def normmod_matmul(x, g, scale, shift, w, out_dtype, seq, emit_h, name, cols=None):
    t, d = x.shape
    col0, n = cols or (0, w.shape[1])
    tm = _tile(seq, 512)
    tn = _col_tile(n, 2048)
    assert col0 % tn == 0
    jb = col0 // tn
    per_seq = seq // tm

    def body(x_ref, g_ref, sc_ref, sh_ref, w_ref, *rest):
        if emit_h:
            o_ref, h_ref, h_scr = rest
        else:
            o_ref, h_scr = rest

        @pl.when(pl.program_id(1) == 0)
        def _():
            h = _modulated_norm(x_ref[...], g_ref[...], sc_ref[0], sh_ref[0]).astype(BF16)
            h_scr[...] = h
            if emit_h:
                h_ref[...] = h

        o_ref[...] = jnp.dot(h_scr[...], w_ref[...], preferred_element_type=F32).astype(out_dtype)

    mod_spec = pl.BlockSpec((1, 1, d), lambda i, j: (i // per_seq, 0, 0))
    out_specs = [pl.BlockSpec((tm, tn), lambda i, j: (i, j))]
    out_shape = [jax.ShapeDtypeStruct((t, n), out_dtype)]
    if emit_h:
        out_specs.append(pl.BlockSpec((tm, d), lambda i, j: (i, 0)))
        out_shape.append(jax.ShapeDtypeStruct((t, d), BF16))
    return pl.pallas_call(
        body, name=name, grid=(t // tm, n // tn),
        in_specs=[pl.BlockSpec((tm, d), lambda i, j: (i, 0)), pl.BlockSpec((1, d), lambda i, j: (0, 0)),
                  mod_spec, mod_spec, pl.BlockSpec((d, tn), lambda i, j: (0, jb + j))],
        out_specs=out_specs, out_shape=out_shape,
        scratch_shapes=[pltpu.VMEM((tm, d), BF16)],
        compiler_params=_cp("parallel", "arbitrary"),
    )(x, g, scale, shift, w)


def out_proj_fwd(o, zf, w_out, x, gate, seq, name):
    t, di = o.shape
    d = x.shape[1]
    tm = _tile(seq, 256)
    per_seq = seq // tm

    def body(o_ref, z_ref, w_ref, x_ref, gt_ref, xn_ref, y_ref, u_ref):
        z = z_ref[...]
        u = (o_ref[...] * (z * _sigmoid(z))).astype(BF16)
        y = jnp.dot(u, w_ref[...], preferred_element_type=F32)
        u_ref[...] = u
        y_ref[...] = y
        xn_ref[...] = x_ref[...] + gt_ref[0] * y

    row = lambda c: pl.BlockSpec((tm, c), lambda i: (i, 0))
    return pl.pallas_call(
        body, name=name, grid=(t // tm,),
        in_specs=[row(di), row(di), pl.BlockSpec((di, d), lambda i: (0, 0)), row(d),
                  pl.BlockSpec((1, 1, d), lambda i: (i // per_seq, 0, 0))],
        out_specs=[row(d), row(d), row(di)],
        out_shape=[jax.ShapeDtypeStruct((t, d), F32), jax.ShapeDtypeStruct((t, d), F32),
                   jax.ShapeDtypeStruct((t, di), BF16)],
        compiler_params=_cp("parallel"),
    )(o, zf, w_out, x, gate)


def out_proj_bwd(dxo, y, gate, w_out, o, zf, seq, name):
    t, d = dxo.shape
    di = o.shape[1]
    tm = _tile(seq, 256)
    per_seq = seq // tm

    def body(dx_ref, y_ref, gt_ref, w_ref, o_ref, z_ref, do_ref, dz_ref, dy_ref, dgt_ref):
        dx = dx_ref[...]
        part = jnp.sum(dx * y_ref[...], axis=0, keepdims=True)

        @pl.when(pl.program_id(0) % per_seq == 0)
        def _():
            dgt_ref[0] = part

        @pl.when(pl.program_id(0) % per_seq != 0)
        def _():
            dgt_ref[0] += part

        dy = (dx * gt_ref[0]).astype(BF16)
        dy_ref[...] = dy
        du = lax.dot_general(dy, w_ref[...], (((1,), (1,)), ((), ())), preferred_element_type=F32)
        z = z_ref[...]
        sg = _sigmoid(z)
        do_ref[...] = (du * (z * sg)).astype(BF16)
        dz_ref[...] = (du * o_ref[...] * (sg * (1.0 + z * (1.0 - sg)))).astype(BF16)

    row = lambda c: pl.BlockSpec((tm, c), lambda i: (i, 0))
    seq_spec = pl.BlockSpec((1, 1, d), lambda i: (i // per_seq, 0, 0))
    return pl.pallas_call(
        body, name=name, grid=(t // tm,),
        in_specs=[row(d), row(d), seq_spec, pl.BlockSpec((di, d), lambda i: (0, 0)), row(di), row(di)],
        out_specs=[row(di), row(di), row(d), seq_spec],
        out_shape=[jax.ShapeDtypeStruct((t, di), BF16), jax.ShapeDtypeStruct((t, di), BF16),
                   jax.ShapeDtypeStruct((t, d), BF16), jax.ShapeDtypeStruct((t // seq, 1, d), F32)],
        compiler_params=_cp("arbitrary"),
    )(dxo, y, gate, w_out, o, zf)


def tn_matmul(a, b, name):
    t, m = a.shape
    n = b.shape[1]
    tn = _col_tile(n, 1024)
    tk = _tile(t, 1024)
    nk = t // tk

    def body(a_ref, b_ref, o_ref, acc):
        part = lax.dot_general(a_ref[...], b_ref[...], (((0,), (0,)), ((), ())), preferred_element_type=F32)
        k = pl.program_id(1)

        @pl.when(k == 0)
        def _():
            acc[...] = part

        @pl.when(k != 0)
        def _():
            acc[...] += part

        @pl.when(k == nk - 1)
        def _():
            o_ref[...] = acc[...].astype(BF16)

    return pl.pallas_call(
        body, name=name, grid=(n // tn, nk),
        in_specs=[pl.BlockSpec((tk, m), lambda j, k: (k, 0)), pl.BlockSpec((tk, tn), lambda j, k: (k, j))],
        out_specs=pl.BlockSpec((m, tn), lambda j, k: (0, j)),
        out_shape=jax.ShapeDtypeStruct((m, n), BF16),
        scratch_shapes=[pltpu.VMEM((m, tn), F32)],
        compiler_params=_cp("parallel", "arbitrary"),
    )(a, b)


def dh_norm_bwd(parts, w_t, extra, x, g, scale, dxo, seq, name, rider=None):
    t, d = x.shape
    nparts = len(parts)
    kw = parts[0].shape[1]
    tm = _tile(seq, 512)
    per_seq = seq // tm
    has_extra = extra is not None
    grid = (t // tm,)
    nr = rider.n if rider else 0
    n_in = nparts + 1 + (2 if has_extra else 0) + 4

    def body(*refs):
        p_refs = refs[:nparts]
        w_ref = refs[nparts]
        pos = nparts + 1
        if has_extra:
            e_ref, we_ref = refs[pos], refs[pos + 1]
            pos += 2
        x_ref, g_ref, sc_ref, dxo_ref = refs[pos:pos + 4]
        dx_ref, pp_ref, ss_ref = refs[n_in + nr:n_in + nr + 3]
        if rider:
            _ride(rider, refs[n_in:n_in + nr] + refs[n_in + nr + 3:], *_grid_ends(grid))
        i = pl.program_id(0)

        dh = jnp.dot(p_refs[0][...], w_ref[0:kw, :], preferred_element_type=F32)
        for kk in range(1, nparts):
            dh = dh + jnp.dot(p_refs[kk][...], w_ref[kk * kw:(kk + 1) * kw, :], preferred_element_type=F32)
        if has_extra:
            dh = dh + jnp.dot(e_ref[...], we_ref[...], preferred_element_type=F32)
        xx = x_ref[...]
        rstd = lax.rsqrt(jnp.mean(xx * xx, axis=-1, keepdims=True) + NORM_EPS)
        xhat = xx * rstd
        dxhat = dh * (g_ref[...] * (1.0 + sc_ref[0]))
        dx_ref[...] = dxo_ref[...] + rstd * (dxhat - xhat * jnp.mean(dxhat * xhat, axis=-1, keepdims=True))
        pp = jnp.sum(dh * xhat, axis=0, keepdims=True)
        ss = jnp.sum(dh, axis=0, keepdims=True)

        @pl.when(i % per_seq == 0)
        def _():
            pp_ref[0] = pp
            ss_ref[0] = ss

        @pl.when(i % per_seq != 0)
        def _():
            pp_ref[0] += pp
            ss_ref[0] += ss

    row = lambda c: pl.BlockSpec((tm, c), lambda i: (i, 0))
    whole = lambda a: pl.BlockSpec(a.shape, lambda i: (0, 0), pipeline_mode=pl.Buffered(1))
    seq_spec = pl.BlockSpec((1, 1, d), lambda i: (i // per_seq, 0, 0))
    in_specs = [row(kw)] * nparts + [whole(w_t)]
    args = list(parts) + [w_t]
    if has_extra:
        in_specs += [row(extra[0].shape[1]), whole(extra[1])]
        args += list(extra)
    in_specs += [row(d), pl.BlockSpec((1, d), lambda i: (0, 0)), seq_spec, row(d)]
    args += [x, g, scale, dxo]
    return pl.pallas_call(
        body, name=name, grid=grid,
        in_specs=in_specs + (rider.in_specs if rider else []),
        out_specs=[row(d), seq_spec, seq_spec] + (rider.out_specs if rider else []),
        out_shape=[jax.ShapeDtypeStruct((t, d), F32), jax.ShapeDtypeStruct((t // seq, 1, d), F32),
                   jax.ShapeDtypeStruct((t // seq, 1, d), F32)] + (rider.out_shape if rider else []),
        scratch_shapes=rider.scratch if rider else [],
        compiler_params=_cp("arbitrary"),
    )(*args, *(rider.arrs if rider else []))


def final_loss(x, g, target, name):
    t, d = x.shape
    tm = _tile(t, 512)

    def body(x_ref, g_ref, t_ref, dx_ref, dg_ref, sq_ref):
        xx = x_ref[...]
        gg = g_ref[...]
        rstd = lax.rsqrt(jnp.mean(xx * xx, axis=-1, keepdims=True) + NORM_EPS)
        xhat = xx * rstd
        err = xhat * gg - t_ref[...]
        dy = err * (1.0 / d)
        dxhat = dy * gg
        dx_ref[...] = rstd * (dxhat - xhat * jnp.mean(dxhat * xhat, axis=-1, keepdims=True))
        dg = jnp.sum(dy * xhat, axis=0, keepdims=True)
        sq = jnp.sum(err * err, axis=0, keepdims=True)

        @pl.when(pl.program_id(0) == 0)
        def _():
            dg_ref[...] = dg
            sq_ref[...] = sq

        @pl.when(pl.program_id(0) != 0)
        def _():
            dg_ref[...] += dg
            sq_ref[...] += sq

    row = pl.BlockSpec((tm, d), lambda i: (i, 0))
    vec = pl.BlockSpec((1, d), lambda i: (0, 0))
    return pl.pallas_call(
        body, name=name, grid=(t // tm,),
        in_specs=[row, vec, row], out_specs=[row, vec, vec],
        out_shape=[jax.ShapeDtypeStruct((t, d), F32), jax.ShapeDtypeStruct((1, d), F32),
                   jax.ShapeDtypeStruct((1, d), F32)],
        compiler_params=_cp("arbitrary"),
    )(x, g, target)


def _split3(x):
    x1 = x.astype(BF16)
    r1 = x - x1.astype(F32)
    x2 = r1.astype(BF16)
    x3 = (r1 - x2.astype(F32)).astype(BF16)
    return x1, x2, x3


def _dot3(x, u):
    x1, x2, x3 = _split3(x)
    return (jnp.dot(x1, u, preferred_element_type=F32) + jnp.dot(x2, u, preferred_element_type=F32)
            + jnp.dot(x3, u, preferred_element_type=F32))


def _log_sigmoid(x):
    return jnp.minimum(x, 0.0) - jnp.log1p(jnp.exp(-jnp.abs(x)))


def forget_cumsum(fl_t, b_f, name):
    b, h, s = fl_t.shape

    def body(f_ref, b_ref, o_ref):
        lf = _log_sigmoid(f_ref[0] + b_ref[...])
        u = (lax.broadcasted_iota(jnp.int32, (s, s), 0) <= lax.broadcasted_iota(jnp.int32, (s, s), 1)).astype(BF16)
        o_ref[0] = _dot3(lf, u)

    return pl.pallas_call(
        body, name=name, grid=(b,),
        in_specs=[pl.BlockSpec((1, h, s), lambda i: (i, 0, 0)), pl.BlockSpec((h, 1), lambda i: (0, 0))],
        out_specs=pl.BlockSpec((1, h, s), lambda i: (i, 0, 0)),
        out_shape=jax.ShapeDtypeStruct((b, h, s), F32),
        compiler_params=_cp("parallel"),
    )(fl_t, b_f)


def forget_cumsum_bwd(dcum, fl_t, b_f, name):
    b, h, s = fl_t.shape

    def body(d_ref, f_ref, b_ref, o_ref, db_ref):
        u = (lax.broadcasted_iota(jnp.int32, (s, s), 0) >= lax.broadcasted_iota(jnp.int32, (s, s), 1)).astype(BF16)
        dlf = _dot3(d_ref[0], u)
        df = dlf * _sigmoid(-(f_ref[0] + b_ref[...]))
        o_ref[0] = df
        part = jnp.sum(df, axis=-1, keepdims=True)

        @pl.when(pl.program_id(0) == 0)
        def _():
            db_ref[...] = part

        @pl.when(pl.program_id(0) != 0)
        def _():
            db_ref[...] += part

    blk = pl.BlockSpec((1, h, s), lambda i: (i, 0, 0))
    return pl.pallas_call(
        body, name=name, grid=(b,),
        in_specs=[blk, blk, pl.BlockSpec((h, 1), lambda i: (0, 0))],
        out_specs=[blk, pl.BlockSpec((h, 1), lambda i: (0, 0))],
        out_shape=[jax.ShapeDtypeStruct((b, h, s), F32), jax.ShapeDtypeStruct((h, 1), F32)],
        compiler_params=_cp("arbitrary"),
    )(dcum, fl_t, b_f)


def _nt(a, b):
    return lax.dot_general(a, b, (((1,), (1,)), ((), ())), preferred_element_type=F32)


def _attn_dims(t, seq, dh):
    tq = _tile(seq, 512)
    tk = _tile(seq, 256)
    return t // seq, tq, tk, seq // tq, tq // tk, ATTN_LANES // dh


def _blocks_per_trip(r):
    return ATTN_BLOCKS_PER_TRIP if r % ATTN_BLOCKS_PER_TRIP == 0 else 1


def _scaled_q(q_ref, dh):
    scale = dh ** -0.5
    if math.log2(dh) % 2 == 0:
        return (q_ref[...].astype(F32) * scale).astype(BF16), None
    return q_ref[...], scale


def _diag_masks(tq, tk, r, strict):
    row = lax.broadcasted_iota(jnp.int32, (tq, tk), 0)
    col = lax.broadcasted_iota(jnp.int32, (tq, tk), 1)
    return [(col + d * tk < row) if strict else (col + d * tk <= row) for d in range(r)]


def _rows(state, lo):
    return tuple(a[lo:] for a in state)


def _put_rows(state, part, lo):
    if lo == 0:
        return tuple(part)
    return tuple(jnp.concatenate([a[:lo], p], axis=0) for a, p in zip(state, part))


def _transposed(x):
    return jnp.transpose(x.astype(F32)).astype(BF16)


def _store_transposed(dst_ref, src_scr, tk):
    for jb in range(src_scr.shape[0]):
        dst_ref[jb * tk:(jb + 1) * tk, :] = jnp.transpose(src_scr[jb]).astype(BF16)


def _dot2(x, u):
    hi = x.astype(BF16)
    lo = (x - hi.astype(F32)).astype(BF16)
    return jnp.dot(jnp.concatenate([hi, lo], axis=1), jnp.concatenate([u, u], axis=0), preferred_element_type=F32)


def _attn_specs(seq, di, tq):
    nq = seq // tq
    cb = di // ATTN_LANES
    q_spec = pl.BlockSpec((tq, ATTN_LANES), lambda b, hp, i: (b * nq + i, hp))
    k_spec = pl.BlockSpec((seq, ATTN_LANES), lambda b, hp, i: (b, cb + hp))
    v_spec = pl.BlockSpec((seq, ATTN_LANES), lambda b, hp, i: (b, 2 * cb + hp))
    kv_out = pl.BlockSpec((seq, ATTN_LANES), lambda b, hp, i: (b, hp))
    return q_spec, k_spec, v_spec, kv_out


def _grid_ends(grid):
    ids = [pl.program_id(a) for a in range(len(grid))]
    first = functools.reduce(jnp.logical_and, [p == 0 for p in ids])
    last = functools.reduce(jnp.logical_and, [p == g - 1 for p, g in zip(ids, grid)])
    return first, last


def fox_fwd(qkv, ck, seq, di, dh, name, rider=None):
    t = qkv.shape[0]
    nb, tq, tk, nq, r, hpb = _attn_dims(t, seq, dh)
    per_trip = _blocks_per_trip(r)
    heads = [slice(hh * dh, (hh + 1) * dh) for hh in range(hpb)]
    assert hpb >= 2, "the row sums of p ride in another head's lanes of the p @ v matmul"
    q_spec, k_spec, v_spec, _ = _attn_specs(seq, di, tq)
    grid = (nb, di // ATTN_LANES, nq)
    nr = rider.n if rider else 0

    def body(q_ref, k_ref, v_ref, ck_ref, *rest):
        o_ref, lse_ref = rest[nr:nr + 2]
        if rider:
            _ride(rider, rest[:nr] + rest[nr + 2:], *_grid_ends(grid))
        i = pl.program_id(2)
        masks = _diag_masks(tq, tk, r, False)
        q, scale = _scaled_q(q_ref, dh)
        qs = [q[:, hd] for hd in heads]
        lane_head = lax.broadcasted_iota(jnp.int32, (tk, ATTN_LANES), 1) // dh

        def scores(hh, j):
            start = pl.multiple_of(j * tk, tk)
            kh = k_ref[pl.ds(start, tk), heads[hh]]
            vt = v_ref[pl.ds(start, tk), :]
            vh = jnp.where(lane_head == hh, vt, jnp.ones_like(vt))
            s = _nt(qs[hh], kh)
            if scale is not None:
                s = s * scale
            return s - ck_ref[0, hh, pl.ds(j, 1), :], vh

        def update(state, s, vh):
            m, acc = state
            m_new = jnp.maximum(m, jnp.max(s, axis=-1, keepdims=True))
            p = jnp.exp(s - m_new)
            return m_new, jnp.exp(m - m_new) * acc + _dot2(p, vh)

        states = []
        for hh in range(hpb):
            s, vh = scores(hh, r * i)
            s = jnp.where(masks[0], s, -jnp.inf)
            m = jnp.max(s, axis=-1, keepdims=True)
            states.append((m, _dot2(jnp.exp(s - m), vh)))
        for d in range(1, r):
            for hh in range(hpb):
                s, vh = scores(hh, r * i + d)
                states[hh] = update(states[hh], jnp.where(masks[d], s, -jnp.inf), vh)

        def step(jj, states):
            for sub in range(per_trip):
                states = tuple(update(states[hh], *scores(hh, r * i - 1 - (per_trip * jj + sub))) for hh in range(hpb))
            return states

        states = lax.fori_loop(0, r * i // per_trip, step, tuple(states))
        for hh in range(hpb):
            m, acc = states[hh]
            other = heads[(hh + 1) % hpb].start
            l = acc[:, other:other + 1]
            o_ref[:, heads[hh]] = acc[:, heads[hh]] / l
            lse_ref[0, 0, :, hh:hh + 1] = m + jnp.log(l)

    return pl.pallas_call(
        body, name=name, grid=grid,
        in_specs=[q_spec, k_spec, v_spec,
                  pl.BlockSpec((1, hpb, seq // tk, tk), lambda b, hp, i: (b, hp, 0, 0))] + (rider.in_specs if rider else []),
        out_specs=[q_spec, pl.BlockSpec((1, 1, tq, hpb), lambda b, hp, i: (b, hp, i, 0))]
        + (rider.out_specs if rider else []),
        out_shape=[jax.ShapeDtypeStruct((t, di), F32), jax.ShapeDtypeStruct((nb, di // ATTN_LANES, seq, hpb), F32)]
        + (rider.out_shape if rider else []),
        scratch_shapes=rider.scratch if rider else [],
        compiler_params=_cp("arbitrary", "arbitrary", "arbitrary"),
    )(qkv, qkv, qkv, ck, *(rider.arrs if rider else []))


def fox_bwd(qkv, ck, o, do, lse, seq, di, dh, name, rider=None):
    t = qkv.shape[0]
    nb, tq, tk, nq, r, hpb = _attn_dims(t, seq, dh)
    per_trip = _blocks_per_trip(r)
    heads = [slice(hh * dh, (hh + 1) * dh) for hh in range(hpb)]
    q_spec, k_spec, v_spec, kv_out = _attn_specs(seq, di, tq)
    ck_spec = pl.BlockSpec((1, hpb, seq // tk, tk), lambda b, hp, i: (b, hp, 0, 0))
    grid = (nb, di // ATTN_LANES, nq)
    nr = rider.n if rider else 0

    def body(q_ref, k_ref, v_ref, ck_ref, o_ref, do_ref, lse_ref, *rest):
        dq_ref, dk_ref, dv_ref, dck_ref = rest[nr:nr + 4]
        dkt_scr, dvt_scr = rest[2 * nr + 4:2 * nr + 6]
        if rider:
            _ride(rider, rest[:nr] + rest[nr + 4:2 * nr + 4] + rest[2 * nr + 6:], *_grid_ends(grid))
        i = pl.program_id(2)

        @pl.when(i == 0)
        def _():
            dkt_scr[...] = jnp.zeros_like(dkt_scr)
            dvt_scr[...] = jnp.zeros_like(dvt_scr)
            dck_ref[...] = jnp.zeros_like(dck_ref)

        masks = _diag_masks(tq, tk, r, False)
        q, scale = _scaled_q(q_ref, dh)
        do = do_ref[...]
        q_t, do_t = _transposed(q), _transposed(do)
        qs = [q[:, hd] for hd in heads]
        dos = [do[:, hd] for hd in heads]
        deltas = [jnp.sum(dos[hh].astype(F32) * o_ref[:, heads[hh]], axis=-1, keepdims=True) for hh in range(hpb)]
        lses = [lse_ref[0, 0, :, hh:hh + 1] for hh in range(hpb)]

        def block(hh, j, mask, dq_acc, lo=0):
            start = pl.multiple_of(j * tk, tk)
            kh = k_ref[pl.ds(start, tk), heads[hh]]
            vh = v_ref[pl.ds(start, tk), heads[hh]]
            s = _nt(qs[hh][lo:], kh)
            if scale is not None:
                s = s * scale
            p = jnp.exp(s - ck_ref[0, hh, pl.ds(j, 1), :] - lses[hh][lo:])
            if mask is not None:
                p = jnp.where(mask[lo:], p, 0.0)
            ds = p * (_nt(dos[hh][lo:], vh) - deltas[hh][lo:])
            dsb = ds.astype(BF16)
            dkt = jnp.dot(q_t[heads[hh], lo:], dsb, preferred_element_type=F32)
            if scale is not None:
                dkt = dkt * scale
            dkt_scr[j, heads[hh], :] += dkt
            dvt_scr[j, heads[hh], :] += jnp.dot(do_t[heads[hh], lo:], p.astype(BF16), preferred_element_type=F32)
            dck_ref[0, hh, pl.ds(j, 1), :] -= jnp.sum(ds, axis=0, keepdims=True)
            return dq_acc + jnp.dot(dsb, kh, preferred_element_type=F32)

        accs = [jnp.zeros((tq, dh), F32)] * hpb
        for d in range(r):
            lo = d * tk
            accs = [_put_rows((accs[hh],), (block(hh, r * i + d, masks[d], accs[hh][lo:], lo),), lo)[0]
                    for hh in range(hpb)]

        def step(jj, accs):
            for sub in range(per_trip):
                accs = tuple(block(hh, per_trip * jj + sub, None, accs[hh]) for hh in range(hpb))
            return accs

        accs = lax.fori_loop(0, r * i // per_trip, step, tuple(accs))
        for hh in range(hpb):
            dq_ref[:, heads[hh]] = (accs[hh] * dh ** -0.5).astype(BF16)

        @pl.when(i == nq - 1)
        def _():
            _store_transposed(dk_ref, dkt_scr, tk)
            _store_transposed(dv_ref, dvt_scr, tk)

    return pl.pallas_call(
        body, name=name, grid=grid,
        in_specs=[q_spec, k_spec, v_spec, ck_spec, q_spec, q_spec,
                  pl.BlockSpec((1, 1, tq, hpb), lambda b, hp, i: (b, hp, i, 0))] + (rider.in_specs if rider else []),
        out_specs=[q_spec, kv_out, kv_out, ck_spec] + (rider.out_specs if rider else []),
        out_shape=[jax.ShapeDtypeStruct((t, di), BF16)] * 3 + [jax.ShapeDtypeStruct(ck.shape, F32)]
        + (rider.out_shape if rider else []),
        scratch_shapes=[pltpu.VMEM((seq // tk, ATTN_LANES, tk), F32), pltpu.VMEM((seq // tk, ATTN_LANES, tk), F32)]
        + (rider.scratch if rider else []),
        compiler_params=_cp("arbitrary", "arbitrary", "arbitrary"),
    )(qkv, qkv, qkv, ck, o, do, lse, *(rider.arrs if rider else []))


def _sb_logits(qh, kh, scale, strict):
    z = _nt(qh, kh)
    if scale is not None:
        z = z * scale
    e = jnp.exp(-jnp.abs(z))
    lb = jnp.minimum(z, 0.0) - jnp.log(1.0 + e)
    lk = lb - z
    return lb, lk if strict is None else jnp.where(strict, lk, 0.0)


def _cum(x, u):
    return jnp.dot(x.astype(BF16), u, preferred_element_type=F32)


def sb_fwd(qkv, seq, di, dh, name):
    t = qkv.shape[0]
    nb, tq, tk, nq, r, hpb = _attn_dims(t, seq, dh)
    per_trip = _blocks_per_trip(r)
    heads = [slice(hh * dh, (hh + 1) * dh) for hh in range(hpb)]
    q_spec, k_spec, v_spec, _ = _attn_specs(seq, di, tq)

    def body(q_ref, k_ref, v_ref, o_ref, rt_ref):
        i = pl.program_id(2)
        masks = _diag_masks(tq, tk, r, True)
        row = lax.broadcasted_iota(jnp.int32, (tk, tk), 0)
        col = lax.broadcasted_iota(jnp.int32, (tk, tk), 1)
        u_after = (row > col).astype(BF16)
        q, scale = _scaled_q(q_ref, dh)
        qs = [q[:, hd] for hd in heads]

        def block(hh, j, mask, state, lo=0):
            rr, acc = state
            start = pl.multiple_of(j * tk, tk)
            kh = k_ref[pl.ds(start, tk), heads[hh]]
            vh = v_ref[pl.ds(start, tk), heads[hh]]
            if mask is not None:
                mask = mask[lo:]
            lb, lk = _sb_logits(qs[hh][lo:], kh, scale, mask)
            a = jnp.exp(lb + _cum(lk, u_after) + rr)
            if mask is not None:
                a = jnp.where(mask, a, 0.0)
            acc = acc + jnp.dot(a.astype(BF16), vh, preferred_element_type=F32)
            return rr + jnp.sum(lk, axis=-1, keepdims=True), acc

        states = [(jnp.zeros((tq, 1), F32), jnp.zeros((tq, dh), F32))] * hpb
        for d in reversed(range(r)):
            lo = d * tk
            states = [_put_rows(states[hh], block(hh, r * i + d, masks[d], _rows(states[hh], lo), lo), lo)
                      for hh in range(hpb)]

        def step(jj, states):
            for sub in range(per_trip):
                states = tuple(block(hh, r * i - 1 - (per_trip * jj + sub), None, states[hh]) for hh in range(hpb))
            return states

        states = lax.fori_loop(0, r * i // per_trip, step, tuple(states))
        for hh in range(hpb):
            o_ref[:, heads[hh]] = states[hh][1]
            rt_ref[0, 0, :, hh:hh + 1] = states[hh][0]

    return pl.pallas_call(
        body, name=name, grid=(nb, di // ATTN_LANES, nq),
        in_specs=[q_spec, k_spec, v_spec],
        out_specs=[q_spec, pl.BlockSpec((1, 1, tq, hpb), lambda b, hp, i: (b, hp, i, 0))],
        out_shape=[jax.ShapeDtypeStruct((t, di), F32), jax.ShapeDtypeStruct((nb, di // ATTN_LANES, seq, hpb), F32)],
        compiler_params=_cp("parallel", "parallel", "arbitrary"),
    )(qkv, qkv, qkv)


def sb_bwd(qkv, do, rtot, seq, di, dh, name):
    t = qkv.shape[0]
    nb, tq, tk, nq, r, hpb = _attn_dims(t, seq, dh)
    per_trip = _blocks_per_trip(r)
    heads = [slice(hh * dh, (hh + 1) * dh) for hh in range(hpb)]
    q_spec, k_spec, v_spec, kv_out = _attn_specs(seq, di, tq)

    def body(q_ref, k_ref, v_ref, do_ref, rt_ref, dq_ref, dk_ref, dv_ref, dkt_scr, dvt_scr):
        i = pl.program_id(2)

        @pl.when(i == 0)
        def _():
            dkt_scr[...] = jnp.zeros_like(dkt_scr)
            dvt_scr[...] = jnp.zeros_like(dvt_scr)

        masks = _diag_masks(tq, tk, r, True)
        row = lax.broadcasted_iota(jnp.int32, (tk, tk), 0)
        col = lax.broadcasted_iota(jnp.int32, (tk, tk), 1)
        u_after = (row > col).astype(BF16)
        u_before = (row < col).astype(BF16)
        q, scale = _scaled_q(q_ref, dh)
        do = do_ref[...]
        q_t, do_t = _transposed(q), _transposed(do)
        qs = [q[:, hd] for hd in heads]
        dos = [do[:, hd] for hd in heads]
        rts = [rt_ref[0, 0, :, hh:hh + 1] for hh in range(hpb)]

        def block(hh, j, mask, state, lo=0):
            lc, gc, dq_acc = state
            start = pl.multiple_of(j * tk, tk)
            kh = k_ref[pl.ds(start, tk), heads[hh]]
            vh = v_ref[pl.ds(start, tk), heads[hh]]
            if mask is not None:
                mask = mask[lo:]
            lb, lk = _sb_logits(qs[hh][lo:], kh, scale, mask)
            lc = lc + jnp.sum(lk, axis=-1, keepdims=True)
            a = jnp.exp(lb + ((rts[hh][lo:] - lc) + _cum(lk, u_after)))
            if mask is not None:
                a = jnp.where(mask, a, 0.0)
            de = a * _nt(dos[hh][lo:], vh)
            g = gc + _cum(de, u_before)
            dz = de - jnp.exp(lb) * (de + g)
            if mask is not None:
                dz = jnp.where(mask, dz, 0.0)
            dzb = dz.astype(BF16)
            dkt = jnp.dot(q_t[heads[hh], lo:], dzb, preferred_element_type=F32)
            if scale is not None:
                dkt = dkt * scale
            dkt_scr[j, heads[hh], :] += dkt
            dvt_scr[j, heads[hh], :] += jnp.dot(do_t[heads[hh], lo:], a.astype(BF16), preferred_element_type=F32)
            return (lc, gc + jnp.sum(de, axis=-1, keepdims=True),
                    dq_acc + jnp.dot(dzb, kh, preferred_element_type=F32))

        zero = jnp.zeros((tq, 1), F32)

        def step(jj, states):
            for sub in range(per_trip):
                states = tuple(block(hh, per_trip * jj + sub, None, states[hh]) for hh in range(hpb))
            return states

        states = lax.fori_loop(0, r * i // per_trip, step, ((zero, zero, jnp.zeros((tq, dh), F32)),) * hpb)
        for d in range(r):
            lo = d * tk
            states = [_put_rows(states[hh], block(hh, r * i + d, masks[d], _rows(states[hh], lo), lo), lo)
                      for hh in range(hpb)]
        for hh in range(hpb):
            dq_ref[:, heads[hh]] = (states[hh][2] * dh ** -0.5).astype(BF16)

        @pl.when(i == nq - 1)
        def _():
            _store_transposed(dk_ref, dkt_scr, tk)
            _store_transposed(dv_ref, dvt_scr, tk)

    return pl.pallas_call(
        body, name=name, grid=(nb, di // ATTN_LANES, nq),
        in_specs=[q_spec, k_spec, v_spec, q_spec, pl.BlockSpec((1, 1, tq, hpb), lambda b, hp, i: (b, hp, i, 0))],
        out_specs=[q_spec, kv_out, kv_out],
        out_shape=[jax.ShapeDtypeStruct((t, di), BF16)] * 3,
        scratch_shapes=[pltpu.VMEM((seq // tk, ATTN_LANES, tk), F32), pltpu.VMEM((seq // tk, ATTN_LANES, tk), F32)],
        compiler_params=_cp("parallel", "parallel", "arbitrary"),
    )(qkv, qkv, qkv, do, rtot)


def _cols(g):
    return jnp.transpose(g, (1, 0, 2)).reshape(g.shape[1], NDEV * g.shape[2])


def _col_blocks(w):
    r, c8 = w.shape
    return jnp.transpose(w.reshape(r, NDEV, c8 // NDEV), (1, 0, 2))


def _pad_rows16(a):
    return jnp.pad(a, ((0, 16 - a.shape[0]), (0, 0)))


def _pad_cols(a, n):
    return jnp.pad(a, ((0, 0), (0, n - a.shape[1])))


def kernel(x, c, fox_norm_g, fox_w_ada, fox_b_ada, fox_w_in, fox_b_f, fox_w_out, sb_norm_g, sb_w_ada, sb_b_ada, sb_w_in, sb_w_out, final_norm_g, loss_target, m_fox_norm_g, m_fox_w_ada, m_fox_b_ada, m_fox_w_in, m_fox_b_f, m_fox_w_out, m_sb_norm_g, m_sb_w_ada, m_sb_b_ada, m_sb_w_in, m_sb_w_out, m_final_norm_g, v_fox_norm_g, v_fox_w_ada, v_fox_b_ada, v_fox_w_in, v_fox_b_f, v_fox_w_out, v_sb_norm_g, v_sb_w_ada, v_sb_b_ada, v_sb_w_in, v_sb_w_out, v_final_norm_g):
    nb, seq, d = x.shape
    t = nb * seq
    h = fox_b_f.shape[-1]
    di = fox_w_out.shape[1] * NDEV
    dh = di // h
    tq = _tile(seq, 256)
    me = _my_index()

    gathered = all_gather_by_chip([w[0].astype(BF16) for w in (fox_w_ada, fox_w_in, fox_w_out)], "gather_fox_weights")
    fox_wada, fox_win = _cols(gathered[0]), _cols(gathered[1])
    fox_wout = gathered[2].reshape(di, d)
    sb_gather = Exchange("gather", [w[0].astype(BF16) for w in (sb_w_ada, sb_w_in, sb_w_out)] + [sb_norm_g, sb_b_ada])

    x0 = x.reshape(t, d)
    target = loss_target.reshape(t, d)
    c16 = _pad_rows16(c)

    def layer_fwd(xin, g, wada, bada, win, wout, b_f, tag, rider=None):
        mod = adaln_fwd(c16, wada, bada, tag + "_adaln")[:nb]
        shift, scale, gate = (mod[:, k * d:(k + 1) * d].reshape(nb, 1, d) for k in range(3))
        qkv, hmod = normmod_matmul(xin, g, scale, shift, win, BF16, seq, True, tag + "_qkv", (0, 3 * di))
        if b_f is not None:
            (zf,) = normmod_matmul(xin, g, scale, shift, _pad_cols(win[:, 3 * di:], di + 128), F32, seq, False, tag + "_z")
        else:
            (zf,) = normmod_matmul(xin, g, scale, shift, win, F32, seq, False, tag + "_z", (3 * di, di))
        saved = dict(x=xin, g=g, scale=scale, gate=gate, qkv=qkv, h=hmod, zf=zf, win=win, wout=wout)
        if b_f is not None:
            fl_t = jnp.transpose(zf[:, di:di + h].reshape(nb, seq, h), (0, 2, 1))
            bf_col = b_f.reshape(h, 1)
            cum = forget_cumsum(fl_t, bf_col, tag + "_cum")
            ck = cum.reshape(nb, h, seq // tq, tq)
            o, lse, *rode = fox_fwd(qkv, ck, seq, di, dh, tag + "_attn", rider)
            saved.update(fl_t=fl_t, bf_col=bf_col, ck=ck, lse=lse, rode=rode)
        else:
            o, rtot = sb_fwd(qkv, seq, di, dh, tag + "_attn")
            saved.update(rtot=rtot)
        xout, y, u = out_proj_fwd(o, zf, wout, xin, gate, seq, tag + "_out")
        saved.update(o=o, y=y, u=u)
        return xout, saved

    def layer_bwd(dxo, sv, is_fox, tag, riding=()):
        do, dz, dy, dgate = out_proj_bwd(dxo, sv["y"], sv["gate"], sv["wout"], sv["o"], sv["zf"], seq, tag + "_dout")
        wout_blocks = tn_matmul(sv["u"], dy, tag + "_dwout").reshape(NDEV, di // NDEV, d)
        extra = None
        if is_fox:
            dq, dk, dv, dck, *rode = fox_bwd(sv["qkv"], sv["ck"], sv["o"], do, sv["lse"], seq, di, dh, tag + "_dattn",
                                             Exchange("scatter", list(riding) + [wout_blocks]))
            df_t, dbf = forget_cumsum_bwd(dck.reshape(nb, h, seq), sv["fl_t"], sv["bf_col"], tag + "_dcum")
            df = _pad_cols(jnp.transpose(df_t, (0, 2, 1)).reshape(t, h), 128).astype(BF16)
            extra = (df, jnp.transpose(_pad_cols(sv["win"][:, 4 * di:], 128)))
        else:
            dq, dk, dv = sb_bwd(sv["qkv"], do, sv["rtot"], seq, di, dh, tag + "_dattn")
        parts = [dq, dk, dv, dz]
        dwin = [tn_matmul(sv["h"], p, tag + "_dwin%d" % k) for k, p in enumerate(parts)]
        if is_fox:
            dwin.append(tn_matmul(sv["h"], df, tag + "_dwinf")[:, :h])
        dwin = jnp.concatenate(dwin, axis=1)
        w_t = jnp.transpose(sv["win"][:, :4 * di])
        win_blocks = _col_blocks(dwin)
        dh_rider = Exchange("scatter", [win_blocks]) if is_fox else None
        dxin, psum, ssum, *landed = dh_norm_bwd(parts, w_t, extra, sv["x"], sv["g"], sv["scale"], dxo, seq,
                                                tag + "_dh", dh_rider)
        pad = lambda a: _pad_rows16(a.reshape(nb, d))
        dwada, dbada, dng = adaln_bwd(c16, pad(ssum), pad(psum), pad(dgate), pad(sv["scale"]), sv["g"], tag + "_dadaln")
        grads = dict(wada=dwada, bada=dbada, ng=dng, win=win_blocks, wout=wout_blocks)
        if is_fox:
            grads["bf"] = dbf.reshape(1, h)
            grads["rode"] = rode
            grads["landed"] = landed
        return dxin, grads

    x1, sv_fox = layer_fwd(x0, fox_norm_g, fox_wada, fox_b_ada, fox_win, fox_wout, fox_b_f, "fox", sb_gather)
    sb_wada, sb_win = _cols(sv_fox["rode"][0]), _cols(sv_fox["rode"][1])
    sb_wout = sv_fox["rode"][2].reshape(di, d)
    sb_g, sb_bada = _cols(sv_fox["rode"][3]), _cols(sv_fox["rode"][4])
    x2, sv_sb = layer_fwd(x1, sb_g, sb_wada, sb_bada, sb_win, sb_wout, None, "sb")
    dx2, dgf, sq = final_loss(x2, final_norm_g.reshape(1, d), target, "loss_head")
    dx1, g_sb = layer_bwd(dx2, sv_sb, False, "sb")
    dx0, g_fox = layer_bwd(dx1, sv_fox, True, "fox", [g_sb["wada"], g_sb["win"], g_sb["wout"]])

    sb_wada_land, sb_win_land, sb_wout_land, fox_wout_land = g_fox["rode"]
    (fox_wada_land,) = all_to_all([g_fox["wada"]], "scatter_fox_wada_grad")
    land = [fox_wada_land, g_fox["landed"][0], fox_wout_land, sb_wada_land, sb_win_land, sb_wout_land]
    small = jnp.concatenate([g_fox["ng"], g_fox["bada"], _pad_cols(g_fox["bf"], 128), g_sb["ng"], g_sb["bada"],
                             dgf, sq], axis=1)
    (small_all,) = all_gather([small], "gather_small")
    small_sum, loss_row = sum_slots(small_all, d, "sum_small")
    loss = loss_row[0, 0]

    offs = {}
    pos = 0
    for nm, width in (("fox_ng", d), ("fox_bada", 3 * d), ("fox_bf", 128), ("sb_ng", d), ("sb_bada", 3 * d), ("fin_g", d)):
        offs[nm] = (pos, width)
        pos += width

    def small_grad(nm, width=None, shard=False):
        p0, wd = offs[nm]
        wd = width or wd
        if shard:
            blk = wd // NDEV
            return lax.dynamic_slice(small_sum, (0, p0 + me * blk), (1, blk))
        return small_sum[:, p0:p0 + wd]

    results = {}
    big_params = [("fox_w_ada", fox_w_ada, m_fox_w_ada, v_fox_w_ada), ("fox_w_in", fox_w_in, m_fox_w_in, v_fox_w_in),
                  ("fox_w_out", fox_w_out, m_fox_w_out, v_fox_w_out), ("sb_w_ada", sb_w_ada, m_sb_w_ada, v_sb_w_ada),
                  ("sb_w_in", sb_w_in, m_sb_w_in, v_sb_w_in), ("sb_w_out", sb_w_out, m_sb_w_out, v_sb_w_out)]
    for k, (nm, w, m, v) in enumerate(big_params):
        results[nm] = adamw(land[k], w, m, v, "adamw_" + nm)
    small_params = [("fox_norm_g", fox_norm_g, m_fox_norm_g, v_fox_norm_g, small_grad("fox_ng")),
                    ("fox_b_ada", fox_b_ada, m_fox_b_ada, v_fox_b_ada, small_grad("fox_bada")),
                    ("fox_b_f", fox_b_f, m_fox_b_f, v_fox_b_f, small_grad("fox_bf", h)),
                    ("sb_norm_g", sb_norm_g, m_sb_norm_g, v_sb_norm_g, small_grad("sb_ng", shard=True)),
                    ("sb_b_ada", sb_b_ada, m_sb_b_ada, v_sb_b_ada, small_grad("sb_bada", shard=True)),
                    ("final_norm_g", final_norm_g.reshape(1, d), m_final_norm_g.reshape(1, d),
                     v_final_norm_g.reshape(1, d), small_grad("fin_g"))]
    for nm, w, m, v, g in small_params:
        outs = adamw(g[None], w, m, v, "adamw_" + nm)
        if nm == "final_norm_g":
            outs = [o.reshape(d) for o in outs]
        results[nm] = outs

    order = ["fox_norm_g", "fox_w_ada", "fox_b_ada", "fox_w_in", "fox_b_f", "fox_w_out", "sb_norm_g", "sb_w_ada",
             "sb_b_ada", "sb_w_in", "sb_w_out", "final_norm_g"]
    out = [loss, dx0.reshape(nb, seq, d)]
    for k in range(4):
        out += [results[nm][k] for nm in order]
    return tuple(out)
```

```python
import functools
import math

import jax
import jax.numpy as jnp
from jax import lax
from jax.experimental import pallas as pl
from jax.experimental.pallas import tpu as pltpu

F32 = jnp.float32
BF16 = jnp.bfloat16
NDEV = 8
VMEM_LIMIT = 56 * 1024 * 1024
NORM_EPS = 1e-6
ADAM_LR, ADAM_B1, ADAM_B2, ADAM_EPS, ADAM_WD, ADAM_STEP = 0.001, 0.9, 0.999, 1e-08, 0.01, 10
ATTN_LANES = 128
ATTN_BLOCKS_PER_TRIP = 2
EXP_IS_ZERO_BELOW = -104.0
MESH = pl.DeviceIdType.MESH
ANY = pl.BlockSpec(memory_space=pl.ANY)


def _cp(*sem):
    return pltpu.CompilerParams(dimension_semantics=sem, vmem_limit_bytes=VMEM_LIMIT)


def _my_index():
    return 4 * lax.axis_index("x") + 2 * lax.axis_index("y") + lax.axis_index("c")


def _flip(k):
    x, y, c = lax.axis_index("x"), lax.axis_index("y"), lax.axis_index("c")
    kx, ky, kc = (k >> 2) & 1, (k >> 1) & 1, k & 1
    px = 1 - x if kx else x
    py = 1 - y if ky else y
    pc = 1 - c if kc else c
    return (px, py, pc), 4 * px + 2 * py + pc


class Exchange:
    def __init__(self, mode, arrs):
        self.gather = mode == "gather"
        self.arrs = list(arrs)
        self.n = len(self.arrs)
        self.in_specs = [ANY] * self.n
        self.out_specs = [ANY] * self.n
        self.out_shape = [jax.ShapeDtypeStruct(((NDEV,) + a.shape) if self.gather else a.shape, a.dtype)
                          for a in self.arrs]
        self.scratch = [pltpu.SemaphoreType.DMA((self.n * (NDEV - 1),)), pltpu.SemaphoreType.DMA((self.n * (NDEV - 1),)),
                        pltpu.SemaphoreType.DMA((self.n,))]

    def copies(self, ins, outs, sems):
        send_sems, recv_sems, local_sems = sems
        me = _my_index()
        out = []
        for a in range(self.n):
            out.append(pltpu.make_async_copy(ins[a] if self.gather else ins[a].at[me], outs[a].at[me], local_sems.at[a]))
            for k in range(1, NDEV):
                peer, pidx = _flip(k)
                out.append(pltpu.make_async_remote_copy(
                    src_ref=ins[a] if self.gather else ins[a].at[pidx], dst_ref=outs[a].at[me],
                    send_sem=send_sems.at[a * (NDEV - 1) + k - 1], recv_sem=recv_sems.at[a * (NDEV - 1) + k - 1],
                    device_id=peer, device_id_type=MESH))
        return out

    def run(self, name):
        n = self.n

        def body(*refs):
            cps = self.copies(refs[:n], refs[n:2 * n], refs[2 * n:])
            for cp in cps:
                cp.start()
            for cp in cps:
                cp.wait()

        return pl.pallas_call(body, name=name, out_shape=self.out_shape, in_specs=self.in_specs,
                              out_specs=self.out_specs, scratch_shapes=self.scratch)(*self.arrs)


def _ride(exchange, refs, first, last):
    n = exchange.n
    cps = exchange.copies(refs[:n], refs[n:2 * n], refs[2 * n:])

    @pl.when(first)
    def _():
        for cp in cps:
            cp.start()

    @pl.when(last)
    def _():
        for cp in cps:
            cp.wait()


def all_gather(arrs, name):
    return Exchange("gather", arrs).run(name)


def all_gather_by_chip(arrs, name):
    n = len(arrs)
    chips = (2, 4, 6)

    def body(*refs):
        ins, outs = refs[:n], refs[n:2 * n]
        send_sems, recv_sems, local_sems = refs[2 * n:]
        me = _my_index()
        sibling, _ = _flip(1)

        def copy(a, k, block, to, src=None):
            return pltpu.make_async_remote_copy(
                src_ref=outs[a].at[block] if src is None else src, dst_ref=outs[a].at[block],
                send_sem=send_sems.at[7 * a + k], recv_sem=recv_sems.at[7 * a + k], device_id=to, device_id_type=MESH)

        own, sent = [], []
        for a in range(n):
            own.append(pltpu.make_async_copy(ins[a], outs[a].at[me], local_sems.at[a]))
            own[-1].start()
            first = [copy(a, 0, me, sibling, ins[a])] + [copy(a, 1 + j, me, _flip(k)[0], ins[a]) for j, k in enumerate(chips)]
            for cp in first:
                cp.start()
            sent += first
        for a in range(n):
            for j, k in enumerate(chips):
                peer, pidx = _flip(k)
                copy(a, 1 + j, pidx, peer).wait_recv()
                sent.append(copy(a, 4 + j, pidx, sibling))
                sent[-1].start()
        for a in range(n):
            copy(a, 0, _flip(1)[1], sibling).wait_recv()
            for j, k in enumerate(chips):
                copy(a, 4 + j, _flip(k + 1)[1], sibling).wait_recv()
        for cp in sent:
            cp.wait_send()
        for cp in own:
            cp.wait()

    return pl.pallas_call(
        body, name=name,
        out_shape=[jax.ShapeDtypeStruct((NDEV,) + a.shape, a.dtype) for a in arrs],
        in_specs=[ANY] * n, out_specs=[ANY] * n,
        scratch_shapes=[pltpu.SemaphoreType.DMA((7 * n,)), pltpu.SemaphoreType.DMA((7 * n,)),
                        pltpu.SemaphoreType.DMA((n,))],
    )(*arrs)


def all_to_all(arrs, name):
    return Exchange("scatter", arrs).run(name)


def _tile(n, pref):
    t = min(n, pref)
    while n % t:
        t //= 2
    return t


def _col_tile(n, pref):
    t = _tile(n, pref)
    return t if t % 128 == 0 and t >= 512 else n


def adamw(land, w, m, v, name):
    slots, r, c = land.shape
    tr = _tile(r, 64)
    bc1 = 1.0 - ADAM_B1 ** ADAM_STEP
    bc2 = 1.0 - ADAM_B2 ** ADAM_STEP
    lead = w.ndim - 2

    def body(land_ref, w_ref, m_ref, v_ref, g_ref, d_ref, nm_ref, nv_ref):
        at = (0,) * lead + (Ellipsis,)
        g = land_ref[0].astype(F32)
        for s in range(1, slots):
            g = g + land_ref[s].astype(F32)
        nm = ADAM_B1 * m_ref[at] + (1.0 - ADAM_B1) * g
        nv = ADAM_B2 * v_ref[at] + (1.0 - ADAM_B2) * (g * g)
        m_hat = nm / bc1
        v_hat = nv / bc2
        g_ref[at] = g
        nm_ref[at] = nm
        nv_ref[at] = nv
        d_ref[at] = -ADAM_LR * (m_hat / (jnp.sqrt(v_hat) + ADAM_EPS) + ADAM_WD * w_ref[at])

    blk = pl.BlockSpec((1,) * lead + (tr, c), lambda i: (0,) * lead + (i, 0))
    return pl.pallas_call(
        body, name=name, grid=(r // tr,),
        in_specs=[pl.BlockSpec((slots, tr, c), lambda i: (0, i, 0)), blk, blk, blk],
        out_specs=[blk] * 4,
        out_shape=[jax.ShapeDtypeStruct(w.shape, F32)] * 4,
        compiler_params=_cp("parallel"),
    )(land, w, m, v)


def sum_slots(land, d_model, name):
    slots, _, n = land.shape

    def body(land_ref, o_ref, loss_ref):
        g = land_ref[0]
        for s in range(1, slots):
            g = g + land_ref[s]
        o_ref[...] = g
        sq = jnp.sum(g[:, n - d_model:], axis=-1, keepdims=True)
        loss_ref[...] = jnp.broadcast_to(sq * (0.5 / d_model), (1, 128))

    return pl.pallas_call(
        body, name=name,
        out_shape=[jax.ShapeDtypeStruct((1, n), F32), jax.ShapeDtypeStruct((1, 128), F32)],
    )(land)


def _sigmoid(x):
    return 1.0 / (1.0 + jnp.exp(-x))


def adaln_fwd(c16, w_ada, b_ada, name):
    d3 = w_ada.shape[1]

    def body(c_ref, w_ref, b_ref, o_ref):
        cc = c_ref[...]
        sc = (cc * _sigmoid(cc)).astype(BF16)
        o_ref[...] = jnp.dot(sc, w_ref[...], preferred_element_type=F32) + b_ref[...]

    return pl.pallas_call(body, name=name, out_shape=jax.ShapeDtypeStruct((16, d3), F32),
                          compiler_params=pltpu.CompilerParams(vmem_limit_bytes=VMEM_LIMIT))(c16, w_ada, b_ada)


def adaln_bwd(c16, dshift16, p16, dgate16, scale16, g, name):
    d = c16.shape[1]
    nb = 3 * d // NDEV

    def body(c_ref, ds_ref, p_ref, dg_ref, sc_ref, g_ref, dw_ref, db_ref, dng_ref, dmod_scr):
        j = pl.program_id(0)

        @pl.when(j == 0)
        def _():
            p = p_ref[...]
            dmod = jnp.concatenate([ds_ref[...], p * g_ref[...], dg_ref[...]], axis=-1)
            dmod_scr[...] = dmod
            db_ref[...] = jnp.sum(dmod, axis=0, keepdims=True)
            dng_ref[...] = jnp.sum((1.0 + sc_ref[...]) * p, axis=0, keepdims=True)

        cc = c_ref[...]
        sc = (cc * _sigmoid(cc)).astype(BF16)
        start = pl.multiple_of(j * nb, 128)
        dm = dmod_scr[:, pl.ds(start, nb)].astype(BF16)
        dw_ref[0] = lax.dot_general(sc, dm, (((0,), (0,)), ((), ())), preferred_element_type=F32).astype(BF16)

    full = lambda shape: pl.BlockSpec(shape, lambda j: (0,) * len(shape))
    return pl.pallas_call(
        body, name=name, grid=(NDEV,),
        in_specs=[full((16, d))] * 5 + [full((1, d))],
        out_specs=[pl.BlockSpec((1, d, nb), lambda j: (j, 0, 0)), full((1, 3 * d)), full((1, d))],
        out_shape=[jax.ShapeDtypeStruct((NDEV, d, nb), BF16), jax.ShapeDtypeStruct((1, 3 * d), F32),
                   jax.ShapeDtypeStruct((1, d), F32)],
        scratch_shapes=[pltpu.VMEM((16, 3 * d), F32)],
        compiler_params=_cp("arbitrary"),
    )(c16, dshift16, p16, dgate16, scale16, g)


def _modulated_norm(x, g, scale, shift):
    rstd = lax.rsqrt(jnp.mean(x * x, axis=-1, keepdims=True) + NORM_EPS)
    return ((x * rstd) * g) * (1.0 + scale) + shift


def normmod_matmul(x, g, scale, shift, w, out_dtype, seq, emit_h, name, cols=None):
    t, d = x.shape
    col0, n = cols or (0, w.shape[1])
    tm = _tile(seq, 512)
    tn = _col_tile(n, 2048)
    assert col0 % tn == 0
    jb = col0 // tn
    per_seq = seq // tm

    def body(x_ref, g_ref, sc_ref, sh_ref, w_ref, *rest):
        if emit_h:
            o_ref, h_ref, h_scr = rest
        else:
            o_ref, h_scr = rest

        @pl.when(pl.program_id(1) == 0)
        def _():
            h = _modulated_norm(x_ref[...], g_ref[...], sc_ref[0], sh_ref[0]).astype(BF16)
            h_scr[...] = h
            if emit_h:
                h_ref[...] = h

        o_ref[...] = jnp.dot(h_scr[...], w_ref[...], preferred_element_type=F32).astype(out_dtype)

    mod_spec = pl.BlockSpec((1, 1, d), lambda i, j: (i // per_seq, 0, 0))
    out_specs = [pl.BlockSpec((tm, tn), lambda i, j: (i, j))]
    out_shape = [jax.ShapeDtypeStruct((t, n), out_dtype)]
    if emit_h:
        out_specs.append(pl.BlockSpec((tm, d), lambda i, j: (i, 0)))
        out_shape.append(jax.ShapeDtypeStruct((t, d), BF16))
    return pl.pallas_call(
        body, name=name, grid=(t // tm, n // tn),
        in_specs=[pl.BlockSpec((tm, d), lambda i, j: (i, 0)), pl.BlockSpec((1, d), lambda i, j: (0, 0)),
                  mod_spec, mod_spec, pl.BlockSpec((d, tn), lambda i, j: (0, jb + j))],
        out_specs=out_specs, out_shape=out_shape,
        scratch_shapes=[pltpu.VMEM((tm, d), BF16)],
        compiler_params=_cp("parallel", "arbitrary"),
    )(x, g, scale, shift, w)


def out_proj_fwd(o, zf, w_out, x, gate, seq, name):
    t, di = o.shape
    d = x.shape[1]
    tm = _tile(seq, 256)
    per_seq = seq // tm

    def body(o_ref, z_ref, w_ref, x_ref, gt_ref, xn_ref, y_ref, u_ref):
        z = z_ref[...]
        u = (o_ref[...] * (z * _sigmoid(z))).astype(BF16)
        y = jnp.dot(u, w_ref[...], preferred_element_type=F32)
        u_ref[...] = u
        y_ref[...] = y
        xn_ref[...] = x_ref[...] + gt_ref[0] * y

    row = lambda c: pl.BlockSpec((tm, c), lambda i: (i, 0))
    return pl.pallas_call(
        body, name=name, grid=(t // tm,),
        in_specs=[row(di), row(di), pl.BlockSpec((di, d), lambda i: (0, 0)), row(d),
                  pl.BlockSpec((1, 1, d), lambda i: (i // per_seq, 0, 0))],
        out_specs=[row(d), row(d), row(di)],
        out_shape=[jax.ShapeDtypeStruct((t, d), F32), jax.ShapeDtypeStruct((t, d), F32),
                   jax.ShapeDtypeStruct((t, di), BF16)],
        compiler_params=_cp("parallel"),
    )(o, zf, w_out, x, gate)


def out_proj_bwd(dxo, y, gate, w_out, o, zf, seq, name):
    t, d = dxo.shape
    di = o.shape[1]
    tm = _tile(seq, 256)
    per_seq = seq // tm

    def body(dx_ref, y_ref, gt_ref, w_ref, o_ref, z_ref, do_ref, dz_ref, dy_ref, dgt_ref):
        dx = dx_ref[...]
        part = jnp.sum(dx * y_ref[...], axis=0, keepdims=True)

        @pl.when(pl.program_id(0) % per_seq == 0)
        def _():
            dgt_ref[0] = part

        @pl.when(pl.program_id(0) % per_seq != 0)
        def _():
            dgt_ref[0] += part

        dy = (dx * gt_ref[0]).astype(BF16)
        dy_ref[...] = dy
        du = lax.dot_general(dy, w_ref[...], (((1,), (1,)), ((), ())), preferred_element_type=F32)
        z = z_ref[...]
        sg = _sigmoid(z)
        do_ref[...] = (du * (z * sg)).astype(BF16)
        dz_ref[...] = (du * o_ref[...] * (sg * (1.0 + z * (1.0 - sg)))).astype(BF16)

    row = lambda c: pl.BlockSpec((tm, c), lambda i: (i, 0))
    seq_spec = pl.BlockSpec((1, 1, d), lambda i: (i // per_seq, 0, 0))
    return pl.pallas_call(
        body, name=name, grid=(t // tm,),
        in_specs=[row(d), row(d), seq_spec, pl.BlockSpec((di, d), lambda i: (0, 0)), row(di), row(di)],
        out_specs=[row(di), row(di), row(d), seq_spec],
        out_shape=[jax.ShapeDtypeStruct((t, di), BF16), jax.ShapeDtypeStruct((t, di), BF16),
                   jax.ShapeDtypeStruct((t, d), BF16), jax.ShapeDtypeStruct((t // seq, 1, d), F32)],
        compiler_params=_cp("arbitrary"),
    )(dxo, y, gate, w_out, o, zf)


def tn_matmul(a, b, name):
    t, m = a.shape
    n = b.shape[1]
    tn = _col_tile(n, 1024)
    tk = _tile(t, 1024)
    nk = t // tk

    def body(a_ref, b_ref, o_ref, acc):
        part = lax.dot_general(a_ref[...], b_ref[...], (((0,), (0,)), ((), ())), preferred_element_type=F32)
        k = pl.program_id(1)

        @pl.when(k == 0)
        def _():
            acc[...] = part

        @pl.when(k != 0)
        def _():
            acc[...] += part

        @pl.when(k == nk - 1)
        def _():
            o_ref[...] = acc[...].astype(BF16)

    return pl.pallas_call(
        body, name=name, grid=(n // tn, nk),
        in_specs=[pl.BlockSpec((tk, m), lambda j, k: (k, 0)), pl.BlockSpec((tk, tn), lambda j, k: (k, j))],
        out_specs=pl.BlockSpec((m, tn), lambda j, k: (0, j)),
        out_shape=jax.ShapeDtypeStruct((m, n), BF16),
        scratch_shapes=[pltpu.VMEM((m, tn), F32)],
        compiler_params=_cp("parallel", "arbitrary"),
    )(a, b)


def dh_norm_bwd(parts, w_t, extra, x, g, scale, dxo, seq, name, rider=None):
    t, d = x.shape
    nparts = len(parts)
    kw = parts[0].shape[1]
    tm = _tile(seq, 512)
    per_seq = seq // tm
    has_extra = extra is not None
    grid = (t // tm,)
    nr = rider.n if rider else 0
    n_in = nparts + 1 + (2 if has_extra else 0) + 4

    def body(*refs):
        p_refs = refs[:nparts]
        w_ref = refs[nparts]
        pos = nparts + 1
        if has_extra:
            e_ref, we_ref = refs[pos], refs[pos + 1]
            pos += 2
        x_ref, g_ref, sc_ref, dxo_ref = refs[pos:pos + 4]
        dx_ref, pp_ref, ss_ref = refs[n_in + nr:n_in + nr + 3]
        if rider:
            _ride(rider, refs[n_in:n_in + nr] + refs[n_in + nr + 3:], *_grid_ends(grid))
        i = pl.program_id(0)

        dh = jnp.dot(p_refs[0][...], w_ref[0:kw, :], preferred_element_type=F32)
        for kk in range(1, nparts):
            dh = dh + jnp.dot(p_refs[kk][...], w_ref[kk * kw:(kk + 1) * kw, :], preferred_element_type=F32)
        if has_extra:
            dh = dh + jnp.dot(e_ref[...], we_ref[...], preferred_element_type=F32)
        xx = x_ref[...]
        rstd = lax.rsqrt(jnp.mean(xx * xx, axis=-1, keepdims=True) + NORM_EPS)
        xhat = xx * rstd
        dxhat = dh * (g_ref[...] * (1.0 + sc_ref[0]))
        dx_ref[...] = dxo_ref[...] + rstd * (dxhat - xhat * jnp.mean(dxhat * xhat, axis=-1, keepdims=True))
        pp = jnp.sum(dh * xhat, axis=0, keepdims=True)
        ss = jnp.sum(dh, axis=0, keepdims=True)

        @pl.when(i % per_seq == 0)
        def _():
            pp_ref[0] = pp
            ss_ref[0] = ss

        @pl.when(i % per_seq != 0)
        def _():
            pp_ref[0] += pp
            ss_ref[0] += ss

    row = lambda c: pl.BlockSpec((tm, c), lambda i: (i, 0))
    whole = lambda a: pl.BlockSpec(a.shape, lambda i: (0, 0), pipeline_mode=pl.Buffered(1))
    seq_spec = pl.BlockSpec((1, 1, d), lambda i: (i // per_seq, 0, 0))
    in_specs = [row(kw)] * nparts + [whole(w_t)]
    args = list(parts) + [w_t]
    if has_extra:
        in_specs += [row(extra[0].shape[1]), whole(extra[1])]
        args += list(extra)
    in_specs += [row(d), pl.BlockSpec((1, d), lambda i: (0, 0)), seq_spec, row(d)]
    args += [x, g, scale, dxo]
    return pl.pallas_call(
        body, name=name, grid=grid,
        in_specs=in_specs + (rider.in_specs if rider else []),
        out_specs=[row(d), seq_spec, seq_spec] + (rider.out_specs if rider else []),
        out_shape=[jax.ShapeDtypeStruct((t, d), F32), jax.ShapeDtypeStruct((t // seq, 1, d), F32),
                   jax.ShapeDtypeStruct((t // seq, 1, d), F32)] + (rider.out_shape if rider else []),
        scratch_shapes=rider.scratch if rider else [],
        compiler_params=_cp("arbitrary"),
    )(*args, *(rider.arrs if rider else []))


def final_loss(x, g, target, name):
    t, d = x.shape
    tm = _tile(t, 512)

    def body(x_ref, g_ref, t_ref, dx_ref, dg_ref, sq_ref):
        xx = x_ref[...]
        gg = g_ref[...]
        rstd = lax.rsqrt(jnp.mean(xx * xx, axis=-1, keepdims=True) + NORM_EPS)
        xhat = xx * rstd
        err = xhat * gg - t_ref[...]
        dy = err * (1.0 / d)
        dxhat = dy * gg
        dx_ref[...] = rstd * (dxhat - xhat * jnp.mean(dxhat * xhat, axis=-1, keepdims=True))
        dg = jnp.sum(dy * xhat, axis=0, keepdims=True)
        sq = jnp.sum(err * err, axis=0, keepdims=True)

        @pl.when(pl.program_id(0) == 0)
        def _():
            dg_ref[...] = dg
            sq_ref[...] = sq

        @pl.when(pl.program_id(0) != 0)
        def _():
            dg_ref[...] += dg
            sq_ref[...] += sq

    row = pl.BlockSpec((tm, d), lambda i: (i, 0))
    vec = pl.BlockSpec((1, d), lambda i: (0, 0))
    return pl.pallas_call(
        body, name=name, grid=(t // tm,),
        in_specs=[row, vec, row], out_specs=[row, vec, vec],
        out_shape=[jax.ShapeDtypeStruct((t, d), F32), jax.ShapeDtypeStruct((1, d), F32),
                   jax.ShapeDtypeStruct((1, d), F32)],
        compiler_params=_cp("arbitrary"),
    )(x, g, target)


def _split3(x):
    x1 = x.astype(BF16)
    r1 = x - x1.astype(F32)
    x2 = r1.astype(BF16)
    x3 = (r1 - x2.astype(F32)).astype(BF16)
    return x1, x2, x3


def _dot3(x, u):
    x1, x2, x3 = _split3(x)
    return (jnp.dot(x1, u, preferred_element_type=F32) + jnp.dot(x2, u, preferred_element_type=F32)
            + jnp.dot(x3, u, preferred_element_type=F32))


def _log_sigmoid(x):
    return jnp.minimum(x, 0.0) - jnp.log1p(jnp.exp(-jnp.abs(x)))


def forget_cumsum(fl_t, b_f, name):
    b, h, s = fl_t.shape

    def body(f_ref, b_ref, o_ref):
        lf = _log_sigmoid(f_ref[0] + b_ref[...])
        u = (lax.broadcasted_iota(jnp.int32, (s, s), 0) <= lax.broadcasted_iota(jnp.int32, (s, s), 1)).astype(BF16)
        o_ref[0] = _dot3(lf, u)

    return pl.pallas_call(
        body, name=name, grid=(b,),
        in_specs=[pl.BlockSpec((1, h, s), lambda i: (i, 0, 0)), pl.BlockSpec((h, 1), lambda i: (0, 0))],
        out_specs=pl.BlockSpec((1, h, s), lambda i: (i, 0, 0)),
        out_shape=jax.ShapeDtypeStruct((b, h, s), F32),
        compiler_params=_cp("parallel"),
    )(fl_t, b_f)


def forget_cumsum_bwd(dcum, fl_t, b_f, name):
    b, h, s = fl_t.shape

    def body(d_ref, f_ref, b_ref, o_ref, db_ref):
        u = (lax.broadcasted_iota(jnp.int32, (s, s), 0) >= lax.broadcasted_iota(jnp.int32, (s, s), 1)).astype(BF16)
        dlf = _dot3(d_ref[0], u)
        df = dlf * _sigmoid(-(f_ref[0] + b_ref[...]))
        o_ref[0] = df
        part = jnp.sum(df, axis=-1, keepdims=True)

        @pl.when(pl.program_id(0) == 0)
        def _():
            db_ref[...] = part

        @pl.when(pl.program_id(0) != 0)
        def _():
            db_ref[...] += part

    blk = pl.BlockSpec((1, h, s), lambda i: (i, 0, 0))
    return pl.pallas_call(
        body, name=name, grid=(b,),
        in_specs=[blk, blk, pl.BlockSpec((h, 1), lambda i: (0, 0))],
        out_specs=[blk, pl.BlockSpec((h, 1), lambda i: (0, 0))],
        out_shape=[jax.ShapeDtypeStruct((b, h, s), F32), jax.ShapeDtypeStruct((h, 1), F32)],
        compiler_params=_cp("arbitrary"),
    )(dcum, fl_t, b_f)


def _nt(a, b):
    return lax.dot_general(a, b, (((1,), (1,)), ((), ())), preferred_element_type=F32)


def _attn_dims(t, seq, dh):
    tq = _tile(seq, 512)
    tk = _tile(seq, 256)
    return t // seq, tq, tk, seq // tq, tq // tk, ATTN_LANES // dh


def _blocks_per_trip(r):
    return ATTN_BLOCKS_PER_TRIP if r % ATTN_BLOCKS_PER_TRIP == 0 else 1


def _scaled_q(q_ref, dh):
    scale = dh ** -0.5
    if math.log2(dh) % 2 == 0:
        return (q_ref[...].astype(F32) * scale).astype(BF16), None
    return q_ref[...], scale


def _diag_masks(tq, tk, r, strict):
    row = lax.broadcasted_iota(jnp.int32, (tq, tk), 0)
    col = lax.broadcasted_iota(jnp.int32, (tq, tk), 1)
    return [(col + d * tk < row) if strict else (col + d * tk <= row) for d in range(r)]


def _rows(state, lo):
    return tuple(a[lo:] for a in state)


def _put_rows(state, part, lo):
    if lo == 0:
        return tuple(part)
    return tuple(jnp.concatenate([a[:lo], p], axis=0) for a, p in zip(state, part))


def _transposed(x):
    return jnp.transpose(x.astype(F32)).astype(BF16)


def _store_transposed(dst_ref, src_scr, tk):
    for jb in range(src_scr.shape[0]):
        dst_ref[jb * tk:(jb + 1) * tk, :] = jnp.transpose(src_scr[jb]).astype(BF16)


def _dot2(x, u):
    hi = x.astype(BF16)
    lo = (x - hi.astype(F32)).astype(BF16)
    return jnp.dot(jnp.concatenate([hi, lo], axis=1), jnp.concatenate([u, u], axis=0), preferred_element_type=F32)


def _attn_specs(seq, di, tq):
    nq = seq // tq
    cb = di // ATTN_LANES
    q_spec = pl.BlockSpec((tq, ATTN_LANES), lambda b, hp, i: (b * nq + i, hp))
    k_spec = pl.BlockSpec((seq, ATTN_LANES), lambda b, hp, i: (b, cb + hp))
    v_spec = pl.BlockSpec((seq, ATTN_LANES), lambda b, hp, i: (b, 2 * cb + hp))
    kv_out = pl.BlockSpec((seq, ATTN_LANES), lambda b, hp, i: (b, hp))
    return q_spec, k_spec, v_spec, kv_out


def _grid_ends(grid):
    ids = [pl.program_id(a) for a in range(len(grid))]
    first = functools.reduce(jnp.logical_and, [p == 0 for p in ids])
    last = functools.reduce(jnp.logical_and, [p == g - 1 for p, g in zip(ids, grid)])
    return first, last


def fox_fwd(qkv, ck, seq, di, dh, name, rider=None):
    t = qkv.shape[0]
    nb, tq, tk, nq, r, hpb = _attn_dims(t, seq, dh)
    per_trip = _blocks_per_trip(r)
    heads = [slice(hh * dh, (hh + 1) * dh) for hh in range(hpb)]
    assert hpb >= 2, "the row sums of p ride in another head's lanes of the p @ v matmul"
    q_spec, k_spec, v_spec, _ = _attn_specs(seq, di, tq)
    grid = (nb, di // ATTN_LANES, nq)
    nr = rider.n if rider else 0

    def body(q_ref, k_ref, v_ref, ck_ref, *rest):
        o_ref, lse_ref = rest[nr:nr + 2]
        if rider:
            _ride(rider, rest[:nr] + rest[nr + 2:], *_grid_ends(grid))
        i = pl.program_id(2)
        masks = _diag_masks(tq, tk, r, False)
        q, scale = _scaled_q(q_ref, dh)
        qs = [q[:, hd] for hd in heads]
        lane_head = lax.broadcasted_iota(jnp.int32, (tk, ATTN_LANES), 1) // dh

        def scores(hh, j):
            start = pl.multiple_of(j * tk, tk)
            kh = k_ref[pl.ds(start, tk), heads[hh]]
            vt = v_ref[pl.ds(start, tk), :]
            vh = jnp.where(lane_head == hh, vt, jnp.ones_like(vt))
            s = _nt(qs[hh], kh)
            if scale is not None:
                s = s * scale
            return s - ck_ref[0, hh, pl.ds(j, 1), :], vh

        def update(state, s, vh):
            m, acc = state
            m_new = jnp.maximum(m, jnp.max(s, axis=-1, keepdims=True))
            p = jnp.exp(s - m_new)
            return m_new, jnp.exp(m - m_new) * acc + _dot2(p, vh)

        states = []
        for hh in range(hpb):
            s, vh = scores(hh, r * i)
            s = jnp.where(masks[0], s, -jnp.inf)
            m = jnp.max(s, axis=-1, keepdims=True)
            states.append((m, _dot2(jnp.exp(s - m), vh)))
        for d in range(1, r):
            for hh in range(hpb):
                s, vh = scores(hh, r * i + d)
                states[hh] = update(states[hh], jnp.where(masks[d], s, -jnp.inf), vh)

        def step(jj, states):
            for sub in range(per_trip):
                states = tuple(update(states[hh], *scores(hh, r * i - 1 - (per_trip * jj + sub))) for hh in range(hpb))
            return states

        states = lax.fori_loop(0, r * i // per_trip, step, tuple(states))
        for hh in range(hpb):
            m, acc = states[hh]
            other = heads[(hh + 1) % hpb].start
            l = acc[:, other:other + 1]
            o_ref[:, heads[hh]] = acc[:, heads[hh]] / l
            lse_ref[0, 0, :, hh:hh + 1] = m + jnp.log(l)

    return pl.pallas_call(
        body, name=name, grid=grid,
        in_specs=[q_spec, k_spec, v_spec,
                  pl.BlockSpec((1, hpb, seq // tk, tk), lambda b, hp, i: (b, hp, 0, 0))] + (rider.in_specs if rider else []),
        out_specs=[q_spec, pl.BlockSpec((1, 1, tq, hpb), lambda b, hp, i: (b, hp, i, 0))]
        + (rider.out_specs if rider else []),
        out_shape=[jax.ShapeDtypeStruct((t, di), F32), jax.ShapeDtypeStruct((nb, di // ATTN_LANES, seq, hpb), F32)]
        + (rider.out_shape if rider else []),
        scratch_shapes=rider.scratch if rider else [],
        compiler_params=_cp("arbitrary", "arbitrary", "arbitrary"),
    )(qkv, qkv, qkv, ck, *(rider.arrs if rider else []))


def fox_bwd(qkv, ck, o, do, lse, seq, di, dh, name, rider=None):
    t = qkv.shape[0]
    nb, tq, tk, nq, r, hpb = _attn_dims(t, seq, dh)
    per_trip = _blocks_per_trip(r)
    heads = [slice(hh * dh, (hh + 1) * dh) for hh in range(hpb)]
    q_spec, k_spec, v_spec, kv_out = _attn_specs(seq, di, tq)
    ck_spec = pl.BlockSpec((1, hpb, seq // tk, tk), lambda b, hp, i: (b, hp, 0, 0))
    grid = (nb, di // ATTN_LANES, nq)
    nr = rider.n if rider else 0

    def body(q_ref, k_ref, v_ref, ck_ref, o_ref, do_ref, lse_ref, *rest):
        dq_ref, dk_ref, dv_ref, dck_ref = rest[nr:nr + 4]
        dkt_scr, dvt_scr = rest[2 * nr + 4:2 * nr + 6]
        if rider:
            _ride(rider, rest[:nr] + rest[nr + 4:2 * nr + 4] + rest[2 * nr + 6:], *_grid_ends(grid))
        i = pl.program_id(2)

        @pl.when(i == 0)
        def _():
            dkt_scr[...] = jnp.zeros_like(dkt_scr)
            dvt_scr[...] = jnp.zeros_like(dvt_scr)
            dck_ref[...] = jnp.zeros_like(dck_ref)

        masks = _diag_masks(tq, tk, r, False)
        q, scale = _scaled_q(q_ref, dh)
        do = do_ref[...]
        q_t, do_t = _transposed(q), _transposed(do)
        qs = [q[:, hd] for hd in heads]
        dos = [do[:, hd] for hd in heads]
        deltas = [jnp.sum(dos[hh].astype(F32) * o_ref[:, heads[hh]], axis=-1, keepdims=True) for hh in range(hpb)]
        lses = [lse_ref[0, 0, :, hh:hh + 1] for hh in range(hpb)]

        def block(hh, j, mask, dq_acc, lo=0):
            start = pl.multiple_of(j * tk, tk)
            kh = k_ref[pl.ds(start, tk), heads[hh]]
            vh = v_ref[pl.ds(start, tk), heads[hh]]
            s = _nt(qs[hh][lo:], kh)
            if scale is not None:
                s = s * scale
            p = jnp.exp(s - ck_ref[0, hh, pl.ds(j, 1), :] - lses[hh][lo:])
            if mask is not None:
                p = jnp.where(mask[lo:], p, 0.0)
            ds = p * (_nt(dos[hh][lo:], vh) - deltas[hh][lo:])
            dsb = ds.astype(BF16)
            dkt = jnp.dot(q_t[heads[hh], lo:], dsb, preferred_element_type=F32)
            if scale is not None:
                dkt = dkt * scale
            dkt_scr[j, heads[hh], :] += dkt
            dvt_scr[j, heads[hh], :] += jnp.dot(do_t[heads[hh], lo:], p.astype(BF16), preferred_element_type=F32)
            dck_ref[0, hh, pl.ds(j, 1), :] -= jnp.sum(ds, axis=0, keepdims=True)
            return dq_acc + jnp.dot(dsb, kh, preferred_element_type=F32)

        accs = [jnp.zeros((tq, dh), F32)] * hpb
        for d in range(r):
            lo = d * tk
            accs = [_put_rows((accs[hh],), (block(hh, r * i + d, masks[d], accs[hh][lo:], lo),), lo)[0]
                    for hh in range(hpb)]

        def step(jj, accs):
            for sub in range(per_trip):
                accs = tuple(block(hh, per_trip * jj + sub, None, accs[hh]) for hh in range(hpb))
            return accs

        accs = lax.fori_loop(0, r * i // per_trip, step, tuple(accs))
        for hh in range(hpb):
            dq_ref[:, heads[hh]] = (accs[hh] * dh ** -0.5).astype(BF16)

        @pl.when(i == nq - 1)
        def _():
            _store_transposed(dk_ref, dkt_scr, tk)
            _store_transposed(dv_ref, dvt_scr, tk)

    return pl.pallas_call(
        body, name=name, grid=grid,
        in_specs=[q_spec, k_spec, v_spec, ck_spec, q_spec, q_spec,
                  pl.BlockSpec((1, 1, tq, hpb), lambda b, hp, i: (b, hp, i, 0))] + (rider.in_specs if rider else []),
        out_specs=[q_spec, kv_out, kv_out, ck_spec] + (rider.out_specs if rider else []),
        out_shape=[jax.ShapeDtypeStruct((t, di), BF16)] * 3 + [jax.ShapeDtypeStruct(ck.shape, F32)]
        + (rider.out_shape if rider else []),
        scratch_shapes=[pltpu.VMEM((seq // tk, ATTN_LANES, tk), F32), pltpu.VMEM((seq // tk, ATTN_LANES, tk), F32)]
        + (rider.scratch if rider else []),
        compiler_params=_cp("arbitrary", "arbitrary", "arbitrary"),
    )(qkv, qkv, qkv, ck, o, do, lse, *(rider.arrs if rider else []))


def _sb_logits(qh, kh, scale, strict):
    z = _nt(qh, kh)
    if scale is not None:
        z = z * scale
    e = jnp.exp(-jnp.abs(z))
    lb = jnp.minimum(z, 0.0) - jnp.log(1.0 + e)
    lk = lb - z
    return lb, lk if strict is None else jnp.where(strict, lk, 0.0)


def _cum(x, u):
    return jnp.dot(x.astype(BF16), u, preferred_element_type=F32)


def sb_fwd(qkv, seq, di, dh, name):
    t = qkv.shape[0]
    nb, tq, tk, nq, r, hpb = _attn_dims(t, seq, dh)
    heads = [slice(hh * dh, (hh + 1) * dh) for hh in range(hpb)]
    q_spec, k_spec, v_spec, _ = _attn_specs(seq, di, tq)

    def body(q_ref, k_ref, v_ref, o_ref, rt_ref):
        i = pl.program_id(2)
        masks = _diag_masks(tq, tk, r, True)
        row = lax.broadcasted_iota(jnp.int32, (tk, tk), 0)
        col = lax.broadcasted_iota(jnp.int32, (tk, tk), 1)
        u_after = (row > col).astype(BF16)
        q, scale = _scaled_q(q_ref, dh)
        qs = [q[:, hd] for hd in heads]

        def block(hh, j, mask, state, lo=0):
            rr, acc = state
            start = pl.multiple_of(j * tk, tk)
            kh = k_ref[pl.ds(start, tk), heads[hh]]
            vh = v_ref[pl.ds(start, tk), heads[hh]]
            if mask is not None:
                mask = mask[lo:]
            lb, lk = _sb_logits(qs[hh][lo:], kh, scale, mask)
            a = jnp.exp(lb + _cum(lk, u_after) + rr)
            if mask is not None:
                a = jnp.where(mask, a, 0.0)
            acc = acc + jnp.dot(a.astype(BF16), vh, preferred_element_type=F32)
            return rr + jnp.sum(lk, axis=-1, keepdims=True), acc

        states = [(jnp.zeros((tq, 1), F32), jnp.zeros((tq, dh), F32))] * hpb
        for d in reversed(range(r)):
            lo = d * tk
            states = [_put_rows(states[hh], block(hh, r * i + d, masks[d], _rows(states[hh], lo), lo), lo)
                      for hh in range(hpb)]

        def any_weight_left(states):
            top = functools.reduce(jnp.maximum, [jnp.max(st[0]) for st in states])
            return (top > EXP_IS_ZERO_BELOW).astype(jnp.int32)

        def step(carry):
            jj, _, states = carry
            states = tuple(block(hh, r * i - 1 - jj, None, states[hh]) for hh in range(hpb))
            return jj + 1, any_weight_left(states), states

        visited, _, states = lax.while_loop(lambda c: jnp.logical_and(c[0] < r * i, c[1] > 0), step,
                                            (jnp.int32(0), any_weight_left(states), tuple(states)))
        for hh in range(hpb):
            o_ref[:, heads[hh]] = states[hh][1]
            rt_ref[0, 0, :, hh:hh + 1] = states[hh][0]
        rt_ref[0, 0, :, hpb:hpb + 1] = jnp.full((tq, 1), visited.astype(F32))

    return pl.pallas_call(
        body, name=name, grid=(nb, di // ATTN_LANES, nq),
        in_specs=[q_spec, k_spec, v_spec],
        out_specs=[q_spec, pl.BlockSpec((1, 1, tq, hpb + 1), lambda b, hp, i: (b, hp, i, 0))],
        out_shape=[jax.ShapeDtypeStruct((t, di), F32), jax.ShapeDtypeStruct((nb, di // ATTN_LANES, seq, hpb + 1), F32)],
        compiler_params=_cp("parallel", "parallel", "arbitrary"),
    )(qkv, qkv, qkv)


def sb_bwd(qkv, do, rtot, seq, di, dh, name):
    t = qkv.shape[0]
    nb, tq, tk, nq, r, hpb = _attn_dims(t, seq, dh)
    heads = [slice(hh * dh, (hh + 1) * dh) for hh in range(hpb)]
    q_spec, k_spec, v_spec, kv_out = _attn_specs(seq, di, tq)

    def body(q_ref, k_ref, v_ref, do_ref, rt_ref, dq_ref, dk_ref, dv_ref, dkt_scr, dvt_scr):
        i = pl.program_id(2)

        @pl.when(i == 0)
        def _():
            dkt_scr[...] = jnp.zeros_like(dkt_scr)
            dvt_scr[...] = jnp.zeros_like(dvt_scr)

        masks = _diag_masks(tq, tk, r, True)
        row = lax.broadcasted_iota(jnp.int32, (tk, tk), 0)
        col = lax.broadcasted_iota(jnp.int32, (tk, tk), 1)
        u_after = (row > col).astype(BF16)
        u_before = (row < col).astype(BF16)
        q, scale = _scaled_q(q_ref, dh)
        do = do_ref[...]
        q_t, do_t = _transposed(q), _transposed(do)
        qs = [q[:, hd] for hd in heads]
        dos = [do[:, hd] for hd in heads]
        rts = [rt_ref[0, 0, :, hh:hh + 1] for hh in range(hpb)]

        def block(hh, j, mask, state, lo=0):
            lc, gc, dq_acc = state
            start = pl.multiple_of(j * tk, tk)
            kh = k_ref[pl.ds(start, tk), heads[hh]]
            vh = v_ref[pl.ds(start, tk), heads[hh]]
            if mask is not None:
                mask = mask[lo:]
            lb, lk = _sb_logits(qs[hh][lo:], kh, scale, mask)
            lc = lc + jnp.sum(lk, axis=-1, keepdims=True)
            a = jnp.exp(lb + ((rts[hh][lo:] - lc) + _cum(lk, u_after)))
            if mask is not None:
                a = jnp.where(mask, a, 0.0)
            de = a * _nt(dos[hh][lo:], vh)
            g = gc + _cum(de, u_before)
            dz = de - jnp.exp(lb) * (de + g)
            if mask is not None:
                dz = jnp.where(mask, dz, 0.0)
            dzb = dz.astype(BF16)
            dkt = jnp.dot(q_t[heads[hh], lo:], dzb, preferred_element_type=F32)
            if scale is not None:
                dkt = dkt * scale
            dkt_scr[j, heads[hh], :] += dkt
            dvt_scr[j, heads[hh], :] += jnp.dot(do_t[heads[hh], lo:], a.astype(BF16), preferred_element_type=F32)
            return (lc, gc + jnp.sum(de, axis=-1, keepdims=True),
                    dq_acc + jnp.dot(dzb, kh, preferred_element_type=F32))

        zero = jnp.zeros((tq, 1), F32)
        visited = jnp.clip(jnp.max(rt_ref[0, 0, :, hpb:hpb + 1]).astype(jnp.int32), 0, r * i)
        first = r * i - visited

        def step(jj, states):
            return tuple(block(hh, first + jj, None, states[hh]) for hh in range(hpb))

        states = lax.fori_loop(0, visited, step, ((zero, zero, jnp.zeros((tq, dh), F32)),) * hpb)
        for d in range(r):
            lo = d * tk
            states = [_put_rows(states[hh], block(hh, r * i + d, masks[d], _rows(states[hh], lo), lo), lo)
                      for hh in range(hpb)]
        for hh in range(hpb):
            dq_ref[:, heads[hh]] = (states[hh][2] * dh ** -0.5).astype(BF16)

        @pl.when(i == nq - 1)
        def _():
            _store_transposed(dk_ref, dkt_scr, tk)
            _store_transposed(dv_ref, dvt_scr, tk)

    return pl.pallas_call(
        body, name=name, grid=(nb, di // ATTN_LANES, nq),
        in_specs=[q_spec, k_spec, v_spec, q_spec, pl.BlockSpec((1, 1, tq, hpb + 1), lambda b, hp, i: (b, hp, i, 0))],
        out_specs=[q_spec, kv_out, kv_out],
        out_shape=[jax.ShapeDtypeStruct((t, di), BF16)] * 3,
        scratch_shapes=[pltpu.VMEM((seq // tk, ATTN_LANES, tk), F32), pltpu.VMEM((seq // tk, ATTN_LANES, tk), F32)],
        compiler_params=_cp("parallel", "parallel", "arbitrary"),
    )(qkv, qkv, qkv, do, rtot)


def _cols(g):
    return jnp.transpose(g, (1, 0, 2)).reshape(g.shape[1], NDEV * g.shape[2])


def _col_blocks(w):
    r, c8 = w.shape
    return jnp.transpose(w.reshape(r, NDEV, c8 // NDEV), (1, 0, 2))


def _pad_rows16(a):
    return jnp.pad(a, ((0, 16 - a.shape[0]), (0, 0)))


def _pad_cols(a, n):
    return jnp.pad(a, ((0, 0), (0, n - a.shape[1])))


def kernel(x, c, fox_norm_g, fox_w_ada, fox_b_ada, fox_w_in, fox_b_f, fox_w_out, sb_norm_g, sb_w_ada, sb_b_ada, sb_w_in, sb_w_out, final_norm_g, loss_target, m_fox_norm_g, m_fox_w_ada, m_fox_b_ada, m_fox_w_in, m_fox_b_f, m_fox_w_out, m_sb_norm_g, m_sb_w_ada, m_sb_b_ada, m_sb_w_in, m_sb_w_out, m_final_norm_g, v_fox_norm_g, v_fox_w_ada, v_fox_b_ada, v_fox_w_in, v_fox_b_f, v_fox_w_out, v_sb_norm_g, v_sb_w_ada, v_sb_b_ada, v_sb_w_in, v_sb_w_out, v_final_norm_g):
    nb, seq, d = x.shape
    t = nb * seq
    h = fox_b_f.shape[-1]
    di = fox_w_out.shape[1] * NDEV
    dh = di // h
    tq = _tile(seq, 256)
    me = _my_index()

    gathered = all_gather_by_chip([w[0].astype(BF16) for w in (fox_w_ada, fox_w_in, fox_w_out)], "gather_fox_weights")
    fox_wada, fox_win = _cols(gathered[0]), _cols(gathered[1])
    fox_wout = gathered[2].reshape(di, d)
    sb_gather = Exchange("gather", [w[0].astype(BF16) for w in (sb_w_ada, sb_w_in, sb_w_out)] + [sb_norm_g, sb_b_ada])

    x0 = x.reshape(t, d)
    target = loss_target.reshape(t, d)
    c16 = _pad_rows16(c)

    def layer_fwd(xin, g, wada, bada, win, wout, b_f, tag, rider=None):
        mod = adaln_fwd(c16, wada, bada, tag + "_adaln")[:nb]
        shift, scale, gate = (mod[:, k * d:(k + 1) * d].reshape(nb, 1, d) for k in range(3))
        qkv, hmod = normmod_matmul(xin, g, scale, shift, win, BF16, seq, True, tag + "_qkv", (0, 3 * di))
        if b_f is not None:
            (zf,) = normmod_matmul(xin, g, scale, shift, _pad_cols(win[:, 3 * di:], di + 128), F32, seq, False, tag + "_z")
        else:
            (zf,) = normmod_matmul(xin, g, scale, shift, win, F32, seq, False, tag + "_z", (3 * di, di))
        saved = dict(x=xin, g=g, scale=scale, gate=gate, qkv=qkv, h=hmod, zf=zf, win=win, wout=wout)
        if b_f is not None:
            fl_t = jnp.transpose(zf[:, di:di + h].reshape(nb, seq, h), (0, 2, 1))
            bf_col = b_f.reshape(h, 1)
            cum = forget_cumsum(fl_t, bf_col, tag + "_cum")
            ck = cum.reshape(nb, h, seq // tq, tq)
            o, lse, *rode = fox_fwd(qkv, ck, seq, di, dh, tag + "_attn", rider)
            saved.update(fl_t=fl_t, bf_col=bf_col, ck=ck, lse=lse, rode=rode)
        else:
            o, rtot = sb_fwd(qkv, seq, di, dh, tag + "_attn")
            saved.update(rtot=rtot)
        xout, y, u = out_proj_fwd(o, zf, wout, xin, gate, seq, tag + "_out")
        saved.update(o=o, y=y, u=u)
        return xout, saved

    def layer_bwd(dxo, sv, is_fox, tag, riding=()):
        do, dz, dy, dgate = out_proj_bwd(dxo, sv["y"], sv["gate"], sv["wout"], sv["o"], sv["zf"], seq, tag + "_dout")
        wout_blocks = tn_matmul(sv["u"], dy, tag + "_dwout").reshape(NDEV, di // NDEV, d)
        extra = None
        if is_fox:
            dq, dk, dv, dck, *rode = fox_bwd(sv["qkv"], sv["ck"], sv["o"], do, sv["lse"], seq, di, dh, tag + "_dattn",
                                             Exchange("scatter", list(riding) + [wout_blocks]))
            df_t, dbf = forget_cumsum_bwd(dck.reshape(nb, h, seq), sv["fl_t"], sv["bf_col"], tag + "_dcum")
            df = _pad_cols(jnp.transpose(df_t, (0, 2, 1)).reshape(t, h), 128).astype(BF16)
            extra = (df, jnp.transpose(_pad_cols(sv["win"][:, 4 * di:], 128)))
        else:
            dq, dk, dv = sb_bwd(sv["qkv"], do, sv["rtot"], seq, di, dh, tag + "_dattn")
        parts = [dq, dk, dv, dz]
        dwin = [tn_matmul(sv["h"], p, tag + "_dwin%d" % k) for k, p in enumerate(parts)]
        if is_fox:
            dwin.append(tn_matmul(sv["h"], df, tag + "_dwinf")[:, :h])
        dwin = jnp.concatenate(dwin, axis=1)
        w_t = jnp.transpose(sv["win"][:, :4 * di])
        win_blocks = _col_blocks(dwin)
        dh_rider = Exchange("scatter", [win_blocks]) if is_fox else None
        dxin, psum, ssum, *landed = dh_norm_bwd(parts, w_t, extra, sv["x"], sv["g"], sv["scale"], dxo, seq,
                                                tag + "_dh", dh_rider)
        pad = lambda a: _pad_rows16(a.reshape(nb, d))
        dwada, dbada, dng = adaln_bwd(c16, pad(ssum), pad(psum), pad(dgate), pad(sv["scale"]), sv["g"], tag + "_dadaln")
        grads = dict(wada=dwada, bada=dbada, ng=dng, win=win_blocks, wout=wout_blocks)
        if is_fox:
            grads["bf"] = dbf.reshape(1, h)
            grads["rode"] = rode
            grads["landed"] = landed
        return dxin, grads

    x1, sv_fox = layer_fwd(x0, fox_norm_g, fox_wada, fox_b_ada, fox_win, fox_wout, fox_b_f, "fox", sb_gather)
    sb_wada, sb_win = _cols(sv_fox["rode"][0]), _cols(sv_fox["rode"][1])
    sb_wout = sv_fox["rode"][2].reshape(di, d)
    sb_g, sb_bada = _cols(sv_fox["rode"][3]), _cols(sv_fox["rode"][4])
    x2, sv_sb = layer_fwd(x1, sb_g, sb_wada, sb_bada, sb_win, sb_wout, None, "sb")
    dx2, dgf, sq = final_loss(x2, final_norm_g.reshape(1, d), target, "loss_head")
    dx1, g_sb = layer_bwd(dx2, sv_sb, False, "sb")
    dx0, g_fox = layer_bwd(dx1, sv_fox, True, "fox", [g_sb["wada"], g_sb["win"], g_sb["wout"]])

    sb_wada_land, sb_win_land, sb_wout_land, fox_wout_land = g_fox["rode"]
    (fox_wada_land,) = all_to_all([g_fox["wada"]], "scatter_fox_wada_grad")
    land = [fox_wada_land, g_fox["landed"][0], fox_wout_land, sb_wada_land, sb_win_land, sb_wout_land]
    small = jnp.concatenate([g_fox["ng"], g_fox["bada"], _pad_cols(g_fox["bf"], 128), g_sb["ng"], g_sb["bada"],
                             dgf, sq], axis=1)
    (small_all,) = all_gather([small], "gather_small")
    small_sum, loss_row = sum_slots(small_all, d, "sum_small")
    loss = loss_row[0, 0]

    offs = {}
    pos = 0
    for nm, width in (("fox_ng", d), ("fox_bada", 3 * d), ("fox_bf", 128), ("sb_ng", d), ("sb_bada", 3 * d), ("fin_g", d)):
        offs[nm] = (pos, width)
        pos += width

    def small_grad(nm, width=None, shard=False):
        p0, wd = offs[nm]
        wd = width or wd
        if shard:
            blk = wd // NDEV
            return lax.dynamic_slice(small_sum, (0, p0 + me * blk), (1, blk))
        return small_sum[:, p0:p0 + wd]

    results = {}
    big_params = [("fox_w_ada", fox_w_ada, m_fox_w_ada, v_fox_w_ada), ("fox_w_in", fox_w_in, m_fox_w_in, v_fox_w_in),
                  ("fox_w_out", fox_w_out, m_fox_w_out, v_fox_w_out), ("sb_w_ada", sb_w_ada, m_sb_w_ada, v_sb_w_ada),
                  ("sb_w_in", sb_w_in, m_sb_w_in, v_sb_w_in), ("sb_w_out", sb_w_out, m_sb_w_out, v_sb_w_out)]
    for k, (nm, w, m, v) in enumerate(big_params):
        results[nm] = adamw(land[k], w, m, v, "adamw_" + nm)
    small_params = [("fox_norm_g", fox_norm_g, m_fox_norm_g, v_fox_norm_g, small_grad("fox_ng")),
                    ("fox_b_ada", fox_b_ada, m_fox_b_ada, v_fox_b_ada, small_grad("fox_bada")),
                    ("fox_b_f", fox_b_f, m_fox_b_f, v_fox_b_f, small_grad("fox_bf", h)),
                    ("sb_norm_g", sb_norm_g, m_sb_norm_g, v_sb_norm_g, small_grad("sb_ng", shard=True)),
                    ("sb_b_ada", sb_b_ada, m_sb_b_ada, v_sb_b_ada, small_grad("sb_bada", shard=True)),
                    ("final_norm_g", final_norm_g.reshape(1, d), m_final_norm_g.reshape(1, d),
                     v_final_norm_g.reshape(1, d), small_grad("fin_g"))]
    for nm, w, m, v, g in small_params:
        outs = adamw(g[None], w, m, v, "adamw_" + nm)
        if nm == "final_norm_g":
            outs = [o.reshape(d) for o in outs]
        results[nm] = outs

    order = ["fox_norm_g", "fox_w_ada", "fox_b_ada", "fox_w_in", "fox_b_f", "fox_w_out", "sb_norm_g", "sb_w_ada",
             "sb_b_ada", "sb_w_in", "sb_w_out", "final_norm_g"]
    out = [loss, dx0.reshape(nb, seq, d)]
    for k in range(4):
        out += [results[nm][k] for nm in order]
    return tuple(out)
```

```python
import functools
import math

import jax
import jax.numpy as jnp
from jax import lax
from jax.experimental import pallas as pl
from jax.experimental.pallas import tpu as pltpu

F32 = jnp.float32
BF16 = jnp.bfloat16
NDEV = 8
VMEM_LIMIT = 56 * 1024 * 1024
NORM_EPS = 1e-6
ADAM_LR, ADAM_B1, ADAM_B2, ADAM_EPS, ADAM_WD, ADAM_STEP = 0.001, 0.9, 0.999, 1e-08, 0.01, 10
ATTN_LANES = 128
ATTN_BLOCKS_PER_TRIP = 2
EXP_IS_ZERO_BELOW = -104.0
MESH = pl.DeviceIdType.MESH
ANY = pl.BlockSpec(memory_space=pl.ANY)


def _cp(*sem):
    return pltpu.CompilerParams(dimension_semantics=sem, vmem_limit_bytes=VMEM_LIMIT)


def _my_index():
    return 4 * lax.axis_index("x") + 2 * lax.axis_index("y") + lax.axis_index("c")


def _flip(k):
    x, y, c = lax.axis_index("x"), lax.axis_index("y"), lax.axis_index("c")
    kx, ky, kc = (k >> 2) & 1, (k >> 1) & 1, k & 1
    px = 1 - x if kx else x
    py = 1 - y if ky else y
    pc = 1 - c if kc else c
    return (px, py, pc), 4 * px + 2 * py + pc


class Exchange:
    def __init__(self, mode, arrs):
        self.gather = mode == "gather"
        self.arrs = list(arrs)
        self.n = len(self.arrs)
        self.in_specs = [ANY] * self.n
        self.out_specs = [ANY] * self.n
        self.out_shape = [jax.ShapeDtypeStruct(((NDEV,) + a.shape) if self.gather else a.shape, a.dtype)
                          for a in self.arrs]
        self.scratch = [pltpu.SemaphoreType.DMA((self.n * (NDEV - 1),)), pltpu.SemaphoreType.DMA((self.n * (NDEV - 1),)),
                        pltpu.SemaphoreType.DMA((self.n,))]

    def copies(self, ins, outs, sems):
        send_sems, recv_sems, local_sems = sems
        me = _my_index()
        out = []
        for a in range(self.n):
            out.append(pltpu.make_async_copy(ins[a] if self.gather else ins[a].at[me], outs[a].at[me], local_sems.at[a]))
            for k in range(1, NDEV):
                peer, pidx = _flip(k)
                out.append(pltpu.make_async_remote_copy(
                    src_ref=ins[a] if self.gather else ins[a].at[pidx], dst_ref=outs[a].at[me],
                    send_sem=send_sems.at[a * (NDEV - 1) + k - 1], recv_sem=recv_sems.at[a * (NDEV - 1) + k - 1],
                    device_id=peer, device_id_type=MESH))
        return out

    def run(self, name):
        n = self.n

        def body(*refs):
            cps = self.copies(refs[:n], refs[n:2 * n], refs[2 * n:])
            for cp in cps:
                cp.start()
            for cp in cps:
                cp.wait()

        return pl.pallas_call(body, name=name, out_shape=self.out_shape, in_specs=self.in_specs,
                              out_specs=self.out_specs, scratch_shapes=self.scratch)(*self.arrs)


def _ride(exchange, refs, first, last):
    n = exchange.n
    cps = exchange.copies(refs[:n], refs[n:2 * n], refs[2 * n:])

    @pl.when(first)
    def _():
        for cp in cps:
            cp.start()

    @pl.when(last)
    def _():
        for cp in cps:
            cp.wait()


def all_gather(arrs, name):
    return Exchange("gather", arrs).run(name)


def all_gather_by_chip(arrs, name):
    n = len(arrs)
    chips = (2, 4, 6)

    def body(*refs):
        ins, outs = refs[:n], refs[n:2 * n]
        send_sems, recv_sems, local_sems = refs[2 * n:]
        me = _my_index()
        sibling, _ = _flip(1)

        def copy(a, k, block, to, src=None):
            return pltpu.make_async_remote_copy(
                src_ref=outs[a].at[block] if src is None else src, dst_ref=outs[a].at[block],
                send_sem=send_sems.at[7 * a + k], recv_sem=recv_sems.at[7 * a + k], device_id=to, device_id_type=MESH)

        own, sent = [], []
        for a in range(n):
            own.append(pltpu.make_async_copy(ins[a], outs[a].at[me], local_sems.at[a]))
            own[-1].start()
            first = [copy(a, 0, me, sibling, ins[a])] + [copy(a, 1 + j, me, _flip(k)[0], ins[a]) for j, k in enumerate(chips)]
            for cp in first:
                cp.start()
            sent += first
        for a in range(n):
            for j, k in enumerate(chips):
                peer, pidx = _flip(k)
                copy(a, 1 + j, pidx, peer).wait_recv()
                sent.append(copy(a, 4 + j, pidx, sibling))
                sent[-1].start()
        for a in range(n):
            copy(a, 0, _flip(1)[1], sibling).wait_recv()
            for j, k in enumerate(chips):
                copy(a, 4 + j, _flip(k + 1)[1], sibling).wait_recv()
        for cp in sent:
            cp.wait_send()
        for cp in own:
            cp.wait()

    return pl.pallas_call(
        body, name=name,
        out_shape=[jax.ShapeDtypeStruct((NDEV,) + a.shape, a.dtype) for a in arrs],
        in_specs=[ANY] * n, out_specs=[ANY] * n,
        scratch_shapes=[pltpu.SemaphoreType.DMA((7 * n,)), pltpu.SemaphoreType.DMA((7 * n,)),
                        pltpu.SemaphoreType.DMA((n,))],
    )(*arrs)


def all_to_all(arrs, name):
    return Exchange("scatter", arrs).run(name)


def _tile(n, pref):
    t = min(n, pref)
    while n % t:
        t //= 2
    return t


def _col_tile(n, pref):
    t = _tile(n, pref)
    return t if t % 128 == 0 and t >= 512 else n


def adamw(land, w, m, v, name):
    slots, r, c = land.shape
    tr = _tile(r, 64)
    bc1 = 1.0 - ADAM_B1 ** ADAM_STEP
    bc2 = 1.0 - ADAM_B2 ** ADAM_STEP
    lead = w.ndim - 2

    def body(land_ref, w_ref, m_ref, v_ref, g_ref, d_ref, nm_ref, nv_ref):
        at = (0,) * lead + (Ellipsis,)
        g = land_ref[0].astype(F32)
        for s in range(1, slots):
            g = g + land_ref[s].astype(F32)
        nm = ADAM_B1 * m_ref[at] + (1.0 - ADAM_B1) * g
        nv = ADAM_B2 * v_ref[at] + (1.0 - ADAM_B2) * (g * g)
        m_hat = nm / bc1
        v_hat = nv / bc2
        g_ref[at] = g
        nm_ref[at] = nm
        nv_ref[at] = nv
        d_ref[at] = -ADAM_LR * (m_hat / (jnp.sqrt(v_hat) + ADAM_EPS) + ADAM_WD * w_ref[at])

    blk = pl.BlockSpec((1,) * lead + (tr, c), lambda i: (0,) * lead + (i, 0))
    return pl.pallas_call(
        body, name=name, grid=(r // tr,),
        in_specs=[pl.BlockSpec((slots, tr, c), lambda i: (0, i, 0)), blk, blk, blk],
        out_specs=[blk] * 4,
        out_shape=[jax.ShapeDtypeStruct(w.shape, F32)] * 4,
        compiler_params=_cp("parallel"),
    )(land, w, m, v)


def sum_slots(land, d_model, name):
    slots, _, n = land.shape

    def body(land_ref, o_ref, loss_ref):
        g = land_ref[0]
        for s in range(1, slots):
            g = g + land_ref[s]
        o_ref[...] = g
        sq = jnp.sum(g[:, n - d_model:], axis=-1, keepdims=True)
        loss_ref[...] = jnp.broadcast_to(sq * (0.5 / d_model), (1, 128))

    return pl.pallas_call(
        body, name=name,
        out_shape=[jax.ShapeDtypeStruct((1, n), F32), jax.ShapeDtypeStruct((1, 128), F32)],
    )(land)


def _sigmoid(x):
    return 1.0 / (1.0 + jnp.exp(-x))


def adaln_fwd(c16, w_ada, b_ada, name):
    d3 = w_ada.shape[1]

    def body(c_ref, w_ref, b_ref, o_ref):
        cc = c_ref[...]
        sc = (cc * _sigmoid(cc)).astype(BF16)
        o_ref[...] = jnp.dot(sc, w_ref[...], preferred_element_type=F32) + b_ref[...]

    return pl.pallas_call(body, name=name, out_shape=jax.ShapeDtypeStruct((16, d3), F32),
                          compiler_params=pltpu.CompilerParams(vmem_limit_bytes=VMEM_LIMIT))(c16, w_ada, b_ada)


def adaln_bwd(c16, dshift16, p16, dgate16, scale16, g, name):
    d = c16.shape[1]
    nb = 3 * d // NDEV

    def body(c_ref, ds_ref, p_ref, dg_ref, sc_ref, g_ref, dw_ref, db_ref, dng_ref, dmod_scr):
        j = pl.program_id(0)

        @pl.when(j == 0)
        def _():
            p = p_ref[...]
            dmod = jnp.concatenate([ds_ref[...], p * g_ref[...], dg_ref[...]], axis=-1)
            dmod_scr[...] = dmod
            db_ref[...] = jnp.sum(dmod, axis=0, keepdims=True)
            dng_ref[...] = jnp.sum((1.0 + sc_ref[...]) * p, axis=0, keepdims=True)

        cc = c_ref[...]
        sc = (cc * _sigmoid(cc)).astype(BF16)
        start = pl.multiple_of(j * nb, 128)
        dm = dmod_scr[:, pl.ds(start, nb)].astype(BF16)
        dw_ref[0] = lax.dot_general(sc, dm, (((0,), (0,)), ((), ())), preferred_element_type=F32).astype(BF16)

    full = lambda shape: pl.BlockSpec(shape, lambda j: (0,) * len(shape))
    return pl.pallas_call(
        body, name=name, grid=(NDEV,),
        in_specs=[full((16, d))] * 5 + [full((1, d))],
        out_specs=[pl.BlockSpec((1, d, nb), lambda j: (j, 0, 0)), full((1, 3 * d)), full((1, d))],
        out_shape=[jax.ShapeDtypeStruct((NDEV, d, nb), BF16), jax.ShapeDtypeStruct((1, 3 * d), F32),
                   jax.ShapeDtypeStruct((1, d), F32)],
        scratch_shapes=[pltpu.VMEM((16, 3 * d), F32)],
        compiler_params=_cp("arbitrary"),
    )(c16, dshift16, p16, dgate16, scale16, g)


def _modulated_norm(x, g, scale, shift):
    rstd = lax.rsqrt(jnp.mean(x * x, axis=-1, keepdims=True) + NORM_EPS)
    return ((x * rstd) * g) * (1.0 + scale) + shift


def normmod_matmul(x, g, scale, shift, w, out_dtype, seq, emit_h, name, cols=None):
    t, d = x.shape
    col0, n = cols or (0, w.shape[1])
    tm = _tile(seq, 512)
    tn = _col_tile(n, 2048)
    assert col0 % tn == 0
    jb = col0 // tn
    per_seq = seq // tm

    def body(x_ref, g_ref, sc_ref, sh_ref, w_ref, *rest):
        if emit_h:
            o_ref, h_ref, h_scr = rest
        else:
            o_ref, h_scr = rest

        @pl.when(pl.program_id(1) == 0)
        def _():
            h = _modulated_norm(x_ref[...], g_ref[...], sc_ref[0], sh_ref[0]).astype(BF16)
            h_scr[...] = h
            if emit_h:
                h_ref[...] = h

        o_ref[...] = jnp.dot(h_scr[...], w_ref[...], preferred_element_type=F32).astype(out_dtype)

    mod_spec = pl.BlockSpec((1, 1, d), lambda i, j: (i // per_seq, 0, 0))
    out_specs = [pl.BlockSpec((tm, tn), lambda i, j: (i, j))]
    out_shape = [jax.ShapeDtypeStruct((t, n), out_dtype)]
    if emit_h:
        out_specs.append(pl.BlockSpec((tm, d), lambda i, j: (i, 0)))
        out_shape.append(jax.ShapeDtypeStruct((t, d), BF16))
    return pl.pallas_call(
        body, name=name, grid=(t // tm, n // tn),
        in_specs=[pl.BlockSpec((tm, d), lambda i, j: (i, 0)), pl.BlockSpec((1, d), lambda i, j: (0, 0)),
                  mod_spec, mod_spec, pl.BlockSpec((d, tn), lambda i, j: (0, jb + j))],
        out_specs=out_specs, out_shape=out_shape,
        scratch_shapes=[pltpu.VMEM((tm, d), BF16)],
        compiler_params=_cp("parallel", "arbitrary"),
    )(x, g, scale, shift, w)


def out_proj_fwd(o, zf, w_out, x, gate, seq, name):
    t, di = o.shape
    d = x.shape[1]
    tm = _tile(seq, 256)
    per_seq = seq // tm

    def body(o_ref, z_ref, w_ref, x_ref, gt_ref, xn_ref, y_ref, u_ref):
        z = z_ref[...]
        u = (o_ref[...] * (z * _sigmoid(z))).astype(BF16)
        y = jnp.dot(u, w_ref[...], preferred_element_type=F32)
        u_ref[...] = u
        y_ref[...] = y
        xn_ref[...] = x_ref[...] + gt_ref[0] * y

    row = lambda c: pl.BlockSpec((tm, c), lambda i: (i, 0))
    return pl.pallas_call(
        body, name=name, grid=(t // tm,),
        in_specs=[row(di), row(di), pl.BlockSpec((di, d), lambda i: (0, 0)), row(d),
                  pl.BlockSpec((1, 1, d), lambda i: (i // per_seq, 0, 0))],
        out_specs=[row(d), row(d), row(di)],
        out_shape=[jax.ShapeDtypeStruct((t, d), F32), jax.ShapeDtypeStruct((t, d), F32),
                   jax.ShapeDtypeStruct((t, di), BF16)],
        compiler_params=_cp("parallel"),
    )(o, zf, w_out, x, gate)


def out_proj_bwd(dxo, y, gate, w_out, o, zf, seq, name):
    t, d = dxo.shape
    di = o.shape[1]
    tm = _tile(seq, 256)
    per_seq = seq // tm

    def body(dx_ref, y_ref, gt_ref, w_ref, o_ref, z_ref, do_ref, dz_ref, dy_ref, dgt_ref):
        dx = dx_ref[...]
        part = jnp.sum(dx * y_ref[...], axis=0, keepdims=True)

        @pl.when(pl.program_id(0) % per_seq == 0)
        def _():
            dgt_ref[0] = part

        @pl.when(pl.program_id(0) % per_seq != 0)
        def _():
            dgt_ref[0] += part

        dy = (dx * gt_ref[0]).astype(BF16)
        dy_ref[...] = dy
        du = lax.dot_general(dy, w_ref[...], (((1,), (1,)), ((), ())), preferred_element_type=F32)
        z = z_ref[...]
        sg = _sigmoid(z)
        do_ref[...] = (du * (z * sg)).astype(BF16)
        dz_ref[...] = (du * o_ref[...] * (sg * (1.0 + z * (1.0 - sg)))).astype(BF16)

    row = lambda c: pl.BlockSpec((tm, c), lambda i: (i, 0))
    seq_spec = pl.BlockSpec((1, 1, d), lambda i: (i // per_seq, 0, 0))
    return pl.pallas_call(
        body, name=name, grid=(t // tm,),
        in_specs=[row(d), row(d), seq_spec, pl.BlockSpec((di, d), lambda i: (0, 0)), row(di), row(di)],
        out_specs=[row(di), row(di), row(d), seq_spec],
        out_shape=[jax.ShapeDtypeStruct((t, di), BF16), jax.ShapeDtypeStruct((t, di), BF16),
                   jax.ShapeDtypeStruct((t, d), BF16), jax.ShapeDtypeStruct((t // seq, 1, d), F32)],
        compiler_params=_cp("arbitrary"),
    )(dxo, y, gate, w_out, o, zf)


def tn_matmul(a, b, name):
    t, m = a.shape
    n = b.shape[1]
    tn = _col_tile(n, 1024)
    tk = _tile(t, 1024)
    nk = t // tk

    def body(a_ref, b_ref, o_ref, acc):
        part = lax.dot_general(a_ref[...], b_ref[...], (((0,), (0,)), ((), ())), preferred_element_type=F32)
        k = pl.program_id(1)

        @pl.when(k == 0)
        def _():
            acc[...] = part

        @pl.when(k != 0)
        def _():
            acc[...] += part

        @pl.when(k == nk - 1)
        def _():
            o_ref[...] = acc[...].astype(BF16)

    return pl.pallas_call(
        body, name=name, grid=(n // tn, nk),
        in_specs=[pl.BlockSpec((tk, m), lambda j, k: (k, 0)), pl.BlockSpec((tk, tn), lambda j, k: (k, j))],
        out_specs=pl.BlockSpec((m, tn), lambda j, k: (0, j)),
        out_shape=jax.ShapeDtypeStruct((m, n), BF16),
        scratch_shapes=[pltpu.VMEM((m, tn), F32)],
        compiler_params=_cp("parallel", "arbitrary"),
    )(a, b)


def dh_norm_bwd(parts, w_t, extra, x, g, scale, dxo, seq, name, rider=None):
    t, d = x.shape
    nparts = len(parts)
    kw = parts[0].shape[1]
    tm = _tile(seq, 512)
    per_seq = seq // tm
    has_extra = extra is not None
    grid = (t // tm,)
    nr = rider.n if rider else 0
    n_in = nparts + 1 + (2 if has_extra else 0) + 4

    def body(*refs):
        p_refs = refs[:nparts]
        w_ref = refs[nparts]
        pos = nparts + 1
        if has_extra:
            e_ref, we_ref = refs[pos], refs[pos + 1]
            pos += 2
        x_ref, g_ref, sc_ref, dxo_ref = refs[pos:pos + 4]
        dx_ref, pp_ref, ss_ref = refs[n_in + nr:n_in + nr + 3]
        if rider:
            _ride(rider, refs[n_in:n_in + nr] + refs[n_in + nr + 3:], *_grid_ends(grid))
        i = pl.program_id(0)

        dh = jnp.dot(p_refs[0][...], w_ref[0:kw, :], preferred_element_type=F32)
        for kk in range(1, nparts):
            dh = dh + jnp.dot(p_refs[kk][...], w_ref[kk * kw:(kk + 1) * kw, :], preferred_element_type=F32)
        if has_extra:
            dh = dh + jnp.dot(e_ref[...], we_ref[...], preferred_element_type=F32)
        xx = x_ref[...]
        rstd = lax.rsqrt(jnp.mean(xx * xx, axis=-1, keepdims=True) + NORM_EPS)
        xhat = xx * rstd
        dxhat = dh * (g_ref[...] * (1.0 + sc_ref[0]))
        dx_ref[...] = dxo_ref[...] + rstd * (dxhat - xhat * jnp.mean(dxhat * xhat, axis=-1, keepdims=True))
        pp = jnp.sum(dh * xhat, axis=0, keepdims=True)
        ss = jnp.sum(dh, axis=0, keepdims=True)

        @pl.when(i % per_seq == 0)
        def _():
            pp_ref[0] = pp
            ss_ref[0] = ss

        @pl.when(i % per_seq != 0)
        def _():
            pp_ref[0] += pp
            ss_ref[0] += ss

    row = lambda c: pl.BlockSpec((tm, c), lambda i: (i, 0))
    whole = lambda a: pl.BlockSpec(a.shape, lambda i: (0, 0), pipeline_mode=pl.Buffered(1))
    seq_spec = pl.BlockSpec((1, 1, d), lambda i: (i // per_seq, 0, 0))
    in_specs = [row(kw)] * nparts + [whole(w_t)]
    args = list(parts) + [w_t]
    if has_extra:
        in_specs += [row(extra[0].shape[1]), whole(extra[1])]
        args += list(extra)
    in_specs += [row(d), pl.BlockSpec((1, d), lambda i: (0, 0)), seq_spec, row(d)]
    args += [x, g, scale, dxo]
    return pl.pallas_call(
        body, name=name, grid=grid,
        in_specs=in_specs + (rider.in_specs if rider else []),
        out_specs=[row(d), seq_spec, seq_spec] + (rider.out_specs if rider else []),
        out_shape=[jax.ShapeDtypeStruct((t, d), F32), jax.ShapeDtypeStruct((t // seq, 1, d), F32),
                   jax.ShapeDtypeStruct((t // seq, 1, d), F32)] + (rider.out_shape if rider else []),
        scratch_shapes=rider.scratch if rider else [],
        compiler_params=_cp("arbitrary"),
    )(*args, *(rider.arrs if rider else []))


def final_loss(x, g, target, name):
    t, d = x.shape
    tm = _tile(t, 512)

    def body(x_ref, g_ref, t_ref, dx_ref, dg_ref, sq_ref):
        xx = x_ref[...]
        gg = g_ref[...]
        rstd = lax.rsqrt(jnp.mean(xx * xx, axis=-1, keepdims=True) + NORM_EPS)
        xhat = xx * rstd
        err = xhat * gg - t_ref[...]
        dy = err * (1.0 / d)
        dxhat = dy * gg
        dx_ref[...] = rstd * (dxhat - xhat * jnp.mean(dxhat * xhat, axis=-1, keepdims=True))
        dg = jnp.sum(dy * xhat, axis=0, keepdims=True)
        sq = jnp.sum(err * err, axis=0, keepdims=True)

        @pl.when(pl.program_id(0) == 0)
        def _():
            dg_ref[...] = dg
            sq_ref[...] = sq

        @pl.when(pl.program_id(0) != 0)
        def _():
            dg_ref[...] += dg
            sq_ref[...] += sq

    row = pl.BlockSpec((tm, d), lambda i: (i, 0))
    vec = pl.BlockSpec((1, d), lambda i: (0, 0))
    return pl.pallas_call(
        body, name=name, grid=(t // tm,),
        in_specs=[row, vec, row], out_specs=[row, vec, vec],
        out_shape=[jax.ShapeDtypeStruct((t, d), F32), jax.ShapeDtypeStruct((1, d), F32),
                   jax.ShapeDtypeStruct((1, d), F32)],
        compiler_params=_cp("arbitrary"),
    )(x, g, target)


def _split3(x):
    x1 = x.astype(BF16)
    r1 = x - x1.astype(F32)
    x2 = r1.astype(BF16)
    x3 = (r1 - x2.astype(F32)).astype(BF16)
    return x1, x2, x3


def _dot3(x, u):
    x1, x2, x3 = _split3(x)
    return (jnp.dot(x1, u, preferred_element_type=F32) + jnp.dot(x2, u, preferred_element_type=F32)
            + jnp.dot(x3, u, preferred_element_type=F32))


def _log_sigmoid(x):
    return jnp.minimum(x, 0.0) - jnp.log1p(jnp.exp(-jnp.abs(x)))


def forget_cumsum(fl_t, b_f, name):
    b, h, s = fl_t.shape

    def body(f_ref, b_ref, o_ref):
        lf = _log_sigmoid(f_ref[0] + b_ref[...])
        u = (lax.broadcasted_iota(jnp.int32, (s, s), 0) <= lax.broadcasted_iota(jnp.int32, (s, s), 1)).astype(BF16)
        o_ref[0] = _dot3(lf, u)

    return pl.pallas_call(
        body, name=name, grid=(b,),
        in_specs=[pl.BlockSpec((1, h, s), lambda i: (i, 0, 0)), pl.BlockSpec((h, 1), lambda i: (0, 0))],
        out_specs=pl.BlockSpec((1, h, s), lambda i: (i, 0, 0)),
        out_shape=jax.ShapeDtypeStruct((b, h, s), F32),
        compiler_params=_cp("parallel"),
    )(fl_t, b_f)


def forget_cumsum_bwd(dcum, fl_t, b_f, name):
    b, h, s = fl_t.shape

    def body(d_ref, f_ref, b_ref, o_ref, db_ref):
        u = (lax.broadcasted_iota(jnp.int32, (s, s), 0) >= lax.broadcasted_iota(jnp.int32, (s, s), 1)).astype(BF16)
        dlf = _dot3(d_ref[0], u)
        df = dlf * _sigmoid(-(f_ref[0] + b_ref[...]))
        o_ref[0] = df
        part = jnp.sum(df, axis=-1, keepdims=True)

        @pl.when(pl.program_id(0) == 0)
        def _():
            db_ref[...] = part

        @pl.when(pl.program_id(0) != 0)
        def _():
            db_ref[...] += part

    blk = pl.BlockSpec((1, h, s), lambda i: (i, 0, 0))
    return pl.pallas_call(
        body, name=name, grid=(b,),
        in_specs=[blk, blk, pl.BlockSpec((h, 1), lambda i: (0, 0))],
        out_specs=[blk, pl.BlockSpec((h, 1), lambda i: (0, 0))],
        out_shape=[jax.ShapeDtypeStruct((b, h, s), F32), jax.ShapeDtypeStruct((h, 1), F32)],
        compiler_params=_cp("arbitrary"),
    )(dcum, fl_t, b_f)


def _nt(a, b):
    return lax.dot_general(a, b, (((1,), (1,)), ((), ())), preferred_element_type=F32)


def _attn_dims(t, seq, dh):
    tq = _tile(seq, 512)
    tk = _tile(seq, 256)
    return t // seq, tq, tk, seq // tq, tq // tk, ATTN_LANES // dh


def _blocks_per_trip(r):
    return ATTN_BLOCKS_PER_TRIP if r % ATTN_BLOCKS_PER_TRIP == 0 else 1


def _scaled_q(q_ref, dh):
    scale = dh ** -0.5
    if math.log2(dh) % 2 == 0:
        return (q_ref[...].astype(F32) * scale).astype(BF16), None
    return q_ref[...], scale


def _diag_masks(tq, tk, r, strict):
    row = lax.broadcasted_iota(jnp.int32, (tq, tk), 0)
    col = lax.broadcasted_iota(jnp.int32, (tq, tk), 1)
    return [(col + d * tk < row) if strict else (col + d * tk <= row) for d in range(r)]


def _rows(state, lo):
    return tuple(a[lo:] for a in state)


def _put_rows(state, part, lo):
    if lo == 0:
        return tuple(part)
    return tuple(jnp.concatenate([a[:lo], p], axis=0) for a, p in zip(state, part))


def _transposed(x):
    return jnp.transpose(x.astype(F32)).astype(BF16)


def _store_transposed(dst_ref, src_scr, tk):
    for jb in range(src_scr.shape[0]):
        dst_ref[jb * tk:(jb + 1) * tk, :] = jnp.transpose(src_scr[jb]).astype(BF16)


def _dot2(x, u):
    hi = x.astype(BF16)
    lo = (x - hi.astype(F32)).astype(BF16)
    return jnp.dot(jnp.concatenate([hi, lo], axis=1), jnp.concatenate([u, u], axis=0), preferred_element_type=F32)


def _attn_specs(seq, di, tq):
    nq = seq // tq
    cb = di // ATTN_LANES
    q_spec = pl.BlockSpec((tq, ATTN_LANES), lambda b, hp, i: (b * nq + i, hp))
    k_spec = pl.BlockSpec((seq, ATTN_LANES), lambda b, hp, i: (b, cb + hp))
    v_spec = pl.BlockSpec((seq, ATTN_LANES), lambda b, hp, i: (b, 2 * cb + hp))
    kv_out = pl.BlockSpec((seq, ATTN_LANES), lambda b, hp, i: (b, hp))
    return q_spec, k_spec, v_spec, kv_out


def _grid_ends(grid):
    ids = [pl.program_id(a) for a in range(len(grid))]
    first = functools.reduce(jnp.logical_and, [p == 0 for p in ids])
    last = functools.reduce(jnp.logical_and, [p == g - 1 for p, g in zip(ids, grid)])
    return first, last


def fox_fwd(qkv, ck, seq, di, dh, name, rider=None):
    t = qkv.shape[0]
    nb, tq, tk, nq, r, hpb = _attn_dims(t, seq, dh)
    per_trip = _blocks_per_trip(r)
    heads = [slice(hh * dh, (hh + 1) * dh) for hh in range(hpb)]
    assert hpb >= 2, "the row sums of p ride in another head's lanes of the p @ v matmul"
    q_spec, k_spec, v_spec, _ = _attn_specs(seq, di, tq)
    grid = (nb, di // ATTN_LANES, nq)
    nr = rider.n if rider else 0

    def body(q_ref, k_ref, v_ref, ck_ref, *rest):
        o_ref, lse_ref = rest[nr:nr + 2]
        knorm_ref = rest[2 * nr + 2]
        if rider:
            _ride(rider, rest[:nr] + rest[nr + 2:2 * nr + 2] + rest[2 * nr + 3:], *_grid_ends(grid))
        i = pl.program_id(2)
        masks = _diag_masks(tq, tk, r, False)
        q, scale = _scaled_q(q_ref, dh)
        qs = [q[:, hd] for hd in heads]
        lane_head = lax.broadcasted_iota(jnp.int32, (tk, ATTN_LANES), 1) // dh

        def largest_norm(x):
            x = x.astype(F32)
            return jnp.sqrt(jnp.max(jnp.sum(x * x, axis=-1, keepdims=True)))

        @pl.when(i == 0)
        def _():
            for hh in range(hpb):
                knorm_ref[hh] = largest_norm(k_ref[:, heads[hh]])

        qk_bound = [largest_norm(qs[hh]) * knorm_ref[hh] * (1.0 if scale is None else scale) + 1.0 for hh in range(hpb)]

        def scores(hh, j):
            start = pl.multiple_of(j * tk, tk)
            kh = k_ref[pl.ds(start, tk), heads[hh]]
            vt = v_ref[pl.ds(start, tk), :]
            vh = jnp.where(lane_head == hh, vt, jnp.ones_like(vt))
            s = _nt(qs[hh], kh)
            if scale is not None:
                s = s * scale
            return s - ck_ref[0, hh, pl.ds(j, 1), :], vh

        def update(state, s, vh):
            m, acc = state
            m_new = jnp.maximum(m, jnp.max(s, axis=-1, keepdims=True))
            p = jnp.exp(s - m_new)
            return m_new, jnp.exp(m - m_new) * acc + _dot2(p, vh)

        states = []
        for hh in range(hpb):
            s, vh = scores(hh, r * i)
            s = jnp.where(masks[0], s, -jnp.inf)
            m = jnp.max(s, axis=-1, keepdims=True)
            states.append((m, _dot2(jnp.exp(s - m), vh)))
        for d in range(1, r):
            for hh in range(hpb):
                s, vh = scores(hh, r * i + d)
                states[hh] = update(states[hh], jnp.where(masks[d], s, -jnp.inf), vh)

        def any_weight_left(states, j):
            jc = jnp.maximum(j, 0)
            tops = [jnp.max(-ck_ref[0, hh, pl.ds(jc, 1), :]) + qk_bound[hh] - jnp.min(states[hh][0]) for hh in range(hpb)]
            return (functools.reduce(jnp.maximum, tops) > EXP_IS_ZERO_BELOW).astype(jnp.int32)

        def step(carry):
            jj, _, states = carry
            for sub in range(per_trip):
                states = tuple(update(states[hh], *scores(hh, r * i - 1 - (per_trip * jj + sub))) for hh in range(hpb))
            return jj + 1, any_weight_left(states, r * i - 1 - per_trip * (jj + 1)), states

        trips, _, states = lax.while_loop(lambda c: jnp.logical_and(c[0] < r * i // per_trip, c[1] > 0), step,
                                          (jnp.int32(0), any_weight_left(states, r * i - 1), tuple(states)))
        for hh in range(hpb):
            m, acc = states[hh]
            other = heads[(hh + 1) % hpb].start
            l = acc[:, other:other + 1]
            o_ref[:, heads[hh]] = acc[:, heads[hh]] / l
            lse_ref[0, 0, :, hh:hh + 1] = m + jnp.log(l)
        lse_ref[0, 0, :, hpb:hpb + 1] = jnp.full((tq, 1), (trips * per_trip).astype(F32))

    return pl.pallas_call(
        body, name=name, grid=grid,
        in_specs=[q_spec, k_spec, v_spec,
                  pl.BlockSpec((1, hpb, seq // tk, tk), lambda b, hp, i: (b, hp, 0, 0))] + (rider.in_specs if rider else []),
        out_specs=[q_spec, pl.BlockSpec((1, 1, tq, hpb + 1), lambda b, hp, i: (b, hp, i, 0))]
        + (rider.out_specs if rider else []),
        out_shape=[jax.ShapeDtypeStruct((t, di), F32), jax.ShapeDtypeStruct((nb, di // ATTN_LANES, seq, hpb + 1), F32)]
        + (rider.out_shape if rider else []),
        scratch_shapes=[pltpu.SMEM((hpb,), F32)] + (rider.scratch if rider else []),
        compiler_params=_cp("arbitrary", "arbitrary", "arbitrary"),
    )(qkv, qkv, qkv, ck, *(rider.arrs if rider else []))


def fox_bwd(qkv, ck, o, do, lse, seq, di, dh, name, rider=None):
    t = qkv.shape[0]
    nb, tq, tk, nq, r, hpb = _attn_dims(t, seq, dh)
    per_trip = _blocks_per_trip(r)
    heads = [slice(hh * dh, (hh + 1) * dh) for hh in range(hpb)]
    q_spec, k_spec, v_spec, kv_out = _attn_specs(seq, di, tq)
    ck_spec = pl.BlockSpec((1, hpb, seq // tk, tk), lambda b, hp, i: (b, hp, 0, 0))
    grid = (nb, di // ATTN_LANES, nq)
    nr = rider.n if rider else 0

    def body(q_ref, k_ref, v_ref, ck_ref, o_ref, do_ref, lse_ref, *rest):
        dq_ref, dk_ref, dv_ref, dck_ref = rest[nr:nr + 4]
        dkt_scr, dvt_scr = rest[2 * nr + 4:2 * nr + 6]
        if rider:
            _ride(rider, rest[:nr] + rest[nr + 4:2 * nr + 4] + rest[2 * nr + 6:], *_grid_ends(grid))
        i = pl.program_id(2)

        @pl.when(i == 0)
        def _():
            dkt_scr[...] = jnp.zeros_like(dkt_scr)
            dvt_scr[...] = jnp.zeros_like(dvt_scr)
            dck_ref[...] = jnp.zeros_like(dck_ref)

        masks = _diag_masks(tq, tk, r, False)
        q, scale = _scaled_q(q_ref, dh)
        do = do_ref[...]
        q_t, do_t = _transposed(q), _transposed(do)
        qs = [q[:, hd] for hd in heads]
        dos = [do[:, hd] for hd in heads]
        deltas = [jnp.sum(dos[hh].astype(F32) * o_ref[:, heads[hh]], axis=-1, keepdims=True) for hh in range(hpb)]
        lses = [lse_ref[0, 0, :, hh:hh + 1] for hh in range(hpb)]

        def block(hh, j, mask, dq_acc, lo=0):
            start = pl.multiple_of(j * tk, tk)
            kh = k_ref[pl.ds(start, tk), heads[hh]]
            vh = v_ref[pl.ds(start, tk), heads[hh]]
            s = _nt(qs[hh][lo:], kh)
            if scale is not None:
                s = s * scale
            p = jnp.exp(s - ck_ref[0, hh, pl.ds(j, 1), :] - lses[hh][lo:])
            if mask is not None:
                p = jnp.where(mask[lo:], p, 0.0)
            ds = p * (_nt(dos[hh][lo:], vh) - deltas[hh][lo:])
            dsb = ds.astype(BF16)
            dkt = jnp.dot(q_t[heads[hh], lo:], dsb, preferred_element_type=F32)
            if scale is not None:
                dkt = dkt * scale
            dkt_scr[j, heads[hh], :] += dkt
            dvt_scr[j, heads[hh], :] += jnp.dot(do_t[heads[hh], lo:], p.astype(BF16), preferred_element_type=F32)
            dck_ref[0, hh, pl.ds(j, 1), :] -= jnp.sum(ds, axis=0, keepdims=True)
            return dq_acc + jnp.dot(dsb, kh, preferred_element_type=F32)

        accs = [jnp.zeros((tq, dh), F32)] * hpb
        for d in range(r):
            lo = d * tk
            accs = [_put_rows((accs[hh],), (block(hh, r * i + d, masks[d], accs[hh][lo:], lo),), lo)[0]
                    for hh in range(hpb)]

        trips = jnp.clip(jnp.max(lse_ref[0, 0, :, hpb:hpb + 1]).astype(jnp.int32), 0, r * i) // per_trip
        first = r * i - trips * per_trip

        def step(jj, accs):
            for sub in range(per_trip):
                accs = tuple(block(hh, first + per_trip * jj + sub, None, accs[hh]) for hh in range(hpb))
            return accs

        accs = lax.fori_loop(0, trips, step, tuple(accs))
        for hh in range(hpb):
            dq_ref[:, heads[hh]] = (accs[hh] * dh ** -0.5).astype(BF16)

        @pl.when(i == nq - 1)
        def _():
            _store_transposed(dk_ref, dkt_scr, tk)
            _store_transposed(dv_ref, dvt_scr, tk)

    return pl.pallas_call(
        body, name=name, grid=grid,
        in_specs=[q_spec, k_spec, v_spec, ck_spec, q_spec, q_spec,
                  pl.BlockSpec((1, 1, tq, hpb + 1), lambda b, hp, i: (b, hp, i, 0))] + (rider.in_specs if rider else []),
        out_specs=[q_spec, kv_out, kv_out, ck_spec] + (rider.out_specs if rider else []),
        out_shape=[jax.ShapeDtypeStruct((t, di), BF16)] * 3 + [jax.ShapeDtypeStruct(ck.shape, F32)]
        + (rider.out_shape if rider else []),
        scratch_shapes=[pltpu.VMEM((seq // tk, ATTN_LANES, tk), F32), pltpu.VMEM((seq // tk, ATTN_LANES, tk), F32)]
        + (rider.scratch if rider else []),
        compiler_params=_cp("arbitrary", "arbitrary", "arbitrary"),
    )(qkv, qkv, qkv, ck, o, do, lse, *(rider.arrs if rider else []))


def _sb_logits(qh, kh, scale, strict):
    z = _nt(qh, kh)
    if scale is not None:
        z = z * scale
    e = jnp.exp(-jnp.abs(z))
    lb = jnp.minimum(z, 0.0) - jnp.log(1.0 + e)
    lk = lb - z
    return lb, lk if strict is None else jnp.where(strict, lk, 0.0)


def _cum(x, u):
    return jnp.dot(x.astype(BF16), u, preferred_element_type=F32)


def sb_fwd(qkv, seq, di, dh, name):
    t = qkv.shape[0]
    nb, tq, tk, nq, r, hpb = _attn_dims(t, seq, dh)
    heads = [slice(hh * dh, (hh + 1) * dh) for hh in range(hpb)]
    q_spec, k_spec, v_spec, _ = _attn_specs(seq, di, tq)

    def body(q_ref, k_ref, v_ref, o_ref, rt_ref):
        i = pl.program_id(2)
        masks = _diag_masks(tq, tk, r, True)
        row = lax.broadcasted_iota(jnp.int32, (tk, tk), 0)
        col = lax.broadcasted_iota(jnp.int32, (tk, tk), 1)
        u_after = (row > col).astype(BF16)
        q, scale = _scaled_q(q_ref, dh)
        qs = [q[:, hd] for hd in heads]

        def block(hh, j, mask, state, lo=0):
            rr, acc = state
            start = pl.multiple_of(j * tk, tk)
            kh = k_ref[pl.ds(start, tk), heads[hh]]
            vh = v_ref[pl.ds(start, tk), heads[hh]]
            if mask is not None:
                mask = mask[lo:]
            lb, lk = _sb_logits(qs[hh][lo:], kh, scale, mask)
            a = jnp.exp(lb + _cum(lk, u_after) + rr)
            if mask is not None:
                a = jnp.where(mask, a, 0.0)
            acc = acc + jnp.dot(a.astype(BF16), vh, preferred_element_type=F32)
            return rr + jnp.sum(lk, axis=-1, keepdims=True), acc

        states = [(jnp.zeros((tq, 1), F32), jnp.zeros((tq, dh), F32))] * hpb
        for d in reversed(range(r)):
            lo = d * tk
            states = [_put_rows(states[hh], block(hh, r * i + d, masks[d], _rows(states[hh], lo), lo), lo)
                      for hh in range(hpb)]

        def any_weight_left(states):
            top = functools.reduce(jnp.maximum, [jnp.max(st[0]) for st in states])
            return (top > EXP_IS_ZERO_BELOW).astype(jnp.int32)

        def step(carry):
            jj, _, states = carry
            states = tuple(block(hh, r * i - 1 - jj, None, states[hh]) for hh in range(hpb))
            return jj + 1, any_weight_left(states), states

        visited, _, states = lax.while_loop(lambda c: jnp.logical_and(c[0] < r * i, c[1] > 0), step,
                                            (jnp.int32(0), any_weight_left(states), tuple(states)))
        for hh in range(hpb):
            o_ref[:, heads[hh]] = states[hh][1]
            rt_ref[0, 0, :, hh:hh + 1] = states[hh][0]
        rt_ref[0, 0, :, hpb:hpb + 1] = jnp.full((tq, 1), visited.astype(F32))

    return pl.pallas_call(
        body, name=name, grid=(nb, di // ATTN_LANES, nq),
        in_specs=[q_spec, k_spec, v_spec],
        out_specs=[q_spec, pl.BlockSpec((1, 1, tq, hpb + 1), lambda b, hp, i: (b, hp, i, 0))],
        out_shape=[jax.ShapeDtypeStruct((t, di), F32), jax.ShapeDtypeStruct((nb, di // ATTN_LANES, seq, hpb + 1), F32)],
        compiler_params=_cp("parallel", "parallel", "arbitrary"),
    )(qkv, qkv, qkv)


def sb_bwd(qkv, do, rtot, seq, di, dh, name):
    t = qkv.shape[0]
    nb, tq, tk, nq, r, hpb = _attn_dims(t, seq, dh)
    heads = [slice(hh * dh, (hh + 1) * dh) for hh in range(hpb)]
    q_spec, k_spec, v_spec, kv_out = _attn_specs(seq, di, tq)

    def body(q_ref, k_ref, v_ref, do_ref, rt_ref, dq_ref, dk_ref, dv_ref, dkt_scr, dvt_scr):
        i = pl.program_id(2)

        @pl.when(i == 0)
        def _():
            dkt_scr[...] = jnp.zeros_like(dkt_scr)
            dvt_scr[...] = jnp.zeros_like(dvt_scr)

        masks = _diag_masks(tq, tk, r, True)
        row = lax.broadcasted_iota(jnp.int32, (tk, tk), 0)
        col = lax.broadcasted_iota(jnp.int32, (tk, tk), 1)
        u_after = (row > col).astype(BF16)
        u_before = (row < col).astype(BF16)
        q, scale = _scaled_q(q_ref, dh)
        do = do_ref[...]
        q_t, do_t = _transposed(q), _transposed(do)
        qs = [q[:, hd] for hd in heads]
        dos = [do[:, hd] for hd in heads]
        rts = [rt_ref[0, 0, :, hh:hh + 1] for hh in range(hpb)]

        def block(hh, j, mask, state, lo=0):
            lc, gc, dq_acc = state
            start = pl.multiple_of(j * tk, tk)
            kh = k_ref[pl.ds(start, tk), heads[hh]]
            vh = v_ref[pl.ds(start, tk), heads[hh]]
            if mask is not None:
                mask = mask[lo:]
            lb, lk = _sb_logits(qs[hh][lo:], kh, scale, mask)
            lc = lc + jnp.sum(lk, axis=-1, keepdims=True)
            a = jnp.exp(lb + ((rts[hh][lo:] - lc) + _cum(lk, u_after)))
            if mask is not None:
                a = jnp.where(mask, a, 0.0)
            de = a * _nt(dos[hh][lo:], vh)
            g = gc + _cum(de, u_before)
            dz = de - jnp.exp(lb) * (de + g)
            if mask is not None:
                dz = jnp.where(mask, dz, 0.0)
            dzb = dz.astype(BF16)
            dkt = jnp.dot(q_t[heads[hh], lo:], dzb, preferred_element_type=F32)
            if scale is not None:
                dkt = dkt * scale
            dkt_scr[j, heads[hh], :] += dkt
            dvt_scr[j, heads[hh], :] += jnp.dot(do_t[heads[hh], lo:], a.astype(BF16), preferred_element_type=F32)
            return (lc, gc + jnp.sum(de, axis=-1, keepdims=True),
                    dq_acc + jnp.dot(dzb, kh, preferred_element_type=F32))

        zero = jnp.zeros((tq, 1), F32)
        visited = jnp.clip(jnp.max(rt_ref[0, 0, :, hpb:hpb + 1]).astype(jnp.int32), 0, r * i)
        first = r * i - visited

        def step(jj, states):
            return tuple(block(hh, first + jj, None, states[hh]) for hh in range(hpb))

        states = lax.fori_loop(0, visited, step, ((zero, zero, jnp.zeros((tq, dh), F32)),) * hpb)
        for d in range(r):
            lo = d * tk
            states = [_put_rows(states[hh], block(hh, r * i + d, masks[d], _rows(states[hh], lo), lo), lo)
                      for hh in range(hpb)]
        for hh in range(hpb):
            dq_ref[:, heads[hh]] = (states[hh][2] * dh ** -0.5).astype(BF16)

        @pl.when(i == nq - 1)
        def _():
            _store_transposed(dk_ref, dkt_scr, tk)
            _store_transposed(dv_ref, dvt_scr, tk)

    return pl.pallas_call(
        body, name=name, grid=(nb, di // ATTN_LANES, nq),
        in_specs=[q_spec, k_spec, v_spec, q_spec, pl.BlockSpec((1, 1, tq, hpb + 1), lambda b, hp, i: (b, hp, i, 0))],
        out_specs=[q_spec, kv_out, kv_out],
        out_shape=[jax.ShapeDtypeStruct((t, di), BF16)] * 3,
        scratch_shapes=[pltpu.VMEM((seq // tk, ATTN_LANES, tk), F32), pltpu.VMEM((seq // tk, ATTN_LANES, tk), F32)],
        compiler_params=_cp("parallel", "parallel", "arbitrary"),
    )(qkv, qkv, qkv, do, rtot)


def _cols(g):
    return jnp.transpose(g, (1, 0, 2)).reshape(g.shape[1], NDEV * g.shape[2])


def _col_blocks(w):
    r, c8 = w.shape
    return jnp.transpose(w.reshape(r, NDEV, c8 // NDEV), (1, 0, 2))


def _pad_rows16(a):
    return jnp.pad(a, ((0, 16 - a.shape[0]), (0, 0)))


def _pad_cols(a, n):
    return jnp.pad(a, ((0, 0), (0, n - a.shape[1])))


def kernel(x, c, fox_norm_g, fox_w_ada, fox_b_ada, fox_w_in, fox_b_f, fox_w_out, sb_norm_g, sb_w_ada, sb_b_ada, sb_w_in, sb_w_out, final_norm_g, loss_target, m_fox_norm_g, m_fox_w_ada, m_fox_b_ada, m_fox_w_in, m_fox_b_f, m_fox_w_out, m_sb_norm_g, m_sb_w_ada, m_sb_b_ada, m_sb_w_in, m_sb_w_out, m_final_norm_g, v_fox_norm_g, v_fox_w_ada, v_fox_b_ada, v_fox_w_in, v_fox_b_f, v_fox_w_out, v_sb_norm_g, v_sb_w_ada, v_sb_b_ada, v_sb_w_in, v_sb_w_out, v_final_norm_g):
    nb, seq, d = x.shape
    t = nb * seq
    h = fox_b_f.shape[-1]
    di = fox_w_out.shape[1] * NDEV
    dh = di // h
    tq = _tile(seq, 256)
    me = _my_index()

    gathered = all_gather_by_chip([w[0].astype(BF16) for w in (fox_w_ada, fox_w_in, fox_w_out)], "gather_fox_weights")
    fox_wada, fox_win = _cols(gathered[0]), _cols(gathered[1])
    fox_wout = gathered[2].reshape(di, d)
    sb_gather = Exchange("gather", [w[0].astype(BF16) for w in (sb_w_ada, sb_w_in, sb_w_out)] + [sb_norm_g, sb_b_ada])

    x0 = x.reshape(t, d)
    target = loss_target.reshape(t, d)
    c16 = _pad_rows16(c)

    def layer_fwd(xin, g, wada, bada, win, wout, b_f, tag, rider=None):
        mod = adaln_fwd(c16, wada, bada, tag + "_adaln")[:nb]
        shift, scale, gate = (mod[:, k * d:(k + 1) * d].reshape(nb, 1, d) for k in range(3))
        qkv, hmod = normmod_matmul(xin, g, scale, shift, win, BF16, seq, True, tag + "_qkv", (0, 3 * di))
        if b_f is not None:
            (zf,) = normmod_matmul(xin, g, scale, shift, _pad_cols(win[:, 3 * di:], di + 128), F32, seq, False, tag + "_z")
        else:
            (zf,) = normmod_matmul(xin, g, scale, shift, win, F32, seq, False, tag + "_z", (3 * di, di))
        saved = dict(x=xin, g=g, scale=scale, gate=gate, qkv=qkv, h=hmod, zf=zf, win=win, wout=wout)
        if b_f is not None:
            fl_t = jnp.transpose(zf[:, di:di + h].reshape(nb, seq, h), (0, 2, 1))
            bf_col = b_f.reshape(h, 1)
            cum = forget_cumsum(fl_t, bf_col, tag + "_cum")
            ck = cum.reshape(nb, h, seq // tq, tq)
            o, lse, *rode = fox_fwd(qkv, ck, seq, di, dh, tag + "_attn", rider)
            saved.update(fl_t=fl_t, bf_col=bf_col, ck=ck, lse=lse, rode=rode)
        else:
            o, rtot = sb_fwd(qkv, seq, di, dh, tag + "_attn")
            saved.update(rtot=rtot)
        xout, y, u = out_proj_fwd(o, zf, wout, xin, gate, seq, tag + "_out")
        saved.update(o=o, y=y, u=u)
        return xout, saved

    def layer_bwd(dxo, sv, is_fox, tag, riding=()):
        do, dz, dy, dgate = out_proj_bwd(dxo, sv["y"], sv["gate"], sv["wout"], sv["o"], sv["zf"], seq, tag + "_dout")
        wout_blocks = tn_matmul(sv["u"], dy, tag + "_dwout").reshape(NDEV, di // NDEV, d)
        extra = None
        if is_fox:
            dq, dk, dv, dck, *rode = fox_bwd(sv["qkv"], sv["ck"], sv["o"], do, sv["lse"], seq, di, dh, tag + "_dattn",
                                             Exchange("scatter", list(riding) + [wout_blocks]))
            df_t, dbf = forget_cumsum_bwd(dck.reshape(nb, h, seq), sv["fl_t"], sv["bf_col"], tag + "_dcum")
            df = _pad_cols(jnp.transpose(df_t, (0, 2, 1)).reshape(t, h), 128).astype(BF16)
            extra = (df, jnp.transpose(_pad_cols(sv["win"][:, 4 * di:], 128)))
        else:
            dq, dk, dv = sb_bwd(sv["qkv"], do, sv["rtot"], seq, di, dh, tag + "_dattn")
        parts = [dq, dk, dv, dz]
        dwin = [tn_matmul(sv["h"], p, tag + "_dwin%d" % k) for k, p in enumerate(parts)]
        if is_fox:
            dwin.append(tn_matmul(sv["h"], df, tag + "_dwinf")[:, :h])
        dwin = jnp.concatenate(dwin, axis=1)
        w_t = jnp.transpose(sv["win"][:, :4 * di])
        win_blocks = _col_blocks(dwin)
        dh_rider = Exchange("scatter", [win_blocks]) if is_fox else None
        dxin, psum, ssum, *landed = dh_norm_bwd(parts, w_t, extra, sv["x"], sv["g"], sv["scale"], dxo, seq,
                                                tag + "_dh", dh_rider)
        pad = lambda a: _pad_rows16(a.reshape(nb, d))
        dwada, dbada, dng = adaln_bwd(c16, pad(ssum), pad(psum), pad(dgate), pad(sv["scale"]), sv["g"], tag + "_dadaln")
        grads = dict(wada=dwada, bada=dbada, ng=dng, win=win_blocks, wout=wout_blocks)
        if is_fox:
            grads["bf"] = dbf.reshape(1, h)
            grads["rode"] = rode
            grads["landed"] = landed
        return dxin, grads

    x1, sv_fox = layer_fwd(x0, fox_norm_g, fox_wada, fox_b_ada, fox_win, fox_wout, fox_b_f, "fox", sb_gather)
    sb_wada, sb_win = _cols(sv_fox["rode"][0]), _cols(sv_fox["rode"][1])
    sb_wout = sv_fox["rode"][2].reshape(di, d)
    sb_g, sb_bada = _cols(sv_fox["rode"][3]), _cols(sv_fox["rode"][4])
    x2, sv_sb = layer_fwd(x1, sb_g, sb_wada, sb_bada, sb_win, sb_wout, None, "sb")
    dx2, dgf, sq = final_loss(x2, final_norm_g.reshape(1, d), target, "loss_head")
    dx1, g_sb = layer_bwd(dx2, sv_sb, False, "sb")
    dx0, g_fox = layer_bwd(dx1, sv_fox, True, "fox", [g_sb["wada"], g_sb["win"], g_sb["wout"]])

    sb_wada_land, sb_win_land, sb_wout_land, fox_wout_land = g_fox["rode"]
    (fox_wada_land,) = all_to_all([g_fox["wada"]], "scatter_fox_wada_grad")
    land = [fox_wada_land, g_fox["landed"][0], fox_wout_land, sb_wada_land, sb_win_land, sb_wout_land]
    small = jnp.concatenate([g_fox["ng"], g_fox["bada"], _pad_cols(g_fox["bf"], 128), g_sb["ng"], g_sb["bada"],
                             dgf, sq], axis=1)
    (small_all,) = all_gather([small], "gather_small")
    small_sum, loss_row = sum_slots(small_all, d, "sum_small")
    loss = loss_row[0, 0]

    offs = {}
    pos = 0
    for nm, width in (("fox_ng", d), ("fox_bada", 3 * d), ("fox_bf", 128), ("sb_ng", d), ("sb_bada", 3 * d), ("fin_g", d)):
        offs[nm] = (pos, width)
        pos += width

    def small_grad(nm, width=None, shard=False):
        p0, wd = offs[nm]
        wd = width or wd
        if shard:
            blk = wd // NDEV
            return lax.dynamic_slice(small_sum, (0, p0 + me * blk), (1, blk))
        return small_sum[:, p0:p0 + wd]

    results = {}
    big_params = [("fox_w_ada", fox_w_ada, m_fox_w_ada, v_fox_w_ada), ("fox_w_in", fox_w_in, m_fox_w_in, v_fox_w_in),
                  ("fox_w_out", fox_w_out, m_fox_w_out, v_fox_w_out), ("sb_w_ada", sb_w_ada, m_sb_w_ada, v_sb_w_ada),
                  ("sb_w_in", sb_w_in, m_sb_w_in, v_sb_w_in), ("sb_w_out", sb_w_out, m_sb_w_out, v_sb_w_out)]
    for k, (nm, w, m, v) in enumerate(big_params):
        results[nm] = adamw(land[k], w, m, v, "adamw_" + nm)
    small_params = [("fox_norm_g", fox_norm_g, m_fox_norm_g, v_fox_norm_g, small_grad("fox_ng")),
                    ("fox_b_ada", fox_b_ada, m_fox_b_ada, v_fox_b_ada, small_grad("fox_bada")),
                    ("fox_b_f", fox_b_f, m_fox_b_f, v_fox_b_f, small_grad("fox_bf", h)),
                    ("sb_norm_g", sb_norm_g, m_sb_norm_g, v_sb_norm_g, small_grad("sb_ng", shard=True)),
                    ("sb_b_ada", sb_b_ada, m_sb_b_ada, v_sb_b_ada, small_grad("sb_bada", shard=True)),
                    ("final_norm_g", final_norm_g.reshape(1, d), m_final_norm_g.reshape(1, d),
                     v_final_norm_g.reshape(1, d), small_grad("fin_g"))]
    for nm, w, m, v, g in small_params:
        outs = adamw(g[None], w, m, v, "adamw_" + nm)
        if nm == "final_norm_g":
            outs = [o.reshape(d) for o in outs]
        results[nm] = outs

    order = ["fox_norm_g", "fox_w_ada", "fox_b_ada", "fox_w_in", "fox_b_f", "fox_w_out", "sb_norm_g", "sb_w_ada",
             "sb_b_ada", "sb_w_in", "sb_w_out", "final_norm_g"]
    out = [loss, dx0.reshape(nb, seq, d)]
    for k in range(4):
        out += [results[nm][k] for nm in order]
    return tuple(out)
```

```python
import functools
import math

import jax
import jax.numpy as jnp
from jax import lax
from jax.experimental import pallas as pl
from jax.experimental.pallas import tpu as pltpu

F32 = jnp.float32
BF16 = jnp.bfloat16
NDEV = 8
VMEM_LIMIT = 56 * 1024 * 1024
NORM_EPS = 1e-6
ADAM_LR, ADAM_B1, ADAM_B2, ADAM_EPS, ADAM_WD, ADAM_STEP = 0.001, 0.9, 0.999, 1e-08, 0.01, 10
ATTN_LANES = 128
ATTN_BLOCKS_PER_TRIP = 2
FOX_Q_ROWS = 1024
EXP_IS_ZERO_BELOW = -104.0
MESH = pl.DeviceIdType.MESH
ANY = pl.BlockSpec(memory_space=pl.ANY)


def _cp(*sem):
    return pltpu.CompilerParams(dimension_semantics=sem, vmem_limit_bytes=VMEM_LIMIT)


def _my_index():
    return 4 * lax.axis_index("x") + 2 * lax.axis_index("y") + lax.axis_index("c")


def _flip(k):
    x, y, c = lax.axis_index("x"), lax.axis_index("y"), lax.axis_index("c")
    kx, ky, kc = (k >> 2) & 1, (k >> 1) & 1, k & 1
    px = 1 - x if kx else x
    py = 1 - y if ky else y
    pc = 1 - c if kc else c
    return (px, py, pc), 4 * px + 2 * py + pc


class Exchange:
    def __init__(self, mode, arrs):
        self.gather = mode == "gather"
        self.arrs = list(arrs)
        self.n = len(self.arrs)
        self.in_specs = [ANY] * self.n
        self.out_specs = [ANY] * self.n
        self.out_shape = [jax.ShapeDtypeStruct(((NDEV,) + a.shape) if self.gather else a.shape, a.dtype)
                          for a in self.arrs]
        self.scratch = [pltpu.SemaphoreType.DMA((self.n * (NDEV - 1),)), pltpu.SemaphoreType.DMA((self.n * (NDEV - 1),)),
                        pltpu.SemaphoreType.DMA((self.n,))]

    def copies(self, ins, outs, sems):
        send_sems, recv_sems, local_sems = sems
        me = _my_index()
        out = []
        for a in range(self.n):
            out.append(pltpu.make_async_copy(ins[a] if self.gather else ins[a].at[me], outs[a].at[me], local_sems.at[a]))
            for k in range(1, NDEV):
                peer, pidx = _flip(k)
                out.append(pltpu.make_async_remote_copy(
                    src_ref=ins[a] if self.gather else ins[a].at[pidx], dst_ref=outs[a].at[me],
                    send_sem=send_sems.at[a * (NDEV - 1) + k - 1], recv_sem=recv_sems.at[a * (NDEV - 1) + k - 1],
                    device_id=peer, device_id_type=MESH))
        return out

    def run(self, name):
        n = self.n

        def body(*refs):
            cps = self.copies(refs[:n], refs[n:2 * n], refs[2 * n:])
            for cp in cps:
                cp.start()
            for cp in cps:
                cp.wait()

        return pl.pallas_call(body, name=name, out_shape=self.out_shape, in_specs=self.in_specs,
                              out_specs=self.out_specs, scratch_shapes=self.scratch)(*self.arrs)


def _ride(exchange, refs, first, last):
    n = exchange.n
    cps = exchange.copies(refs[:n], refs[n:2 * n], refs[2 * n:])

    @pl.when(first)
    def _():
        for cp in cps:
            cp.start()

    @pl.when(last)
    def _():
        for cp in cps:
            cp.wait()


def all_gather(arrs, name):
    return Exchange("gather", arrs).run(name)


def all_gather_by_chip(arrs, name):
    n = len(arrs)
    chips = (2, 4, 6)

    def body(*refs):
        ins, outs = refs[:n], refs[n:2 * n]
        send_sems, recv_sems, local_sems = refs[2 * n:]
        me = _my_index()
        sibling, _ = _flip(1)

        def copy(a, k, block, to, src=None):
            return pltpu.make_async_remote_copy(
                src_ref=outs[a].at[block] if src is None else src, dst_ref=outs[a].at[block],
                send_sem=send_sems.at[7 * a + k], recv_sem=recv_sems.at[7 * a + k], device_id=to, device_id_type=MESH)

        own, sent = [], []
        for a in range(n):
            own.append(pltpu.make_async_copy(ins[a], outs[a].at[me], local_sems.at[a]))
            own[-1].start()
            first = [copy(a, 0, me, sibling, ins[a])] + [copy(a, 1 + j, me, _flip(k)[0], ins[a]) for j, k in enumerate(chips)]
            for cp in first:
                cp.start()
            sent += first
        for a in range(n):
            for j, k in enumerate(chips):
                peer, pidx = _flip(k)
                copy(a, 1 + j, pidx, peer).wait_recv()
                sent.append(copy(a, 4 + j, pidx, sibling))
                sent[-1].start()
        for a in range(n):
            copy(a, 0, _flip(1)[1], sibling).wait_recv()
            for j, k in enumerate(chips):
                copy(a, 4 + j, _flip(k + 1)[1], sibling).wait_recv()
        for cp in sent:
            cp.wait_send()
        for cp in own:
            cp.wait()

    return pl.pallas_call(
        body, name=name,
        out_shape=[jax.ShapeDtypeStruct((NDEV,) + a.shape, a.dtype) for a in arrs],
        in_specs=[ANY] * n, out_specs=[ANY] * n,
        scratch_shapes=[pltpu.SemaphoreType.DMA((7 * n,)), pltpu.SemaphoreType.DMA((7 * n,)),
                        pltpu.SemaphoreType.DMA((n,))],
    )(*arrs)


def all_to_all(arrs, name):
    return Exchange("scatter", arrs).run(name)


def _tile(n, pref):
    t = min(n, pref)
    while n % t:
        t //= 2
    return t


def _col_tile(n, pref):
    t = _tile(n, pref)
    return t if t % 128 == 0 and t >= 512 else n


def adamw(land, w, m, v, name):
    slots, r, c = land.shape
    tr = _tile(r, 64)
    bc1 = 1.0 - ADAM_B1 ** ADAM_STEP
    bc2 = 1.0 - ADAM_B2 ** ADAM_STEP
    lead = w.ndim - 2

    def body(land_ref, w_ref, m_ref, v_ref, g_ref, d_ref, nm_ref, nv_ref):
        at = (0,) * lead + (Ellipsis,)
        g = land_ref[0].astype(F32)
        for s in range(1, slots):
            g = g + land_ref[s].astype(F32)
        nm = ADAM_B1 * m_ref[at] + (1.0 - ADAM_B1) * g
        nv = ADAM_B2 * v_ref[at] + (1.0 - ADAM_B2) * (g * g)
        m_hat = nm / bc1
        v_hat = nv / bc2
        g_ref[at] = g
        nm_ref[at] = nm
        nv_ref[at] = nv
        d_ref[at] = -ADAM_LR * (m_hat / (jnp.sqrt(v_hat) + ADAM_EPS) + ADAM_WD * w_ref[at])

    blk = pl.BlockSpec((1,) * lead + (tr, c), lambda i: (0,) * lead + (i, 0))
    return pl.pallas_call(
        body, name=name, grid=(r // tr,),
        in_specs=[pl.BlockSpec((slots, tr, c), lambda i: (0, i, 0)), blk, blk, blk],
        out_specs=[blk] * 4,
        out_shape=[jax.ShapeDtypeStruct(w.shape, F32)] * 4,
        compiler_params=_cp("parallel"),
    )(land, w, m, v)


def sum_slots(land, d_model, name):
    slots, _, n = land.shape

    def body(land_ref, o_ref, loss_ref):
        g = land_ref[0]
        for s in range(1, slots):
            g = g + land_ref[s]
        o_ref[...] = g
        sq = jnp.sum(g[:, n - d_model:], axis=-1, keepdims=True)
        loss_ref[...] = jnp.broadcast_to(sq * (0.5 / d_model), (1, 128))

    return pl.pallas_call(
        body, name=name,
        out_shape=[jax.ShapeDtypeStruct((1, n), F32), jax.ShapeDtypeStruct((1, 128), F32)],
    )(land)


def _sigmoid(x):
    return 1.0 / (1.0 + jnp.exp(-x))


def adaln_fwd(c16, w_ada, b_ada, name):
    d3 = w_ada.shape[1]

    def body(c_ref, w_ref, b_ref, o_ref):
        cc = c_ref[...]
        sc = (cc * _sigmoid(cc)).astype(BF16)
        o_ref[...] = jnp.dot(sc, w_ref[...], preferred_element_type=F32) + b_ref[...]

    return pl.pallas_call(body, name=name, out_shape=jax.ShapeDtypeStruct((16, d3), F32),
                          compiler_params=pltpu.CompilerParams(vmem_limit_bytes=VMEM_LIMIT))(c16, w_ada, b_ada)


def adaln_bwd(c16, dshift16, p16, dgate16, scale16, g, name):
    d = c16.shape[1]
    nb = 3 * d // NDEV

    def body(c_ref, ds_ref, p_ref, dg_ref, sc_ref, g_ref, dw_ref, db_ref, dng_ref, dmod_scr):
        j = pl.program_id(0)

        @pl.when(j == 0)
        def _():
            p = p_ref[...]
            dmod = jnp.concatenate([ds_ref[...], p * g_ref[...], dg_ref[...]], axis=-1)
            dmod_scr[...] = dmod
            db_ref[...] = jnp.sum(dmod, axis=0, keepdims=True)
            dng_ref[...] = jnp.sum((1.0 + sc_ref[...]) * p, axis=0, keepdims=True)

        cc = c_ref[...]
        sc = (cc * _sigmoid(cc)).astype(BF16)
        start = pl.multiple_of(j * nb, 128)
        dm = dmod_scr[:, pl.ds(start, nb)].astype(BF16)
        dw_ref[0] = lax.dot_general(sc, dm, (((0,), (0,)), ((), ())), preferred_element_type=F32).astype(BF16)

    full = lambda shape: pl.BlockSpec(shape, lambda j: (0,) * len(shape))
    return pl.pallas_call(
        body, name=name, grid=(NDEV,),
        in_specs=[full((16, d))] * 5 + [full((1, d))],
        out_specs=[pl.BlockSpec((1, d, nb), lambda j: (j, 0, 0)), full((1, 3 * d)), full((1, d))],
        out_shape=[jax.ShapeDtypeStruct((NDEV, d, nb), BF16), jax.ShapeDtypeStruct((1, 3 * d), F32),
                   jax.ShapeDtypeStruct((1, d), F32)],
        scratch_shapes=[pltpu.VMEM((16, 3 * d), F32)],
        compiler_params=_cp("arbitrary"),
    )(c16, dshift16, p16, dgate16, scale16, g)


def _modulated_norm(x, g, scale, shift):
    rstd = lax.rsqrt(jnp.mean(x * x, axis=-1, keepdims=True) + NORM_EPS)
    return ((x * rstd) * g) * (1.0 + scale) + shift


def normmod_matmul(x, g, scale, shift, w, out_dtype, seq, emit_h, name, cols=None):
    t, d = x.shape
    col0, n = cols or (0, w.shape[1])
    tm = _tile(seq, 512)
    tn = _col_tile(n, 2048)
    assert col0 % tn == 0
    jb = col0 // tn
    per_seq = seq // tm

    def body(x_ref, g_ref, sc_ref, sh_ref, w_ref, *rest):
        if emit_h:
            o_ref, h_ref, h_scr = rest
        else:
            o_ref, h_scr = rest

        @pl.when(pl.program_id(1) == 0)
        def _():
            h = _modulated_norm(x_ref[...], g_ref[...], sc_ref[0], sh_ref[0]).astype(BF16)
            h_scr[...] = h
            if emit_h:
                h_ref[...] = h

        o_ref[...] = jnp.dot(h_scr[...], w_ref[...], preferred_element_type=F32).astype(out_dtype)

    mod_spec = pl.BlockSpec((1, 1, d), lambda i, j: (i // per_seq, 0, 0))
    out_specs = [pl.BlockSpec((tm, tn), lambda i, j: (i, j))]
    out_shape = [jax.ShapeDtypeStruct((t, n), out_dtype)]
    if emit_h:
        out_specs.append(pl.BlockSpec((tm, d), lambda i, j: (i, 0)))
        out_shape.append(jax.ShapeDtypeStruct((t, d), BF16))
    return pl.pallas_call(
        body, name=name, grid=(t // tm, n // tn),
        in_specs=[pl.BlockSpec((tm, d), lambda i, j: (i, 0)), pl.BlockSpec((1, d), lambda i, j: (0, 0)),
                  mod_spec, mod_spec, pl.BlockSpec((d, tn), lambda i, j: (0, jb + j))],
        out_specs=out_specs, out_shape=out_shape,
        scratch_shapes=[pltpu.VMEM((tm, d), BF16)],
        compiler_params=_cp("parallel", "arbitrary"),
    )(x, g, scale, shift, w)


def out_proj_fwd(o, zf, w_out, x, gate, seq, name):
    t, di = o.shape
    d = x.shape[1]
    tm = _tile(seq, 256)
    per_seq = seq // tm

    def body(o_ref, z_ref, w_ref, x_ref, gt_ref, xn_ref, y_ref, u_ref):
        z = z_ref[...]
        u = (o_ref[...] * (z * _sigmoid(z))).astype(BF16)
        y = jnp.dot(u, w_ref[...], preferred_element_type=F32)
        u_ref[...] = u
        y_ref[...] = y
        xn_ref[...] = x_ref[...] + gt_ref[0] * y

    row = lambda c: pl.BlockSpec((tm, c), lambda i: (i, 0))
    return pl.pallas_call(
        body, name=name, grid=(t // tm,),
        in_specs=[row(di), row(di), pl.BlockSpec((di, d), lambda i: (0, 0)), row(d),
                  pl.BlockSpec((1, 1, d), lambda i: (i // per_seq, 0, 0))],
        out_specs=[row(d), row(d), row(di)],
        out_shape=[jax.ShapeDtypeStruct((t, d), F32), jax.ShapeDtypeStruct((t, d), F32),
                   jax.ShapeDtypeStruct((t, di), BF16)],
        compiler_params=_cp("parallel"),
    )(o, zf, w_out, x, gate)


def out_proj_bwd(dxo, y, gate, w_out, o, zf, seq, name):
    t, d = dxo.shape
    di = o.shape[1]
    tm = _tile(seq, 256)
    per_seq = seq // tm

    def body(dx_ref, y_ref, gt_ref, w_ref, o_ref, z_ref, do_ref, dz_ref, dy_ref, dgt_ref):
        dx = dx_ref[...]
        part = jnp.sum(dx * y_ref[...], axis=0, keepdims=True)

        @pl.when(pl.program_id(0) % per_seq == 0)
        def _():
            dgt_ref[0] = part

        @pl.when(pl.program_id(0) % per_seq != 0)
        def _():
            dgt_ref[0] += part

        dy = (dx * gt_ref[0]).astype(BF16)
        dy_ref[...] = dy
        du = lax.dot_general(dy, w_ref[...], (((1,), (1,)), ((), ())), preferred_element_type=F32)
        z = z_ref[...]
        sg = _sigmoid(z)
        do_ref[...] = (du * (z * sg)).astype(BF16)
        dz_ref[...] = (du * o_ref[...] * (sg * (1.0 + z * (1.0 - sg)))).astype(BF16)

    row = lambda c: pl.BlockSpec((tm, c), lambda i: (i, 0))
    seq_spec = pl.BlockSpec((1, 1, d), lambda i: (i // per_seq, 0, 0))
    return pl.pallas_call(
        body, name=name, grid=(t // tm,),
        in_specs=[row(d), row(d), seq_spec, pl.BlockSpec((di, d), lambda i: (0, 0)), row(di), row(di)],
        out_specs=[row(di), row(di), row(d), seq_spec],
        out_shape=[jax.ShapeDtypeStruct((t, di), BF16), jax.ShapeDtypeStruct((t, di), BF16),
                   jax.ShapeDtypeStruct((t, d), BF16), jax.ShapeDtypeStruct((t // seq, 1, d), F32)],
        compiler_params=_cp("arbitrary"),
    )(dxo, y, gate, w_out, o, zf)


def tn_matmul(a, b, name):
    t, m = a.shape
    n = b.shape[1]
    tn = _col_tile(n, 1024)
    tk = _tile(t, 1024)
    nk = t // tk

    def body(a_ref, b_ref, o_ref, acc):
        part = lax.dot_general(a_ref[...], b_ref[...], (((0,), (0,)), ((), ())), preferred_element_type=F32)
        k = pl.program_id(1)

        @pl.when(k == 0)
        def _():
            acc[...] = part

        @pl.when(k != 0)
        def _():
            acc[...] += part

        @pl.when(k == nk - 1)
        def _():
            o_ref[...] = acc[...].astype(BF16)

    return pl.pallas_call(
        body, name=name, grid=(n // tn, nk),
        in_specs=[pl.BlockSpec((tk, m), lambda j, k: (k, 0)), pl.BlockSpec((tk, tn), lambda j, k: (k, j))],
        out_specs=pl.BlockSpec((m, tn), lambda j, k: (0, j)),
        out_shape=jax.ShapeDtypeStruct((m, n), BF16),
        scratch_shapes=[pltpu.VMEM((m, tn), F32)],
        compiler_params=_cp("parallel", "arbitrary"),
    )(a, b)


def dh_norm_bwd(parts, w_t, extra, x, g, scale, dxo, seq, name, rider=None):
    t, d = x.shape
    nparts = len(parts)
    kw = parts[0].shape[1]
    tm = _tile(seq, 512)
    per_seq = seq // tm
    has_extra = extra is not None
    grid = (t // tm,)
    nr = rider.n if rider else 0
    n_in = nparts + 1 + (2 if has_extra else 0) + 4

    def body(*refs):
        p_refs = refs[:nparts]
        w_ref = refs[nparts]
        pos = nparts + 1
        if has_extra:
            e_ref, we_ref = refs[pos], refs[pos + 1]
            pos += 2
        x_ref, g_ref, sc_ref, dxo_ref = refs[pos:pos + 4]
        dx_ref, pp_ref, ss_ref = refs[n_in + nr:n_in + nr + 3]
        if rider:
            _ride(rider, refs[n_in:n_in + nr] + refs[n_in + nr + 3:], *_grid_ends(grid))
        i = pl.program_id(0)

        dh = jnp.dot(p_refs[0][...], w_ref[0:kw, :], preferred_element_type=F32)
        for kk in range(1, nparts):
            dh = dh + jnp.dot(p_refs[kk][...], w_ref[kk * kw:(kk + 1) * kw, :], preferred_element_type=F32)
        if has_extra:
            dh = dh + jnp.dot(e_ref[...], we_ref[...], preferred_element_type=F32)
        xx = x_ref[...]
        rstd = lax.rsqrt(jnp.mean(xx * xx, axis=-1, keepdims=True) + NORM_EPS)
        xhat = xx * rstd
        dxhat = dh * (g_ref[...] * (1.0 + sc_ref[0]))
        dx_ref[...] = dxo_ref[...] + rstd * (dxhat - xhat * jnp.mean(dxhat * xhat, axis=-1, keepdims=True))
        pp = jnp.sum(dh * xhat, axis=0, keepdims=True)
        ss = jnp.sum(dh, axis=0, keepdims=True)

        @pl.when(i % per_seq == 0)
        def _():
            pp_ref[0] = pp
            ss_ref[0] = ss

        @pl.when(i % per_seq != 0)
        def _():
            pp_ref[0] += pp
            ss_ref[0] += ss

    row = lambda c: pl.BlockSpec((tm, c), lambda i: (i, 0))
    whole = lambda a: pl.BlockSpec(a.shape, lambda i: (0, 0), pipeline_mode=pl.Buffered(1))
    seq_spec = pl.BlockSpec((1, 1, d), lambda i: (i // per_seq, 0, 0))
    in_specs = [row(kw)] * nparts + [whole(w_t)]
    args = list(parts) + [w_t]
    if has_extra:
        in_specs += [row(extra[0].shape[1]), whole(extra[1])]
        args += list(extra)
    in_specs += [row(d), pl.BlockSpec((1, d), lambda i: (0, 0)), seq_spec, row(d)]
    args += [x, g, scale, dxo]
    return pl.pallas_call(
        body, name=name, grid=grid,
        in_specs=in_specs + (rider.in_specs if rider else []),
        out_specs=[row(d), seq_spec, seq_spec] + (rider.out_specs if rider else []),
        out_shape=[jax.ShapeDtypeStruct((t, d), F32), jax.ShapeDtypeStruct((t // seq, 1, d), F32),
                   jax.ShapeDtypeStruct((t // seq, 1, d), F32)] + (rider.out_shape if rider else []),
        scratch_shapes=rider.scratch if rider else [],
        compiler_params=_cp("arbitrary"),
    )(*args, *(rider.arrs if rider else []))


def final_loss(x, g, target, name):
    t, d = x.shape
    tm = _tile(t, 512)

    def body(x_ref, g_ref, t_ref, dx_ref, dg_ref, sq_ref):
        xx = x_ref[...]
        gg = g_ref[...]
        rstd = lax.rsqrt(jnp.mean(xx * xx, axis=-1, keepdims=True) + NORM_EPS)
        xhat = xx * rstd
        err = xhat * gg - t_ref[...]
        dy = err * (1.0 / d)
        dxhat = dy * gg
        dx_ref[...] = rstd * (dxhat - xhat * jnp.mean(dxhat * xhat, axis=-1, keepdims=True))
        dg = jnp.sum(dy * xhat, axis=0, keepdims=True)
        sq = jnp.sum(err * err, axis=0, keepdims=True)

        @pl.when(pl.program_id(0) == 0)
        def _():
            dg_ref[...] = dg
            sq_ref[...] = sq

        @pl.when(pl.program_id(0) != 0)
        def _():
            dg_ref[...] += dg
            sq_ref[...] += sq

    row = pl.BlockSpec((tm, d), lambda i: (i, 0))
    vec = pl.BlockSpec((1, d), lambda i: (0, 0))
    return pl.pallas_call(
        body, name=name, grid=(t // tm,),
        in_specs=[row, vec, row], out_specs=[row, vec, vec],
        out_shape=[jax.ShapeDtypeStruct((t, d), F32), jax.ShapeDtypeStruct((1, d), F32),
                   jax.ShapeDtypeStruct((1, d), F32)],
        compiler_params=_cp("arbitrary"),
    )(x, g, target)


def _split3(x):
    x1 = x.astype(BF16)
    r1 = x - x1.astype(F32)
    x2 = r1.astype(BF16)
    x3 = (r1 - x2.astype(F32)).astype(BF16)
    return x1, x2, x3


def _dot3(x, u):
    x1, x2, x3 = _split3(x)
    return (jnp.dot(x1, u, preferred_element_type=F32) + jnp.dot(x2, u, preferred_element_type=F32)
            + jnp.dot(x3, u, preferred_element_type=F32))


def _log_sigmoid(x):
    return jnp.minimum(x, 0.0) - jnp.log1p(jnp.exp(-jnp.abs(x)))


def forget_cumsum(fl_t, b_f, name):
    b, h, s = fl_t.shape

    def body(f_ref, b_ref, o_ref):
        lf = _log_sigmoid(f_ref[0] + b_ref[...])
        u = (lax.broadcasted_iota(jnp.int32, (s, s), 0) <= lax.broadcasted_iota(jnp.int32, (s, s), 1)).astype(BF16)
        o_ref[0] = _dot3(lf, u)

    return pl.pallas_call(
        body, name=name, grid=(b,),
        in_specs=[pl.BlockSpec((1, h, s), lambda i: (i, 0, 0)), pl.BlockSpec((h, 1), lambda i: (0, 0))],
        out_specs=pl.BlockSpec((1, h, s), lambda i: (i, 0, 0)),
        out_shape=jax.ShapeDtypeStruct((b, h, s), F32),
        compiler_params=_cp("parallel"),
    )(fl_t, b_f)


def forget_cumsum_bwd(dcum, fl_t, b_f, name):
    b, h, s = fl_t.shape

    def body(d_ref, f_ref, b_ref, o_ref, db_ref):
        u = (lax.broadcasted_iota(jnp.int32, (s, s), 0) >= lax.broadcasted_iota(jnp.int32, (s, s), 1)).astype(BF16)
        dlf = _dot3(d_ref[0], u)
        df = dlf * _sigmoid(-(f_ref[0] + b_ref[...]))
        o_ref[0] = df
        part = jnp.sum(df, axis=-1, keepdims=True)

        @pl.when(pl.program_id(0) == 0)
        def _():
            db_ref[...] = part

        @pl.when(pl.program_id(0) != 0)
        def _():
            db_ref[...] += part

    blk = pl.BlockSpec((1, h, s), lambda i: (i, 0, 0))
    return pl.pallas_call(
        body, name=name, grid=(b,),
        in_specs=[blk, blk, pl.BlockSpec((h, 1), lambda i: (0, 0))],
        out_specs=[blk, pl.BlockSpec((h, 1), lambda i: (0, 0))],
        out_shape=[jax.ShapeDtypeStruct((b, h, s), F32), jax.ShapeDtypeStruct((h, 1), F32)],
        compiler_params=_cp("arbitrary"),
    )(dcum, fl_t, b_f)


def _nt(a, b):
    return lax.dot_general(a, b, (((1,), (1,)), ((), ())), preferred_element_type=F32)


def _attn_dims(t, seq, dh, q_rows=512):
    tq = _tile(seq, q_rows)
    tk = _tile(seq, 256)
    return t // seq, tq, tk, seq // tq, tq // tk, ATTN_LANES // dh


def _blocks_per_trip(r):
    return ATTN_BLOCKS_PER_TRIP if r % ATTN_BLOCKS_PER_TRIP == 0 else 1


def _scaled_q(q_ref, dh):
    scale = dh ** -0.5
    if math.log2(dh) % 2 == 0:
        return (q_ref[...].astype(F32) * scale).astype(BF16), None
    return q_ref[...], scale


def _diag_masks(tq, tk, r, strict):
    row = lax.broadcasted_iota(jnp.int32, (tq, tk), 0)
    col = lax.broadcasted_iota(jnp.int32, (tq, tk), 1)
    return [(col + d * tk < row) if strict else (col + d * tk <= row) for d in range(r)]


def _rows(state, lo):
    return tuple(a[lo:] for a in state)


def _put_rows(state, part, lo):
    if lo == 0:
        return tuple(part)
    return tuple(jnp.concatenate([a[:lo], p], axis=0) for a, p in zip(state, part))


def _transposed(x):
    return jnp.transpose(x.astype(F32)).astype(BF16)


def _store_transposed(dst_ref, src_scr, tk):
    for jb in range(src_scr.shape[0]):
        dst_ref[jb * tk:(jb + 1) * tk, :] = jnp.transpose(src_scr[jb]).astype(BF16)


def _dot2(x, u):
    hi = x.astype(BF16)
    lo = (x - hi.astype(F32)).astype(BF16)
    return jnp.dot(jnp.concatenate([hi, lo], axis=1), jnp.concatenate([u, u], axis=0), preferred_element_type=F32)


def _attn_specs(seq, di, tq):
    nq = seq // tq
    cb = di // ATTN_LANES
    q_spec = pl.BlockSpec((tq, ATTN_LANES), lambda b, hp, i: (b * nq + i, hp))
    k_spec = pl.BlockSpec((seq, ATTN_LANES), lambda b, hp, i: (b, cb + hp))
    v_spec = pl.BlockSpec((seq, ATTN_LANES), lambda b, hp, i: (b, 2 * cb + hp))
    kv_out = pl.BlockSpec((seq, ATTN_LANES), lambda b, hp, i: (b, hp))
    return q_spec, k_spec, v_spec, kv_out


def _grid_ends(grid):
    ids = [pl.program_id(a) for a in range(len(grid))]
    first = functools.reduce(jnp.logical_and, [p == 0 for p in ids])
    last = functools.reduce(jnp.logical_and, [p == g - 1 for p, g in zip(ids, grid)])
    return first, last


def fox_fwd(qkv, ck, seq, di, dh, name, rider=None):
    t = qkv.shape[0]
    nb, tq, tk, nq, r, hpb = _attn_dims(t, seq, dh, FOX_Q_ROWS)
    per_trip = _blocks_per_trip(r)
    heads = [slice(hh * dh, (hh + 1) * dh) for hh in range(hpb)]
    assert hpb >= 2, "the row sums of p ride in another head's lanes of the p @ v matmul"
    q_spec, k_spec, v_spec, _ = _attn_specs(seq, di, tq)
    grid = (nb, di // ATTN_LANES, nq)
    nr = rider.n if rider else 0

    def body(q_ref, k_ref, v_ref, ck_ref, *rest):
        o_ref, lse_ref = rest[nr:nr + 2]
        knorm_ref = rest[2 * nr + 2]
        if rider:
            _ride(rider, rest[:nr] + rest[nr + 2:2 * nr + 2] + rest[2 * nr + 3:], *_grid_ends(grid))
        i = pl.program_id(2)
        masks = _diag_masks(tq, tk, r, False)
        q, scale = _scaled_q(q_ref, dh)
        qs = [q[:, hd] for hd in heads]
        lane_head = lax.broadcasted_iota(jnp.int32, (tk, ATTN_LANES), 1) // dh

        def largest_norm(x):
            x = x.astype(F32)
            return jnp.sqrt(jnp.max(jnp.sum(x * x, axis=-1, keepdims=True)))

        @pl.when(i == 0)
        def _():
            for hh in range(hpb):
                knorm_ref[hh] = largest_norm(k_ref[:, heads[hh]])

        qk_bound = [largest_norm(qs[hh]) * knorm_ref[hh] * (1.0 if scale is None else scale) + 1.0 for hh in range(hpb)]

        def scores(hh, j):
            start = pl.multiple_of(j * tk, tk)
            kh = k_ref[pl.ds(start, tk), heads[hh]]
            vt = v_ref[pl.ds(start, tk), :]
            vh = jnp.where(lane_head == hh, vt, jnp.ones_like(vt))
            s = _nt(qs[hh], kh)
            if scale is not None:
                s = s * scale
            return s - ck_ref[0, hh, pl.ds(j, 1), :], vh

        def update(state, s, vh):
            m, acc = state
            m_new = jnp.maximum(m, jnp.max(s, axis=-1, keepdims=True))
            p = jnp.exp(s - m_new)
            return m_new, jnp.exp(m - m_new) * acc + _dot2(p, vh)

        states = []
        for hh in range(hpb):
            s, vh = scores(hh, r * i)
            s = jnp.where(masks[0], s, -jnp.inf)
            m = jnp.max(s, axis=-1, keepdims=True)
            states.append((m, _dot2(jnp.exp(s - m), vh)))
        for d in range(1, r):
            for hh in range(hpb):
                s, vh = scores(hh, r * i + d)
                states[hh] = update(states[hh], jnp.where(masks[d], s, -jnp.inf), vh)

        def any_weight_left(states, j):
            jc = jnp.maximum(j, 0)
            tops = [jnp.max(-ck_ref[0, hh, pl.ds(jc, 1), :]) + qk_bound[hh] - jnp.min(states[hh][0]) for hh in range(hpb)]
            return (functools.reduce(jnp.maximum, tops) > EXP_IS_ZERO_BELOW).astype(jnp.int32)

        def step(carry):
            jj, _, states = carry
            for sub in range(per_trip):
                states = tuple(update(states[hh], *scores(hh, r * i - 1 - (per_trip * jj + sub))) for hh in range(hpb))
            return jj + 1, any_weight_left(states, r * i - 1 - per_trip * (jj + 1)), states

        trips, _, states = lax.while_loop(lambda c: jnp.logical_and(c[0] < r * i // per_trip, c[1] > 0), step,
                                          (jnp.int32(0), any_weight_left(states, r * i - 1), tuple(states)))
        for hh in range(hpb):
            m, acc = states[hh]
            other = heads[(hh + 1) % hpb].start
            l = acc[:, other:other + 1]
            o_ref[:, heads[hh]] = acc[:, heads[hh]] / l
            lse_ref[0, 0, :, hh:hh + 1] = m + jnp.log(l)
        lse_ref[0, 0, :, hpb:hpb + 1] = jnp.full((tq, 1), (trips * per_trip).astype(F32))

    return pl.pallas_call(
        body, name=name, grid=grid,
        in_specs=[q_spec, k_spec, v_spec,
                  pl.BlockSpec((1, hpb, seq // tk, tk), lambda b, hp, i: (b, hp, 0, 0))] + (rider.in_specs if rider else []),
        out_specs=[q_spec, pl.BlockSpec((1, 1, tq, hpb + 1), lambda b, hp, i: (b, hp, i, 0))]
        + (rider.out_specs if rider else []),
        out_shape=[jax.ShapeDtypeStruct((t, di), F32), jax.ShapeDtypeStruct((nb, di // ATTN_LANES, seq, hpb + 1), F32)]
        + (rider.out_shape if rider else []),
        scratch_shapes=[pltpu.SMEM((hpb,), F32)] + (rider.scratch if rider else []),
        compiler_params=_cp("arbitrary", "arbitrary", "arbitrary"),
    )(qkv, qkv, qkv, ck, *(rider.arrs if rider else []))


def fox_bwd(qkv, ck, o, do, lse, seq, di, dh, name, rider=None):
    t = qkv.shape[0]
    nb, tq, tk, nq, r, hpb = _attn_dims(t, seq, dh, FOX_Q_ROWS)
    per_trip = _blocks_per_trip(r)
    heads = [slice(hh * dh, (hh + 1) * dh) for hh in range(hpb)]
    q_spec, k_spec, v_spec, kv_out = _attn_specs(seq, di, tq)
    ck_spec = pl.BlockSpec((1, hpb, seq // tk, tk), lambda b, hp, i: (b, hp, 0, 0))
    grid = (nb, di // ATTN_LANES, nq)
    nr = rider.n if rider else 0

    def body(q_ref, k_ref, v_ref, ck_ref, o_ref, do_ref, lse_ref, *rest):
        dq_ref, dk_ref, dv_ref, dck_ref = rest[nr:nr + 4]
        dkt_scr, dvt_scr = rest[2 * nr + 4:2 * nr + 6]
        if rider:
            _ride(rider, rest[:nr] + rest[nr + 4:2 * nr + 4] + rest[2 * nr + 6:], *_grid_ends(grid))
        i = pl.program_id(2)

        @pl.when(i == 0)
        def _():
            dkt_scr[...] = jnp.zeros_like(dkt_scr)
            dvt_scr[...] = jnp.zeros_like(dvt_scr)
            dck_ref[...] = jnp.zeros_like(dck_ref)

        masks = _diag_masks(tq, tk, r, False)
        q, scale = _scaled_q(q_ref, dh)
        do = do_ref[...]
        q_t, do_t = _transposed(q), _transposed(do)
        qs = [q[:, hd] for hd in heads]
        dos = [do[:, hd] for hd in heads]
        deltas = [jnp.sum(dos[hh].astype(F32) * o_ref[:, heads[hh]], axis=-1, keepdims=True) for hh in range(hpb)]
        lses = [lse_ref[0, 0, :, hh:hh + 1] for hh in range(hpb)]

        def block(hh, j, mask, dq_acc, lo=0):
            start = pl.multiple_of(j * tk, tk)
            kh = k_ref[pl.ds(start, tk), heads[hh]]
            vh = v_ref[pl.ds(start, tk), heads[hh]]
            s = _nt(qs[hh][lo:], kh)
            if scale is not None:
                s = s * scale
            p = jnp.exp(s - ck_ref[0, hh, pl.ds(j, 1), :] - lses[hh][lo:])
            if mask is not None:
                p = jnp.where(mask[lo:], p, 0.0)
            ds = p * (_nt(dos[hh][lo:], vh) - deltas[hh][lo:])
            dsb = ds.astype(BF16)
            dkt = jnp.dot(q_t[heads[hh], lo:], dsb, preferred_element_type=F32)
            if scale is not None:
                dkt = dkt * scale
            dkt_scr[j, heads[hh], :] += dkt
            dvt_scr[j, heads[hh], :] += jnp.dot(do_t[heads[hh], lo:], p.astype(BF16), preferred_element_type=F32)
            dck_ref[0, hh, pl.ds(j, 1), :] -= jnp.sum(ds, axis=0, keepdims=True)
            return dq_acc + jnp.dot(dsb, kh, preferred_element_type=F32)

        accs = [jnp.zeros((tq, dh), F32)] * hpb
        for d in range(r):
            lo = d * tk
            accs = [_put_rows((accs[hh],), (block(hh, r * i + d, masks[d], accs[hh][lo:], lo),), lo)[0]
                    for hh in range(hpb)]

        trips = jnp.clip(jnp.max(lse_ref[0, 0, :, hpb:hpb + 1]).astype(jnp.int32), 0, r * i) // per_trip
        first = r * i - trips * per_trip

        def step(jj, accs):
            for sub in range(per_trip):
                accs = tuple(block(hh, first + per_trip * jj + sub, None, accs[hh]) for hh in range(hpb))
            return accs

        accs = lax.fori_loop(0, trips, step, tuple(accs))
        for hh in range(hpb):
            dq_ref[:, heads[hh]] = (accs[hh] * dh ** -0.5).astype(BF16)

        @pl.when(i == nq - 1)
        def _():
            _store_transposed(dk_ref, dkt_scr, tk)
            _store_transposed(dv_ref, dvt_scr, tk)

    return pl.pallas_call(
        body, name=name, grid=grid,
        in_specs=[q_spec, k_spec, v_spec, ck_spec, q_spec, q_spec,
                  pl.BlockSpec((1, 1, tq, hpb + 1), lambda b, hp, i: (b, hp, i, 0))] + (rider.in_specs if rider else []),
        out_specs=[q_spec, kv_out, kv_out, ck_spec] + (rider.out_specs if rider else []),
        out_shape=[jax.ShapeDtypeStruct((t, di), BF16)] * 3 + [jax.ShapeDtypeStruct(ck.shape, F32)]
        + (rider.out_shape if rider else []),
        scratch_shapes=[pltpu.VMEM((seq // tk, ATTN_LANES, tk), F32), pltpu.VMEM((seq // tk, ATTN_LANES, tk), F32)]
        + (rider.scratch if rider else []),
        compiler_params=_cp("arbitrary", "arbitrary", "arbitrary"),
    )(qkv, qkv, qkv, ck, o, do, lse, *(rider.arrs if rider else []))


def _sb_logits(qh, kh, scale, strict):
    z = _nt(qh, kh)
    if scale is not None:
        z = z * scale
    e = jnp.exp(-jnp.abs(z))
    lb = jnp.minimum(z, 0.0) - jnp.log(1.0 + e)
    lk = lb - z
    return lb, lk if strict is None else jnp.where(strict, lk, 0.0)


def _cum(x, u):
    return jnp.dot(x.astype(BF16), u, preferred_element_type=F32)


def sb_fwd(qkv, seq, di, dh, name):
    t = qkv.shape[0]
    nb, tq, tk, nq, r, hpb = _attn_dims(t, seq, dh)
    heads = [slice(hh * dh, (hh + 1) * dh) for hh in range(hpb)]
    q_spec, k_spec, v_spec, _ = _attn_specs(seq, di, tq)

    def body(q_ref, k_ref, v_ref, o_ref, rt_ref):
        i = pl.program_id(2)
        masks = _diag_masks(tq, tk, r, True)
        row = lax.broadcasted_iota(jnp.int32, (tk, tk), 0)
        col = lax.broadcasted_iota(jnp.int32, (tk, tk), 1)
        u_after = (row > col).astype(BF16)
        q, scale = _scaled_q(q_ref, dh)
        qs = [q[:, hd] for hd in heads]

        def block(hh, j, mask, state, lo=0):
            rr, acc = state
            start = pl.multiple_of(j * tk, tk)
            kh = k_ref[pl.ds(start, tk), heads[hh]]
            vh = v_ref[pl.ds(start, tk), heads[hh]]
            if mask is not None:
                mask = mask[lo:]
            lb, lk = _sb_logits(qs[hh][lo:], kh, scale, mask)
            a = jnp.exp(lb + _cum(lk, u_after) + rr)
            if mask is not None:
                a = jnp.where(mask, a, 0.0)
            acc = acc + jnp.dot(a.astype(BF16), vh, preferred_element_type=F32)
            return rr + jnp.sum(lk, axis=-1, keepdims=True), acc

        states = [(jnp.zeros((tq, 1), F32), jnp.zeros((tq, dh), F32))] * hpb
        for d in reversed(range(r)):
            lo = d * tk
            states = [_put_rows(states[hh], block(hh, r * i + d, masks[d], _rows(states[hh], lo), lo), lo)
                      for hh in range(hpb)]

        def any_weight_left(states):
            top = functools.reduce(jnp.maximum, [jnp.max(st[0]) for st in states])
            return (top > EXP_IS_ZERO_BELOW).astype(jnp.int32)

        def step(carry):
            jj, _, states = carry
            states = tuple(block(hh, r * i - 1 - jj, None, states[hh]) for hh in range(hpb))
            return jj + 1, any_weight_left(states), states

        visited, _, states = lax.while_loop(lambda c: jnp.logical_and(c[0] < r * i, c[1] > 0), step,
                                            (jnp.int32(0), any_weight_left(states), tuple(states)))
        for hh in range(hpb):
            o_ref[:, heads[hh]] = states[hh][1]
            rt_ref[0, 0, :, hh:hh + 1] = states[hh][0]
        rt_ref[0, 0, :, hpb:hpb + 1] = jnp.full((tq, 1), visited.astype(F32))

    return pl.pallas_call(
        body, name=name, grid=(nb, di // ATTN_LANES, nq),
        in_specs=[q_spec, k_spec, v_spec],
        out_specs=[q_spec, pl.BlockSpec((1, 1, tq, hpb + 1), lambda b, hp, i: (b, hp, i, 0))],
        out_shape=[jax.ShapeDtypeStruct((t, di), F32), jax.ShapeDtypeStruct((nb, di // ATTN_LANES, seq, hpb + 1), F32)],
        compiler_params=_cp("parallel", "parallel", "arbitrary"),
    )(qkv, qkv, qkv)


def sb_bwd(qkv, do, rtot, seq, di, dh, name):
    t = qkv.shape[0]
    nb, tq, tk, nq, r, hpb = _attn_dims(t, seq, dh)
    heads = [slice(hh * dh, (hh + 1) * dh) for hh in range(hpb)]
    q_spec, k_spec, v_spec, kv_out = _attn_specs(seq, di, tq)

    def body(q_ref, k_ref, v_ref, do_ref, rt_ref, dq_ref, dk_ref, dv_ref, dkt_scr, dvt_scr):
        i = pl.program_id(2)

        @pl.when(i == 0)
        def _():
            dkt_scr[...] = jnp.zeros_like(dkt_scr)
            dvt_scr[...] = jnp.zeros_like(dvt_scr)

        masks = _diag_masks(tq, tk, r, True)
        row = lax.broadcasted_iota(jnp.int32, (tk, tk), 0)
        col = lax.broadcasted_iota(jnp.int32, (tk, tk), 1)
        u_after = (row > col).astype(BF16)
        u_before = (row < col).astype(BF16)
        q, scale = _scaled_q(q_ref, dh)
        do = do_ref[...]
        q_t, do_t = _transposed(q), _transposed(do)
        qs = [q[:, hd] for hd in heads]
        dos = [do[:, hd] for hd in heads]
        rts = [rt_ref[0, 0, :, hh:hh + 1] for hh in range(hpb)]

        def block(hh, j, mask, state, lo=0):
            lc, gc, dq_acc = state
            start = pl.multiple_of(j * tk, tk)
            kh = k_ref[pl.ds(start, tk), heads[hh]]
            vh = v_ref[pl.ds(start, tk), heads[hh]]
            if mask is not None:
                mask = mask[lo:]
            lb, lk = _sb_logits(qs[hh][lo:], kh, scale, mask)
            lc = lc + jnp.sum(lk, axis=-1, keepdims=True)
            a = jnp.exp(lb + ((rts[hh][lo:] - lc) + _cum(lk, u_after)))
            if mask is not None:
                a = jnp.where(mask, a, 0.0)
            de = a * _nt(dos[hh][lo:], vh)
            g = gc + _cum(de, u_before)
            dz = de - jnp.exp(lb) * (de + g)
            if mask is not None:
                dz = jnp.where(mask, dz, 0.0)
            dzb = dz.astype(BF16)
            dkt = jnp.dot(q_t[heads[hh], lo:], dzb, preferred_element_type=F32)
            if scale is not None:
                dkt = dkt * scale
            dkt_scr[j, heads[hh], :] += dkt
            dvt_scr[j, heads[hh], :] += jnp.dot(do_t[heads[hh], lo:], a.astype(BF16), preferred_element_type=F32)
            return (lc, gc + jnp.sum(de, axis=-1, keepdims=True),
                    dq_acc + jnp.dot(dzb, kh, preferred_element_type=F32))

        zero = jnp.zeros((tq, 1), F32)
        visited = jnp.clip(jnp.max(rt_ref[0, 0, :, hpb:hpb + 1]).astype(jnp.int32), 0, r * i)
        first = r * i - visited

        def step(jj, states):
            return tuple(block(hh, first + jj, None, states[hh]) for hh in range(hpb))

        states = lax.fori_loop(0, visited, step, ((zero, zero, jnp.zeros((tq, dh), F32)),) * hpb)
        for d in range(r):
            lo = d * tk
            states = [_put_rows(states[hh], block(hh, r * i + d, masks[d], _rows(states[hh], lo), lo), lo)
                      for hh in range(hpb)]
        for hh in range(hpb):
            dq_ref[:, heads[hh]] = (states[hh][2] * dh ** -0.5).astype(BF16)

        @pl.when(i == nq - 1)
        def _():
            _store_transposed(dk_ref, dkt_scr, tk)
            _store_transposed(dv_ref, dvt_scr, tk)

    return pl.pallas_call(
        body, name=name, grid=(nb, di // ATTN_LANES, nq),
        in_specs=[q_spec, k_spec, v_spec, q_spec, pl.BlockSpec((1, 1, tq, hpb + 1), lambda b, hp, i: (b, hp, i, 0))],
        out_specs=[q_spec, kv_out, kv_out],
        out_shape=[jax.ShapeDtypeStruct((t, di), BF16)] * 3,
        scratch_shapes=[pltpu.VMEM((seq // tk, ATTN_LANES, tk), F32), pltpu.VMEM((seq // tk, ATTN_LANES, tk), F32)],
        compiler_params=_cp("parallel", "parallel", "arbitrary"),
    )(qkv, qkv, qkv, do, rtot)


def _cols(g):
    return jnp.transpose(g, (1, 0, 2)).reshape(g.shape[1], NDEV * g.shape[2])


def _col_blocks(w):
    r, c8 = w.shape
    return jnp.transpose(w.reshape(r, NDEV, c8 // NDEV), (1, 0, 2))


def _pad_rows16(a):
    return jnp.pad(a, ((0, 16 - a.shape[0]), (0, 0)))


def _pad_cols(a, n):
    return jnp.pad(a, ((0, 0), (0, n - a.shape[1])))


def kernel(x, c, fox_norm_g, fox_w_ada, fox_b_ada, fox_w_in, fox_b_f, fox_w_out, sb_norm_g, sb_w_ada, sb_b_ada, sb_w_in, sb_w_out, final_norm_g, loss_target, m_fox_norm_g, m_fox_w_ada, m_fox_b_ada, m_fox_w_in, m_fox_b_f, m_fox_w_out, m_sb_norm_g, m_sb_w_ada, m_sb_b_ada, m_sb_w_in, m_sb_w_out, m_final_norm_g, v_fox_norm_g, v_fox_w_ada, v_fox_b_ada, v_fox_w_in, v_fox_b_f, v_fox_w_out, v_sb_norm_g, v_sb_w_ada, v_sb_b_ada, v_sb_w_in, v_sb_w_out, v_final_norm_g):
    nb, seq, d = x.shape
    t = nb * seq
    h = fox_b_f.shape[-1]
    di = fox_w_out.shape[1] * NDEV
    dh = di // h
    tq = _tile(seq, 256)
    me = _my_index()

    gathered = all_gather_by_chip([w[0].astype(BF16) for w in (fox_w_ada, fox_w_in, fox_w_out)], "gather_fox_weights")
    fox_wada, fox_win = _cols(gathered[0]), _cols(gathered[1])
    fox_wout = gathered[2].reshape(di, d)
    sb_gather = Exchange("gather", [w[0].astype(BF16) for w in (sb_w_ada, sb_w_in, sb_w_out)] + [sb_norm_g, sb_b_ada])

    x0 = x.reshape(t, d)
    target = loss_target.reshape(t, d)
    c16 = _pad_rows16(c)

    def layer_fwd(xin, g, wada, bada, win, wout, b_f, tag, rider=None):
        mod = adaln_fwd(c16, wada, bada, tag + "_adaln")[:nb]
        shift, scale, gate = (mod[:, k * d:(k + 1) * d].reshape(nb, 1, d) for k in range(3))
        qkv, hmod = normmod_matmul(xin, g, scale, shift, win, BF16, seq, True, tag + "_qkv", (0, 3 * di))
        if b_f is not None:
            (zf,) = normmod_matmul(xin, g, scale, shift, _pad_cols(win[:, 3 * di:], di + 128), F32, seq, False, tag + "_z")
        else:
            (zf,) = normmod_matmul(xin, g, scale, shift, win, F32, seq, False, tag + "_z", (3 * di, di))
        saved = dict(x=xin, g=g, scale=scale, gate=gate, qkv=qkv, h=hmod, zf=zf, win=win, wout=wout)
        if b_f is not None:
            fl_t = jnp.transpose(zf[:, di:di + h].reshape(nb, seq, h), (0, 2, 1))
            bf_col = b_f.reshape(h, 1)
            cum = forget_cumsum(fl_t, bf_col, tag + "_cum")
            ck = cum.reshape(nb, h, seq // tq, tq)
            o, lse, *rode = fox_fwd(qkv, ck, seq, di, dh, tag + "_attn", rider)
            saved.update(fl_t=fl_t, bf_col=bf_col, ck=ck, lse=lse, rode=rode)
        else:
            o, rtot = sb_fwd(qkv, seq, di, dh, tag + "_attn")
            saved.update(rtot=rtot)
        xout, y, u = out_proj_fwd(o, zf, wout, xin, gate, seq, tag + "_out")
        saved.update(o=o, y=y, u=u)
        return xout, saved

    def layer_bwd(dxo, sv, is_fox, tag, riding=()):
        do, dz, dy, dgate = out_proj_bwd(dxo, sv["y"], sv["gate"], sv["wout"], sv["o"], sv["zf"], seq, tag + "_dout")
        wout_blocks = tn_matmul(sv["u"], dy, tag + "_dwout").reshape(NDEV, di // NDEV, d)
        extra = None
        if is_fox:
            dq, dk, dv, dck, *rode = fox_bwd(sv["qkv"], sv["ck"], sv["o"], do, sv["lse"], seq, di, dh, tag + "_dattn",
                                             Exchange("scatter", list(riding) + [wout_blocks]))
            df_t, dbf = forget_cumsum_bwd(dck.reshape(nb, h, seq), sv["fl_t"], sv["bf_col"], tag + "_dcum")
            df = _pad_cols(jnp.transpose(df_t, (0, 2, 1)).reshape(t, h), 128).astype(BF16)
            extra = (df, jnp.transpose(_pad_cols(sv["win"][:, 4 * di:], 128)))
        else:
            dq, dk, dv = sb_bwd(sv["qkv"], do, sv["rtot"], seq, di, dh, tag + "_dattn")
        parts = [dq, dk, dv, dz]
        dwin = [tn_matmul(sv["h"], p, tag + "_dwin%d" % k) for k, p in enumerate(parts)]
        if is_fox:
            dwin.append(tn_matmul(sv["h"], df, tag + "_dwinf")[:, :h])
        dwin = jnp.concatenate(dwin, axis=1)
        w_t = jnp.transpose(sv["win"][:, :4 * di])
        win_blocks = _col_blocks(dwin)
        dh_rider = Exchange("scatter", [win_blocks]) if is_fox else None
        dxin, psum, ssum, *landed = dh_norm_bwd(parts, w_t, extra, sv["x"], sv["g"], sv["scale"], dxo, seq,
                                                tag + "_dh", dh_rider)
        pad = lambda a: _pad_rows16(a.reshape(nb, d))
        dwada, dbada, dng = adaln_bwd(c16, pad(ssum), pad(psum), pad(dgate), pad(sv["scale"]), sv["g"], tag + "_dadaln")
        grads = dict(wada=dwada, bada=dbada, ng=dng, win=win_blocks, wout=wout_blocks)
        if is_fox:
            grads["bf"] = dbf.reshape(1, h)
            grads["rode"] = rode
            grads["landed"] = landed
        return dxin, grads

    x1, sv_fox = layer_fwd(x0, fox_norm_g, fox_wada, fox_b_ada, fox_win, fox_wout, fox_b_f, "fox", sb_gather)
    sb_wada, sb_win = _cols(sv_fox["rode"][0]), _cols(sv_fox["rode"][1])
    sb_wout = sv_fox["rode"][2].reshape(di, d)
    sb_g, sb_bada = _cols(sv_fox["rode"][3]), _cols(sv_fox["rode"][4])
    x2, sv_sb = layer_fwd(x1, sb_g, sb_wada, sb_bada, sb_win, sb_wout, None, "sb")
    dx2, dgf, sq = final_loss(x2, final_norm_g.reshape(1, d), target, "loss_head")
    dx1, g_sb = layer_bwd(dx2, sv_sb, False, "sb")
    dx0, g_fox = layer_bwd(dx1, sv_fox, True, "fox", [g_sb["wada"], g_sb["win"], g_sb["wout"]])

    sb_wada_land, sb_win_land, sb_wout_land, fox_wout_land = g_fox["rode"]
    (fox_wada_land,) = all_to_all([g_fox["wada"]], "scatter_fox_wada_grad")
    land = [fox_wada_land, g_fox["landed"][0], fox_wout_land, sb_wada_land, sb_win_land, sb_wout_land]
    small = jnp.concatenate([g_fox["ng"], g_fox["bada"], _pad_cols(g_fox["bf"], 128), g_sb["ng"], g_sb["bada"],
                             dgf, sq], axis=1)
    (small_all,) = all_gather([small], "gather_small")
    small_sum, loss_row = sum_slots(small_all, d, "sum_small")
    loss = loss_row[0, 0]

    offs = {}
    pos = 0
    for nm, width in (("fox_ng", d), ("fox_bada", 3 * d), ("fox_bf", 128), ("sb_ng", d), ("sb_bada", 3 * d), ("fin_g", d)):
        offs[nm] = (pos, width)
        pos += width

    def small_grad(nm, width=None, shard=False):
        p0, wd = offs[nm]
        wd = width or wd
        if shard:
            blk = wd // NDEV
            return lax.dynamic_slice(small_sum, (0, p0 + me * blk), (1, blk))
        return small_sum[:, p0:p0 + wd]

    results = {}
    big_params = [("fox_w_ada", fox_w_ada, m_fox_w_ada, v_fox_w_ada), ("fox_w_in", fox_w_in, m_fox_w_in, v_fox_w_in),
                  ("fox_w_out", fox_w_out, m_fox_w_out, v_fox_w_out), ("sb_w_ada", sb_w_ada, m_sb_w_ada, v_sb_w_ada),
                  ("sb_w_in", sb_w_in, m_sb_w_in, v_sb_w_in), ("sb_w_out", sb_w_out, m_sb_w_out, v_sb_w_out)]
    for k, (nm, w, m, v) in enumerate(big_params):
        results[nm] = adamw(land[k], w, m, v, "adamw_" + nm)
    small_params = [("fox_norm_g", fox_norm_g, m_fox_norm_g, v_fox_norm_g, small_grad("fox_ng")),
                    ("fox_b_ada", fox_b_ada, m_fox_b_ada, v_fox_b_ada, small_grad("fox_bada")),
                    ("fox_b_f", fox_b_f, m_fox_b_f, v_fox_b_f, small_grad("fox_bf", h)),
                    ("sb_norm_g", sb_norm_g, m_sb_norm_g, v_sb_norm_g, small_grad("sb_ng", shard=True)),
                    ("sb_b_ada", sb_b_ada, m_sb_b_ada, v_sb_b_ada, small_grad("sb_bada", shard=True)),
                    ("final_norm_g", final_norm_g.reshape(1, d), m_final_norm_g.reshape(1, d),
                     v_final_norm_g.reshape(1, d), small_grad("fin_g"))]
    for nm, w, m, v, g in small_params:
        outs = adamw(g[None], w, m, v, "adamw_" + nm)
        if nm == "final_norm_g":
            outs = [o.reshape(d) for o in outs]
        results[nm] = outs

    order = ["fox_norm_g", "fox_w_ada", "fox_b_ada", "fox_w_in", "fox_b_f", "fox_w_out", "sb_norm_g", "sb_w_ada",
             "sb_b_ada", "sb_w_in", "sb_w_out", "final_norm_g"]
    out = [loss, dx0.reshape(nb, seq, d)]
    for k in range(4):
        out += [results[nm][k] for nm in order]
    return tuple(out)
```

```python
import functools
import math

import jax
import jax.numpy as jnp
from jax import lax
from jax.experimental import pallas as pl
from jax.experimental.pallas import tpu as pltpu

F32 = jnp.float32
BF16 = jnp.bfloat16
NDEV = 8
VMEM_LIMIT = 56 * 1024 * 1024
NORM_EPS = 1e-6
ADAM_LR, ADAM_B1, ADAM_B2, ADAM_EPS, ADAM_WD, ADAM_STEP = 0.001, 0.9, 0.999, 1e-08, 0.01, 10
ATTN_LANES = 128
ATTN_BLOCKS_PER_TRIP = 2
FOX_Q_ROWS = 1024
EXP_IS_ZERO_BELOW = -104.0
MESH = pl.DeviceIdType.MESH
ANY = pl.BlockSpec(memory_space=pl.ANY)


def _cp(*sem):
    return pltpu.CompilerParams(dimension_semantics=sem, vmem_limit_bytes=VMEM_LIMIT)


def _my_index():
    return 4 * lax.axis_index("x") + 2 * lax.axis_index("y") + lax.axis_index("c")


def _flip(k):
    x, y, c = lax.axis_index("x"), lax.axis_index("y"), lax.axis_index("c")
    kx, ky, kc = (k >> 2) & 1, (k >> 1) & 1, k & 1
    px = 1 - x if kx else x
    py = 1 - y if ky else y
    pc = 1 - c if kc else c
    return (px, py, pc), 4 * px + 2 * py + pc


class Exchange:
    def __init__(self, mode, arrs):
        self.gather = mode == "gather"
        self.arrs = list(arrs)
        self.n = len(self.arrs)
        self.in_specs = [ANY] * self.n
        self.out_specs = [ANY] * self.n
        self.out_shape = [jax.ShapeDtypeStruct(((NDEV,) + a.shape) if self.gather else a.shape, a.dtype)
                          for a in self.arrs]
        self.scratch = [pltpu.SemaphoreType.DMA((self.n * (NDEV - 1),)), pltpu.SemaphoreType.DMA((self.n * (NDEV - 1),)),
                        pltpu.SemaphoreType.DMA((self.n,))]

    def copies(self, ins, outs, sems):
        send_sems, recv_sems, local_sems = sems
        me = _my_index()
        out = []
        for a in range(self.n):
            out.append(pltpu.make_async_copy(ins[a] if self.gather else ins[a].at[me], outs[a].at[me], local_sems.at[a]))
            for k in range(1, NDEV):
                peer, pidx = _flip(k)
                out.append(pltpu.make_async_remote_copy(
                    src_ref=ins[a] if self.gather else ins[a].at[pidx], dst_ref=outs[a].at[me],
                    send_sem=send_sems.at[a * (NDEV - 1) + k - 1], recv_sem=recv_sems.at[a * (NDEV - 1) + k - 1],
                    device_id=peer, device_id_type=MESH))
        return out

    def run(self, name):
        n = self.n

        def body(*refs):
            cps = self.copies(refs[:n], refs[n:2 * n], refs[2 * n:])
            for cp in cps:
                cp.start()
            for cp in cps:
                cp.wait()

        return pl.pallas_call(body, name=name, out_shape=self.out_shape, in_specs=self.in_specs,
                              out_specs=self.out_specs, scratch_shapes=self.scratch)(*self.arrs)


def _ride(exchange, refs, first, last):
    n = exchange.n
    cps = exchange.copies(refs[:n], refs[n:2 * n], refs[2 * n:])

    @pl.when(first)
    def _():
        for cp in cps:
            cp.start()

    @pl.when(last)
    def _():
        for cp in cps:
            cp.wait()


def all_gather(arrs, name):
    return Exchange("gather", arrs).run(name)


def all_gather_by_chip(arrs, name):
    n = len(arrs)
    chips = (2, 4, 6)

    def body(*refs):
        ins, outs = refs[:n], refs[n:2 * n]
        send_sems, recv_sems, local_sems = refs[2 * n:]
        me = _my_index()
        sibling, _ = _flip(1)

        def copy(a, k, block, to, src=None):
            return pltpu.make_async_remote_copy(
                src_ref=outs[a].at[block] if src is None else src, dst_ref=outs[a].at[block],
                send_sem=send_sems.at[7 * a + k], recv_sem=recv_sems.at[7 * a + k], device_id=to, device_id_type=MESH)

        own, sent = [], []
        for a in range(n):
            own.append(pltpu.make_async_copy(ins[a], outs[a].at[me], local_sems.at[a]))
            own[-1].start()
            first = [copy(a, 0, me, sibling, ins[a])] + [copy(a, 1 + j, me, _flip(k)[0], ins[a]) for j, k in enumerate(chips)]
            for cp in first:
                cp.start()
            sent += first
        for a in range(n):
            for j, k in enumerate(chips):
                peer, pidx = _flip(k)
                copy(a, 1 + j, pidx, peer).wait_recv()
                sent.append(copy(a, 4 + j, pidx, sibling))
                sent[-1].start()
        for a in range(n):
            copy(a, 0, _flip(1)[1], sibling).wait_recv()
            for j, k in enumerate(chips):
                copy(a, 4 + j, _flip(k + 1)[1], sibling).wait_recv()
        for cp in sent:
            cp.wait_send()
        for cp in own:
            cp.wait()

    return pl.pallas_call(
        body, name=name,
        out_shape=[jax.ShapeDtypeStruct((NDEV,) + a.shape, a.dtype) for a in arrs],
        in_specs=[ANY] * n, out_specs=[ANY] * n,
        scratch_shapes=[pltpu.SemaphoreType.DMA((7 * n,)), pltpu.SemaphoreType.DMA((7 * n,)),
                        pltpu.SemaphoreType.DMA((n,))],
    )(*arrs)


def all_to_all(arrs, name):
    return Exchange("scatter", arrs).run(name)


def _tile(n, pref):
    t = min(n, pref)
    while n % t:
        t //= 2
    return t


def _col_tile(n, pref):
    t = _tile(n, pref)
    return t if t % 128 == 0 and t >= 512 else n


def adamw(land, w, m, v, name):
    slots, r, c = land.shape
    tr = _tile(r, 64)
    bc1 = 1.0 - ADAM_B1 ** ADAM_STEP
    bc2 = 1.0 - ADAM_B2 ** ADAM_STEP
    lead = w.ndim - 2

    def body(land_ref, w_ref, m_ref, v_ref, g_ref, d_ref, nm_ref, nv_ref):
        at = (0,) * lead + (Ellipsis,)
        g = land_ref[0].astype(F32)
        for s in range(1, slots):
            g = g + land_ref[s].astype(F32)
        nm = ADAM_B1 * m_ref[at] + (1.0 - ADAM_B1) * g
        nv = ADAM_B2 * v_ref[at] + (1.0 - ADAM_B2) * (g * g)
        m_hat = nm / bc1
        v_hat = nv / bc2
        g_ref[at] = g
        nm_ref[at] = nm
        nv_ref[at] = nv
        d_ref[at] = -ADAM_LR * (m_hat / (jnp.sqrt(v_hat) + ADAM_EPS) + ADAM_WD * w_ref[at])

    blk = pl.BlockSpec((1,) * lead + (tr, c), lambda i: (0,) * lead + (i, 0))
    return pl.pallas_call(
        body, name=name, grid=(r // tr,),
        in_specs=[pl.BlockSpec((slots, tr, c), lambda i: (0, i, 0)), blk, blk, blk],
        out_specs=[blk] * 4,
        out_shape=[jax.ShapeDtypeStruct(w.shape, F32)] * 4,
        compiler_params=_cp("parallel"),
    )(land, w, m, v)


def sum_slots(land, d_model, name):
    slots, _, n = land.shape

    def body(land_ref, o_ref, loss_ref):
        g = land_ref[0]
        for s in range(1, slots):
            g = g + land_ref[s]
        o_ref[...] = g
        sq = jnp.sum(g[:, n - d_model:], axis=-1, keepdims=True)
        loss_ref[...] = jnp.broadcast_to(sq * (0.5 / d_model), (1, 128))

    return pl.pallas_call(
        body, name=name,
        out_shape=[jax.ShapeDtypeStruct((1, n), F32), jax.ShapeDtypeStruct((1, 128), F32)],
    )(land)


def _sigmoid(x):
    return 1.0 / (1.0 + jnp.exp(-x))


def adaln_fwd(c16, w_ada, b_ada, name):
    d3 = w_ada.shape[1]

    def body(c_ref, w_ref, b_ref, o_ref):
        cc = c_ref[...]
        sc = (cc * _sigmoid(cc)).astype(BF16)
        o_ref[...] = jnp.dot(sc, w_ref[...], preferred_element_type=F32) + b_ref[...]

    return pl.pallas_call(body, name=name, out_shape=jax.ShapeDtypeStruct((16, d3), F32),
                          compiler_params=pltpu.CompilerParams(vmem_limit_bytes=VMEM_LIMIT))(c16, w_ada, b_ada)


def adaln_bwd(c16, dshift16, p16, dgate16, scale16, g, name):
    d = c16.shape[1]
    nb = 3 * d // NDEV

    def body(c_ref, ds_ref, p_ref, dg_ref, sc_ref, g_ref, dw_ref, db_ref, dng_ref, dmod_scr):
        j = pl.program_id(0)

        @pl.when(j == 0)
        def _():
            p = p_ref[...]
            dmod = jnp.concatenate([ds_ref[...], p * g_ref[...], dg_ref[...]], axis=-1)
            dmod_scr[...] = dmod
            db_ref[...] = jnp.sum(dmod, axis=0, keepdims=True)
            dng_ref[...] = jnp.sum((1.0 + sc_ref[...]) * p, axis=0, keepdims=True)

        cc = c_ref[...]
        sc = (cc * _sigmoid(cc)).astype(BF16)
        start = pl.multiple_of(j * nb, 128)
        dm = dmod_scr[:, pl.ds(start, nb)].astype(BF16)
        dw_ref[0] = lax.dot_general(sc, dm, (((0,), (0,)), ((), ())), preferred_element_type=F32).astype(BF16)

    full = lambda shape: pl.BlockSpec(shape, lambda j: (0,) * len(shape))
    return pl.pallas_call(
        body, name=name, grid=(NDEV,),
        in_specs=[full((16, d))] * 5 + [full((1, d))],
        out_specs=[pl.BlockSpec((1, d, nb), lambda j: (j, 0, 0)), full((1, 3 * d)), full((1, d))],
        out_shape=[jax.ShapeDtypeStruct((NDEV, d, nb), BF16), jax.ShapeDtypeStruct((1, 3 * d), F32),
                   jax.ShapeDtypeStruct((1, d), F32)],
        scratch_shapes=[pltpu.VMEM((16, 3 * d), F32)],
        compiler_params=_cp("arbitrary"),
    )(c16, dshift16, p16, dgate16, scale16, g)


def _modulated_norm(x, g, scale, shift):
    rstd = lax.rsqrt(jnp.mean(x * x, axis=-1, keepdims=True) + NORM_EPS)
    return ((x * rstd) * g) * (1.0 + scale) + shift


def normmod_matmul(x, g, scale, shift, w, out_dtype, seq, emit_h, name, cols=None):
    t, d = x.shape
    col0, n = cols or (0, w.shape[1])
    tm = _tile(seq, 512)
    tn = _col_tile(n, 2048)
    assert col0 % tn == 0
    jb = col0 // tn
    per_seq = seq // tm

    def body(x_ref, g_ref, sc_ref, sh_ref, w_ref, *rest):
        if emit_h:
            o_ref, h_ref, h_scr = rest
        else:
            o_ref, h_scr = rest

        @pl.when(pl.program_id(1) == 0)
        def _():
            h = _modulated_norm(x_ref[...], g_ref[...], sc_ref[0], sh_ref[0]).astype(BF16)
            h_scr[...] = h
            if emit_h:
                h_ref[...] = h

        o_ref[...] = jnp.dot(h_scr[...], w_ref[...], preferred_element_type=F32).astype(out_dtype)

    mod_spec = pl.BlockSpec((1, 1, d), lambda i, j: (i // per_seq, 0, 0))
    out_specs = [pl.BlockSpec((tm, tn), lambda i, j: (i, j))]
    out_shape = [jax.ShapeDtypeStruct((t, n), out_dtype)]
    if emit_h:
        out_specs.append(pl.BlockSpec((tm, d), lambda i, j: (i, 0)))
        out_shape.append(jax.ShapeDtypeStruct((t, d), BF16))
    return pl.pallas_call(
        body, name=name, grid=(t // tm, n // tn),
        in_specs=[pl.BlockSpec((tm, d), lambda i, j: (i, 0)), pl.BlockSpec((1, d), lambda i, j: (0, 0)),
                  mod_spec, mod_spec, pl.BlockSpec((d, tn), lambda i, j: (0, jb + j))],
        out_specs=out_specs, out_shape=out_shape,
        scratch_shapes=[pltpu.VMEM((tm, d), BF16)],
        compiler_params=_cp("parallel", "arbitrary"),
    )(x, g, scale, shift, w)


def out_proj_fwd(o, zf, w_out, x, gate, seq, name):
    t, di = o.shape
    d = x.shape[1]
    tm = _tile(seq, 256)
    per_seq = seq // tm

    def body(o_ref, z_ref, w_ref, x_ref, gt_ref, xn_ref, y_ref, u_ref):
        z = z_ref[...]
        u = (o_ref[...] * (z * _sigmoid(z))).astype(BF16)
        y = jnp.dot(u, w_ref[...], preferred_element_type=F32)
        u_ref[...] = u
        y_ref[...] = y
        xn_ref[...] = x_ref[...] + gt_ref[0] * y

    row = lambda c: pl.BlockSpec((tm, c), lambda i: (i, 0))
    return pl.pallas_call(
        body, name=name, grid=(t // tm,),
        in_specs=[row(di), row(di), pl.BlockSpec((di, d), lambda i: (0, 0)), row(d),
                  pl.BlockSpec((1, 1, d), lambda i: (i // per_seq, 0, 0))],
        out_specs=[row(d), row(d), row(di)],
        out_shape=[jax.ShapeDtypeStruct((t, d), F32), jax.ShapeDtypeStruct((t, d), F32),
                   jax.ShapeDtypeStruct((t, di), BF16)],
        compiler_params=_cp("parallel"),
    )(o, zf, w_out, x, gate)


def out_proj_bwd(dxo, y, gate, w_out, o, zf, seq, name):
    t, d = dxo.shape
    di = o.shape[1]
    tm = _tile(seq, 256)
    per_seq = seq // tm

    def body(dx_ref, y_ref, gt_ref, w_ref, o_ref, z_ref, do_ref, dz_ref, dy_ref, dgt_ref):
        dx = dx_ref[...]
        part = jnp.sum(dx * y_ref[...], axis=0, keepdims=True)

        @pl.when(pl.program_id(0) % per_seq == 0)
        def _():
            dgt_ref[0] = part

        @pl.when(pl.program_id(0) % per_seq != 0)
        def _():
            dgt_ref[0] += part

        dy = (dx * gt_ref[0]).astype(BF16)
        dy_ref[...] = dy
        du = lax.dot_general(dy, w_ref[...], (((1,), (1,)), ((), ())), preferred_element_type=F32)
        z = z_ref[...]
        sg = _sigmoid(z)
        do_ref[...] = (du * (z * sg)).astype(BF16)
        dz_ref[...] = (du * o_ref[...] * (sg * (1.0 + z * (1.0 - sg)))).astype(BF16)

    row = lambda c: pl.BlockSpec((tm, c), lambda i: (i, 0))
    seq_spec = pl.BlockSpec((1, 1, d), lambda i: (i // per_seq, 0, 0))
    return pl.pallas_call(
        body, name=name, grid=(t // tm,),
        in_specs=[row(d), row(d), seq_spec, pl.BlockSpec((di, d), lambda i: (0, 0)), row(di), row(di)],
        out_specs=[row(di), row(di), row(d), seq_spec],
        out_shape=[jax.ShapeDtypeStruct((t, di), BF16), jax.ShapeDtypeStruct((t, di), BF16),
                   jax.ShapeDtypeStruct((t, d), BF16), jax.ShapeDtypeStruct((t // seq, 1, d), F32)],
        compiler_params=_cp("arbitrary"),
    )(dxo, y, gate, w_out, o, zf)


def tn_matmul(a, b, name):
    t, m = a.shape
    n = b.shape[1]
    tn = _col_tile(n, 1024)
    tk = _tile(t, 1024)
    nk = t // tk

    def body(a_ref, b_ref, o_ref, acc):
        part = lax.dot_general(a_ref[...], b_ref[...], (((0,), (0,)), ((), ())), preferred_element_type=F32)
        k = pl.program_id(1)

        @pl.when(k == 0)
        def _():
            acc[...] = part

        @pl.when(k != 0)
        def _():
            acc[...] += part

        @pl.when(k == nk - 1)
        def _():
            o_ref[...] = acc[...].astype(BF16)

    return pl.pallas_call(
        body, name=name, grid=(n // tn, nk),
        in_specs=[pl.BlockSpec((tk, m), lambda j, k: (k, 0)), pl.BlockSpec((tk, tn), lambda j, k: (k, j))],
        out_specs=pl.BlockSpec((m, tn), lambda j, k: (0, j)),
        out_shape=jax.ShapeDtypeStruct((m, n), BF16),
        scratch_shapes=[pltpu.VMEM((m, tn), F32)],
        compiler_params=_cp("parallel", "arbitrary"),
    )(a, b)


def dh_norm_bwd(parts, w_t, extra, x, g, scale, dxo, seq, name, rider=None):
    t, d = x.shape
    nparts = len(parts)
    kw = parts[0].shape[1]
    tm = _tile(seq, 512)
    per_seq = seq // tm
    has_extra = extra is not None
    grid = (t // tm,)
    nr = rider.n if rider else 0
    n_in = nparts + 1 + (2 if has_extra else 0) + 4

    def body(*refs):
        p_refs = refs[:nparts]
        w_ref = refs[nparts]
        pos = nparts + 1
        if has_extra:
            e_ref, we_ref = refs[pos], refs[pos + 1]
            pos += 2
        x_ref, g_ref, sc_ref, dxo_ref = refs[pos:pos + 4]
        dx_ref, pp_ref, ss_ref = refs[n_in + nr:n_in + nr + 3]
        if rider:
            _ride(rider, refs[n_in:n_in + nr] + refs[n_in + nr + 3:], *_grid_ends(grid))
        i = pl.program_id(0)

        dh = jnp.dot(p_refs[0][...], w_ref[0:kw, :], preferred_element_type=F32)
        for kk in range(1, nparts):
            dh = dh + jnp.dot(p_refs[kk][...], w_ref[kk * kw:(kk + 1) * kw, :], preferred_element_type=F32)
        if has_extra:
            dh = dh + jnp.dot(e_ref[...], we_ref[...], preferred_element_type=F32)
        xx = x_ref[...]
        rstd = lax.rsqrt(jnp.mean(xx * xx, axis=-1, keepdims=True) + NORM_EPS)
        xhat = xx * rstd
        dxhat = dh * (g_ref[...] * (1.0 + sc_ref[0]))
        dx_ref[...] = dxo_ref[...] + rstd * (dxhat - xhat * jnp.mean(dxhat * xhat, axis=-1, keepdims=True))
        pp = jnp.sum(dh * xhat, axis=0, keepdims=True)
        ss = jnp.sum(dh, axis=0, keepdims=True)

        @pl.when(i % per_seq == 0)
        def _():
            pp_ref[0] = pp
            ss_ref[0] = ss

        @pl.when(i % per_seq != 0)
        def _():
            pp_ref[0] += pp
            ss_ref[0] += ss

    row = lambda c: pl.BlockSpec((tm, c), lambda i: (i, 0))
    whole = lambda a: pl.BlockSpec(a.shape, lambda i: (0, 0), pipeline_mode=pl.Buffered(1))
    seq_spec = pl.BlockSpec((1, 1, d), lambda i: (i // per_seq, 0, 0))
    in_specs = [row(kw)] * nparts + [whole(w_t)]
    args = list(parts) + [w_t]
    if has_extra:
        in_specs += [row(extra[0].shape[1]), whole(extra[1])]
        args += list(extra)
    in_specs += [row(d), pl.BlockSpec((1, d), lambda i: (0, 0)), seq_spec, row(d)]
    args += [x, g, scale, dxo]
    return pl.pallas_call(
        body, name=name, grid=grid,
        in_specs=in_specs + (rider.in_specs if rider else []),
        out_specs=[row(d), seq_spec, seq_spec] + (rider.out_specs if rider else []),
        out_shape=[jax.ShapeDtypeStruct((t, d), F32), jax.ShapeDtypeStruct((t // seq, 1, d), F32),
                   jax.ShapeDtypeStruct((t // seq, 1, d), F32)] + (rider.out_shape if rider else []),
        scratch_shapes=rider.scratch if rider else [],
        compiler_params=_cp("arbitrary"),
    )(*args, *(rider.arrs if rider else []))


def final_loss(x, g, target, name):
    t, d = x.shape
    tm = _tile(t, 512)

    def body(x_ref, g_ref, t_ref, dx_ref, dg_ref, sq_ref):
        xx = x_ref[...]
        gg = g_ref[...]
        rstd = lax.rsqrt(jnp.mean(xx * xx, axis=-1, keepdims=True) + NORM_EPS)
        xhat = xx * rstd
        err = xhat * gg - t_ref[...]
        dy = err * (1.0 / d)
        dxhat = dy * gg
        dx_ref[...] = rstd * (dxhat - xhat * jnp.mean(dxhat * xhat, axis=-1, keepdims=True))
        dg = jnp.sum(dy * xhat, axis=0, keepdims=True)
        sq = jnp.sum(err * err, axis=0, keepdims=True)

        @pl.when(pl.program_id(0) == 0)
        def _():
            dg_ref[...] = dg
            sq_ref[...] = sq

        @pl.when(pl.program_id(0) != 0)
        def _():
            dg_ref[...] += dg
            sq_ref[...] += sq

    row = pl.BlockSpec((tm, d), lambda i: (i, 0))
    vec = pl.BlockSpec((1, d), lambda i: (0, 0))
    return pl.pallas_call(
        body, name=name, grid=(t // tm,),
        in_specs=[row, vec, row], out_specs=[row, vec, vec],
        out_shape=[jax.ShapeDtypeStruct((t, d), F32), jax.ShapeDtypeStruct((1, d), F32),
                   jax.ShapeDtypeStruct((1, d), F32)],
        compiler_params=_cp("arbitrary"),
    )(x, g, target)


def _split3(x):
    x1 = x.astype(BF16)
    r1 = x - x1.astype(F32)
    x2 = r1.astype(BF16)
    x3 = (r1 - x2.astype(F32)).astype(BF16)
    return x1, x2, x3


def _dot3(x, u):
    x1, x2, x3 = _split3(x)
    return (jnp.dot(x1, u, preferred_element_type=F32) + jnp.dot(x2, u, preferred_element_type=F32)
            + jnp.dot(x3, u, preferred_element_type=F32))


def _log_sigmoid(x):
    return jnp.minimum(x, 0.0) - jnp.log1p(jnp.exp(-jnp.abs(x)))


def forget_cumsum(fl_t, b_f, name):
    b, h, s = fl_t.shape

    def body(f_ref, b_ref, o_ref):
        lf = _log_sigmoid(f_ref[0] + b_ref[...])
        u = (lax.broadcasted_iota(jnp.int32, (s, s), 0) <= lax.broadcasted_iota(jnp.int32, (s, s), 1)).astype(BF16)
        o_ref[0] = _dot3(lf, u)

    return pl.pallas_call(
        body, name=name, grid=(b,),
        in_specs=[pl.BlockSpec((1, h, s), lambda i: (i, 0, 0)), pl.BlockSpec((h, 1), lambda i: (0, 0))],
        out_specs=pl.BlockSpec((1, h, s), lambda i: (i, 0, 0)),
        out_shape=jax.ShapeDtypeStruct((b, h, s), F32),
        compiler_params=_cp("parallel"),
    )(fl_t, b_f)


def forget_cumsum_bwd(dcum, fl_t, b_f, name):
    b, h, s = fl_t.shape

    def body(d_ref, f_ref, b_ref, o_ref, db_ref):
        u = (lax.broadcasted_iota(jnp.int32, (s, s), 0) >= lax.broadcasted_iota(jnp.int32, (s, s), 1)).astype(BF16)
        dlf = _dot3(d_ref[0], u)
        df = dlf * _sigmoid(-(f_ref[0] + b_ref[...]))
        o_ref[0] = df
        part = jnp.sum(df, axis=-1, keepdims=True)

        @pl.when(pl.program_id(0) == 0)
        def _():
            db_ref[...] = part

        @pl.when(pl.program_id(0) != 0)
        def _():
            db_ref[...] += part

    blk = pl.BlockSpec((1, h, s), lambda i: (i, 0, 0))
    return pl.pallas_call(
        body, name=name, grid=(b,),
        in_specs=[blk, blk, pl.BlockSpec((h, 1), lambda i: (0, 0))],
        out_specs=[blk, pl.BlockSpec((h, 1), lambda i: (0, 0))],
        out_shape=[jax.ShapeDtypeStruct((b, h, s), F32), jax.ShapeDtypeStruct((h, 1), F32)],
        compiler_params=_cp("arbitrary"),
    )(dcum, fl_t, b_f)


def _nt(a, b):
    return lax.dot_general(a, b, (((1,), (1,)), ((), ())), preferred_element_type=F32)


def _attn_dims(t, seq, dh, q_rows=512):
    tq = _tile(seq, q_rows)
    tk = _tile(seq, 256)
    return t // seq, tq, tk, seq // tq, tq // tk, ATTN_LANES // dh


def _blocks_per_trip(r):
    return ATTN_BLOCKS_PER_TRIP if r % ATTN_BLOCKS_PER_TRIP == 0 else 1


def _scaled_q(q_ref, dh):
    scale = dh ** -0.5
    if math.log2(dh) % 2 == 0:
        return (q_ref[...].astype(F32) * scale).astype(BF16), None
    return q_ref[...], scale


def _diag_masks(tq, tk, r, strict):
    row = lax.broadcasted_iota(jnp.int32, (tq, tk), 0)
    col = lax.broadcasted_iota(jnp.int32, (tq, tk), 1)
    return [(col + d * tk < row) if strict else (col + d * tk <= row) for d in range(r)]


def _rows(state, lo):
    return tuple(a[lo:] for a in state)


def _put_rows(state, part, lo):
    if lo == 0:
        return tuple(part)
    return tuple(jnp.concatenate([a[:lo], p], axis=0) for a, p in zip(state, part))


def _transposed(x):
    return jnp.transpose(x.astype(F32)).astype(BF16)


def _store_transposed(dst_ref, src_scr, tk):
    for jb in range(src_scr.shape[0]):
        dst_ref[jb * tk:(jb + 1) * tk, :] = jnp.transpose(src_scr[jb]).astype(BF16)


def _dot2(x, u):
    hi = x.astype(BF16)
    lo = (x - hi.astype(F32)).astype(BF16)
    return jnp.dot(jnp.concatenate([hi, lo], axis=1), jnp.concatenate([u, u], axis=0), preferred_element_type=F32)


def _attn_specs(seq, di, tq):
    nq = seq // tq
    cb = di // ATTN_LANES
    q_spec = pl.BlockSpec((tq, ATTN_LANES), lambda b, hp, i: (b * nq + i, hp))
    k_spec = pl.BlockSpec((seq, ATTN_LANES), lambda b, hp, i: (b, cb + hp))
    v_spec = pl.BlockSpec((seq, ATTN_LANES), lambda b, hp, i: (b, 2 * cb + hp))
    kv_out = pl.BlockSpec((seq, ATTN_LANES), lambda b, hp, i: (b, hp))
    return q_spec, k_spec, v_spec, kv_out


def _grid_ends(grid):
    ids = [pl.program_id(a) for a in range(len(grid))]
    first = functools.reduce(jnp.logical_and, [p == 0 for p in ids])
    last = functools.reduce(jnp.logical_and, [p == g - 1 for p, g in zip(ids, grid)])
    return first, last


def fox_fwd(qkv, ck, seq, di, dh, name, rider=None):
    t = qkv.shape[0]
    nb, tq, tk, nq, r, hpb = _attn_dims(t, seq, dh, FOX_Q_ROWS)
    per_trip = _blocks_per_trip(r)
    heads = [slice(hh * dh, (hh + 1) * dh) for hh in range(hpb)]
    assert hpb >= 2, "the row sums of p ride in another head's lanes of the p @ v matmul"
    q_spec, k_spec, v_spec, _ = _attn_specs(seq, di, tq)
    grid = (nb, di // ATTN_LANES, nq)
    nr = rider.n if rider else 0

    def body(q_ref, k_ref, v_ref, ck_ref, *rest):
        o_ref, lse_ref = rest[nr:nr + 2]
        knorm_ref = rest[2 * nr + 2]
        if rider:
            _ride(rider, rest[:nr] + rest[nr + 2:2 * nr + 2] + rest[2 * nr + 3:], *_grid_ends(grid))
        i = pl.program_id(2)
        masks = _diag_masks(tq, tk, r, False)
        q, scale = _scaled_q(q_ref, dh)
        qs = [q[:, hd] for hd in heads]
        lane_head = lax.broadcasted_iota(jnp.int32, (tk, ATTN_LANES), 1) // dh

        def largest_norm(x):
            x = x.astype(F32)
            return jnp.sqrt(jnp.max(jnp.sum(x * x, axis=-1, keepdims=True)))

        @pl.when(i == 0)
        def _():
            for hh in range(hpb):
                knorm_ref[hh] = largest_norm(k_ref[:, heads[hh]])

        qk_bound = [largest_norm(qs[hh]) * knorm_ref[hh] * (1.0 if scale is None else scale) + 1.0 for hh in range(hpb)]

        def scores(hh, j):
            start = pl.multiple_of(j * tk, tk)
            kh = k_ref[pl.ds(start, tk), heads[hh]]
            vt = v_ref[pl.ds(start, tk), :]
            vh = jnp.where(lane_head == hh, vt, jnp.ones_like(vt))
            s = _nt(qs[hh], kh)
            if scale is not None:
                s = s * scale
            return s - ck_ref[0, hh, pl.ds(j, 1), :], vh

        def update(state, s, vh):
            m, acc = state
            m_new = jnp.maximum(m, jnp.max(s, axis=-1, keepdims=True))
            p = jnp.exp(s - m_new)
            return m_new, jnp.exp(m - m_new) * acc + _dot2(p, vh)

        states = []
        for hh in range(hpb):
            s, vh = scores(hh, r * i)
            s = jnp.where(masks[0], s, -jnp.inf)
            m = jnp.max(s, axis=-1, keepdims=True)
            states.append((m, _dot2(jnp.exp(s - m), vh)))
        for d in range(1, r):
            for hh in range(hpb):
                s, vh = scores(hh, r * i + d)
                states[hh] = update(states[hh], jnp.where(masks[d], s, -jnp.inf), vh)

        def any_weight_left(states, j):
            jc = jnp.maximum(j, 0)
            tops = [jnp.max(-ck_ref[0, hh, pl.ds(jc, 1), :]) + qk_bound[hh] - jnp.min(states[hh][0]) for hh in range(hpb)]
            done = functools.reduce(jnp.logical_and, [top <= EXP_IS_ZERO_BELOW for top in tops])
            return jnp.logical_not(done).astype(jnp.int32)

        def step(carry):
            jj, _, states = carry
            for sub in range(per_trip):
                states = tuple(update(states[hh], *scores(hh, r * i - 1 - (per_trip * jj + sub))) for hh in range(hpb))
            return jj + 1, any_weight_left(states, r * i - 1 - per_trip * (jj + 1)), states

        trips, _, states = lax.while_loop(lambda c: jnp.logical_and(c[0] < r * i // per_trip, c[1] > 0), step,
                                          (jnp.int32(0), any_weight_left(states, r * i - 1), tuple(states)))
        for hh in range(hpb):
            m, acc = states[hh]
            other = heads[(hh + 1) % hpb].start
            l = acc[:, other:other + 1]
            o_ref[:, heads[hh]] = acc[:, heads[hh]] / l
            lse_ref[0, 0, :, hh:hh + 1] = m + jnp.log(l)
        lse_ref[0, 0, :, hpb:hpb + 1] = jnp.full((tq, 1), (trips * per_trip).astype(F32))

    return pl.pallas_call(
        body, name=name, grid=grid,
        in_specs=[q_spec, k_spec, v_spec,
                  pl.BlockSpec((1, hpb, seq // tk, tk), lambda b, hp, i: (b, hp, 0, 0))] + (rider.in_specs if rider else []),
        out_specs=[q_spec, pl.BlockSpec((1, 1, tq, hpb + 1), lambda b, hp, i: (b, hp, i, 0))]
        + (rider.out_specs if rider else []),
        out_shape=[jax.ShapeDtypeStruct((t, di), F32), jax.ShapeDtypeStruct((nb, di // ATTN_LANES, seq, hpb + 1), F32)]
        + (rider.out_shape if rider else []),
        scratch_shapes=[pltpu.SMEM((hpb,), F32)] + (rider.scratch if rider else []),
        compiler_params=_cp("arbitrary", "arbitrary", "arbitrary"),
    )(qkv, qkv, qkv, ck, *(rider.arrs if rider else []))


def fox_bwd(qkv, ck, o, do, lse, seq, di, dh, name, rider=None):
    t = qkv.shape[0]
    nb, tq, tk, nq, r, hpb = _attn_dims(t, seq, dh, FOX_Q_ROWS)
    per_trip = _blocks_per_trip(r)
    heads = [slice(hh * dh, (hh + 1) * dh) for hh in range(hpb)]
    q_spec, k_spec, v_spec, kv_out = _attn_specs(seq, di, tq)
    ck_spec = pl.BlockSpec((1, hpb, seq // tk, tk), lambda b, hp, i: (b, hp, 0, 0))
    grid = (nb, di // ATTN_LANES, nq)
    nr = rider.n if rider else 0

    def body(q_ref, k_ref, v_ref, ck_ref, o_ref, do_ref, lse_ref, *rest):
        dq_ref, dk_ref, dv_ref, dck_ref = rest[nr:nr + 4]
        dkt_scr, dvt_scr = rest[2 * nr + 4:2 * nr + 6]
        if rider:
            _ride(rider, rest[:nr] + rest[nr + 4:2 * nr + 4] + rest[2 * nr + 6:], *_grid_ends(grid))
        i = pl.program_id(2)

        @pl.when(i == 0)
        def _():
            dkt_scr[...] = jnp.zeros_like(dkt_scr)
            dvt_scr[...] = jnp.zeros_like(dvt_scr)
            dck_ref[...] = jnp.zeros_like(dck_ref)

        masks = _diag_masks(tq, tk, r, False)
        q, scale = _scaled_q(q_ref, dh)
        do = do_ref[...]
        q_t, do_t = _transposed(q), _transposed(do)
        qs = [q[:, hd] for hd in heads]
        dos = [do[:, hd] for hd in heads]
        deltas = [jnp.sum(dos[hh].astype(F32) * o_ref[:, heads[hh]], axis=-1, keepdims=True) for hh in range(hpb)]
        lses = [lse_ref[0, 0, :, hh:hh + 1] for hh in range(hpb)]

        def block(hh, j, mask, dq_acc, lo=0):
            start = pl.multiple_of(j * tk, tk)
            kh = k_ref[pl.ds(start, tk), heads[hh]]
            vh = v_ref[pl.ds(start, tk), heads[hh]]
            s = _nt(qs[hh][lo:], kh)
            if scale is not None:
                s = s * scale
            p = jnp.exp(s - ck_ref[0, hh, pl.ds(j, 1), :] - lses[hh][lo:])
            if mask is not None:
                p = jnp.where(mask[lo:], p, 0.0)
            ds = p * (_nt(dos[hh][lo:], vh) - deltas[hh][lo:])
            dsb = ds.astype(BF16)
            dkt = jnp.dot(q_t[heads[hh], lo:], dsb, preferred_element_type=F32)
            if scale is not None:
                dkt = dkt * scale
            dkt_scr[j, heads[hh], :] += dkt
            dvt_scr[j, heads[hh], :] += jnp.dot(do_t[heads[hh], lo:], p.astype(BF16), preferred_element_type=F32)
            dck_ref[0, hh, pl.ds(j, 1), :] -= jnp.sum(ds, axis=0, keepdims=True)
            return dq_acc + jnp.dot(dsb, kh, preferred_element_type=F32)

        accs = [jnp.zeros((tq, dh), F32)] * hpb
        for d in range(r):
            lo = d * tk
            accs = [_put_rows((accs[hh],), (block(hh, r * i + d, masks[d], accs[hh][lo:], lo),), lo)[0]
                    for hh in range(hpb)]

        trips = jnp.clip(jnp.max(lse_ref[0, 0, :, hpb:hpb + 1]).astype(jnp.int32), 0, r * i) // per_trip
        first = r * i - trips * per_trip

        def step(jj, accs):
            for sub in range(per_trip):
                accs = tuple(block(hh, first + per_trip * jj + sub, None, accs[hh]) for hh in range(hpb))
            return accs

        accs = lax.fori_loop(0, trips, step, tuple(accs))
        for hh in range(hpb):
            dq_ref[:, heads[hh]] = (accs[hh] * dh ** -0.5).astype(BF16)

        @pl.when(i == nq - 1)
        def _():
            _store_transposed(dk_ref, dkt_scr, tk)
            _store_transposed(dv_ref, dvt_scr, tk)

    return pl.pallas_call(
        body, name=name, grid=grid,
        in_specs=[q_spec, k_spec, v_spec, ck_spec, q_spec, q_spec,
                  pl.BlockSpec((1, 1, tq, hpb + 1), lambda b, hp, i: (b, hp, i, 0))] + (rider.in_specs if rider else []),
        out_specs=[q_spec, kv_out, kv_out, ck_spec] + (rider.out_specs if rider else []),
        out_shape=[jax.ShapeDtypeStruct((t, di), BF16)] * 3 + [jax.ShapeDtypeStruct(ck.shape, F32)]
        + (rider.out_shape if rider else []),
        scratch_shapes=[pltpu.VMEM((seq // tk, ATTN_LANES, tk), F32), pltpu.VMEM((seq // tk, ATTN_LANES, tk), F32)]
        + (rider.scratch if rider else []),
        compiler_params=_cp("arbitrary", "arbitrary", "arbitrary"),
    )(qkv, qkv, qkv, ck, o, do, lse, *(rider.arrs if rider else []))


def _sb_logits(qh, kh, scale, strict):
    z = _nt(qh, kh)
    if scale is not None:
        z = z * scale
    e = jnp.exp(-jnp.abs(z))
    lb = jnp.minimum(z, 0.0) - jnp.log(1.0 + e)
    lk = lb - z
    return lb, lk if strict is None else jnp.where(strict, lk, 0.0)


def _cum(x, u):
    return jnp.dot(x.astype(BF16), u, preferred_element_type=F32)


def sb_fwd(qkv, seq, di, dh, name):
    t = qkv.shape[0]
    nb, tq, tk, nq, r, hpb = _attn_dims(t, seq, dh)
    heads = [slice(hh * dh, (hh + 1) * dh) for hh in range(hpb)]
    q_spec, k_spec, v_spec, _ = _attn_specs(seq, di, tq)

    def body(q_ref, k_ref, v_ref, o_ref, rt_ref):
        i = pl.program_id(2)
        masks = _diag_masks(tq, tk, r, True)
        row = lax.broadcasted_iota(jnp.int32, (tk, tk), 0)
        col = lax.broadcasted_iota(jnp.int32, (tk, tk), 1)
        u_after = (row > col).astype(BF16)
        q, scale = _scaled_q(q_ref, dh)
        qs = [q[:, hd] for hd in heads]

        def block(hh, j, mask, state, lo=0):
            rr, acc = state
            start = pl.multiple_of(j * tk, tk)
            kh = k_ref[pl.ds(start, tk), heads[hh]]
            vh = v_ref[pl.ds(start, tk), heads[hh]]
            if mask is not None:
                mask = mask[lo:]
            lb, lk = _sb_logits(qs[hh][lo:], kh, scale, mask)
            a = jnp.exp(lb + _cum(lk, u_after) + rr)
            if mask is not None:
                a = jnp.where(mask, a, 0.0)
            acc = acc + jnp.dot(a.astype(BF16), vh, preferred_element_type=F32)
            return rr + jnp.sum(lk, axis=-1, keepdims=True), acc

        states = [(jnp.zeros((tq, 1), F32), jnp.zeros((tq, dh), F32))] * hpb
        for d in reversed(range(r)):
            lo = d * tk
            states = [_put_rows(states[hh], block(hh, r * i + d, masks[d], _rows(states[hh], lo), lo), lo)
                      for hh in range(hpb)]

        def any_weight_left(states):
            done = functools.reduce(jnp.logical_and, [jnp.max(st[0]) <= EXP_IS_ZERO_BELOW for st in states])
            return jnp.logical_not(done).astype(jnp.int32)

        def step(carry):
            jj, _, states = carry
            states = tuple(block(hh, r * i - 1 - jj, None, states[hh]) for hh in range(hpb))
            return jj + 1, any_weight_left(states), states

        visited, _, states = lax.while_loop(lambda c: jnp.logical_and(c[0] < r * i, c[1] > 0), step,
                                            (jnp.int32(0), any_weight_left(states), tuple(states)))
        for hh in range(hpb):
            o_ref[:, heads[hh]] = states[hh][1]
            rt_ref[0, 0, :, hh:hh + 1] = states[hh][0]
        rt_ref[0, 0, :, hpb:hpb + 1] = jnp.full((tq, 1), visited.astype(F32))

    return pl.pallas_call(
        body, name=name, grid=(nb, di // ATTN_LANES, nq),
        in_specs=[q_spec, k_spec, v_spec],
        out_specs=[q_spec, pl.BlockSpec((1, 1, tq, hpb + 1), lambda b, hp, i: (b, hp, i, 0))],
        out_shape=[jax.ShapeDtypeStruct((t, di), F32), jax.ShapeDtypeStruct((nb, di // ATTN_LANES, seq, hpb + 1), F32)],
        compiler_params=_cp("parallel", "parallel", "arbitrary"),
    )(qkv, qkv, qkv)


def sb_bwd(qkv, do, rtot, seq, di, dh, name):
    t = qkv.shape[0]
    nb, tq, tk, nq, r, hpb = _attn_dims(t, seq, dh)
    heads = [slice(hh * dh, (hh + 1) * dh) for hh in range(hpb)]
    q_spec, k_spec, v_spec, kv_out = _attn_specs(seq, di, tq)

    def body(q_ref, k_ref, v_ref, do_ref, rt_ref, dq_ref, dk_ref, dv_ref, dkt_scr, dvt_scr):
        i = pl.program_id(2)

        @pl.when(i == 0)
        def _():
            dkt_scr[...] = jnp.zeros_like(dkt_scr)
            dvt_scr[...] = jnp.zeros_like(dvt_scr)

        masks = _diag_masks(tq, tk, r, True)
        row = lax.broadcasted_iota(jnp.int32, (tk, tk), 0)
        col = lax.broadcasted_iota(jnp.int32, (tk, tk), 1)
        u_after = (row > col).astype(BF16)
        u_before = (row < col).astype(BF16)
        q, scale = _scaled_q(q_ref, dh)
        do = do_ref[...]
        q_t, do_t = _transposed(q), _transposed(do)
        qs = [q[:, hd] for hd in heads]
        dos = [do[:, hd] for hd in heads]
        rts = [rt_ref[0, 0, :, hh:hh + 1] for hh in range(hpb)]

        def block(hh, j, mask, state, lo=0):
            lc, gc, dq_acc = state
            start = pl.multiple_of(j * tk, tk)
            kh = k_ref[pl.ds(start, tk), heads[hh]]
            vh = v_ref[pl.ds(start, tk), heads[hh]]
            if mask is not None:
                mask = mask[lo:]
            lb, lk = _sb_logits(qs[hh][lo:], kh, scale, mask)
            lc = lc + jnp.sum(lk, axis=-1, keepdims=True)
            a = jnp.exp(lb + ((rts[hh][lo:] - lc) + _cum(lk, u_after)))
            if mask is not None:
                a = jnp.where(mask, a, 0.0)
            de = a * _nt(dos[hh][lo:], vh)
            g = gc + _cum(de, u_before)
            dz = de - jnp.exp(lb) * (de + g)
            if mask is not None:
                dz = jnp.where(mask, dz, 0.0)
            dzb = dz.astype(BF16)
            dkt = jnp.dot(q_t[heads[hh], lo:], dzb, preferred_element_type=F32)
            if scale is not None:
                dkt = dkt * scale
            dkt_scr[j, heads[hh], :] += dkt
            dvt_scr[j, heads[hh], :] += jnp.dot(do_t[heads[hh], lo:], a.astype(BF16), preferred_element_type=F32)
            return (lc, gc + jnp.sum(de, axis=-1, keepdims=True),
                    dq_acc + jnp.dot(dzb, kh, preferred_element_type=F32))

        zero = jnp.zeros((tq, 1), F32)
        visited = jnp.clip(jnp.max(rt_ref[0, 0, :, hpb:hpb + 1]).astype(jnp.int32), 0, r * i)
        first = r * i - visited

        def step(jj, states):
            return tuple(block(hh, first + jj, None, states[hh]) for hh in range(hpb))

        states = lax.fori_loop(0, visited, step, ((zero, zero, jnp.zeros((tq, dh), F32)),) * hpb)
        for d in range(r):
            lo = d * tk
            states = [_put_rows(states[hh], block(hh, r * i + d, masks[d], _rows(states[hh], lo), lo), lo)
                      for hh in range(hpb)]
        for hh in range(hpb):
            dq_ref[:, heads[hh]] = (states[hh][2] * dh ** -0.5).astype(BF16)

        @pl.when(i == nq - 1)
        def _():
            _store_transposed(dk_ref, dkt_scr, tk)
            _store_transposed(dv_ref, dvt_scr, tk)

    return pl.pallas_call(
        body, name=name, grid=(nb, di // ATTN_LANES, nq),
        in_specs=[q_spec, k_spec, v_spec, q_spec, pl.BlockSpec((1, 1, tq, hpb + 1), lambda b, hp, i: (b, hp, i, 0))],
        out_specs=[q_spec, kv_out, kv_out],
        out_shape=[jax.ShapeDtypeStruct((t, di), BF16)] * 3,
        scratch_shapes=[pltpu.VMEM((seq // tk, ATTN_LANES, tk), F32), pltpu.VMEM((seq // tk, ATTN_LANES, tk), F32)],
        compiler_params=_cp("parallel", "parallel", "arbitrary"),
    )(qkv, qkv, qkv, do, rtot)


def _cols(g):
    return jnp.transpose(g, (1, 0, 2)).reshape(g.shape[1], NDEV * g.shape[2])


def _col_blocks(w):
    r, c8 = w.shape
    return jnp.transpose(w.reshape(r, NDEV, c8 // NDEV), (1, 0, 2))


def _pad_rows16(a):
    return jnp.pad(a, ((0, 16 - a.shape[0]), (0, 0)))


def _pad_cols(a, n):
    return jnp.pad(a, ((0, 0), (0, n - a.shape[1])))


def kernel(x, c, fox_norm_g, fox_w_ada, fox_b_ada, fox_w_in, fox_b_f, fox_w_out, sb_norm_g, sb_w_ada, sb_b_ada, sb_w_in, sb_w_out, final_norm_g, loss_target, m_fox_norm_g, m_fox_w_ada, m_fox_b_ada, m_fox_w_in, m_fox_b_f, m_fox_w_out, m_sb_norm_g, m_sb_w_ada, m_sb_b_ada, m_sb_w_in, m_sb_w_out, m_final_norm_g, v_fox_norm_g, v_fox_w_ada, v_fox_b_ada, v_fox_w_in, v_fox_b_f, v_fox_w_out, v_sb_norm_g, v_sb_w_ada, v_sb_b_ada, v_sb_w_in, v_sb_w_out, v_final_norm_g):
    nb, seq, d = x.shape
    t = nb * seq
    h = fox_b_f.shape[-1]
    di = fox_w_out.shape[1] * NDEV
    dh = di // h
    tk = _attn_dims(t, seq, dh)[2]
    me = _my_index()

    gathered = all_gather_by_chip([w[0].astype(BF16) for w in (fox_w_ada, fox_w_in, fox_w_out)], "gather_fox_weights")
    fox_wada, fox_win = _cols(gathered[0]), _cols(gathered[1])
    fox_wout = gathered[2].reshape(di, d)
    sb_gather = Exchange("gather", [w[0].astype(BF16) for w in (sb_w_ada, sb_w_in, sb_w_out)] + [sb_norm_g, sb_b_ada])

    x0 = x.reshape(t, d)
    target = loss_target.reshape(t, d)
    c16 = _pad_rows16(c)

    def layer_fwd(xin, g, wada, bada, win, wout, b_f, tag, rider=None):
        mod = adaln_fwd(c16, wada, bada, tag + "_adaln")[:nb]
        shift, scale, gate = (mod[:, k * d:(k + 1) * d].reshape(nb, 1, d) for k in range(3))
        qkv, hmod = normmod_matmul(xin, g, scale, shift, win, BF16, seq, True, tag + "_qkv", (0, 3 * di))
        if b_f is not None:
            (zf,) = normmod_matmul(xin, g, scale, shift, _pad_cols(win[:, 3 * di:], di + 128), F32, seq, False, tag + "_z")
        else:
            (zf,) = normmod_matmul(xin, g, scale, shift, win, F32, seq, False, tag + "_z", (3 * di, di))
        saved = dict(x=xin, g=g, scale=scale, gate=gate, qkv=qkv, h=hmod, zf=zf, win=win, wout=wout)
        if b_f is not None:
            fl_t = jnp.transpose(zf[:, di:di + h].reshape(nb, seq, h), (0, 2, 1))
            bf_col = b_f.reshape(h, 1)
            cum = forget_cumsum(fl_t, bf_col, tag + "_cum")
            ck = cum.reshape(nb, h, seq // tk, tk)
            o, lse, *rode = fox_fwd(qkv, ck, seq, di, dh, tag + "_attn", rider)
            saved.update(fl_t=fl_t, bf_col=bf_col, ck=ck, lse=lse, rode=rode)
        else:
            o, rtot = sb_fwd(qkv, seq, di, dh, tag + "_attn")
            saved.update(rtot=rtot)
        xout, y, u = out_proj_fwd(o, zf, wout, xin, gate, seq, tag + "_out")
        saved.update(o=o, y=y, u=u)
        return xout, saved

    def layer_bwd(dxo, sv, is_fox, tag, riding=()):
        do, dz, dy, dgate = out_proj_bwd(dxo, sv["y"], sv["gate"], sv["wout"], sv["o"], sv["zf"], seq, tag + "_dout")
        wout_blocks = tn_matmul(sv["u"], dy, tag + "_dwout").reshape(NDEV, di // NDEV, d)
        extra = None
        if is_fox:
            dq, dk, dv, dck, *rode = fox_bwd(sv["qkv"], sv["ck"], sv["o"], do, sv["lse"], seq, di, dh, tag + "_dattn",
                                             Exchange("scatter", list(riding) + [wout_blocks]))
            df_t, dbf = forget_cumsum_bwd(dck.reshape(nb, h, seq), sv["fl_t"], sv["bf_col"], tag + "_dcum")
            df = _pad_cols(jnp.transpose(df_t, (0, 2, 1)).reshape(t, h), 128).astype(BF16)
            extra = (df, jnp.transpose(_pad_cols(sv["win"][:, 4 * di:], 128)))
        else:
            dq, dk, dv = sb_bwd(sv["qkv"], do, sv["rtot"], seq, di, dh, tag + "_dattn")
        parts = [dq, dk, dv, dz]
        dwin = [tn_matmul(sv["h"], p, tag + "_dwin%d" % k) for k, p in enumerate(parts)]
        if is_fox:
            dwin.append(tn_matmul(sv["h"], df, tag + "_dwinf")[:, :h])
        dwin = jnp.concatenate(dwin, axis=1)
        w_t = jnp.transpose(sv["win"][:, :4 * di])
        win_blocks = _col_blocks(dwin)
        dh_rider = Exchange("scatter", [win_blocks]) if is_fox else None
        dxin, psum, ssum, *landed = dh_norm_bwd(parts, w_t, extra, sv["x"], sv["g"], sv["scale"], dxo, seq,
                                                tag + "_dh", dh_rider)
        pad = lambda a: _pad_rows16(a.reshape(nb, d))
        dwada, dbada, dng = adaln_bwd(c16, pad(ssum), pad(psum), pad(dgate), pad(sv["scale"]), sv["g"], tag + "_dadaln")
        grads = dict(wada=dwada, bada=dbada, ng=dng, win=win_blocks, wout=wout_blocks)
        if is_fox:
            grads["bf"] = dbf.reshape(1, h)
            grads["rode"] = rode
            grads["landed"] = landed
        return dxin, grads

    x1, sv_fox = layer_fwd(x0, fox_norm_g, fox_wada, fox_b_ada, fox_win, fox_wout, fox_b_f, "fox", sb_gather)
    sb_wada, sb_win = _cols(sv_fox["rode"][0]), _cols(sv_fox["rode"][1])
    sb_wout = sv_fox["rode"][2].reshape(di, d)
    sb_g, sb_bada = _cols(sv_fox["rode"][3]), _cols(sv_fox["rode"][4])
    x2, sv_sb = layer_fwd(x1, sb_g, sb_wada, sb_bada, sb_win, sb_wout, None, "sb")
    dx2, dgf, sq = final_loss(x2, final_norm_g.reshape(1, d), target, "loss_head")
    dx1, g_sb = layer_bwd(dx2, sv_sb, False, "sb")
    dx0, g_fox = layer_bwd(dx1, sv_fox, True, "fox", [g_sb["wada"], g_sb["win"], g_sb["wout"]])

    sb_wada_land, sb_win_land, sb_wout_land, fox_wout_land = g_fox["rode"]
    (fox_wada_land,) = all_to_all([g_fox["wada"]], "scatter_fox_wada_grad")
    land = [fox_wada_land, g_fox["landed"][0], fox_wout_land, sb_wada_land, sb_win_land, sb_wout_land]
    small = jnp.concatenate([g_fox["ng"], g_fox["bada"], _pad_cols(g_fox["bf"], 128), g_sb["ng"], g_sb["bada"],
                             dgf, sq], axis=1)
    (small_all,) = all_gather([small], "gather_small")
    small_sum, loss_row = sum_slots(small_all, d, "sum_small")
    loss = loss_row[0, 0]

    offs = {}
    pos = 0
    for nm, width in (("fox_ng", d), ("fox_bada", 3 * d), ("fox_bf", 128), ("sb_ng", d), ("sb_bada", 3 * d), ("fin_g", d)):
        offs[nm] = (pos, width)
        pos += width

    def small_grad(nm, width=None, shard=False):
        p0, wd = offs[nm]
        wd = width or wd
        if shard:
            blk = wd // NDEV
            return lax.dynamic_slice(small_sum, (0, p0 + me * blk), (1, blk))
        return small_sum[:, p0:p0 + wd]

    results = {}
    big_params = [("fox_w_ada", fox_w_ada, m_fox_w_ada, v_fox_w_ada), ("fox_w_in", fox_w_in, m_fox_w_in, v_fox_w_in),
                  ("fox_w_out", fox_w_out, m_fox_w_out, v_fox_w_out), ("sb_w_ada", sb_w_ada, m_sb_w_ada, v_sb_w_ada),
                  ("sb_w_in", sb_w_in, m_sb_w_in, v_sb_w_in), ("sb_w_out", sb_w_out, m_sb_w_out, v_sb_w_out)]
    for k, (nm, w, m, v) in enumerate(big_params):
        results[nm] = adamw(land[k], w, m, v, "adamw_" + nm)
    small_params = [("fox_norm_g", fox_norm_g, m_fox_norm_g, v_fox_norm_g, small_grad("fox_ng")),
                    ("fox_b_ada", fox_b_ada, m_fox_b_ada, v_fox_b_ada, small_grad("fox_bada")),
                    ("fox_b_f", fox_b_f, m_fox_b_f, v_fox_b_f, small_grad("fox_bf", h)),
                    ("sb_norm_g", sb_norm_g, m_sb_norm_g, v_sb_norm_g, small_grad("sb_ng", shard=True)),
                    ("sb_b_ada", sb_b_ada, m_sb_b_ada, v_sb_b_ada, small_grad("sb_bada", shard=True)),
                    ("final_norm_g", final_norm_g.reshape(1, d), m_final_norm_g.reshape(1, d),
                     v_final_norm_g.reshape(1, d), small_grad("fin_g"))]
    for nm, w, m, v, g in small_params:
        outs = adamw(g[None], w, m, v, "adamw_" + nm)
        if nm == "final_norm_g":
            outs = [o.reshape(d) for o in outs]
        results[nm] = outs

    order = ["fox_norm_g", "fox_w_ada", "fox_b_ada", "fox_w_in", "fox_b_f", "fox_w_out", "sb_norm_g", "sb_w_ada",
             "sb_b_ada", "sb_w_in", "sb_w_out", "final_norm_g"]
    out = [loss, dx0.reshape(nb, seq, d)]
    for k in range(4):
        out += [results[nm][k] for nm in order]
    return tuple(out)
```

```python
import functools
import math

import jax
import jax.numpy as jnp
from jax import lax
from jax.experimental import pallas as pl
from jax.experimental.pallas import tpu as pltpu

F32 = jnp.float32
BF16 = jnp.bfloat16
NDEV = 8
VMEM_LIMIT = 56 * 1024 * 1024
NORM_EPS = 1e-6
ADAM_LR, ADAM_B1, ADAM_B2, ADAM_EPS, ADAM_WD, ADAM_STEP = 0.001, 0.9, 0.999, 1e-08, 0.01, 10
ATTN_LANES = 128
ATTN_BLOCKS_PER_TRIP = 2
FOX_Q_ROWS = 1024
EXP_IS_ZERO_BELOW = -104.0
MESH = pl.DeviceIdType.MESH
ANY = pl.BlockSpec(memory_space=pl.ANY)


def _cp(*sem):
    return pltpu.CompilerParams(dimension_semantics=sem, vmem_limit_bytes=VMEM_LIMIT)


def _my_index():
    return 4 * lax.axis_index("x") + 2 * lax.axis_index("y") + lax.axis_index("c")


def _flip(k):
    x, y, c = lax.axis_index("x"), lax.axis_index("y"), lax.axis_index("c")
    kx, ky, kc = (k >> 2) & 1, (k >> 1) & 1, k & 1
    px = 1 - x if kx else x
    py = 1 - y if ky else y
    pc = 1 - c if kc else c
    return (px, py, pc), 4 * px + 2 * py + pc


class Exchange:
    def __init__(self, mode, arrs):
        self.gather = mode == "gather"
        self.arrs = list(arrs)
        self.n = len(self.arrs)
        self.in_specs = [ANY] * self.n
        self.out_specs = [ANY] * self.n
        self.out_shape = [jax.ShapeDtypeStruct(((NDEV,) + a.shape) if self.gather else a.shape, a.dtype)
                          for a in self.arrs]
        self.scratch = [pltpu.SemaphoreType.DMA((self.n * (NDEV - 1),)), pltpu.SemaphoreType.DMA((self.n * (NDEV - 1),)),
                        pltpu.SemaphoreType.DMA((self.n,))]

    def copies(self, ins, outs, sems):
        send_sems, recv_sems, local_sems = sems
        me = _my_index()
        out = []
        for a in range(self.n):
            out.append(pltpu.make_async_copy(ins[a] if self.gather else ins[a].at[me], outs[a].at[me], local_sems.at[a]))
            for k in range(1, NDEV):
                peer, pidx = _flip(k)
                out.append(pltpu.make_async_remote_copy(
                    src_ref=ins[a] if self.gather else ins[a].at[pidx], dst_ref=outs[a].at[me],
                    send_sem=send_sems.at[a * (NDEV - 1) + k - 1], recv_sem=recv_sems.at[a * (NDEV - 1) + k - 1],
                    device_id=peer, device_id_type=MESH))
        return out

    def run(self, name):
        n = self.n

        def body(*refs):
            cps = self.copies(refs[:n], refs[n:2 * n], refs[2 * n:])
            for cp in cps:
                cp.start()
            for cp in cps:
                cp.wait()

        return pl.pallas_call(body, name=name, out_shape=self.out_shape, in_specs=self.in_specs,
                              out_specs=self.out_specs, scratch_shapes=self.scratch)(*self.arrs)


def _ride(exchange, refs, first, last):
    n = exchange.n
    cps = exchange.copies(refs[:n], refs[n:2 * n], refs[2 * n:])

    @pl.when(first)
    def _():
        for cp in cps:
            cp.start()

    @pl.when(last)
    def _():
        for cp in cps:
            cp.wait()


def all_gather(arrs, name):
    return Exchange("gather", arrs).run(name)


def all_gather_by_chip(arrs, name):
    n = len(arrs)
    chips = (2, 4, 6)

    def body(*refs):
        ins, outs = refs[:n], refs[n:2 * n]
        send_sems, recv_sems, local_sems = refs[2 * n:]
        me = _my_index()
        sibling, _ = _flip(1)

        def copy(a, k, block, to, src=None):
            return pltpu.make_async_remote_copy(
                src_ref=outs[a].at[block] if src is None else src, dst_ref=outs[a].at[block],
                send_sem=send_sems.at[7 * a + k], recv_sem=recv_sems.at[7 * a + k], device_id=to, device_id_type=MESH)

        own, sent = [], []
        for a in range(n):
            own.append(pltpu.make_async_copy(ins[a], outs[a].at[me], local_sems.at[a]))
            own[-1].start()
            first = [copy(a, 0, me, sibling, ins[a])] + [copy(a, 1 + j, me, _flip(k)[0], ins[a]) for j, k in enumerate(chips)]
            for cp in first:
                cp.start()
            sent += first
        for a in range(n):
            for j, k in enumerate(chips):
                peer, pidx = _flip(k)
                copy(a, 1 + j, pidx, peer).wait_recv()
                sent.append(copy(a, 4 + j, pidx, sibling))
                sent[-1].start()
        for a in range(n):
            copy(a, 0, _flip(1)[1], sibling).wait_recv()
            for j, k in enumerate(chips):
                copy(a, 4 + j, _flip(k + 1)[1], sibling).wait_recv()
        for cp in sent:
            cp.wait_send()
        for cp in own:
            cp.wait()

    return pl.pallas_call(
        body, name=name,
        out_shape=[jax.ShapeDtypeStruct((NDEV,) + a.shape, a.dtype) for a in arrs],
        in_specs=[ANY] * n, out_specs=[ANY] * n,
        scratch_shapes=[pltpu.SemaphoreType.DMA((7 * n,)), pltpu.SemaphoreType.DMA((7 * n,)),
                        pltpu.SemaphoreType.DMA((n,))],
    )(*arrs)


def all_to_all(arrs, name):
    return Exchange("scatter", arrs).run(name)


def _tile(n, pref):
    t = min(n, pref)
    while n % t:
        t //= 2
    return t


def _col_tile(n, pref):
    t = _tile(n, pref)
    return t if t % 128 == 0 and t >= 512 else n


def adamw(land, w, m, v, name):
    slots, r, c = land.shape
    tr = _tile(r, 64)
    bc1 = 1.0 - ADAM_B1 ** ADAM_STEP
    bc2 = 1.0 - ADAM_B2 ** ADAM_STEP
    lead = w.ndim - 2

    def body(land_ref, w_ref, m_ref, v_ref, g_ref, d_ref, nm_ref, nv_ref):
        at = (0,) * lead + (Ellipsis,)
        g = land_ref[0].astype(F32)
        for s in range(1, slots):
            g = g + land_ref[s].astype(F32)
        nm = ADAM_B1 * m_ref[at] + (1.0 - ADAM_B1) * g
        nv = ADAM_B2 * v_ref[at] + (1.0 - ADAM_B2) * (g * g)
        m_hat = nm / bc1
        v_hat = nv / bc2
        g_ref[at] = g
        nm_ref[at] = nm
        nv_ref[at] = nv
        d_ref[at] = -ADAM_LR * (m_hat / (jnp.sqrt(v_hat) + ADAM_EPS) + ADAM_WD * w_ref[at])

    blk = pl.BlockSpec((1,) * lead + (tr, c), lambda i: (0,) * lead + (i, 0))
    return pl.pallas_call(
        body, name=name, grid=(r // tr,),
        in_specs=[pl.BlockSpec((slots, tr, c), lambda i: (0, i, 0)), blk, blk, blk],
        out_specs=[blk] * 4,
        out_shape=[jax.ShapeDtypeStruct(w.shape, F32)] * 4,
        compiler_params=_cp("parallel"),
    )(land, w, m, v)


def sum_slots(land, d_model, name):
    slots, _, n = land.shape

    def body(land_ref, o_ref, loss_ref):
        g = land_ref[0]
        for s in range(1, slots):
            g = g + land_ref[s]
        o_ref[...] = g
        sq = jnp.sum(g[:, n - d_model:], axis=-1, keepdims=True)
        loss_ref[...] = jnp.broadcast_to(sq * (0.5 / d_model), (1, 128))

    return pl.pallas_call(
        body, name=name,
        out_shape=[jax.ShapeDtypeStruct((1, n), F32), jax.ShapeDtypeStruct((1, 128), F32)],
    )(land)


def _sigmoid(x):
    return 1.0 / (1.0 + jnp.exp(-x))


def adaln_fwd(c16, w_ada, b_ada, name):
    d3 = w_ada.shape[1]

    def body(c_ref, w_ref, b_ref, o_ref):
        cc = c_ref[...]
        sc = (cc * _sigmoid(cc)).astype(BF16)
        o_ref[...] = jnp.dot(sc, w_ref[...], preferred_element_type=F32) + b_ref[...]

    return pl.pallas_call(body, name=name, out_shape=jax.ShapeDtypeStruct((16, d3), F32),
                          compiler_params=pltpu.CompilerParams(vmem_limit_bytes=VMEM_LIMIT))(c16, w_ada, b_ada)


def adaln_bwd(c16, dshift16, p16, dgate16, scale16, g, name, emit_dw=True):
    d = c16.shape[1]
    nb = 3 * d // NDEV

    def body(c_ref, ds_ref, p_ref, dg_ref, sc_ref, g_ref, first_ref, db_ref, dng_ref, *scratch):
        j = pl.program_id(0)
        dmod_ref = scratch[0] if emit_dw else first_ref

        @pl.when(j == 0)
        def _():
            p = p_ref[...]
            dmod = jnp.concatenate([ds_ref[...], p * g_ref[...], dg_ref[...]], axis=-1)
            dmod_ref[...] = dmod
            db_ref[...] = jnp.sum(dmod, axis=0, keepdims=True)
            dng_ref[...] = jnp.sum((1.0 + sc_ref[...]) * p, axis=0, keepdims=True)

        if emit_dw:
            cc = c_ref[...]
            sc = (cc * _sigmoid(cc)).astype(BF16)
            start = pl.multiple_of(j * nb, 128)
            dm = dmod_ref[:, pl.ds(start, nb)].astype(BF16)
            first_ref[0] = lax.dot_general(sc, dm, (((0,), (0,)), ((), ())), preferred_element_type=F32).astype(BF16)

    full = lambda shape: pl.BlockSpec(shape, lambda j: (0,) * len(shape))
    first_spec = pl.BlockSpec((1, d, nb), lambda j: (j, 0, 0)) if emit_dw else full((16, 3 * d))
    first_shape = jax.ShapeDtypeStruct((NDEV, d, nb), BF16) if emit_dw else jax.ShapeDtypeStruct((16, 3 * d), F32)
    return pl.pallas_call(
        body, name=name, grid=(NDEV if emit_dw else 1,),
        in_specs=[full((16, d))] * 5 + [full((1, d))],
        out_specs=[first_spec, full((1, 3 * d)), full((1, d))],
        out_shape=[first_shape, jax.ShapeDtypeStruct((1, 3 * d), F32), jax.ShapeDtypeStruct((1, d), F32)],
        scratch_shapes=[pltpu.VMEM((16, 3 * d), F32)] if emit_dw else [],
        compiler_params=_cp("arbitrary"),
    )(c16, dshift16, p16, dgate16, scale16, g)


def dwada_columns(c_all, dmod_cols, name):
    d = c_all.shape[1]

    def body(c_ref, dm_ref, o_ref):
        cc = c_ref[...]
        sc = (cc * _sigmoid(cc)).astype(BF16)
        o_ref[0] = lax.dot_general(sc, dm_ref[...].astype(BF16), (((0,), (0,)), ((), ())), preferred_element_type=F32)

    return pl.pallas_call(body, name=name, out_shape=jax.ShapeDtypeStruct((1, d, dmod_cols.shape[1]), F32),
                          compiler_params=pltpu.CompilerParams(vmem_limit_bytes=VMEM_LIMIT))(c_all, dmod_cols)


def _modulated_norm(x, g, scale, shift):
    rstd = lax.rsqrt(jnp.mean(x * x, axis=-1, keepdims=True) + NORM_EPS)
    return ((x * rstd) * g) * (1.0 + scale) + shift


def normmod_matmul(x, g, scale, shift, w, out_dtype, seq, emit_h, name, cols=None):
    t, d = x.shape
    col0, n = cols or (0, w.shape[1])
    tm = _tile(seq, 512)
    tn = _col_tile(n, 2048)
    assert col0 % tn == 0
    jb = col0 // tn
    per_seq = seq // tm

    def body(x_ref, g_ref, sc_ref, sh_ref, w_ref, *rest):
        if emit_h:
            o_ref, h_ref, h_scr = rest
        else:
            o_ref, h_scr = rest

        @pl.when(pl.program_id(1) == 0)
        def _():
            h = _modulated_norm(x_ref[...], g_ref[...], sc_ref[0], sh_ref[0]).astype(BF16)
            h_scr[...] = h
            if emit_h:
                h_ref[...] = h

        o_ref[...] = jnp.dot(h_scr[...], w_ref[...], preferred_element_type=F32).astype(out_dtype)

    mod_spec = pl.BlockSpec((1, 1, d), lambda i, j: (i // per_seq, 0, 0))
    out_specs = [pl.BlockSpec((tm, tn), lambda i, j: (i, j))]
    out_shape = [jax.ShapeDtypeStruct((t, n), out_dtype)]
    if emit_h:
        out_specs.append(pl.BlockSpec((tm, d), lambda i, j: (i, 0)))
        out_shape.append(jax.ShapeDtypeStruct((t, d), BF16))
    return pl.pallas_call(
        body, name=name, grid=(t // tm, n // tn),
        in_specs=[pl.BlockSpec((tm, d), lambda i, j: (i, 0)), pl.BlockSpec((1, d), lambda i, j: (0, 0)),
                  mod_spec, mod_spec, pl.BlockSpec((d, tn), lambda i, j: (0, jb + j))],
        out_specs=out_specs, out_shape=out_shape,
        scratch_shapes=[pltpu.VMEM((tm, d), BF16)],
        compiler_params=_cp("parallel", "arbitrary"),
    )(x, g, scale, shift, w)


def out_proj_fwd(o, zf, w_out, x, gate, seq, name):
    t, di = o.shape
    d = x.shape[1]
    tm = _tile(seq, 256)
    per_seq = seq // tm

    def body(o_ref, z_ref, w_ref, x_ref, gt_ref, xn_ref, y_ref, u_ref):
        z = z_ref[...]
        u = (o_ref[...] * (z * _sigmoid(z))).astype(BF16)
        y = jnp.dot(u, w_ref[...], preferred_element_type=F32)
        u_ref[...] = u
        y_ref[...] = y
        xn_ref[...] = x_ref[...] + gt_ref[0] * y

    row = lambda c: pl.BlockSpec((tm, c), lambda i: (i, 0))
    return pl.pallas_call(
        body, name=name, grid=(t // tm,),
        in_specs=[row(di), row(di), pl.BlockSpec((di, d), lambda i: (0, 0)), row(d),
                  pl.BlockSpec((1, 1, d), lambda i: (i // per_seq, 0, 0))],
        out_specs=[row(d), row(d), row(di)],
        out_shape=[jax.ShapeDtypeStruct((t, d), F32), jax.ShapeDtypeStruct((t, d), F32),
                   jax.ShapeDtypeStruct((t, di), BF16)],
        compiler_params=_cp("parallel"),
    )(o, zf, w_out, x, gate)


def out_proj_bwd(dxo, y, gate, w_out, o, zf, seq, name):
    t, d = dxo.shape
    di = o.shape[1]
    tm = _tile(seq, 256)
    per_seq = seq // tm

    def body(dx_ref, y_ref, gt_ref, w_ref, o_ref, z_ref, do_ref, dz_ref, dy_ref, dgt_ref):
        dx = dx_ref[...]
        part = jnp.sum(dx * y_ref[...], axis=0, keepdims=True)

        @pl.when(pl.program_id(0) % per_seq == 0)
        def _():
            dgt_ref[0] = part

        @pl.when(pl.program_id(0) % per_seq != 0)
        def _():
            dgt_ref[0] += part

        dy = (dx * gt_ref[0]).astype(BF16)
        dy_ref[...] = dy
        du = lax.dot_general(dy, w_ref[...], (((1,), (1,)), ((), ())), preferred_element_type=F32)
        z = z_ref[...]
        sg = _sigmoid(z)
        do_ref[...] = (du * (z * sg)).astype(BF16)
        dz_ref[...] = (du * o_ref[...] * (sg * (1.0 + z * (1.0 - sg)))).astype(BF16)

    row = lambda c: pl.BlockSpec((tm, c), lambda i: (i, 0))
    seq_spec = pl.BlockSpec((1, 1, d), lambda i: (i // per_seq, 0, 0))
    return pl.pallas_call(
        body, name=name, grid=(t // tm,),
        in_specs=[row(d), row(d), seq_spec, pl.BlockSpec((di, d), lambda i: (0, 0)), row(di), row(di)],
        out_specs=[row(di), row(di), row(d), seq_spec],
        out_shape=[jax.ShapeDtypeStruct((t, di), BF16), jax.ShapeDtypeStruct((t, di), BF16),
                   jax.ShapeDtypeStruct((t, d), BF16), jax.ShapeDtypeStruct((t // seq, 1, d), F32)],
        compiler_params=_cp("arbitrary"),
    )(dxo, y, gate, w_out, o, zf)


def tn_matmul(a, b, name):
    t, m = a.shape
    n = b.shape[1]
    tn = _col_tile(n, 1024)
    tk = _tile(t, 1024)
    nk = t // tk

    def body(a_ref, b_ref, o_ref, acc):
        part = lax.dot_general(a_ref[...], b_ref[...], (((0,), (0,)), ((), ())), preferred_element_type=F32)
        k = pl.program_id(1)

        @pl.when(k == 0)
        def _():
            acc[...] = part

        @pl.when(k != 0)
        def _():
            acc[...] += part

        @pl.when(k == nk - 1)
        def _():
            o_ref[...] = acc[...].astype(BF16)

    return pl.pallas_call(
        body, name=name, grid=(n // tn, nk),
        in_specs=[pl.BlockSpec((tk, m), lambda j, k: (k, 0)), pl.BlockSpec((tk, tn), lambda j, k: (k, j))],
        out_specs=pl.BlockSpec((m, tn), lambda j, k: (0, j)),
        out_shape=jax.ShapeDtypeStruct((m, n), BF16),
        scratch_shapes=[pltpu.VMEM((m, tn), F32)],
        compiler_params=_cp("parallel", "arbitrary"),
    )(a, b)


def dh_norm_bwd(parts, w_t, extra, x, g, scale, dxo, seq, name, rider=None):
    t, d = x.shape
    nparts = len(parts)
    kw = parts[0].shape[1]
    tm = _tile(seq, 512)
    per_seq = seq // tm
    has_extra = extra is not None
    grid = (t // tm,)
    nr = rider.n if rider else 0
    n_in = nparts + 1 + (2 if has_extra else 0) + 4

    def body(*refs):
        p_refs = refs[:nparts]
        w_ref = refs[nparts]
        pos = nparts + 1
        if has_extra:
            e_ref, we_ref = refs[pos], refs[pos + 1]
            pos += 2
        x_ref, g_ref, sc_ref, dxo_ref = refs[pos:pos + 4]
        dx_ref, pp_ref, ss_ref = refs[n_in + nr:n_in + nr + 3]
        if rider:
            _ride(rider, refs[n_in:n_in + nr] + refs[n_in + nr + 3:], *_grid_ends(grid))
        i = pl.program_id(0)

        dh = jnp.dot(p_refs[0][...], w_ref[0:kw, :], preferred_element_type=F32)
        for kk in range(1, nparts):
            dh = dh + jnp.dot(p_refs[kk][...], w_ref[kk * kw:(kk + 1) * kw, :], preferred_element_type=F32)
        if has_extra:
            dh = dh + jnp.dot(e_ref[...], we_ref[...], preferred_element_type=F32)
        xx = x_ref[...]
        rstd = lax.rsqrt(jnp.mean(xx * xx, axis=-1, keepdims=True) + NORM_EPS)
        xhat = xx * rstd
        dxhat = dh * (g_ref[...] * (1.0 + sc_ref[0]))
        dx_ref[...] = dxo_ref[...] + rstd * (dxhat - xhat * jnp.mean(dxhat * xhat, axis=-1, keepdims=True))
        pp = jnp.sum(dh * xhat, axis=0, keepdims=True)
        ss = jnp.sum(dh, axis=0, keepdims=True)

        @pl.when(i % per_seq == 0)
        def _():
            pp_ref[0] = pp
            ss_ref[0] = ss

        @pl.when(i % per_seq != 0)
        def _():
            pp_ref[0] += pp
            ss_ref[0] += ss

    row = lambda c: pl.BlockSpec((tm, c), lambda i: (i, 0))
    whole = lambda a: pl.BlockSpec(a.shape, lambda i: (0, 0), pipeline_mode=pl.Buffered(1))
    seq_spec = pl.BlockSpec((1, 1, d), lambda i: (i // per_seq, 0, 0))
    in_specs = [row(kw)] * nparts + [whole(w_t)]
    args = list(parts) + [w_t]
    if has_extra:
        in_specs += [row(extra[0].shape[1]), whole(extra[1])]
        args += list(extra)
    in_specs += [row(d), pl.BlockSpec((1, d), lambda i: (0, 0)), seq_spec, row(d)]
    args += [x, g, scale, dxo]
    return pl.pallas_call(
        body, name=name, grid=grid,
        in_specs=in_specs + (rider.in_specs if rider else []),
        out_specs=[row(d), seq_spec, seq_spec] + (rider.out_specs if rider else []),
        out_shape=[jax.ShapeDtypeStruct((t, d), F32), jax.ShapeDtypeStruct((t // seq, 1, d), F32),
                   jax.ShapeDtypeStruct((t // seq, 1, d), F32)] + (rider.out_shape if rider else []),
        scratch_shapes=rider.scratch if rider else [],
        compiler_params=_cp("arbitrary"),
    )(*args, *(rider.arrs if rider else []))


def final_loss(x, g, target, name):
    t, d = x.shape
    tm = _tile(t, 512)

    def body(x_ref, g_ref, t_ref, dx_ref, dg_ref, sq_ref):
        xx = x_ref[...]
        gg = g_ref[...]
        rstd = lax.rsqrt(jnp.mean(xx * xx, axis=-1, keepdims=True) + NORM_EPS)
        xhat = xx * rstd
        err = xhat * gg - t_ref[...]
        dy = err * (1.0 / d)
        dxhat = dy * gg
        dx_ref[...] = rstd * (dxhat - xhat * jnp.mean(dxhat * xhat, axis=-1, keepdims=True))
        dg = jnp.sum(dy * xhat, axis=0, keepdims=True)
        sq = jnp.sum(err * err, axis=0, keepdims=True)

        @pl.when(pl.program_id(0) == 0)
        def _():
            dg_ref[...] = dg
            sq_ref[...] = sq

        @pl.when(pl.program_id(0) != 0)
        def _():
            dg_ref[...] += dg
            sq_ref[...] += sq

    row = pl.BlockSpec((tm, d), lambda i: (i, 0))
    vec = pl.BlockSpec((1, d), lambda i: (0, 0))
    return pl.pallas_call(
        body, name=name, grid=(t // tm,),
        in_specs=[row, vec, row], out_specs=[row, vec, vec],
        out_shape=[jax.ShapeDtypeStruct((t, d), F32), jax.ShapeDtypeStruct((1, d), F32),
                   jax.ShapeDtypeStruct((1, d), F32)],
        compiler_params=_cp("arbitrary"),
    )(x, g, target)


def _split3(x):
    x1 = x.astype(BF16)
    r1 = x - x1.astype(F32)
    x2 = r1.astype(BF16)
    x3 = (r1 - x2.astype(F32)).astype(BF16)
    return x1, x2, x3


def _dot3(x, u):
    x1, x2, x3 = _split3(x)
    return (jnp.dot(x1, u, preferred_element_type=F32) + jnp.dot(x2, u, preferred_element_type=F32)
            + jnp.dot(x3, u, preferred_element_type=F32))


def _log_sigmoid(x):
    return jnp.minimum(x, 0.0) - jnp.log1p(jnp.exp(-jnp.abs(x)))


def forget_cumsum(fl_t, b_f, name):
    b, h, s = fl_t.shape

    def body(f_ref, b_ref, o_ref):
        lf = _log_sigmoid(f_ref[0] + b_ref[...])
        u = (lax.broadcasted_iota(jnp.int32, (s, s), 0) <= lax.broadcasted_iota(jnp.int32, (s, s), 1)).astype(BF16)
        o_ref[0] = _dot3(lf, u)

    return pl.pallas_call(
        body, name=name, grid=(b,),
        in_specs=[pl.BlockSpec((1, h, s), lambda i: (i, 0, 0)), pl.BlockSpec((h, 1), lambda i: (0, 0))],
        out_specs=pl.BlockSpec((1, h, s), lambda i: (i, 0, 0)),
        out_shape=jax.ShapeDtypeStruct((b, h, s), F32),
        compiler_params=_cp("parallel"),
    )(fl_t, b_f)


def forget_cumsum_bwd(dcum, fl_t, b_f, name):
    b, h, s = fl_t.shape

    def body(d_ref, f_ref, b_ref, o_ref, db_ref):
        u = (lax.broadcasted_iota(jnp.int32, (s, s), 0) >= lax.broadcasted_iota(jnp.int32, (s, s), 1)).astype(BF16)
        dlf = _dot3(d_ref[0], u)
        df = dlf * _sigmoid(-(f_ref[0] + b_ref[...]))
        o_ref[0] = df
        part = jnp.sum(df, axis=-1, keepdims=True)

        @pl.when(pl.program_id(0) == 0)
        def _():
            db_ref[...] = part

        @pl.when(pl.program_id(0) != 0)
        def _():
            db_ref[...] += part

    blk = pl.BlockSpec((1, h, s), lambda i: (i, 0, 0))
    return pl.pallas_call(
        body, name=name, grid=(b,),
        in_specs=[blk, blk, pl.BlockSpec((h, 1), lambda i: (0, 0))],
        out_specs=[blk, pl.BlockSpec((h, 1), lambda i: (0, 0))],
        out_shape=[jax.ShapeDtypeStruct((b, h, s), F32), jax.ShapeDtypeStruct((h, 1), F32)],
        compiler_params=_cp("arbitrary"),
    )(dcum, fl_t, b_f)


def _nt(a, b):
    return lax.dot_general(a, b, (((1,), (1,)), ((), ())), preferred_element_type=F32)


def _attn_dims(t, seq, dh, q_rows=512):
    tq = _tile(seq, q_rows)
    tk = _tile(seq, 256)
    return t // seq, tq, tk, seq // tq, tq // tk, ATTN_LANES // dh


def _blocks_per_trip(r):
    return ATTN_BLOCKS_PER_TRIP if r % ATTN_BLOCKS_PER_TRIP == 0 else 1


def _scaled_q(q_ref, dh):
    scale = dh ** -0.5
    if math.log2(dh) % 2 == 0:
        return (q_ref[...].astype(F32) * scale).astype(BF16), None
    return q_ref[...], scale


def _diag_masks(tq, tk, r, strict):
    row = lax.broadcasted_iota(jnp.int32, (tq, tk), 0)
    col = lax.broadcasted_iota(jnp.int32, (tq, tk), 1)
    return [(col + d * tk < row) if strict else (col + d * tk <= row) for d in range(r)]


def _rows(state, lo):
    return tuple(a[lo:] for a in state)


def _put_rows(state, part, lo):
    if lo == 0:
        return tuple(part)
    return tuple(jnp.concatenate([a[:lo], p], axis=0) for a, p in zip(state, part))


def _transposed(x):
    return jnp.transpose(x.astype(F32)).astype(BF16)


def _store_transposed(dst_ref, src_scr, tk):
    for jb in range(src_scr.shape[0]):
        dst_ref[jb * tk:(jb + 1) * tk, :] = jnp.transpose(src_scr[jb]).astype(BF16)


def _dot2(x, u):
    hi = x.astype(BF16)
    lo = (x - hi.astype(F32)).astype(BF16)
    return jnp.dot(jnp.concatenate([hi, lo], axis=1), jnp.concatenate([u, u], axis=0), preferred_element_type=F32)


def _attn_specs(seq, di, tq):
    nq = seq // tq
    cb = di // ATTN_LANES
    q_spec = pl.BlockSpec((tq, ATTN_LANES), lambda b, hp, i: (b * nq + i, hp))
    k_spec = pl.BlockSpec((seq, ATTN_LANES), lambda b, hp, i: (b, cb + hp))
    v_spec = pl.BlockSpec((seq, ATTN_LANES), lambda b, hp, i: (b, 2 * cb + hp))
    kv_out = pl.BlockSpec((seq, ATTN_LANES), lambda b, hp, i: (b, hp))
    return q_spec, k_spec, v_spec, kv_out


def _grid_ends(grid):
    ids = [pl.program_id(a) for a in range(len(grid))]
    first = functools.reduce(jnp.logical_and, [p == 0 for p in ids])
    last = functools.reduce(jnp.logical_and, [p == g - 1 for p, g in zip(ids, grid)])
    return first, last


def fox_fwd(qkv, ck, seq, di, dh, name, rider=None):
    t = qkv.shape[0]
    nb, tq, tk, nq, r, hpb = _attn_dims(t, seq, dh, FOX_Q_ROWS)
    per_trip = _blocks_per_trip(r)
    heads = [slice(hh * dh, (hh + 1) * dh) for hh in range(hpb)]
    assert hpb >= 2, "the row sums of p ride in another head's lanes of the p @ v matmul"
    q_spec, k_spec, v_spec, _ = _attn_specs(seq, di, tq)
    grid = (nb, di // ATTN_LANES, nq)
    nr = rider.n if rider else 0

    def body(q_ref, k_ref, v_ref, ck_ref, *rest):
        o_ref, lse_ref = rest[nr:nr + 2]
        knorm_ref = rest[2 * nr + 2]
        if rider:
            _ride(rider, rest[:nr] + rest[nr + 2:2 * nr + 2] + rest[2 * nr + 3:], *_grid_ends(grid))
        i = pl.program_id(2)
        masks = _diag_masks(tq, tk, r, False)
        q, scale = _scaled_q(q_ref, dh)
        qs = [q[:, hd] for hd in heads]
        lane_head = lax.broadcasted_iota(jnp.int32, (tk, ATTN_LANES), 1) // dh

        def largest_norm(x):
            x = x.astype(F32)
            return jnp.sqrt(jnp.max(jnp.sum(x * x, axis=-1, keepdims=True)))

        @pl.when(i == 0)
        def _():
            for hh in range(hpb):
                knorm_ref[hh] = largest_norm(k_ref[:, heads[hh]])

        qk_bound = [largest_norm(qs[hh]) * knorm_ref[hh] * (1.0 if scale is None else scale) + 1.0 for hh in range(hpb)]

        def scores(hh, j):
            start = pl.multiple_of(j * tk, tk)
            kh = k_ref[pl.ds(start, tk), heads[hh]]
            vt = v_ref[pl.ds(start, tk), :]
            vh = jnp.where(lane_head == hh, vt, jnp.ones_like(vt))
            s = _nt(qs[hh], kh)
            if scale is not None:
                s = s * scale
            return s - ck_ref[0, hh, pl.ds(j, 1), :], vh

        def update(state, s, vh):
            m, acc = state
            m_new = jnp.maximum(m, jnp.max(s, axis=-1, keepdims=True))
            p = jnp.exp(s - m_new)
            return m_new, jnp.exp(m - m_new) * acc + _dot2(p, vh)

        states = []
        for hh in range(hpb):
            s, vh = scores(hh, r * i)
            s = jnp.where(masks[0], s, -jnp.inf)
            m = jnp.max(s, axis=-1, keepdims=True)
            states.append((m, _dot2(jnp.exp(s - m), vh)))
        for d in range(1, r):
            for hh in range(hpb):
                s, vh = scores(hh, r * i + d)
                states[hh] = update(states[hh], jnp.where(masks[d], s, -jnp.inf), vh)

        def any_weight_left(states, j):
            jc = jnp.maximum(j, 0)
            tops = [jnp.max(-ck_ref[0, hh, pl.ds(jc, 1), :]) + qk_bound[hh] - jnp.min(states[hh][0]) for hh in range(hpb)]
            done = functools.reduce(jnp.logical_and, [top <= EXP_IS_ZERO_BELOW for top in tops])
            return jnp.logical_not(done).astype(jnp.int32)

        def step(carry):
            jj, _, states = carry
            for sub in range(per_trip):
                states = tuple(update(states[hh], *scores(hh, r * i - 1 - (per_trip * jj + sub))) for hh in range(hpb))
            return jj + 1, any_weight_left(states, r * i - 1 - per_trip * (jj + 1)), states

        trips, _, states = lax.while_loop(lambda c: jnp.logical_and(c[0] < r * i // per_trip, c[1] > 0), step,
                                          (jnp.int32(0), any_weight_left(states, r * i - 1), tuple(states)))
        for hh in range(hpb):
            m, acc = states[hh]
            other = heads[(hh + 1) % hpb].start
            l = acc[:, other:other + 1]
            o_ref[:, heads[hh]] = acc[:, heads[hh]] / l
            lse_ref[0, 0, :, hh:hh + 1] = m + jnp.log(l)
        lse_ref[0, 0, :, hpb:hpb + 1] = jnp.full((tq, 1), (trips * per_trip).astype(F32))

    return pl.pallas_call(
        body, name=name, grid=grid,
        in_specs=[q_spec, k_spec, v_spec,
                  pl.BlockSpec((1, hpb, seq // tk, tk), lambda b, hp, i: (b, hp, 0, 0))] + (rider.in_specs if rider else []),
        out_specs=[q_spec, pl.BlockSpec((1, 1, tq, hpb + 1), lambda b, hp, i: (b, hp, i, 0))]
        + (rider.out_specs if rider else []),
        out_shape=[jax.ShapeDtypeStruct((t, di), F32), jax.ShapeDtypeStruct((nb, di // ATTN_LANES, seq, hpb + 1), F32)]
        + (rider.out_shape if rider else []),
        scratch_shapes=[pltpu.SMEM((hpb,), F32)] + (rider.scratch if rider else []),
        compiler_params=_cp("arbitrary", "arbitrary", "arbitrary"),
    )(qkv, qkv, qkv, ck, *(rider.arrs if rider else []))


def fox_bwd(qkv, ck, o, do, lse, seq, di, dh, name, rider=None):
    t = qkv.shape[0]
    nb, tq, tk, nq, r, hpb = _attn_dims(t, seq, dh, FOX_Q_ROWS)
    per_trip = _blocks_per_trip(r)
    heads = [slice(hh * dh, (hh + 1) * dh) for hh in range(hpb)]
    q_spec, k_spec, v_spec, kv_out = _attn_specs(seq, di, tq)
    ck_spec = pl.BlockSpec((1, hpb, seq // tk, tk), lambda b, hp, i: (b, hp, 0, 0))
    grid = (nb, di // ATTN_LANES, nq)
    nr = rider.n if rider else 0

    def body(q_ref, k_ref, v_ref, ck_ref, o_ref, do_ref, lse_ref, *rest):
        dq_ref, dk_ref, dv_ref, dck_ref = rest[nr:nr + 4]
        dkt_scr, dvt_scr = rest[2 * nr + 4:2 * nr + 6]
        if rider:
            _ride(rider, rest[:nr] + rest[nr + 4:2 * nr + 4] + rest[2 * nr + 6:], *_grid_ends(grid))
        i = pl.program_id(2)

        @pl.when(i == 0)
        def _():
            dkt_scr[...] = jnp.zeros_like(dkt_scr)
            dvt_scr[...] = jnp.zeros_like(dvt_scr)
            dck_ref[...] = jnp.zeros_like(dck_ref)

        masks = _diag_masks(tq, tk, r, False)
        q, scale = _scaled_q(q_ref, dh)
        do = do_ref[...]
        q_t, do_t = _transposed(q), _transposed(do)
        qs = [q[:, hd] for hd in heads]
        dos = [do[:, hd] for hd in heads]
        deltas = [jnp.sum(dos[hh].astype(F32) * o_ref[:, heads[hh]], axis=-1, keepdims=True) for hh in range(hpb)]
        lses = [lse_ref[0, 0, :, hh:hh + 1] for hh in range(hpb)]

        def block(hh, j, mask, dq_acc, lo=0):
            start = pl.multiple_of(j * tk, tk)
            kh = k_ref[pl.ds(start, tk), heads[hh]]
            vh = v_ref[pl.ds(start, tk), heads[hh]]
            s = _nt(qs[hh][lo:], kh)
            if scale is not None:
                s = s * scale
            p = jnp.exp(s - ck_ref[0, hh, pl.ds(j, 1), :] - lses[hh][lo:])
            if mask is not None:
                p = jnp.where(mask[lo:], p, 0.0)
            ds = p * (_nt(dos[hh][lo:], vh) - deltas[hh][lo:])
            dsb = ds.astype(BF16)
            dkt = jnp.dot(q_t[heads[hh], lo:], dsb, preferred_element_type=F32)
            if scale is not None:
                dkt = dkt * scale
            dkt_scr[j, heads[hh], :] += dkt
            dvt_scr[j, heads[hh], :] += jnp.dot(do_t[heads[hh], lo:], p.astype(BF16), preferred_element_type=F32)
            dck_ref[0, hh, pl.ds(j, 1), :] -= jnp.sum(ds, axis=0, keepdims=True)
            return dq_acc + jnp.dot(dsb, kh, preferred_element_type=F32)

        accs = [jnp.zeros((tq, dh), F32)] * hpb
        for d in range(r):
            lo = d * tk
            accs = [_put_rows((accs[hh],), (block(hh, r * i + d, masks[d], accs[hh][lo:], lo),), lo)[0]
                    for hh in range(hpb)]

        trips = jnp.clip(jnp.max(lse_ref[0, 0, :, hpb:hpb + 1]).astype(jnp.int32), 0, r * i) // per_trip
        first = r * i - trips * per_trip

        def step(jj, accs):
            for sub in range(per_trip):
                accs = tuple(block(hh, first + per_trip * jj + sub, None, accs[hh]) for hh in range(hpb))
            return accs

        accs = lax.fori_loop(0, trips, step, tuple(accs))
        for hh in range(hpb):
            dq_ref[:, heads[hh]] = (accs[hh] * dh ** -0.5).astype(BF16)

        @pl.when(i == nq - 1)
        def _():
            _store_transposed(dk_ref, dkt_scr, tk)
            _store_transposed(dv_ref, dvt_scr, tk)

    return pl.pallas_call(
        body, name=name, grid=grid,
        in_specs=[q_spec, k_spec, v_spec, ck_spec, q_spec, q_spec,
                  pl.BlockSpec((1, 1, tq, hpb + 1), lambda b, hp, i: (b, hp, i, 0))] + (rider.in_specs if rider else []),
        out_specs=[q_spec, kv_out, kv_out, ck_spec] + (rider.out_specs if rider else []),
        out_shape=[jax.ShapeDtypeStruct((t, di), BF16)] * 3 + [jax.ShapeDtypeStruct(ck.shape, F32)]
        + (rider.out_shape if rider else []),
        scratch_shapes=[pltpu.VMEM((seq // tk, ATTN_LANES, tk), F32), pltpu.VMEM((seq // tk, ATTN_LANES, tk), F32)]
        + (rider.scratch if rider else []),
        compiler_params=_cp("arbitrary", "arbitrary", "arbitrary"),
    )(qkv, qkv, qkv, ck, o, do, lse, *(rider.arrs if rider else []))


def _sb_logits(qh, kh, scale, strict):
    z = _nt(qh, kh)
    if scale is not None:
        z = z * scale
    e = jnp.exp(-jnp.abs(z))
    lb = jnp.minimum(z, 0.0) - jnp.log(1.0 + e)
    lk = lb - z
    return lb, lk if strict is None else jnp.where(strict, lk, 0.0)


def _cum(x, u):
    return jnp.dot(x.astype(BF16), u, preferred_element_type=F32)


def sb_fwd(qkv, seq, di, dh, name):
    t = qkv.shape[0]
    nb, tq, tk, nq, r, hpb = _attn_dims(t, seq, dh)
    heads = [slice(hh * dh, (hh + 1) * dh) for hh in range(hpb)]
    q_spec, k_spec, v_spec, _ = _attn_specs(seq, di, tq)

    def body(q_ref, k_ref, v_ref, o_ref, rt_ref):
        i = pl.program_id(2)
        masks = _diag_masks(tq, tk, r, True)
        row = lax.broadcasted_iota(jnp.int32, (tk, tk), 0)
        col = lax.broadcasted_iota(jnp.int32, (tk, tk), 1)
        u_after = (row > col).astype(BF16)
        q, scale = _scaled_q(q_ref, dh)
        qs = [q[:, hd] for hd in heads]

        def block(hh, j, mask, state, lo=0):
            rr, acc = state
            start = pl.multiple_of(j * tk, tk)
            kh = k_ref[pl.ds(start, tk), heads[hh]]
            vh = v_ref[pl.ds(start, tk), heads[hh]]
            if mask is not None:
                mask = mask[lo:]
            lb, lk = _sb_logits(qs[hh][lo:], kh, scale, mask)
            a = jnp.exp(lb + _cum(lk, u_after) + rr)
            if mask is not None:
                a = jnp.where(mask, a, 0.0)
            acc = acc + jnp.dot(a.astype(BF16), vh, preferred_element_type=F32)
            return rr + jnp.sum(lk, axis=-1, keepdims=True), acc

        states = [(jnp.zeros((tq, 1), F32), jnp.zeros((tq, dh), F32))] * hpb
        for d in reversed(range(r)):
            lo = d * tk
            states = [_put_rows(states[hh], block(hh, r * i + d, masks[d], _rows(states[hh], lo), lo), lo)
                      for hh in range(hpb)]

        def any_weight_left(states):
            done = functools.reduce(jnp.logical_and, [jnp.max(st[0]) <= EXP_IS_ZERO_BELOW for st in states])
            return jnp.logical_not(done).astype(jnp.int32)

        def step(carry):
            jj, _, states = carry
            states = tuple(block(hh, r * i - 1 - jj, None, states[hh]) for hh in range(hpb))
            return jj + 1, any_weight_left(states), states

        visited, _, states = lax.while_loop(lambda c: jnp.logical_and(c[0] < r * i, c[1] > 0), step,
                                            (jnp.int32(0), any_weight_left(states), tuple(states)))
        for hh in range(hpb):
            o_ref[:, heads[hh]] = states[hh][1]
            rt_ref[0, 0, :, hh:hh + 1] = states[hh][0]
        rt_ref[0, 0, :, hpb:hpb + 1] = jnp.full((tq, 1), visited.astype(F32))

    return pl.pallas_call(
        body, name=name, grid=(nb, di // ATTN_LANES, nq),
        in_specs=[q_spec, k_spec, v_spec],
        out_specs=[q_spec, pl.BlockSpec((1, 1, tq, hpb + 1), lambda b, hp, i: (b, hp, i, 0))],
        out_shape=[jax.ShapeDtypeStruct((t, di), F32), jax.ShapeDtypeStruct((nb, di // ATTN_LANES, seq, hpb + 1), F32)],
        compiler_params=_cp("parallel", "parallel", "arbitrary"),
    )(qkv, qkv, qkv)


def sb_bwd(qkv, do, rtot, seq, di, dh, name):
    t = qkv.shape[0]
    nb, tq, tk, nq, r, hpb = _attn_dims(t, seq, dh)
    heads = [slice(hh * dh, (hh + 1) * dh) for hh in range(hpb)]
    q_spec, k_spec, v_spec, kv_out = _attn_specs(seq, di, tq)

    def body(q_ref, k_ref, v_ref, do_ref, rt_ref, dq_ref, dk_ref, dv_ref, dkt_scr, dvt_scr):
        i = pl.program_id(2)

        @pl.when(i == 0)
        def _():
            dkt_scr[...] = jnp.zeros_like(dkt_scr)
            dvt_scr[...] = jnp.zeros_like(dvt_scr)

        masks = _diag_masks(tq, tk, r, True)
        row = lax.broadcasted_iota(jnp.int32, (tk, tk), 0)
        col = lax.broadcasted_iota(jnp.int32, (tk, tk), 1)
        u_after = (row > col).astype(BF16)
        u_before = (row < col).astype(BF16)
        q, scale = _scaled_q(q_ref, dh)
        do = do_ref[...]
        q_t, do_t = _transposed(q), _transposed(do)
        qs = [q[:, hd] for hd in heads]
        dos = [do[:, hd] for hd in heads]
        rts = [rt_ref[0, 0, :, hh:hh + 1] for hh in range(hpb)]

        def block(hh, j, mask, state, lo=0):
            lc, gc, dq_acc = state
            start = pl.multiple_of(j * tk, tk)
            kh = k_ref[pl.ds(start, tk), heads[hh]]
            vh = v_ref[pl.ds(start, tk), heads[hh]]
            if mask is not None:
                mask = mask[lo:]
            lb, lk = _sb_logits(qs[hh][lo:], kh, scale, mask)
            lc = lc + jnp.sum(lk, axis=-1, keepdims=True)
            a = jnp.exp(lb + ((rts[hh][lo:] - lc) + _cum(lk, u_after)))
            if mask is not None:
                a = jnp.where(mask, a, 0.0)
            de = a * _nt(dos[hh][lo:], vh)
            g = gc + _cum(de, u_before)
            dz = de - jnp.exp(lb) * (de + g)
            if mask is not None:
                dz = jnp.where(mask, dz, 0.0)
            dzb = dz.astype(BF16)
            dkt = jnp.dot(q_t[heads[hh], lo:], dzb, preferred_element_type=F32)
            if scale is not None:
                dkt = dkt * scale
            dkt_scr[j, heads[hh], :] += dkt
            dvt_scr[j, heads[hh], :] += jnp.dot(do_t[heads[hh], lo:], a.astype(BF16), preferred_element_type=F32)
            return (lc, gc + jnp.sum(de, axis=-1, keepdims=True),
                    dq_acc + jnp.dot(dzb, kh, preferred_element_type=F32))

        zero = jnp.zeros((tq, 1), F32)
        visited = jnp.clip(jnp.max(rt_ref[0, 0, :, hpb:hpb + 1]).astype(jnp.int32), 0, r * i)
        first = r * i - visited

        def step(jj, states):
            return tuple(block(hh, first + jj, None, states[hh]) for hh in range(hpb))

        states = lax.fori_loop(0, visited, step, ((zero, zero, jnp.zeros((tq, dh), F32)),) * hpb)
        for d in range(r):
            lo = d * tk
            states = [_put_rows(states[hh], block(hh, r * i + d, masks[d], _rows(states[hh], lo), lo), lo)
                      for hh in range(hpb)]
        for hh in range(hpb):
            dq_ref[:, heads[hh]] = (states[hh][2] * dh ** -0.5).astype(BF16)

        @pl.when(i == nq - 1)
        def _():
            _store_transposed(dk_ref, dkt_scr, tk)
            _store_transposed(dv_ref, dvt_scr, tk)

    return pl.pallas_call(
        body, name=name, grid=(nb, di // ATTN_LANES, nq),
        in_specs=[q_spec, k_spec, v_spec, q_spec, pl.BlockSpec((1, 1, tq, hpb + 1), lambda b, hp, i: (b, hp, i, 0))],
        out_specs=[q_spec, kv_out, kv_out],
        out_shape=[jax.ShapeDtypeStruct((t, di), BF16)] * 3,
        scratch_shapes=[pltpu.VMEM((seq // tk, ATTN_LANES, tk), F32), pltpu.VMEM((seq // tk, ATTN_LANES, tk), F32)],
        compiler_params=_cp("parallel", "parallel", "arbitrary"),
    )(qkv, qkv, qkv, do, rtot)


def _cols(g):
    return jnp.transpose(g, (1, 0, 2)).reshape(g.shape[1], NDEV * g.shape[2])


def _col_blocks(w):
    r, c8 = w.shape
    return jnp.transpose(w.reshape(r, NDEV, c8 // NDEV), (1, 0, 2))


def _pad_rows16(a):
    return jnp.pad(a, ((0, 16 - a.shape[0]), (0, 0)))


def _pad_cols(a, n):
    return jnp.pad(a, ((0, 0), (0, n - a.shape[1])))


def kernel(x, c, fox_norm_g, fox_w_ada, fox_b_ada, fox_w_in, fox_b_f, fox_w_out, sb_norm_g, sb_w_ada, sb_b_ada, sb_w_in, sb_w_out, final_norm_g, loss_target, m_fox_norm_g, m_fox_w_ada, m_fox_b_ada, m_fox_w_in, m_fox_b_f, m_fox_w_out, m_sb_norm_g, m_sb_w_ada, m_sb_b_ada, m_sb_w_in, m_sb_w_out, m_final_norm_g, v_fox_norm_g, v_fox_w_ada, v_fox_b_ada, v_fox_w_in, v_fox_b_f, v_fox_w_out, v_sb_norm_g, v_sb_w_ada, v_sb_b_ada, v_sb_w_in, v_sb_w_out, v_final_norm_g):
    nb, seq, d = x.shape
    t = nb * seq
    h = fox_b_f.shape[-1]
    di = fox_w_out.shape[1] * NDEV
    dh = di // h
    tk = _attn_dims(t, seq, dh)[2]
    me = _my_index()

    gathered = all_gather_by_chip([w[0].astype(BF16) for w in (fox_w_ada, fox_w_in, fox_w_out)], "gather_fox_weights")
    fox_wada, fox_win = _cols(gathered[0]), _cols(gathered[1])
    fox_wout = gathered[2].reshape(di, d)
    sb_gather = Exchange("gather", [w[0].astype(BF16) for w in (sb_w_ada, sb_w_in, sb_w_out)] + [sb_norm_g, sb_b_ada])

    x0 = x.reshape(t, d)
    target = loss_target.reshape(t, d)
    c16 = _pad_rows16(c)

    def layer_fwd(xin, g, wada, bada, win, wout, b_f, tag, rider=None):
        mod = adaln_fwd(c16, wada, bada, tag + "_adaln")[:nb]
        shift, scale, gate = (mod[:, k * d:(k + 1) * d].reshape(nb, 1, d) for k in range(3))
        qkv, hmod = normmod_matmul(xin, g, scale, shift, win, BF16, seq, True, tag + "_qkv", (0, 3 * di))
        if b_f is not None:
            (zf,) = normmod_matmul(xin, g, scale, shift, _pad_cols(win[:, 3 * di:], di + 128), F32, seq, False, tag + "_z")
        else:
            (zf,) = normmod_matmul(xin, g, scale, shift, win, F32, seq, False, tag + "_z", (3 * di, di))
        saved = dict(x=xin, g=g, scale=scale, gate=gate, qkv=qkv, h=hmod, zf=zf, win=win, wout=wout)
        if b_f is not None:
            fl_t = jnp.transpose(zf[:, di:di + h].reshape(nb, seq, h), (0, 2, 1))
            bf_col = b_f.reshape(h, 1)
            cum = forget_cumsum(fl_t, bf_col, tag + "_cum")
            ck = cum.reshape(nb, h, seq // tk, tk)
            o, lse, *rode = fox_fwd(qkv, ck, seq, di, dh, tag + "_attn", rider)
            saved.update(fl_t=fl_t, bf_col=bf_col, ck=ck, lse=lse, rode=rode)
        else:
            o, rtot = sb_fwd(qkv, seq, di, dh, tag + "_attn")
            saved.update(rtot=rtot)
        xout, y, u = out_proj_fwd(o, zf, wout, xin, gate, seq, tag + "_out")
        saved.update(o=o, y=y, u=u)
        return xout, saved

    def layer_bwd(dxo, sv, is_fox, tag, riding=()):
        do, dz, dy, dgate = out_proj_bwd(dxo, sv["y"], sv["gate"], sv["wout"], sv["o"], sv["zf"], seq, tag + "_dout")
        wout_blocks = tn_matmul(sv["u"], dy, tag + "_dwout").reshape(NDEV, di // NDEV, d)
        extra = None
        if is_fox:
            dq, dk, dv, dck, *rode = fox_bwd(sv["qkv"], sv["ck"], sv["o"], do, sv["lse"], seq, di, dh, tag + "_dattn",
                                             Exchange("scatter", list(riding) + [wout_blocks]))
            df_t, dbf = forget_cumsum_bwd(dck.reshape(nb, h, seq), sv["fl_t"], sv["bf_col"], tag + "_dcum")
            df = _pad_cols(jnp.transpose(df_t, (0, 2, 1)).reshape(t, h), 128).astype(BF16)
            extra = (df, jnp.transpose(_pad_cols(sv["win"][:, 4 * di:], 128)))
        else:
            dq, dk, dv = sb_bwd(sv["qkv"], do, sv["rtot"], seq, di, dh, tag + "_dattn")
        parts = [dq, dk, dv, dz]
        dwin = [tn_matmul(sv["h"], p, tag + "_dwin%d" % k) for k, p in enumerate(parts)]
        if is_fox:
            dwin.append(tn_matmul(sv["h"], df, tag + "_dwinf")[:, :h])
        dwin = jnp.concatenate(dwin, axis=1)
        w_t = jnp.transpose(sv["win"][:, :4 * di])
        win_blocks = _col_blocks(dwin)
        dh_rider = Exchange("scatter", [win_blocks]) if is_fox else None
        dxin, psum, ssum, *landed = dh_norm_bwd(parts, w_t, extra, sv["x"], sv["g"], sv["scale"], dxo, seq,
                                                tag + "_dh", dh_rider)
        pad = lambda a: _pad_rows16(a.reshape(nb, d))
        dwada, dbada, dng = adaln_bwd(c16, pad(ssum), pad(psum), pad(dgate), pad(sv["scale"]), sv["g"], tag + "_dadaln",
                                      emit_dw=not is_fox)
        grads = dict(wada=dwada, bada=dbada, ng=dng, win=win_blocks, wout=wout_blocks)
        if is_fox:
            grads["bf"] = dbf.reshape(1, h)
            grads["rode"] = rode
            grads["landed"] = landed
        return dxin, grads

    x1, sv_fox = layer_fwd(x0, fox_norm_g, fox_wada, fox_b_ada, fox_win, fox_wout, fox_b_f, "fox", sb_gather)
    sb_wada, sb_win = _cols(sv_fox["rode"][0]), _cols(sv_fox["rode"][1])
    sb_wout = sv_fox["rode"][2].reshape(di, d)
    sb_g, sb_bada = _cols(sv_fox["rode"][3]), _cols(sv_fox["rode"][4])
    x2, sv_sb = layer_fwd(x1, sb_g, sb_wada, sb_bada, sb_win, sb_wout, None, "sb")
    dx2, dgf, sq = final_loss(x2, final_norm_g.reshape(1, d), target, "loss_head")
    dx1, g_sb = layer_bwd(dx2, sv_sb, False, "sb")
    dx0, g_fox = layer_bwd(dx1, sv_fox, True, "fox", [g_sb["wada"], g_sb["win"], g_sb["wout"]])

    sb_wada_land, sb_win_land, sb_wout_land, fox_wout_land = g_fox["rode"]
    small = jnp.concatenate([g_fox["ng"], g_fox["bada"], _pad_cols(g_fox["bf"], 128), g_sb["ng"], g_sb["bada"],
                             dgf, sq], axis=1)
    rows = jnp.concatenate([g_fox["wada"][:nb], c], axis=1)
    small_all, rows_all = all_gather([small, rows], "gather_small")
    small_sum, loss_row = sum_slots(small_all, d, "sum_small")
    loss = loss_row[0, 0]
    ncol = 3 * d // NDEV
    dmod_cols = lax.dynamic_slice(rows_all[:, :, :3 * d].reshape(NDEV * nb, 3 * d), (0, me * ncol), (NDEV * nb, ncol))
    fox_wada_land = dwada_columns(rows_all[:, :, 3 * d:].reshape(NDEV * nb, d), dmod_cols, "fox_dwada")
    land = [fox_wada_land, g_fox["landed"][0], fox_wout_land, sb_wada_land, sb_win_land, sb_wout_land]

    offs = {}
    pos = 0
    for nm, width in (("fox_ng", d), ("fox_bada", 3 * d), ("fox_bf", 128), ("sb_ng", d), ("sb_bada", 3 * d), ("fin_g", d)):
        offs[nm] = (pos, width)
        pos += width

    def small_grad(nm, width=None, shard=False):
        p0, wd = offs[nm]
        wd = width or wd
        if shard:
            blk = wd // NDEV
            return lax.dynamic_slice(small_sum, (0, p0 + me * blk), (1, blk))
        return small_sum[:, p0:p0 + wd]

    results = {}
    big_params = [("fox_w_ada", fox_w_ada, m_fox_w_ada, v_fox_w_ada), ("fox_w_in", fox_w_in, m_fox_w_in, v_fox_w_in),
                  ("fox_w_out", fox_w_out, m_fox_w_out, v_fox_w_out), ("sb_w_ada", sb_w_ada, m_sb_w_ada, v_sb_w_ada),
                  ("sb_w_in", sb_w_in, m_sb_w_in, v_sb_w_in), ("sb_w_out", sb_w_out, m_sb_w_out, v_sb_w_out)]
    for k, (nm, w, m, v) in enumerate(big_params):
        results[nm] = adamw(land[k], w, m, v, "adamw_" + nm)
    small_params = [("fox_norm_g", fox_norm_g, m_fox_norm_g, v_fox_norm_g, small_grad("fox_ng")),
                    ("fox_b_ada", fox_b_ada, m_fox_b_ada, v_fox_b_ada, small_grad("fox_bada")),
                    ("fox_b_f", fox_b_f, m_fox_b_f, v_fox_b_f, small_grad("fox_bf", h)),
                    ("sb_norm_g", sb_norm_g, m_sb_norm_g, v_sb_norm_g, small_grad("sb_ng", shard=True)),
                    ("sb_b_ada", sb_b_ada, m_sb_b_ada, v_sb_b_ada, small_grad("sb_bada", shard=True)),
                    ("final_norm_g", final_norm_g.reshape(1, d), m_final_norm_g.reshape(1, d),
                     v_final_norm_g.reshape(1, d), small_grad("fin_g"))]
    for nm, w, m, v, g in small_params:
        outs = adamw(g[None], w, m, v, "adamw_" + nm)
        if nm == "final_norm_g":
            outs = [o.reshape(d) for o in outs]
        results[nm] = outs

    order = ["fox_norm_g", "fox_w_ada", "fox_b_ada", "fox_w_in", "fox_b_f", "fox_w_out", "sb_norm_g", "sb_w_ada",
             "sb_b_ada", "sb_w_in", "sb_w_out", "final_norm_g"]
    out = [loss, dx0.reshape(nb, seq, d)]
    for k in range(4):
        out += [results[nm][k] for nm in order]
    return tuple(out)
```

```python
import functools
import math

import jax
import jax.numpy as jnp
from jax import lax
from jax.experimental import pallas as pl
from jax.experimental.pallas import tpu as pltpu

F32 = jnp.float32
BF16 = jnp.bfloat16
NDEV = 8
VMEM_LIMIT = 56 * 1024 * 1024
NORM_EPS = 1e-6
ADAM_LR, ADAM_B1, ADAM_B2, ADAM_EPS, ADAM_WD, ADAM_STEP = 0.001, 0.9, 0.999, 1e-08, 0.01, 10
ATTN_LANES = 128
ATTN_BLOCKS_PER_TRIP = 2
FOX_Q_ROWS = 1024
EXP_IS_ZERO_BELOW = -104.0
MESH = pl.DeviceIdType.MESH
ANY = pl.BlockSpec(memory_space=pl.ANY)


def _cp(*sem):
    return pltpu.CompilerParams(dimension_semantics=sem, vmem_limit_bytes=VMEM_LIMIT)


def _my_index():
    return 4 * lax.axis_index("x") + 2 * lax.axis_index("y") + lax.axis_index("c")


def _flip(k):
    x, y, c = lax.axis_index("x"), lax.axis_index("y"), lax.axis_index("c")
    kx, ky, kc = (k >> 2) & 1, (k >> 1) & 1, k & 1
    px = 1 - x if kx else x
    py = 1 - y if ky else y
    pc = 1 - c if kc else c
    return (px, py, pc), 4 * px + 2 * py + pc


class Exchange:
    def __init__(self, mode, arrs):
        self.gather = mode == "gather"
        self.arrs = list(arrs)
        self.n = len(self.arrs)
        self.in_specs = [ANY] * self.n
        self.out_specs = [ANY] * self.n
        self.out_shape = [jax.ShapeDtypeStruct(((NDEV,) + a.shape) if self.gather else a.shape, a.dtype)
                          for a in self.arrs]
        self.scratch = [pltpu.SemaphoreType.DMA((self.n * (NDEV - 1),)), pltpu.SemaphoreType.DMA((self.n * (NDEV - 1),)),
                        pltpu.SemaphoreType.DMA((self.n,))]

    def copies(self, ins, outs, sems):
        send_sems, recv_sems, local_sems = sems
        me = _my_index()
        out = []
        for a in range(self.n):
            out.append(pltpu.make_async_copy(ins[a] if self.gather else ins[a].at[me], outs[a].at[me], local_sems.at[a]))
            for k in range(1, NDEV):
                peer, pidx = _flip(k)
                out.append(pltpu.make_async_remote_copy(
                    src_ref=ins[a] if self.gather else ins[a].at[pidx], dst_ref=outs[a].at[me],
                    send_sem=send_sems.at[a * (NDEV - 1) + k - 1], recv_sem=recv_sems.at[a * (NDEV - 1) + k - 1],
                    device_id=peer, device_id_type=MESH))
        return out

    def run(self, name):
        n = self.n

        def body(*refs):
            cps = self.copies(refs[:n], refs[n:2 * n], refs[2 * n:])
            for cp in cps:
                cp.start()
            for cp in cps:
                cp.wait()

        return pl.pallas_call(body, name=name, out_shape=self.out_shape, in_specs=self.in_specs,
                              out_specs=self.out_specs, scratch_shapes=self.scratch)(*self.arrs)


def _ride(exchange, refs, first, last):
    n = exchange.n
    cps = exchange.copies(refs[:n], refs[n:2 * n], refs[2 * n:])

    @pl.when(first)
    def _():
        for cp in cps:
            cp.start()

    @pl.when(last)
    def _():
        for cp in cps:
            cp.wait()


def all_gather(arrs, name):
    return Exchange("gather", arrs).run(name)


def all_gather_by_chip(arrs, name):
    n = len(arrs)
    chips = (2, 4, 6)

    def body(*refs):
        ins, outs = refs[:n], refs[n:2 * n]
        send_sems, recv_sems, local_sems = refs[2 * n:]
        me = _my_index()
        sibling, _ = _flip(1)

        def copy(a, k, block, to, src=None):
            return pltpu.make_async_remote_copy(
                src_ref=outs[a].at[block] if src is None else src, dst_ref=outs[a].at[block],
                send_sem=send_sems.at[7 * a + k], recv_sem=recv_sems.at[7 * a + k], device_id=to, device_id_type=MESH)

        own, sent = [], []
        for a in range(n):
            own.append(pltpu.make_async_copy(ins[a], outs[a].at[me], local_sems.at[a]))
            own[-1].start()
            first = [copy(a, 0, me, sibling, ins[a])] + [copy(a, 1 + j, me, _flip(k)[0], ins[a]) for j, k in enumerate(chips)]
            for cp in first:
                cp.start()
            sent += first
        for a in range(n):
            for j, k in enumerate(chips):
                peer, pidx = _flip(k)
                copy(a, 1 + j, pidx, peer).wait_recv()
                sent.append(copy(a, 4 + j, pidx, sibling))
                sent[-1].start()
        for a in range(n):
            copy(a, 0, _flip(1)[1], sibling).wait_recv()
            for j, k in enumerate(chips):
                copy(a, 4 + j, _flip(k + 1)[1], sibling).wait_recv()
        for cp in sent:
            cp.wait_send()
        for cp in own:
            cp.wait()

    return pl.pallas_call(
        body, name=name,
        out_shape=[jax.ShapeDtypeStruct((NDEV,) + a.shape, a.dtype) for a in arrs],
        in_specs=[ANY] * n, out_specs=[ANY] * n,
        scratch_shapes=[pltpu.SemaphoreType.DMA((7 * n,)), pltpu.SemaphoreType.DMA((7 * n,)),
                        pltpu.SemaphoreType.DMA((n,))],
    )(*arrs)


def all_to_all(arrs, name):
    return Exchange("scatter", arrs).run(name)


def _tile(n, pref):
    t = min(n, pref)
    while n % t:
        t //= 2
    return t


def _col_tile(n, pref):
    t = _tile(n, pref)
    return t if t % 128 == 0 and t >= 512 else n


def adamw(land, w, m, v, name):
    slots, r, c = land.shape
    tr = _tile(r, 64)
    bc1 = 1.0 - ADAM_B1 ** ADAM_STEP
    bc2 = 1.0 - ADAM_B2 ** ADAM_STEP
    lead = w.ndim - 2

    def body(land_ref, w_ref, m_ref, v_ref, g_ref, d_ref, nm_ref, nv_ref):
        at = (0,) * lead + (Ellipsis,)
        g = land_ref[0].astype(F32)
        for s in range(1, slots):
            g = g + land_ref[s].astype(F32)
        nm = ADAM_B1 * m_ref[at] + (1.0 - ADAM_B1) * g
        nv = ADAM_B2 * v_ref[at] + (1.0 - ADAM_B2) * (g * g)
        m_hat = nm / bc1
        v_hat = nv / bc2
        g_ref[at] = g
        nm_ref[at] = nm
        nv_ref[at] = nv
        d_ref[at] = -ADAM_LR * (m_hat / (jnp.sqrt(v_hat) + ADAM_EPS) + ADAM_WD * w_ref[at])

    blk = pl.BlockSpec((1,) * lead + (tr, c), lambda i: (0,) * lead + (i, 0))
    return pl.pallas_call(
        body, name=name, grid=(r // tr,),
        in_specs=[pl.BlockSpec((slots, tr, c), lambda i: (0, i, 0)), blk, blk, blk],
        out_specs=[blk] * 4,
        out_shape=[jax.ShapeDtypeStruct(w.shape, F32)] * 4,
        compiler_params=_cp("parallel"),
    )(land, w, m, v)


def sum_slots(land, d_model, name):
    slots, _, n = land.shape

    def body(land_ref, o_ref, loss_ref):
        g = land_ref[0]
        for s in range(1, slots):
            g = g + land_ref[s]
        o_ref[...] = g
        sq = jnp.sum(g[:, n - d_model:], axis=-1, keepdims=True)
        loss_ref[...] = jnp.broadcast_to(sq * (0.5 / d_model), (1, 128))

    return pl.pallas_call(
        body, name=name,
        out_shape=[jax.ShapeDtypeStruct((1, n), F32), jax.ShapeDtypeStruct((1, 128), F32)],
    )(land)


def _sigmoid(x):
    return 1.0 / (1.0 + jnp.exp(-x))


def adaln_fwd(c16, w_ada, b_ada, name):
    d3 = w_ada.shape[1]

    def body(c_ref, w_ref, b_ref, o_ref):
        cc = c_ref[...]
        sc = (cc * _sigmoid(cc)).astype(BF16)
        o_ref[...] = jnp.dot(sc, w_ref[...], preferred_element_type=F32) + b_ref[...]

    return pl.pallas_call(body, name=name, out_shape=jax.ShapeDtypeStruct((16, d3), F32),
                          compiler_params=pltpu.CompilerParams(vmem_limit_bytes=VMEM_LIMIT))(c16, w_ada, b_ada)


def adaln_bwd(c16, dshift16, p16, dgate16, scale16, g, name, emit_dw=True):
    d = c16.shape[1]
    nb = 3 * d // NDEV

    def body(c_ref, ds_ref, p_ref, dg_ref, sc_ref, g_ref, first_ref, db_ref, dng_ref, *scratch):
        j = pl.program_id(0)
        dmod_ref = scratch[0] if emit_dw else first_ref

        @pl.when(j == 0)
        def _():
            p = p_ref[...]
            dmod = jnp.concatenate([ds_ref[...], p * g_ref[...], dg_ref[...]], axis=-1)
            dmod_ref[...] = dmod
            db_ref[...] = jnp.sum(dmod, axis=0, keepdims=True)
            dng_ref[...] = jnp.sum((1.0 + sc_ref[...]) * p, axis=0, keepdims=True)

        if emit_dw:
            cc = c_ref[...]
            sc = (cc * _sigmoid(cc)).astype(BF16)
            start = pl.multiple_of(j * nb, 128)
            dm = dmod_ref[:, pl.ds(start, nb)].astype(BF16)
            first_ref[0] = lax.dot_general(sc, dm, (((0,), (0,)), ((), ())), preferred_element_type=F32).astype(BF16)

    full = lambda shape: pl.BlockSpec(shape, lambda j: (0,) * len(shape))
    first_spec = pl.BlockSpec((1, d, nb), lambda j: (j, 0, 0)) if emit_dw else full((16, 3 * d))
    first_shape = jax.ShapeDtypeStruct((NDEV, d, nb), BF16) if emit_dw else jax.ShapeDtypeStruct((16, 3 * d), F32)
    return pl.pallas_call(
        body, name=name, grid=(NDEV if emit_dw else 1,),
        in_specs=[full((16, d))] * 5 + [full((1, d))],
        out_specs=[first_spec, full((1, 3 * d)), full((1, d))],
        out_shape=[first_shape, jax.ShapeDtypeStruct((1, 3 * d), F32), jax.ShapeDtypeStruct((1, d), F32)],
        scratch_shapes=[pltpu.VMEM((16, 3 * d), F32)] if emit_dw else [],
        compiler_params=_cp("arbitrary"),
    )(c16, dshift16, p16, dgate16, scale16, g)


def dwada_columns(c_all, dmod_cols, name):
    d = c_all.shape[1]

    def body(c_ref, dm_ref, o_ref):
        cc = c_ref[...]
        sc = (cc * _sigmoid(cc)).astype(BF16)
        o_ref[0] = lax.dot_general(sc, dm_ref[...].astype(BF16), (((0,), (0,)), ((), ())), preferred_element_type=F32)

    return pl.pallas_call(body, name=name, out_shape=jax.ShapeDtypeStruct((1, d, dmod_cols.shape[1]), F32),
                          compiler_params=pltpu.CompilerParams(vmem_limit_bytes=VMEM_LIMIT))(c_all, dmod_cols)


def _modulated_norm(x, g, scale, shift):
    rstd = lax.rsqrt(jnp.mean(x * x, axis=-1, keepdims=True) + NORM_EPS)
    return ((x * rstd) * g) * (1.0 + scale) + shift


def normmod_matmul(x, g, scale, shift, w, out_dtype, seq, emit_h, name, cols=None, rider=None):
    t, d = x.shape
    col0, n = cols or (0, w.shape[1])
    tm = _tile(seq, 512)
    tn = _col_tile(n, 2048)
    assert col0 % tn == 0
    jb = col0 // tn
    per_seq = seq // tm
    grid = (t // tm, n // tn)
    nr = rider.n if rider else 0
    n_out = 2 if emit_h else 1

    def body(x_ref, g_ref, sc_ref, sh_ref, w_ref, *rest):
        o_ref = rest[nr]
        h_ref = rest[nr + 1] if emit_h else None
        h_scr = rest[2 * nr + n_out]
        if rider:
            _ride(rider, rest[:nr] + rest[nr + n_out:2 * nr + n_out] + rest[2 * nr + n_out + 1:], *_grid_ends(grid))

        @pl.when(pl.program_id(1) == 0)
        def _():
            h = _modulated_norm(x_ref[...], g_ref[...], sc_ref[0], sh_ref[0]).astype(BF16)
            h_scr[...] = h
            if emit_h:
                h_ref[...] = h

        o_ref[...] = jnp.dot(h_scr[...], w_ref[...], preferred_element_type=F32).astype(out_dtype)

    mod_spec = pl.BlockSpec((1, 1, d), lambda i, j: (i // per_seq, 0, 0))
    out_specs = [pl.BlockSpec((tm, tn), lambda i, j: (i, j))]
    out_shape = [jax.ShapeDtypeStruct((t, n), out_dtype)]
    if emit_h:
        out_specs.append(pl.BlockSpec((tm, d), lambda i, j: (i, 0)))
        out_shape.append(jax.ShapeDtypeStruct((t, d), BF16))
    return pl.pallas_call(
        body, name=name, grid=grid,
        in_specs=[pl.BlockSpec((tm, d), lambda i, j: (i, 0)), pl.BlockSpec((1, d), lambda i, j: (0, 0)),
                  mod_spec, mod_spec, pl.BlockSpec((d, tn), lambda i, j: (0, jb + j))] + (rider.in_specs if rider else []),
        out_specs=out_specs + (rider.out_specs if rider else []),
        out_shape=out_shape + (rider.out_shape if rider else []),
        scratch_shapes=[pltpu.VMEM((tm, d), BF16)] + (rider.scratch if rider else []),
        compiler_params=_cp("arbitrary", "arbitrary"),
    )(x, g, scale, shift, w, *(rider.arrs if rider else []))


def out_proj_fwd(o, zf, w_out, x, gate, seq, name):
    t, di = o.shape
    d = x.shape[1]
    tm = _tile(seq, 256)
    per_seq = seq // tm

    def body(o_ref, z_ref, w_ref, x_ref, gt_ref, xn_ref, y_ref, u_ref):
        z = z_ref[...]
        u = (o_ref[...] * (z * _sigmoid(z))).astype(BF16)
        y = jnp.dot(u, w_ref[...], preferred_element_type=F32)
        u_ref[...] = u
        y_ref[...] = y
        xn_ref[...] = x_ref[...] + gt_ref[0] * y

    row = lambda c: pl.BlockSpec((tm, c), lambda i: (i, 0))
    return pl.pallas_call(
        body, name=name, grid=(t // tm,),
        in_specs=[row(di), row(di), pl.BlockSpec((di, d), lambda i: (0, 0)), row(d),
                  pl.BlockSpec((1, 1, d), lambda i: (i // per_seq, 0, 0))],
        out_specs=[row(d), row(d), row(di)],
        out_shape=[jax.ShapeDtypeStruct((t, d), F32), jax.ShapeDtypeStruct((t, d), F32),
                   jax.ShapeDtypeStruct((t, di), BF16)],
        compiler_params=_cp("parallel"),
    )(o, zf, w_out, x, gate)


def out_proj_bwd(dxo, y, gate, w_out, o, zf, seq, name):
    t, d = dxo.shape
    di = o.shape[1]
    tm = _tile(seq, 256)
    per_seq = seq // tm

    def body(dx_ref, y_ref, gt_ref, w_ref, o_ref, z_ref, do_ref, dz_ref, dy_ref, dgt_ref):
        dx = dx_ref[...]
        part = jnp.sum(dx * y_ref[...], axis=0, keepdims=True)

        @pl.when(pl.program_id(0) % per_seq == 0)
        def _():
            dgt_ref[0] = part

        @pl.when(pl.program_id(0) % per_seq != 0)
        def _():
            dgt_ref[0] += part

        dy = (dx * gt_ref[0]).astype(BF16)
        dy_ref[...] = dy
        du = lax.dot_general(dy, w_ref[...], (((1,), (1,)), ((), ())), preferred_element_type=F32)
        z = z_ref[...]
        sg = _sigmoid(z)
        do_ref[...] = (du * (z * sg)).astype(BF16)
        dz_ref[...] = (du * o_ref[...] * (sg * (1.0 + z * (1.0 - sg)))).astype(BF16)

    row = lambda c: pl.BlockSpec((tm, c), lambda i: (i, 0))
    seq_spec = pl.BlockSpec((1, 1, d), lambda i: (i // per_seq, 0, 0))
    return pl.pallas_call(
        body, name=name, grid=(t // tm,),
        in_specs=[row(d), row(d), seq_spec, pl.BlockSpec((di, d), lambda i: (0, 0)), row(di), row(di)],
        out_specs=[row(di), row(di), row(d), seq_spec],
        out_shape=[jax.ShapeDtypeStruct((t, di), BF16), jax.ShapeDtypeStruct((t, di), BF16),
                   jax.ShapeDtypeStruct((t, d), BF16), jax.ShapeDtypeStruct((t // seq, 1, d), F32)],
        compiler_params=_cp("arbitrary"),
    )(dxo, y, gate, w_out, o, zf)


def tn_matmul(a, b, name):
    t, m = a.shape
    n = b.shape[1]
    tn = _col_tile(n, 1024)
    tk = _tile(t, 1024)
    nk = t // tk

    def body(a_ref, b_ref, o_ref, acc):
        part = lax.dot_general(a_ref[...], b_ref[...], (((0,), (0,)), ((), ())), preferred_element_type=F32)
        k = pl.program_id(1)

        @pl.when(k == 0)
        def _():
            acc[...] = part

        @pl.when(k != 0)
        def _():
            acc[...] += part

        @pl.when(k == nk - 1)
        def _():
            o_ref[...] = acc[...].astype(BF16)

    return pl.pallas_call(
        body, name=name, grid=(n // tn, nk),
        in_specs=[pl.BlockSpec((tk, m), lambda j, k: (k, 0)), pl.BlockSpec((tk, tn), lambda j, k: (k, j))],
        out_specs=pl.BlockSpec((m, tn), lambda j, k: (0, j)),
        out_shape=jax.ShapeDtypeStruct((m, n), BF16),
        scratch_shapes=[pltpu.VMEM((m, tn), F32)],
        compiler_params=_cp("parallel", "arbitrary"),
    )(a, b)


def dh_norm_bwd(parts, w_t, extra, x, g, scale, dxo, seq, name, rider=None):
    t, d = x.shape
    nparts = len(parts)
    kw = parts[0].shape[1]
    tm = _tile(seq, 512)
    per_seq = seq // tm
    has_extra = extra is not None
    grid = (t // tm,)
    nr = rider.n if rider else 0
    n_in = nparts + 1 + (2 if has_extra else 0) + 4

    def body(*refs):
        p_refs = refs[:nparts]
        w_ref = refs[nparts]
        pos = nparts + 1
        if has_extra:
            e_ref, we_ref = refs[pos], refs[pos + 1]
            pos += 2
        x_ref, g_ref, sc_ref, dxo_ref = refs[pos:pos + 4]
        dx_ref, pp_ref, ss_ref = refs[n_in + nr:n_in + nr + 3]
        if rider:
            _ride(rider, refs[n_in:n_in + nr] + refs[n_in + nr + 3:], *_grid_ends(grid))
        i = pl.program_id(0)

        dh = jnp.dot(p_refs[0][...], w_ref[0:kw, :], preferred_element_type=F32)
        for kk in range(1, nparts):
            dh = dh + jnp.dot(p_refs[kk][...], w_ref[kk * kw:(kk + 1) * kw, :], preferred_element_type=F32)
        if has_extra:
            dh = dh + jnp.dot(e_ref[...], we_ref[...], preferred_element_type=F32)
        xx = x_ref[...]
        rstd = lax.rsqrt(jnp.mean(xx * xx, axis=-1, keepdims=True) + NORM_EPS)
        xhat = xx * rstd
        dxhat = dh * (g_ref[...] * (1.0 + sc_ref[0]))
        dx_ref[...] = dxo_ref[...] + rstd * (dxhat - xhat * jnp.mean(dxhat * xhat, axis=-1, keepdims=True))
        pp = jnp.sum(dh * xhat, axis=0, keepdims=True)
        ss = jnp.sum(dh, axis=0, keepdims=True)

        @pl.when(i % per_seq == 0)
        def _():
            pp_ref[0] = pp
            ss_ref[0] = ss

        @pl.when(i % per_seq != 0)
        def _():
            pp_ref[0] += pp
            ss_ref[0] += ss

    row = lambda c: pl.BlockSpec((tm, c), lambda i: (i, 0))
    whole = lambda a: pl.BlockSpec(a.shape, lambda i: (0, 0), pipeline_mode=pl.Buffered(1))
    seq_spec = pl.BlockSpec((1, 1, d), lambda i: (i // per_seq, 0, 0))
    in_specs = [row(kw)] * nparts + [whole(w_t)]
    args = list(parts) + [w_t]
    if has_extra:
        in_specs += [row(extra[0].shape[1]), whole(extra[1])]
        args += list(extra)
    in_specs += [row(d), pl.BlockSpec((1, d), lambda i: (0, 0)), seq_spec, row(d)]
    args += [x, g, scale, dxo]
    return pl.pallas_call(
        body, name=name, grid=grid,
        in_specs=in_specs + (rider.in_specs if rider else []),
        out_specs=[row(d), seq_spec, seq_spec] + (rider.out_specs if rider else []),
        out_shape=[jax.ShapeDtypeStruct((t, d), F32), jax.ShapeDtypeStruct((t // seq, 1, d), F32),
                   jax.ShapeDtypeStruct((t // seq, 1, d), F32)] + (rider.out_shape if rider else []),
        scratch_shapes=rider.scratch if rider else [],
        compiler_params=_cp("arbitrary"),
    )(*args, *(rider.arrs if rider else []))


def final_loss(x, g, target, name):
    t, d = x.shape
    tm = _tile(t, 512)

    def body(x_ref, g_ref, t_ref, dx_ref, dg_ref, sq_ref):
        xx = x_ref[...]
        gg = g_ref[...]
        rstd = lax.rsqrt(jnp.mean(xx * xx, axis=-1, keepdims=True) + NORM_EPS)
        xhat = xx * rstd
        err = xhat * gg - t_ref[...]
        dy = err * (1.0 / d)
        dxhat = dy * gg
        dx_ref[...] = rstd * (dxhat - xhat * jnp.mean(dxhat * xhat, axis=-1, keepdims=True))
        dg = jnp.sum(dy * xhat, axis=0, keepdims=True)
        sq = jnp.sum(err * err, axis=0, keepdims=True)

        @pl.when(pl.program_id(0) == 0)
        def _():
            dg_ref[...] = dg
            sq_ref[...] = sq

        @pl.when(pl.program_id(0) != 0)
        def _():
            dg_ref[...] += dg
            sq_ref[...] += sq

    row = pl.BlockSpec((tm, d), lambda i: (i, 0))
    vec = pl.BlockSpec((1, d), lambda i: (0, 0))
    return pl.pallas_call(
        body, name=name, grid=(t // tm,),
        in_specs=[row, vec, row], out_specs=[row, vec, vec],
        out_shape=[jax.ShapeDtypeStruct((t, d), F32), jax.ShapeDtypeStruct((1, d), F32),
                   jax.ShapeDtypeStruct((1, d), F32)],
        compiler_params=_cp("arbitrary"),
    )(x, g, target)


def _split3(x):
    x1 = x.astype(BF16)
    r1 = x - x1.astype(F32)
    x2 = r1.astype(BF16)
    x3 = (r1 - x2.astype(F32)).astype(BF16)
    return x1, x2, x3


def _dot3(x, u):
    x1, x2, x3 = _split3(x)
    return (jnp.dot(x1, u, preferred_element_type=F32) + jnp.dot(x2, u, preferred_element_type=F32)
            + jnp.dot(x3, u, preferred_element_type=F32))


def _log_sigmoid(x):
    return jnp.minimum(x, 0.0) - jnp.log1p(jnp.exp(-jnp.abs(x)))


def forget_cumsum(fl_t, b_f, name):
    b, h, s = fl_t.shape

    def body(f_ref, b_ref, o_ref):
        lf = _log_sigmoid(f_ref[0] + b_ref[...])
        u = (lax.broadcasted_iota(jnp.int32, (s, s), 0) <= lax.broadcasted_iota(jnp.int32, (s, s), 1)).astype(BF16)
        o_ref[0] = _dot3(lf, u)

    return pl.pallas_call(
        body, name=name, grid=(b,),
        in_specs=[pl.BlockSpec((1, h, s), lambda i: (i, 0, 0)), pl.BlockSpec((h, 1), lambda i: (0, 0))],
        out_specs=pl.BlockSpec((1, h, s), lambda i: (i, 0, 0)),
        out_shape=jax.ShapeDtypeStruct((b, h, s), F32),
        compiler_params=_cp("parallel"),
    )(fl_t, b_f)


def forget_cumsum_bwd(dcum, fl_t, b_f, name):
    b, h, s = fl_t.shape

    def body(d_ref, f_ref, b_ref, o_ref, db_ref):
        u = (lax.broadcasted_iota(jnp.int32, (s, s), 0) >= lax.broadcasted_iota(jnp.int32, (s, s), 1)).astype(BF16)
        dlf = _dot3(d_ref[0], u)
        df = dlf * _sigmoid(-(f_ref[0] + b_ref[...]))
        o_ref[0] = df
        part = jnp.sum(df, axis=-1, keepdims=True)

        @pl.when(pl.program_id(0) == 0)
        def _():
            db_ref[...] = part

        @pl.when(pl.program_id(0) != 0)
        def _():
            db_ref[...] += part

    blk = pl.BlockSpec((1, h, s), lambda i: (i, 0, 0))
    return pl.pallas_call(
        body, name=name, grid=(b,),
        in_specs=[blk, blk, pl.BlockSpec((h, 1), lambda i: (0, 0))],
        out_specs=[blk, pl.BlockSpec((h, 1), lambda i: (0, 0))],
        out_shape=[jax.ShapeDtypeStruct((b, h, s), F32), jax.ShapeDtypeStruct((h, 1), F32)],
        compiler_params=_cp("arbitrary"),
    )(dcum, fl_t, b_f)


def _nt(a, b):
    return lax.dot_general(a, b, (((1,), (1,)), ((), ())), preferred_element_type=F32)


def _attn_dims(t, seq, dh, q_rows=512):
    tq = _tile(seq, q_rows)
    tk = _tile(seq, 256)
    return t // seq, tq, tk, seq // tq, tq // tk, ATTN_LANES // dh


def _blocks_per_trip(r):
    return ATTN_BLOCKS_PER_TRIP if r % ATTN_BLOCKS_PER_TRIP == 0 else 1


def _scaled_q(q_ref, dh):
    scale = dh ** -0.5
    if math.log2(dh) % 2 == 0:
        return (q_ref[...].astype(F32) * scale).astype(BF16), None
    return q_ref[...], scale


def _diag_masks(tq, tk, r, strict):
    row = lax.broadcasted_iota(jnp.int32, (tq, tk), 0)
    col = lax.broadcasted_iota(jnp.int32, (tq, tk), 1)
    return [(col + d * tk < row) if strict else (col + d * tk <= row) for d in range(r)]


def _rows(state, lo):
    return tuple(a[lo:] for a in state)


def _put_rows(state, part, lo):
    if lo == 0:
        return tuple(part)
    return tuple(jnp.concatenate([a[:lo], p], axis=0) for a, p in zip(state, part))


def _transposed(x):
    return jnp.transpose(x.astype(F32)).astype(BF16)


def _store_transposed(dst_ref, src_scr, tk):
    for jb in range(src_scr.shape[0]):
        dst_ref[jb * tk:(jb + 1) * tk, :] = jnp.transpose(src_scr[jb]).astype(BF16)


def _dot2(x, u):
    hi = x.astype(BF16)
    lo = (x - hi.astype(F32)).astype(BF16)
    return jnp.dot(jnp.concatenate([hi, lo], axis=1), jnp.concatenate([u, u], axis=0), preferred_element_type=F32)


def _attn_specs(seq, di, tq):
    nq = seq // tq
    cb = di // ATTN_LANES
    q_spec = pl.BlockSpec((tq, ATTN_LANES), lambda b, hp, i: (b * nq + i, hp))
    k_spec = pl.BlockSpec((seq, ATTN_LANES), lambda b, hp, i: (b, cb + hp))
    v_spec = pl.BlockSpec((seq, ATTN_LANES), lambda b, hp, i: (b, 2 * cb + hp))
    kv_out = pl.BlockSpec((seq, ATTN_LANES), lambda b, hp, i: (b, hp))
    return q_spec, k_spec, v_spec, kv_out


def _grid_ends(grid):
    ids = [pl.program_id(a) for a in range(len(grid))]
    first = functools.reduce(jnp.logical_and, [p == 0 for p in ids])
    last = functools.reduce(jnp.logical_and, [p == g - 1 for p, g in zip(ids, grid)])
    return first, last


def fox_fwd(qkv, ck, seq, di, dh, name, rider=None):
    t = qkv.shape[0]
    nb, tq, tk, nq, r, hpb = _attn_dims(t, seq, dh, FOX_Q_ROWS)
    per_trip = _blocks_per_trip(r)
    heads = [slice(hh * dh, (hh + 1) * dh) for hh in range(hpb)]
    assert hpb >= 2, "the row sums of p ride in another head's lanes of the p @ v matmul"
    q_spec, k_spec, v_spec, _ = _attn_specs(seq, di, tq)
    grid = (nb, di // ATTN_LANES, nq)
    nr = rider.n if rider else 0

    def body(q_ref, k_ref, v_ref, ck_ref, *rest):
        o_ref, lse_ref = rest[nr:nr + 2]
        knorm_ref = rest[2 * nr + 2]
        if rider:
            _ride(rider, rest[:nr] + rest[nr + 2:2 * nr + 2] + rest[2 * nr + 3:], *_grid_ends(grid))
        i = pl.program_id(2)
        masks = _diag_masks(tq, tk, r, False)
        q, scale = _scaled_q(q_ref, dh)
        qs = [q[:, hd] for hd in heads]
        lane_head = lax.broadcasted_iota(jnp.int32, (tk, ATTN_LANES), 1) // dh

        def largest_norm(x):
            x = x.astype(F32)
            return jnp.sqrt(jnp.max(jnp.sum(x * x, axis=-1, keepdims=True)))

        @pl.when(i == 0)
        def _():
            for hh in range(hpb):
                knorm_ref[hh] = largest_norm(k_ref[:, heads[hh]])

        qk_bound = [largest_norm(qs[hh]) * knorm_ref[hh] * (1.0 if scale is None else scale) + 1.0 for hh in range(hpb)]

        def scores(hh, j):
            start = pl.multiple_of(j * tk, tk)
            kh = k_ref[pl.ds(start, tk), heads[hh]]
            vt = v_ref[pl.ds(start, tk), :]
            vh = jnp.where(lane_head == hh, vt, jnp.ones_like(vt))
            s = _nt(qs[hh], kh)
            if scale is not None:
                s = s * scale
            return s - ck_ref[0, hh, pl.ds(j, 1), :], vh

        def update(state, s, vh):
            m, acc = state
            m_new = jnp.maximum(m, jnp.max(s, axis=-1, keepdims=True))
            p = jnp.exp(s - m_new)
            return m_new, jnp.exp(m - m_new) * acc + _dot2(p, vh)

        states = []
        for hh in range(hpb):
            s, vh = scores(hh, r * i)
            s = jnp.where(masks[0], s, -jnp.inf)
            m = jnp.max(s, axis=-1, keepdims=True)
            states.append((m, _dot2(jnp.exp(s - m), vh)))
        for d in range(1, r):
            for hh in range(hpb):
                s, vh = scores(hh, r * i + d)
                states[hh] = update(states[hh], jnp.where(masks[d], s, -jnp.inf), vh)

        def any_weight_left(states, j):
            jc = jnp.maximum(j, 0)
            tops = [jnp.max(-ck_ref[0, hh, pl.ds(jc, 1), :]) + qk_bound[hh] - jnp.min(states[hh][0]) for hh in range(hpb)]
            done = functools.reduce(jnp.logical_and, [top <= EXP_IS_ZERO_BELOW for top in tops])
            return jnp.logical_not(done).astype(jnp.int32)

        def step(carry):
            jj, _, states = carry
            for sub in range(per_trip):
                states = tuple(update(states[hh], *scores(hh, r * i - 1 - (per_trip * jj + sub))) for hh in range(hpb))
            return jj + 1, any_weight_left(states, r * i - 1 - per_trip * (jj + 1)), states

        trips, _, states = lax.while_loop(lambda c: jnp.logical_and(c[0] < r * i // per_trip, c[1] > 0), step,
                                          (jnp.int32(0), any_weight_left(states, r * i - 1), tuple(states)))
        for hh in range(hpb):
            m, acc = states[hh]
            other = heads[(hh + 1) % hpb].start
            l = acc[:, other:other + 1]
            o_ref[:, heads[hh]] = acc[:, heads[hh]] / l
            lse_ref[0, 0, :, hh:hh + 1] = m + jnp.log(l)
        lse_ref[0, 0, :, hpb:hpb + 1] = jnp.full((tq, 1), (trips * per_trip).astype(F32))

    return pl.pallas_call(
        body, name=name, grid=grid,
        in_specs=[q_spec, k_spec, v_spec,
                  pl.BlockSpec((1, hpb, seq // tk, tk), lambda b, hp, i: (b, hp, 0, 0))] + (rider.in_specs if rider else []),
        out_specs=[q_spec, pl.BlockSpec((1, 1, tq, hpb + 1), lambda b, hp, i: (b, hp, i, 0))]
        + (rider.out_specs if rider else []),
        out_shape=[jax.ShapeDtypeStruct((t, di), F32), jax.ShapeDtypeStruct((nb, di // ATTN_LANES, seq, hpb + 1), F32)]
        + (rider.out_shape if rider else []),
        scratch_shapes=[pltpu.SMEM((hpb,), F32)] + (rider.scratch if rider else []),
        compiler_params=_cp("arbitrary", "arbitrary", "arbitrary"),
    )(qkv, qkv, qkv, ck, *(rider.arrs if rider else []))


def fox_bwd(qkv, ck, o, do, lse, seq, di, dh, name, rider=None):
    t = qkv.shape[0]
    nb, tq, tk, nq, r, hpb = _attn_dims(t, seq, dh, FOX_Q_ROWS)
    per_trip = _blocks_per_trip(r)
    heads = [slice(hh * dh, (hh + 1) * dh) for hh in range(hpb)]
    q_spec, k_spec, v_spec, kv_out = _attn_specs(seq, di, tq)
    ck_spec = pl.BlockSpec((1, hpb, seq // tk, tk), lambda b, hp, i: (b, hp, 0, 0))
    grid = (nb, di // ATTN_LANES, nq)
    nr = rider.n if rider else 0

    def body(q_ref, k_ref, v_ref, ck_ref, o_ref, do_ref, lse_ref, *rest):
        dq_ref, dk_ref, dv_ref, dck_ref = rest[nr:nr + 4]
        dkt_scr, dvt_scr = rest[2 * nr + 4:2 * nr + 6]
        if rider:
            _ride(rider, rest[:nr] + rest[nr + 4:2 * nr + 4] + rest[2 * nr + 6:], *_grid_ends(grid))
        i = pl.program_id(2)

        @pl.when(i == 0)
        def _():
            dkt_scr[...] = jnp.zeros_like(dkt_scr)
            dvt_scr[...] = jnp.zeros_like(dvt_scr)
            dck_ref[...] = jnp.zeros_like(dck_ref)

        masks = _diag_masks(tq, tk, r, False)
        q, scale = _scaled_q(q_ref, dh)
        do = do_ref[...]
        q_t, do_t = _transposed(q), _transposed(do)
        qs = [q[:, hd] for hd in heads]
        dos = [do[:, hd] for hd in heads]
        deltas = [jnp.sum(dos[hh].astype(F32) * o_ref[:, heads[hh]], axis=-1, keepdims=True) for hh in range(hpb)]
        lses = [lse_ref[0, 0, :, hh:hh + 1] for hh in range(hpb)]

        def block(hh, j, mask, dq_acc, lo=0):
            start = pl.multiple_of(j * tk, tk)
            kh = k_ref[pl.ds(start, tk), heads[hh]]
            vh = v_ref[pl.ds(start, tk), heads[hh]]
            s = _nt(qs[hh][lo:], kh)
            if scale is not None:
                s = s * scale
            p = jnp.exp(s - ck_ref[0, hh, pl.ds(j, 1), :] - lses[hh][lo:])
            if mask is not None:
                p = jnp.where(mask[lo:], p, 0.0)
            ds = p * (_nt(dos[hh][lo:], vh) - deltas[hh][lo:])
            dsb = ds.astype(BF16)
            dkt = jnp.dot(q_t[heads[hh], lo:], dsb, preferred_element_type=F32)
            if scale is not None:
                dkt = dkt * scale
            dkt_scr[j, heads[hh], :] += dkt
            dvt_scr[j, heads[hh], :] += jnp.dot(do_t[heads[hh], lo:], p.astype(BF16), preferred_element_type=F32)
            dck_ref[0, hh, pl.ds(j, 1), :] -= jnp.sum(ds, axis=0, keepdims=True)
            return dq_acc + jnp.dot(dsb, kh, preferred_element_type=F32)

        accs = [jnp.zeros((tq, dh), F32)] * hpb
        for d in range(r):
            lo = d * tk
            accs = [_put_rows((accs[hh],), (block(hh, r * i + d, masks[d], accs[hh][lo:], lo),), lo)[0]
                    for hh in range(hpb)]

        trips = jnp.clip(jnp.max(lse_ref[0, 0, :, hpb:hpb + 1]).astype(jnp.int32), 0, r * i) // per_trip
        first = r * i - trips * per_trip

        def step(jj, accs):
            for sub in range(per_trip):
                accs = tuple(block(hh, first + per_trip * jj + sub, None, accs[hh]) for hh in range(hpb))
            return accs

        accs = lax.fori_loop(0, trips, step, tuple(accs))
        for hh in range(hpb):
            dq_ref[:, heads[hh]] = (accs[hh] * dh ** -0.5).astype(BF16)

        @pl.when(i == nq - 1)
        def _():
            _store_transposed(dk_ref, dkt_scr, tk)
            _store_transposed(dv_ref, dvt_scr, tk)

    return pl.pallas_call(
        body, name=name, grid=grid,
        in_specs=[q_spec, k_spec, v_spec, ck_spec, q_spec, q_spec,
                  pl.BlockSpec((1, 1, tq, hpb + 1), lambda b, hp, i: (b, hp, i, 0))] + (rider.in_specs if rider else []),
        out_specs=[q_spec, kv_out, kv_out, ck_spec] + (rider.out_specs if rider else []),
        out_shape=[jax.ShapeDtypeStruct((t, di), BF16)] * 3 + [jax.ShapeDtypeStruct(ck.shape, F32)]
        + (rider.out_shape if rider else []),
        scratch_shapes=[pltpu.VMEM((seq // tk, ATTN_LANES, tk), F32), pltpu.VMEM((seq // tk, ATTN_LANES, tk), F32)]
        + (rider.scratch if rider else []),
        compiler_params=_cp("arbitrary", "arbitrary", "arbitrary"),
    )(qkv, qkv, qkv, ck, o, do, lse, *(rider.arrs if rider else []))


def _sb_logits(qh, kh, scale, strict):
    z = _nt(qh, kh)
    if scale is not None:
        z = z * scale
    e = jnp.exp(-jnp.abs(z))
    lb = jnp.minimum(z, 0.0) - jnp.log(1.0 + e)
    lk = lb - z
    return lb, lk if strict is None else jnp.where(strict, lk, 0.0)


def _cum(x, u):
    return jnp.dot(x.astype(BF16), u, preferred_element_type=F32)


def sb_fwd(qkv, seq, di, dh, name):
    t = qkv.shape[0]
    nb, tq, tk, nq, r, hpb = _attn_dims(t, seq, dh)
    heads = [slice(hh * dh, (hh + 1) * dh) for hh in range(hpb)]
    q_spec, k_spec, v_spec, _ = _attn_specs(seq, di, tq)

    def body(q_ref, k_ref, v_ref, o_ref, rt_ref):
        i = pl.program_id(2)
        masks = _diag_masks(tq, tk, r, True)
        row = lax.broadcasted_iota(jnp.int32, (tk, tk), 0)
        col = lax.broadcasted_iota(jnp.int32, (tk, tk), 1)
        u_after = (row > col).astype(BF16)
        q, scale = _scaled_q(q_ref, dh)
        qs = [q[:, hd] for hd in heads]

        def block(hh, j, mask, state, lo=0):
            rr, acc = state
            start = pl.multiple_of(j * tk, tk)
            kh = k_ref[pl.ds(start, tk), heads[hh]]
            vh = v_ref[pl.ds(start, tk), heads[hh]]
            if mask is not None:
                mask = mask[lo:]
            lb, lk = _sb_logits(qs[hh][lo:], kh, scale, mask)
            a = jnp.exp(lb + _cum(lk, u_after) + rr)
            if mask is not None:
                a = jnp.where(mask, a, 0.0)
            acc = acc + jnp.dot(a.astype(BF16), vh, preferred_element_type=F32)
            return rr + jnp.sum(lk, axis=-1, keepdims=True), acc

        states = [(jnp.zeros((tq, 1), F32), jnp.zeros((tq, dh), F32))] * hpb
        for d in reversed(range(r)):
            lo = d * tk
            states = [_put_rows(states[hh], block(hh, r * i + d, masks[d], _rows(states[hh], lo), lo), lo)
                      for hh in range(hpb)]

        def any_weight_left(states):
            done = functools.reduce(jnp.logical_and, [jnp.max(st[0]) <= EXP_IS_ZERO_BELOW for st in states])
            return jnp.logical_not(done).astype(jnp.int32)

        def step(carry):
            jj, _, states = carry
            states = tuple(block(hh, r * i - 1 - jj, None, states[hh]) for hh in range(hpb))
            return jj + 1, any_weight_left(states), states

        visited, _, states = lax.while_loop(lambda c: jnp.logical_and(c[0] < r * i, c[1] > 0), step,
                                            (jnp.int32(0), any_weight_left(states), tuple(states)))
        for hh in range(hpb):
            o_ref[:, heads[hh]] = states[hh][1]
            rt_ref[0, 0, :, hh:hh + 1] = states[hh][0]
        rt_ref[0, 0, :, hpb:hpb + 1] = jnp.full((tq, 1), visited.astype(F32))

    return pl.pallas_call(
        body, name=name, grid=(nb, di // ATTN_LANES, nq),
        in_specs=[q_spec, k_spec, v_spec],
        out_specs=[q_spec, pl.BlockSpec((1, 1, tq, hpb + 1), lambda b, hp, i: (b, hp, i, 0))],
        out_shape=[jax.ShapeDtypeStruct((t, di), F32), jax.ShapeDtypeStruct((nb, di // ATTN_LANES, seq, hpb + 1), F32)],
        compiler_params=_cp("parallel", "parallel", "arbitrary"),
    )(qkv, qkv, qkv)


def sb_bwd(qkv, do, rtot, seq, di, dh, name):
    t = qkv.shape[0]
    nb, tq, tk, nq, r, hpb = _attn_dims(t, seq, dh)
    heads = [slice(hh * dh, (hh + 1) * dh) for hh in range(hpb)]
    q_spec, k_spec, v_spec, kv_out = _attn_specs(seq, di, tq)

    def body(q_ref, k_ref, v_ref, do_ref, rt_ref, dq_ref, dk_ref, dv_ref, dkt_scr, dvt_scr):
        i = pl.program_id(2)

        @pl.when(i == 0)
        def _():
            dkt_scr[...] = jnp.zeros_like(dkt_scr)
            dvt_scr[...] = jnp.zeros_like(dvt_scr)

        masks = _diag_masks(tq, tk, r, True)
        row = lax.broadcasted_iota(jnp.int32, (tk, tk), 0)
        col = lax.broadcasted_iota(jnp.int32, (tk, tk), 1)
        u_after = (row > col).astype(BF16)
        u_before = (row < col).astype(BF16)
        q, scale = _scaled_q(q_ref, dh)
        do = do_ref[...]
        q_t, do_t = _transposed(q), _transposed(do)
        qs = [q[:, hd] for hd in heads]
        dos = [do[:, hd] for hd in heads]
        rts = [rt_ref[0, 0, :, hh:hh + 1] for hh in range(hpb)]

        def block(hh, j, mask, state, lo=0):
            lc, gc, dq_acc = state
            start = pl.multiple_of(j * tk, tk)
            kh = k_ref[pl.ds(start, tk), heads[hh]]
            vh = v_ref[pl.ds(start, tk), heads[hh]]
            if mask is not None:
                mask = mask[lo:]
            lb, lk = _sb_logits(qs[hh][lo:], kh, scale, mask)
            lc = lc + jnp.sum(lk, axis=-1, keepdims=True)
            a = jnp.exp(lb + ((rts[hh][lo:] - lc) + _cum(lk, u_after)))
            if mask is not None:
                a = jnp.where(mask, a, 0.0)
            de = a * _nt(dos[hh][lo:], vh)
            g = gc + _cum(de, u_before)
            dz = de - jnp.exp(lb) * (de + g)
            if mask is not None:
                dz = jnp.where(mask, dz, 0.0)
            dzb = dz.astype(BF16)
            dkt = jnp.dot(q_t[heads[hh], lo:], dzb, preferred_element_type=F32)
            if scale is not None:
                dkt = dkt * scale
            dkt_scr[j, heads[hh], :] += dkt
            dvt_scr[j, heads[hh], :] += jnp.dot(do_t[heads[hh], lo:], a.astype(BF16), preferred_element_type=F32)
            return (lc, gc + jnp.sum(de, axis=-1, keepdims=True),
                    dq_acc + jnp.dot(dzb, kh, preferred_element_type=F32))

        zero = jnp.zeros((tq, 1), F32)
        visited = jnp.clip(jnp.max(rt_ref[0, 0, :, hpb:hpb + 1]).astype(jnp.int32), 0, r * i)
        first = r * i - visited

        def step(jj, states):
            return tuple(block(hh, first + jj, None, states[hh]) for hh in range(hpb))

        states = lax.fori_loop(0, visited, step, ((zero, zero, jnp.zeros((tq, dh), F32)),) * hpb)
        for d in range(r):
            lo = d * tk
            states = [_put_rows(states[hh], block(hh, r * i + d, masks[d], _rows(states[hh], lo), lo), lo)
                      for hh in range(hpb)]
        for hh in range(hpb):
            dq_ref[:, heads[hh]] = (states[hh][2] * dh ** -0.5).astype(BF16)

        @pl.when(i == nq - 1)
        def _():
            _store_transposed(dk_ref, dkt_scr, tk)
            _store_transposed(dv_ref, dvt_scr, tk)

    return pl.pallas_call(
        body, name=name, grid=(nb, di // ATTN_LANES, nq),
        in_specs=[q_spec, k_spec, v_spec, q_spec, pl.BlockSpec((1, 1, tq, hpb + 1), lambda b, hp, i: (b, hp, i, 0))],
        out_specs=[q_spec, kv_out, kv_out],
        out_shape=[jax.ShapeDtypeStruct((t, di), BF16)] * 3,
        scratch_shapes=[pltpu.VMEM((seq // tk, ATTN_LANES, tk), F32), pltpu.VMEM((seq // tk, ATTN_LANES, tk), F32)],
        compiler_params=_cp("parallel", "parallel", "arbitrary"),
    )(qkv, qkv, qkv, do, rtot)


def _cols(g):
    return jnp.transpose(g, (1, 0, 2)).reshape(g.shape[1], NDEV * g.shape[2])


def _col_blocks(w):
    r, c8 = w.shape
    return jnp.transpose(w.reshape(r, NDEV, c8 // NDEV), (1, 0, 2))


def _pad_rows16(a):
    return jnp.pad(a, ((0, 16 - a.shape[0]), (0, 0)))


def _pad_cols(a, n):
    return jnp.pad(a, ((0, 0), (0, n - a.shape[1])))


def kernel(x, c, fox_norm_g, fox_w_ada, fox_b_ada, fox_w_in, fox_b_f, fox_w_out, sb_norm_g, sb_w_ada, sb_b_ada, sb_w_in, sb_w_out, final_norm_g, loss_target, m_fox_norm_g, m_fox_w_ada, m_fox_b_ada, m_fox_w_in, m_fox_b_f, m_fox_w_out, m_sb_norm_g, m_sb_w_ada, m_sb_b_ada, m_sb_w_in, m_sb_w_out, m_final_norm_g, v_fox_norm_g, v_fox_w_ada, v_fox_b_ada, v_fox_w_in, v_fox_b_f, v_fox_w_out, v_sb_norm_g, v_sb_w_ada, v_sb_b_ada, v_sb_w_in, v_sb_w_out, v_final_norm_g):
    nb, seq, d = x.shape
    t = nb * seq
    h = fox_b_f.shape[-1]
    di = fox_w_out.shape[1] * NDEV
    dh = di // h
    tk = _attn_dims(t, seq, dh)[2]
    me = _my_index()

    gathered = all_gather_by_chip([w[0].astype(BF16) for w in (fox_w_ada, fox_w_in)], "gather_fox_weights")
    fox_wada, fox_win = _cols(gathered[0]), _cols(gathered[1])
    fox_wout_gather = Exchange("gather", [fox_w_out[0].astype(BF16)])
    sb_gather = Exchange("gather", [w[0].astype(BF16) for w in (sb_w_ada, sb_w_in, sb_w_out)] + [sb_norm_g, sb_b_ada])

    x0 = x.reshape(t, d)
    target = loss_target.reshape(t, d)
    c16 = _pad_rows16(c)

    def layer_fwd(xin, g, wada, bada, win, wout, b_f, tag, rider=None):
        mod = adaln_fwd(c16, wada, bada, tag + "_adaln")[:nb]
        shift, scale, gate = (mod[:, k * d:(k + 1) * d].reshape(nb, 1, d) for k in range(3))
        wout_rider = wout if isinstance(wout, Exchange) else None
        qkv, hmod, *rode_wout = normmod_matmul(xin, g, scale, shift, win, BF16, seq, True, tag + "_qkv", (0, 3 * di),
                                               wout_rider)
        if wout_rider:
            wout = rode_wout[0].reshape(di, d)
        if b_f is not None:
            (zf,) = normmod_matmul(xin, g, scale, shift, _pad_cols(win[:, 3 * di:], di + 128), F32, seq, False, tag + "_z")
        else:
            (zf,) = normmod_matmul(xin, g, scale, shift, win, F32, seq, False, tag + "_z", (3 * di, di))
        saved = dict(x=xin, g=g, scale=scale, gate=gate, qkv=qkv, h=hmod, zf=zf, win=win, wout=wout)
        if b_f is not None:
            fl_t = jnp.transpose(zf[:, di:di + h].reshape(nb, seq, h), (0, 2, 1))
            bf_col = b_f.reshape(h, 1)
            cum = forget_cumsum(fl_t, bf_col, tag + "_cum")
            ck = cum.reshape(nb, h, seq // tk, tk)
            o, lse, *rode = fox_fwd(qkv, ck, seq, di, dh, tag + "_attn", rider)
            saved.update(fl_t=fl_t, bf_col=bf_col, ck=ck, lse=lse, rode=rode)
        else:
            o, rtot = sb_fwd(qkv, seq, di, dh, tag + "_attn")
            saved.update(rtot=rtot)
        xout, y, u = out_proj_fwd(o, zf, wout, xin, gate, seq, tag + "_out")
        saved.update(o=o, y=y, u=u)
        return xout, saved

    def layer_bwd(dxo, sv, is_fox, tag, riding=()):
        do, dz, dy, dgate = out_proj_bwd(dxo, sv["y"], sv["gate"], sv["wout"], sv["o"], sv["zf"], seq, tag + "_dout")
        wout_blocks = tn_matmul(sv["u"], dy, tag + "_dwout").reshape(NDEV, di // NDEV, d)
        extra = None
        if is_fox:
            dq, dk, dv, dck, *rode = fox_bwd(sv["qkv"], sv["ck"], sv["o"], do, sv["lse"], seq, di, dh, tag + "_dattn",
                                             Exchange("scatter", list(riding) + [wout_blocks]))
            df_t, dbf = forget_cumsum_bwd(dck.reshape(nb, h, seq), sv["fl_t"], sv["bf_col"], tag + "_dcum")
            df = _pad_cols(jnp.transpose(df_t, (0, 2, 1)).reshape(t, h), 128).astype(BF16)
            extra = (df, jnp.transpose(_pad_cols(sv["win"][:, 4 * di:], 128)))
        else:
            dq, dk, dv = sb_bwd(sv["qkv"], do, sv["rtot"], seq, di, dh, tag + "_dattn")
        parts = [dq, dk, dv, dz]
        dwin = [tn_matmul(sv["h"], p, tag + "_dwin%d" % k) for k, p in enumerate(parts)]
        if is_fox:
            dwin.append(tn_matmul(sv["h"], df, tag + "_dwinf")[:, :h])
        dwin = jnp.concatenate(dwin, axis=1)
        w_t = jnp.transpose(sv["win"][:, :4 * di])
        win_blocks = _col_blocks(dwin)
        dh_rider = Exchange("scatter", [win_blocks]) if is_fox else None
        dxin, psum, ssum, *landed = dh_norm_bwd(parts, w_t, extra, sv["x"], sv["g"], sv["scale"], dxo, seq,
                                                tag + "_dh", dh_rider)
        pad = lambda a: _pad_rows16(a.reshape(nb, d))
        dwada, dbada, dng = adaln_bwd(c16, pad(ssum), pad(psum), pad(dgate), pad(sv["scale"]), sv["g"], tag + "_dadaln",
                                      emit_dw=not is_fox)
        grads = dict(wada=dwada, bada=dbada, ng=dng, win=win_blocks, wout=wout_blocks)
        if is_fox:
            grads["bf"] = dbf.reshape(1, h)
            grads["rode"] = rode
            grads["landed"] = landed
        return dxin, grads

    x1, sv_fox = layer_fwd(x0, fox_norm_g, fox_wada, fox_b_ada, fox_win, fox_wout_gather, fox_b_f, "fox", sb_gather)
    sb_wada, sb_win = _cols(sv_fox["rode"][0]), _cols(sv_fox["rode"][1])
    sb_wout = sv_fox["rode"][2].reshape(di, d)
    sb_g, sb_bada = _cols(sv_fox["rode"][3]), _cols(sv_fox["rode"][4])
    x2, sv_sb = layer_fwd(x1, sb_g, sb_wada, sb_bada, sb_win, sb_wout, None, "sb")
    dx2, dgf, sq = final_loss(x2, final_norm_g.reshape(1, d), target, "loss_head")
    dx1, g_sb = layer_bwd(dx2, sv_sb, False, "sb")
    dx0, g_fox = layer_bwd(dx1, sv_fox, True, "fox", [g_sb["wada"], g_sb["win"], g_sb["wout"]])

    sb_wada_land, sb_win_land, sb_wout_land, fox_wout_land = g_fox["rode"]
    small = jnp.concatenate([g_fox["ng"], g_fox["bada"], _pad_cols(g_fox["bf"], 128), g_sb["ng"], g_sb["bada"],
                             dgf, sq], axis=1)
    rows = jnp.concatenate([g_fox["wada"][:nb], c], axis=1)
    small_all, rows_all = all_gather([small, rows], "gather_small")
    small_sum, loss_row = sum_slots(small_all, d, "sum_small")
    loss = loss_row[0, 0]
    ncol = 3 * d // NDEV
    dmod_cols = lax.dynamic_slice(rows_all[:, :, :3 * d].reshape(NDEV * nb, 3 * d), (0, me * ncol), (NDEV * nb, ncol))
    fox_wada_land = dwada_columns(rows_all[:, :, 3 * d:].reshape(NDEV * nb, d), dmod_cols, "fox_dwada")
    land = [fox_wada_land, g_fox["landed"][0], fox_wout_land, sb_wada_land, sb_win_land, sb_wout_land]

    offs = {}
    pos = 0
    for nm, width in (("fox_ng", d), ("fox_bada", 3 * d), ("fox_bf", 128), ("sb_ng", d), ("sb_bada", 3 * d), ("fin_g", d)):
        offs[nm] = (pos, width)
        pos += width

    def small_grad(nm, width=None, shard=False):
        p0, wd = offs[nm]
        wd = width or wd
        if shard:
            blk = wd // NDEV
            return lax.dynamic_slice(small_sum, (0, p0 + me * blk), (1, blk))
        return small_sum[:, p0:p0 + wd]

    results = {}
    big_params = [("fox_w_ada", fox_w_ada, m_fox_w_ada, v_fox_w_ada), ("fox_w_in", fox_w_in, m_fox_w_in, v_fox_w_in),
                  ("fox_w_out", fox_w_out, m_fox_w_out, v_fox_w_out), ("sb_w_ada", sb_w_ada, m_sb_w_ada, v_sb_w_ada),
                  ("sb_w_in", sb_w_in, m_sb_w_in, v_sb_w_in), ("sb_w_out", sb_w_out, m_sb_w_out, v_sb_w_out)]
    for k, (nm, w, m, v) in enumerate(big_params):
        results[nm] = adamw(land[k], w, m, v, "adamw_" + nm)
    small_params = [("fox_norm_g", fox_norm_g, m_fox_norm_g, v_fox_norm_g, small_grad("fox_ng")),
                    ("fox_b_ada", fox_b_ada, m_fox_b_ada, v_fox_b_ada, small_grad("fox_bada")),
                    ("fox_b_f", fox_b_f, m_fox_b_f, v_fox_b_f, small_grad("fox_bf", h)),
                    ("sb_norm_g", sb_norm_g, m_sb_norm_g, v_sb_norm_g, small_grad("sb_ng", shard=True)),
                    ("sb_b_ada", sb_b_ada, m_sb_b_ada, v_sb_b_ada, small_grad("sb_bada", shard=True)),
                    ("final_norm_g", final_norm_g.reshape(1, d), m_final_norm_g.reshape(1, d),
                     v_final_norm_g.reshape(1, d), small_grad("fin_g"))]
    for nm, w, m, v, g in small_params:
        outs = adamw(g[None], w, m, v, "adamw_" + nm)
        if nm == "final_norm_g":
            outs = [o.reshape(d) for o in outs]
        results[nm] = outs

    order = ["fox_norm_g", "fox_w_ada", "fox_b_ada", "fox_w_in", "fox_b_f", "fox_w_out", "sb_norm_g", "sb_w_ada",
             "sb_b_ada", "sb_w_in", "sb_w_out", "final_norm_g"]
    out = [loss, dx0.reshape(nb, seq, d)]
    for k in range(4):
        out += [results[nm][k] for nm in order]
    return tuple(out)
```
